```python
import jax
import jax.numpy as jnp
from jax import lax
import numpy as np

D_MODEL = 1024
BATCH = 2
SEQ = 8192
DEPTH = 2

HEAD_DIM = 64
BRANCH_HEADS = 6
BRANCH_WIDTH = BRANCH_HEADS * HEAD_DIM
N_BRANCH = 5
MOBA_BLOCK = 256
MOBA_TOPK = 3
MOBA_QCHUNK = 64
DILATED_GROUPS = ((128, 1), (512, 4), (2048, 16))
N_DIL_GROUPS = 3
BAND = 128
FOX_BLOCK = 128
GDN_CHUNK = 64
CONV_K = 4
MEM_LEN = 256
MEM_HEADS = 4
MEM_HEAD_DIM = BRANCH_WIDTH // MEM_HEADS

A_QKV = 3 * BRANCH_WIDTH
B_QK = 2 * N_DIL_GROUPS * BRANCH_WIDTH
B_V = BRANCH_WIDTH
C_QKV = 3 * BRANCH_WIDTH
C_F = BRANCH_HEADS
D_QKV = 3 * BRANCH_WIDTH
D_BETA = BRANCH_HEADS
D_DECAY = BRANCH_HEADS
E_Q = BRANCH_WIDTH
Z_W = N_BRANCH * BRANCH_WIDTH
MERGE_W = N_BRANCH * D_MODEL
IN_SPLIT_SIZES = (A_QKV, B_QK, B_V, C_QKV, C_F, D_QKV, D_BETA, D_DECAY, E_Q, Z_W, MERGE_W)
D_IN = A_QKV + B_QK + B_V + C_QKV + C_F + D_QKV + D_BETA + D_DECAY + E_Q + Z_W + MERGE_W

DEEPNORM_ALPHA = (2 * DEPTH) ** 0.25
DEEPNORM_BETA = (8 * DEPTH) ** -0.25
NEG = -1e30
LN_EPS = 1e-5
RMS_EPS = 1e-6

kernel_name = 'hybrid_gated_parallel_moba_dilated_fox_gdn'


def _layer_norm(x, g, b):
    xf = x.astype(jnp.float32)
    mu = xf.mean(-1, keepdims=True)
    var = jnp.square(xf - mu).mean(-1, keepdims=True)
    y = (xf - mu) * lax.rsqrt(var + LN_EPS) * g.astype(jnp.float32) + b.astype(jnp.float32)
    return y.astype(x.dtype)


def _split_heads(t, n):
    B, S, _ = t.shape
    return t.reshape(B, S, n, -1).transpose(0, 2, 1, 3)


def _merge_heads(o):
    B, H, S, dh = o.shape
    return o.transpose(0, 2, 1, 3).reshape(B, S, H * dh)


def _l2norm(t):
    return t * lax.rsqrt(jnp.sum(jnp.square(t), -1, keepdims=True) + RMS_EPS)


def _moba_attention(q, k, v):
    B, H, S, dh = q.shape
    nblk = -(-S // MOBA_BLOCK)
    Sp = nblk * MOBA_BLOCK
    pad = ((0, 0), (0, 0), (0, Sp - S), (0, 0))
    q = jnp.pad(q.astype(jnp.float32) * dh ** -0.5, pad)
    k = jnp.pad(k.astype(jnp.float32), pad)
    v = jnp.pad(v.astype(jnp.float32), pad)
    kb = k.reshape(B, H, nblk, MOBA_BLOCK, dh)
    vb = v.reshape(B, H, nblk, MOBA_BLOCK, dh)
    qblk = jnp.arange(Sp) // MOBA_BLOCK
    gate = jnp.einsum('bhsd,bhnd->bhsn', q, kb.mean(axis=3))
    gate = jnp.where(jnp.arange(nblk)[None, :] < qblk[:, None], gate, NEG)
    ksel = min(MOBA_TOPK, nblk)
    _, sel = lax.top_k(gate, ksel)
    sel_ok = sel < qblk[:, None]
    bi = jnp.arange(B)[:, None, None, None]
    hi = jnp.arange(H)[None, :, None, None]
    qoff = jnp.arange(MOBA_QCHUNK)
    koff = jnp.arange(MOBA_BLOCK)

    def chunk(c):
        start = c * MOBA_QCHUNK
        own = (start // MOBA_BLOCK) * MOBA_BLOCK
        qc = lax.dynamic_slice_in_dim(q, start, MOBA_QCHUNK, axis=2)
        sc = lax.dynamic_slice_in_dim(sel, start, MOBA_QCHUNK, axis=2)
        okc = lax.dynamic_slice_in_dim(sel_ok, start, MOBA_QCHUNK, axis=2)
        kg = kb[bi, hi, sc]
        vg = vb[bi, hi, sc]
        l_sel = jnp.einsum('bhqd,bhqnkd->bhqnk', qc, kg)
        l_sel = jnp.where(okc[..., None], l_sel, NEG).reshape(B, H, MOBA_QCHUNK, ksel * MOBA_BLOCK)
        ko = lax.dynamic_slice_in_dim(k, own, MOBA_BLOCK, axis=2)
        vo = lax.dynamic_slice_in_dim(v, own, MOBA_BLOCK, axis=2)
        l_own = jnp.einsum('bhqd,bhkd->bhqk', qc, ko)
        l_own = jnp.where((start + qoff)[:, None] >= (own + koff)[None, :], l_own, NEG)
        p = jax.nn.softmax(jnp.concatenate([l_sel, l_own], axis=-1), axis=-1)
        p_sel = p[..., :ksel * MOBA_BLOCK].reshape(B, H, MOBA_QCHUNK, ksel, MOBA_BLOCK)
        p_own = p[..., ksel * MOBA_BLOCK:]
        return jnp.einsum('bhqnk,bhqnkd->bhqd', p_sel, vg) + jnp.einsum('bhqk,bhkd->bhqd', p_own, vo)

    out = lax.map(chunk, jnp.arange(Sp // MOBA_QCHUNK))
    return jnp.moveaxis(out, 0, 2).reshape(B, H, Sp, dh)[:, :, :S]


def _dilated_band(q, k, v, window, dilation):
    B, H, S, dh = q.shape
    L = S // dilation
    nb = -(-L // BAND)
    Lp = nb * BAND
    reach = window // dilation

    def streams(t):
        t = t.reshape(B, H, L, dilation, dh).transpose(0, 1, 3, 2, 4)
        t = jnp.pad(t, ((0, 0), (0, 0), (0, 0), (0, Lp - L), (0, 0)))
        return t.reshape(B, H, dilation, nb, BAND, dh)

    def with_prev(t):
        prev = jnp.pad(t, ((0, 0), (0, 0), (0, 0), (1, 0), (0, 0), (0, 0)))[:, :, :, :nb]
        return jnp.concatenate([prev, t], axis=4)

    qs = streams(q)
    kk = with_prev(streams(k))
    vv = with_prev(streams(v))
    logits = jnp.einsum('bhrnqd,bhrnkd->bhrnqk', qs, kk)
    qi = jnp.arange(BAND)[:, None]
    kj = jnp.arange(2 * BAND)[None, :]
    dist = BAND + qi - kj
    blk = jnp.arange(nb)[:, None, None]
    valid = (dist >= 0) & (dist <= reach) & ((blk - 1) * BAND + kj >= 0)
    logits = jnp.where(valid, logits, NEG)
    m = logits.max(-1)
    p = jnp.exp(logits - m[..., None])
    s = p.sum(-1)
    num = jnp.einsum('bhrnqk,bhrnkd->bhrnqd', p, vv)

    def unstream(t):
        t = t.reshape(B, H, dilation, Lp, *t.shape[5:])[:, :, :, :L]
        return jnp.swapaxes(t, 2, 3).reshape(B, H, S, *t.shape[4:])

    return unstream(num), unstream(s), unstream(m)


def _dilated_mixture(qg, kg, v):
    dh = v.shape[-1]
    qg = qg.astype(jnp.float32) * dh ** -0.5
    kg = kg.astype(jnp.float32)
    v = v.astype(jnp.float32)
    parts = [_dilated_band(qg[i], kg[i], v, w, d) for i, (w, d) in enumerate(DILATED_GROUPS)]
    m_max = jnp.max(jnp.stack([pt[2] for pt in parts]), axis=0)
    num, den = None, None
    for pn, ps, pm in parts:
        wgt = jnp.exp(pm - m_max)
        num = pn * wgt[..., None] if num is None else num + pn * wgt[..., None]
        den = ps * wgt if den is None else den + ps * wgt
    return num / den[..., None]


def _forgetting_attention(q, k, v, log_f):
    B, H, S, dh = q.shape
    q = q.astype(jnp.float32) * dh ** -0.5
    k = k.astype(jnp.float32)
    v = v.astype(jnp.float32)
    F = lax.cumsum(log_f, axis=2)
    kpos = jnp.arange(S)
    qoff = jnp.arange(FOX_BLOCK)

    def block(c):
        start = c * FOX_BLOCK
        qc = lax.dynamic_slice_in_dim(q, start, FOX_BLOCK, axis=2)
        Fc = lax.dynamic_slice_in_dim(F, start, FOX_BLOCK, axis=2)
        l = jnp.einsum('bhqd,bhkd->bhqk', qc, k) + Fc[..., :, None] - F[..., None, :]
        l = jnp.where((start + qoff)[:, None] >= kpos[None, :], l, NEG)
        p = jax.nn.softmax(l, axis=-1)
        return jnp.einsum('bhqk,bhkd->bhqd', p, v)

    out = lax.map(block, jnp.arange(S // FOX_BLOCK))
    return jnp.moveaxis(out, 0, 2).reshape(B, H, S, dh)


def _causal_depthwise_conv(x, w):
    C = x.shape[-1]
    return lax.conv_general_dilated(x, w[:, None, :], window_strides=(1,), padding=[(CONV_K - 1, 0)], dimension_numbers=('NWC', 'WIO', 'NWC'), feature_group_count=C)


def _gated_delta_rule(q, k, v, beta, g):
    B, H, S, dk = q.shape
    dv = v.shape[-1]
    C = GDN_CHUNK
    n = S // C
    ch = lambda t: t.reshape(B, H, n, C, *t.shape[3:])
    q, k, v, beta, g = ch(q), ch(k), ch(v), ch(beta), ch(g)
    gc = lax.cumsum(g, axis=3)
    incl = jnp.tril(jnp.ones((C, C), dtype=bool))
    strict = jnp.tril(jnp.ones((C, C), dtype=bool), -1)
    diff = gc[..., :, None] - gc[..., None, :]
    decay = jnp.where(incl, jnp.exp(jnp.where(incl, diff, 0.0)), 0.0)
    kb = k * beta[..., None]
    M = jnp.where(strict, jnp.einsum('bhnid,bhnjd->bhnij', kb, k) * decay, 0.0)
    eye = jnp.eye(C, dtype=q.dtype)
    T = lax.linalg.triangular_solve(M + eye, jnp.broadcast_to(eye, M.shape), left_side=True, lower=True)
    u = T @ (v * beta[..., None])
    w = T @ (kb * jnp.exp(gc)[..., None])
    qk = jnp.einsum('bhnid,bhnjd->bhnij', q, k) * decay
    q_dec = q * jnp.exp(gc)[..., None]
    k_dec = k * jnp.exp(gc[..., -1:] - gc)[..., None]
    g_last = jnp.exp(gc[..., -1])
    xs = (jnp.moveaxis(u, 2, 0), jnp.moveaxis(w, 2, 0), jnp.moveaxis(qk, 2, 0), jnp.moveaxis(q_dec, 2, 0), jnp.moveaxis(k_dec, 2, 0), jnp.moveaxis(g_last, 2, 0))

    def step(state, inp):
        u_i, w_i, qk_i, qd_i, kd_i, gl_i = inp
        v_new = u_i - w_i @ state
        o = qd_i @ state + qk_i @ v_new
        state = state * gl_i[..., None, None] + jnp.swapaxes(kd_i, -1, -2) @ v_new
        return state, o

    state0 = jnp.zeros((B, H, dk, dv), dtype=q.dtype)
    _, o = lax.scan(step, state0, xs)
    return jnp.moveaxis(o, 0, 2).reshape(B, H, S, dv)


def _memory_cross_attention(q, mem_n, w_kv):
    B, S, _ = q.shape
    Mn = mem_n.shape[1]
    k, v = jnp.split(mem_n @ w_kv, 2, axis=-1)
    q = q.reshape(B, S, MEM_HEADS, MEM_HEAD_DIM).astype(jnp.float32) * MEM_HEAD_DIM ** -0.5
    k = k.reshape(B, Mn, MEM_HEADS, MEM_HEAD_DIM).astype(jnp.float32)
    v = v.reshape(B, Mn, MEM_HEADS, MEM_HEAD_DIM).astype(jnp.float32)
    p = jax.nn.softmax(jnp.einsum('bshd,bmhd->bhsm', q, k), axis=-1)
    return jnp.einsum('bhsm,bmhd->bshd', p, v).reshape(B, S, BRANCH_WIDTH)


def _hybrid_layer(x, mem_n, w_in, b_in, conv_w, a_log, dt_bias, gdn_norm_w, w_mem_kv, w_branch, w_out, ln_g, ln_b):
    B, S, _ = x.shape
    f32 = jnp.float32
    h = x @ w_in + b_in
    offs = [int(o) for o in np.cumsum(IN_SPLIT_SIZES)[:-1]]
    (a_qkv, b_qk, b_v, c_qkv, c_f, d_qkv, d_beta, d_decay, e_q, z, merge_logits) = jnp.split(h, offs, axis=-1)

    aq, ak, av = [_split_heads(t, BRANCH_HEADS) for t in jnp.split(a_qkv, 3, axis=-1)]
    o_a = _moba_attention(aq, ak, av)

    bqk = b_qk.reshape(B, S, 2, N_DIL_GROUPS, BRANCH_HEADS, HEAD_DIM).transpose(2, 3, 0, 4, 1, 5)
    o_b = _dilated_mixture(bqk[0], bqk[1], _split_heads(b_v, BRANCH_HEADS))

    cq, ck, cv = [_split_heads(t, BRANCH_HEADS) for t in jnp.split(c_qkv, 3, axis=-1)]
    log_f = jax.nn.log_sigmoid(c_f.astype(f32)).transpose(0, 2, 1)
    o_c = _forgetting_attention(cq, ck, cv, log_f)

    dqkv = jax.nn.silu(_causal_depthwise_conv(d_qkv.astype(f32), conv_w.astype(f32)))
    dq, dk, dv = [_split_heads(t, BRANCH_HEADS) for t in jnp.split(dqkv, 3, axis=-1)]
    dq = _l2norm(dq) * HEAD_DIM ** -0.5
    dk = _l2norm(dk)
    beta = jax.nn.sigmoid(d_beta.astype(f32)).transpose(0, 2, 1)
    g = (-jnp.exp(a_log.astype(f32)) * jax.nn.softplus(d_decay.astype(f32) + dt_bias.astype(f32))).transpose(0, 2, 1)
    o_d = _gated_delta_rule(dq, dk, dv, beta, g)
    o_d = o_d * lax.rsqrt(jnp.square(o_d).mean(-1, keepdims=True) + RMS_EPS) * gdn_norm_w.astype(f32)

    o_e = _memory_cross_attention(e_q, mem_n, w_mem_kv)

    o = jnp.stack([_merge_heads(o_a), _merge_heads(o_b), _merge_heads(o_c), _merge_heads(o_d), o_e], axis=2).astype(x.dtype)
    o = o * jax.nn.silu(z.reshape(B, S, N_BRANCH, BRANCH_WIDTH))
    y = jnp.einsum('bsnc,ncd->bsnd', o, w_branch)
    y = jnp.sum(jax.nn.sigmoid(merge_logits.reshape(B, S, N_BRANCH, D_MODEL)) * y, axis=2)
    y = y @ w_out
    return _layer_norm(DEEPNORM_ALPHA * x + y, ln_g, ln_b)


def setup_inputs(seed: int = 0) -> dict:
    key = jax.random.key(seed)
    ks = jax.random.split(key, 16)
    nrm = jax.random.normal
    x = nrm(ks[0], (BATCH, SEQ, D_MODEL), jnp.float32)
    mem = nrm(ks[1], (BATCH, MEM_LEN, D_MODEL), jnp.float32)
    mem_ln_g = 1.0 + 0.02 * nrm(ks[2], (D_MODEL,), jnp.float32)
    mem_ln_b = 0.02 * nrm(ks[3], (D_MODEL,), jnp.float32)
    w_in = nrm(ks[4], (DEPTH, D_MODEL, D_IN), jnp.float32) * D_MODEL ** -0.5
    b_in = 0.02 * nrm(ks[5], (DEPTH, D_IN), jnp.float32)
    conv_w = nrm(ks[6], (DEPTH, CONV_K, D_QKV), jnp.float32) * CONV_K ** -0.5
    a_log = jnp.log(jax.random.uniform(ks[7], (DEPTH, BRANCH_HEADS), jnp.float32, 1.0, 16.0))
    dt = jnp.exp(jax.random.uniform(ks[8], (DEPTH, BRANCH_HEADS), jnp.float32, np.log(1e-3), np.log(1e-1)))
    dt_bias = dt + jnp.log(-jnp.expm1(-dt))
    gdn_norm_w = 1.0 + 0.02 * nrm(ks[9], (DEPTH, HEAD_DIM), jnp.float32)
    w_mem_kv = nrm(ks[10], (DEPTH, D_MODEL, 2 * BRANCH_WIDTH), jnp.float32) * D_MODEL ** -0.5
    w_branch = nrm(ks[11], (DEPTH, N_BRANCH, BRANCH_WIDTH, D_MODEL), jnp.float32) * (BRANCH_WIDTH ** -0.5 * DEEPNORM_BETA)
    w_out = nrm(ks[12], (DEPTH, D_MODEL, D_MODEL), jnp.float32) * (D_MODEL ** -0.5 * DEEPNORM_BETA)
    ln_g = 1.0 + 0.02 * nrm(ks[13], (DEPTH, D_MODEL), jnp.float32)
    ln_b = 0.02 * nrm(ks[14], (DEPTH, D_MODEL), jnp.float32)
    return {'x': x, 'mem': mem, 'mem_ln_g': mem_ln_g, 'mem_ln_b': mem_ln_b, 'w_in': w_in, 'b_in': b_in, 'conv_w': conv_w, 'a_log': a_log, 'dt_bias': dt_bias, 'gdn_norm_w': gdn_norm_w, 'w_mem_kv': w_mem_kv, 'w_branch': w_branch, 'w_out': w_out, 'ln_g': ln_g, 'ln_b': ln_b}


def reference(x, mem, mem_ln_g, mem_ln_b, w_in, b_in, conv_w, a_log, dt_bias, gdn_norm_w, w_mem_kv, w_branch, w_out, ln_g, ln_b):
    mem_n = _layer_norm(mem, mem_ln_g, mem_ln_b)
    for l in range(DEPTH):
        x = _hybrid_layer(x, mem_n, w_in[l], b_in[l], conv_w[l], a_log[l], dt_bias[l], gdn_norm_w[l], w_mem_kv[l], w_branch[l], w_out[l], ln_g[l], ln_b[l])
    return x
```

```python
import functools

import jax
import jax.numpy as jnp
import numpy as np
from jax import lax
from jax.experimental import pallas as pl
from jax.experimental.pallas import tpu as pltpu

F32 = jnp.float32
BF16 = jnp.bfloat16
HI = lax.Precision.HIGHEST

D_MODEL = 1024
HEAD_DIM = 64
HEADS = 6
WIDTH = HEADS * HEAD_DIM
N_BRANCH = 5
LANES = 128
PAIRS = WIDTH // LANES
MOBA_BLOCK = 256
MOBA_TOPK = 3
DILATIONS = (1, 4, 16)
BAND = 128
GDN_CHUNK = 64
CONV_K = 4
MEM_HEADS = 4
MEM_HEAD_DIM = WIDTH // MEM_HEADS
NEG = -1e30
LN_EPS = 1e-5
RMS_EPS = 1e-6

_SPLIT = (3 * WIDTH, 6 * WIDTH, WIDTH, 3 * WIDTH, HEADS, 3 * WIDTH, HEADS, HEADS, WIDTH, N_BRANCH * WIDTH, N_BRANCH * D_MODEL)
_OFF = tuple(int(v) for v in np.concatenate([[0], np.cumsum(_SPLIT)]))
(O_A, O_BQK, O_BV, O_C, O_CF, O_D, O_DBETA, O_DDECAY, O_E, O_Z, O_MERGE, _) = _OFF

CB_E = 0
CB_AQ, CB_AK, CB_AV = 3, 6, 9
CB_BQ, CB_BK, CB_BV = 12, 21, 30
CB_CQ, CB_CK, CB_CV = 33, 36, 39
HB_COLS = 42 * LANES
HF_COLS = LANES + 3 * WIDTH

VMEM_LIMIT = 56 * 1024 * 1024


def _cparams(sem):
    return pltpu.CompilerParams(dimension_semantics=sem, vmem_limit_bytes=VMEM_LIMIT)


def _dot(a, b):
    return jnp.dot(a, b, preferred_element_type=F32)


def _dot_nt(a, b):
    return lax.dot_general(a, b, (((1,), (1,)), ((), ())), preferred_element_type=F32)


def _dot_hi(a, b):
    return jnp.dot(a, b, preferred_element_type=F32, precision=HI)


def _dot_nt_hi(a, b):
    return lax.dot_general(a, b, (((1,), (1,)), ((), ())), preferred_element_type=F32, precision=HI)


def _iota(shape, dim):
    return lax.broadcasted_iota(jnp.int32, shape, dim)


def _proj_kernel(x_ref, w_ref, b_ref, s_ref, o_ref):
    acc = _dot(x_ref[...], w_ref[...])
    o_ref[...] = ((acc + b_ref[...]) * s_ref[...]).astype(o_ref.dtype)


def _proj(xb, w, b, scale, out_dtype, tm, tn):
    m, k = xb.shape
    n = w.shape[1]
    assert m % tm == 0 and n % tn == 0
    return pl.pallas_call(
        _proj_kernel,
        grid=(m // tm, n // tn),
        in_specs=[
            pl.BlockSpec((tm, k), lambda i, j: (i, 0)),
            pl.BlockSpec((k, tn), lambda i, j: (0, j)),
            pl.BlockSpec((1, tn), lambda i, j: (0, j)),
            pl.BlockSpec((1, tn), lambda i, j: (0, j)),
        ],
        out_specs=pl.BlockSpec((tm, tn), lambda i, j: (i, j)),
        out_shape=jax.ShapeDtypeStruct((m, n), out_dtype),
        compiler_params=_cparams(("parallel", "arbitrary")),
        name="proj",
    )(xb, w, b, scale)


def _memkv_kernel(mem_ref, g_ref, b_ref, w_ref, o_ref):
    x = mem_ref[0]
    mu = jnp.mean(x, axis=-1, keepdims=True)
    xc = x - mu
    var = jnp.mean(xc * xc, axis=-1, keepdims=True)
    y = xc * lax.rsqrt(var + LN_EPS) * g_ref[...] + b_ref[...]
    o_ref[0] = _dot(y.astype(BF16), w_ref[...]).astype(o_ref.dtype)


def _memkv(mem, g, b, w):
    bsz, mlen, d = mem.shape
    n = w.shape[1]
    return pl.pallas_call(
        _memkv_kernel,
        grid=(bsz,),
        in_specs=[
            pl.BlockSpec((1, mlen, d), lambda i: (i, 0, 0)),
            pl.BlockSpec((1, d), lambda i: (0, 0)),
            pl.BlockSpec((1, d), lambda i: (0, 0)),
            pl.BlockSpec((d, n), lambda i: (0, 0)),
        ],
        out_specs=pl.BlockSpec((1, mlen, n), lambda i: (i, 0, 0)),
        out_shape=jax.ShapeDtypeStruct((bsz, mlen, n), BF16),
        compiler_params=_cparams(("parallel",)),
        name="memkv",
    )(mem, g, b, w)


def _fcum_kernel(h_ref, o_ref, *, blk):
    seq = h_ref.shape[1]
    tri = (_iota((blk, blk), 0) >= _iota((blk, blk), 1)).astype(F32)

    def body(i, carry):
        rows = pl.ds(pl.multiple_of(i * blk, blk), blk)
        logf = jax.nn.log_sigmoid(h_ref[0, rows, :])
        c = _dot_hi(tri, logf) + carry
        o_ref[0, :, rows] = c.T[0:8, :]
        return c[blk - 1:blk, :]

    lax.fori_loop(0, seq // blk, body, jnp.zeros((1, LANES), F32))


def _fcum(hf):
    bsz, seq, _ = hf.shape
    return pl.pallas_call(
        functools.partial(_fcum_kernel, blk=LANES),
        grid=(bsz,),
        in_specs=[pl.BlockSpec((1, seq, LANES), lambda b: (b, 0, 0))],
        out_specs=pl.BlockSpec((1, 8, seq), lambda b: (b, 0, 0)),
        out_shape=jax.ShapeDtypeStruct((bsz, 8, seq), F32),
        compiler_params=_cparams(("parallel",)),
        name="fcum",
    )(hf)


def _head_masks(rows):
    lane = _iota((rows, LANES), 1)
    return lane < HEAD_DIM


def _online_update(s, v, m, l, acc):
    m_new = jnp.maximum(m, jnp.max(s, axis=-1, keepdims=True))
    alpha = jnp.exp(m - m_new)
    p = jnp.exp(s - m_new)
    l = alpha * l + jnp.sum(p, axis=-1, keepdims=True)
    acc = alpha * acc + _dot(p.astype(BF16), v)
    return m_new, l, acc


def _fox_kernel(q_ref, k_ref, v_ref, f_ref, o_ref, *, tq):
    i = pl.program_id(2)
    q = q_ref[0]
    hm = _head_masks(tq)
    zero = jnp.zeros_like(q)
    qs = (jnp.where(hm, q, zero), jnp.where(hm, zero, q))
    q0 = pl.multiple_of(i * tq, tq)
    f0 = f_ref[0, 0, :, pl.ds(q0, LANES)][:, 0:1]

    def tile(ks, masked, carry):
        k = k_ref[0, pl.ds(ks, tq), :]
        v = v_ref[0, pl.ds(ks, tq), :]
        fk = f_ref[0, 0, :, pl.ds(ks, tq)] - f0
        out = []
        for h in range(2):
            m, l, acc = carry[h]
            s = _dot_nt(qs[h], k) - fk[h:h + 1, :]
            if masked:
                s = jnp.where(_iota((tq, tq), 0) >= _iota((tq, tq), 1), s, NEG)
            out.append(_online_update(s, v, m, l, acc))
        return tuple(out)

    init = tuple((jnp.full((tq, 1), NEG, F32), jnp.zeros((tq, 1), F32), jnp.zeros((tq, LANES), F32)) for _ in range(2))
    carry = lax.fori_loop(0, i, lambda kt, c: tile(pl.multiple_of(kt * tq, tq), False, c), init)
    carry = tile(q0, True, carry)
    (_, l0, a0), (_, l1, a1) = carry
    o_ref[0] = jnp.where(hm, a0 / l0, a1 / l1)


def _fox(hb, frow, tq=256):
    bsz, seq, _ = hb.shape
    return pl.pallas_call(
        functools.partial(_fox_kernel, tq=tq),
        grid=(bsz, PAIRS, seq // tq),
        in_specs=[
            pl.BlockSpec((1, tq, LANES), lambda b, p, i: (b, i, CB_CQ + p)),
            pl.BlockSpec((1, seq, LANES), lambda b, p, i: (b, 0, CB_CK + p)),
            pl.BlockSpec((1, seq, LANES), lambda b, p, i: (b, 0, CB_CV + p)),
            pl.BlockSpec((1, 1, 2, seq), lambda b, p, i: (b, p, 0, 0)),
        ],
        out_specs=pl.BlockSpec((1, tq, LANES), lambda b, p, i: (b, i, p)),
        out_shape=jax.ShapeDtypeStruct((bsz, seq, WIDTH), F32),
        compiler_params=_cparams(("parallel", "parallel", "arbitrary")),
        name="fox",
    )(hb, hb, hb, frow)


def _moba_kernel(q_ref, k_ref, v_ref, o_ref, kmean_ref, sel_ref, *, nblk):
    tq = MOBA_BLOCK
    i = pl.program_id(2)

    @pl.when(i == 0)
    def _():
        kmean_ref[...] = jnp.zeros_like(kmean_ref)
        for n in range(nblk):
            kb = k_ref[0, n * tq:(n + 1) * tq, :].astype(F32)
            kmean_ref[n:n + 1, :] = jnp.sum(kb, axis=0, keepdims=True) * (1.0 / tq)

    q = q_ref[0]
    hm = _head_masks(tq)
    zero = jnp.zeros_like(q)
    qs = (jnp.where(hm, q, zero), jnp.where(hm, zero, q))
    col = _iota((tq, LANES), 1)
    colf = col.astype(F32)
    valid = col < i
    for h in range(2):
        g = jnp.where(valid, _dot_nt_hi(qs[h].astype(F32), kmean_ref[...]), NEG)
        sel = jnp.zeros((tq, LANES), F32)
        for _ in range(MOBA_TOPK):
            mx = jnp.max(g, axis=-1, keepdims=True)
            first = jnp.min(jnp.where(g == mx, colf, 1e9), axis=-1, keepdims=True)
            pick = colf == first
            sel = jnp.where(jnp.logical_and(pick, valid), 1.0, sel)
            g = jnp.where(pick, -3e38, g)
        sel_ref[h] = sel

    q0 = pl.multiple_of(i * tq, tq)
    k = k_ref[0, pl.ds(q0, tq), :]
    v = v_ref[0, pl.ds(q0, tq), :]
    causal = _iota((tq, tq), 0) >= _iota((tq, tq), 1)
    carry = []
    for h in range(2):
        s = jnp.where(causal, _dot_nt(qs[h], k), NEG)
        m = jnp.max(s, axis=-1, keepdims=True)
        p = jnp.exp(s - m)
        carry.append((m, jnp.sum(p, axis=-1, keepdims=True), _dot(p.astype(BF16), v)))

    def body(j, carry):
        ks = pl.multiple_of(j * tq, tq)
        k = k_ref[0, pl.ds(ks, tq), :]
        v = v_ref[0, pl.ds(ks, tq), :]
        out = []
        for h in range(2):
            m, l, acc = carry[h]
            chosen = jnp.sum(jnp.where(col == j, sel_ref[h], 0.0), axis=-1, keepdims=True) > 0.5
            s = jnp.where(chosen, _dot_nt(qs[h], k), NEG)
            out.append(_online_update(s, v, m, l, acc))
        return tuple(out)

    (_, l0, a0), (_, l1, a1) = lax.fori_loop(0, i, body, tuple(carry))
    o_ref[0] = jnp.where(hm, a0 / l0, a1 / l1)


def _moba(hb):
    bsz, seq, _ = hb.shape
    nblk = seq // MOBA_BLOCK
    assert seq % MOBA_BLOCK == 0 and nblk <= LANES
    return pl.pallas_call(
        functools.partial(_moba_kernel, nblk=nblk),
        grid=(bsz, PAIRS, nblk),
        in_specs=[
            pl.BlockSpec((1, MOBA_BLOCK, LANES), lambda b, p, i: (b, i, CB_AQ + p)),
            pl.BlockSpec((1, seq, LANES), lambda b, p, i: (b, 0, CB_AK + p)),
            pl.BlockSpec((1, seq, LANES), lambda b, p, i: (b, 0, CB_AV + p)),
        ],
        out_specs=pl.BlockSpec((1, MOBA_BLOCK, LANES), lambda b, p, i: (b, i, p)),
        out_shape=jax.ShapeDtypeStruct((bsz, seq, WIDTH), F32),
        scratch_shapes=[pltpu.VMEM((LANES, LANES), F32), pltpu.VMEM((2, MOBA_BLOCK, LANES), F32)],
        compiler_params=_cparams(("parallel", "parallel", "arbitrary")),
        name="moba",
    )(hb, hb, hb)


def _band_kernel(*refs, rows, first, last):
    if first:
        q_ref, kc_ref, kp_ref, vc_ref, vp_ref = refs[:5]
        rest = refs[5:]
    else:
        q_ref, kc_ref, kp_ref, vc_ref, vp_ref, pn_ref, pm_ref, ps_ref = refs[:8]
        rest = refs[8:]
    jt = pl.program_id(3)
    hm = _head_masks(BAND)
    qi = _iota((BAND, 2 * BAND), 0)
    kj = _iota((BAND, 2 * BAND), 1)
    dist = BAND + qi - kj
    band = jnp.logical_and(dist >= 0, dist <= BAND)

    def block(n, kk, vv, valid):
        r0 = pl.multiple_of(n * BAND, BAND)
        q = q_ref[0, pl.ds(r0, BAND), :]
        zero = jnp.zeros_like(q)
        qs = (jnp.where(hm, q, zero), jnp.where(hm, zero, q))
        res = []
        for h in range(2):
            s = jnp.where(valid, _dot_nt(qs[h], kk), NEG)
            m = jnp.max(s, axis=-1, keepdims=True)
            p = jnp.exp(s - m)
            res.append((m, jnp.sum(p, axis=-1, keepdims=True), _dot(p.astype(BF16), vv)))
        (m0, s0, n0), (m1, s1, n1) = res
        m = jnp.where(hm, m0, m1)
        s = jnp.where(hm, s0, s1)
        num = jnp.where(hm, n0, n1)
        if not first:
            pm = pm_ref[0, pl.ds(r0, BAND), :]
            m_new = jnp.maximum(pm, m)
            wp = jnp.exp(pm - m_new)
            wc = jnp.exp(m - m_new)
            num = pn_ref[0, pl.ds(r0, BAND), :] * wp + num * wc
            s = ps_ref[0, pl.ds(r0, BAND), :] * wp + s * wc
            m = m_new
        if last:
            rest[0][0, pl.ds(r0, BAND), :] = num / s
        else:
            rest[0][0, pl.ds(r0, BAND), :] = num
            rest[1][0, pl.ds(r0, BAND), :] = m
            rest[2][0, pl.ds(r0, BAND), :] = s

    kk0 = jnp.concatenate([kp_ref[0], kc_ref[0, 0:BAND, :]], axis=0)
    vv0 = jnp.concatenate([vp_ref[0], vc_ref[0, 0:BAND, :]], axis=0)
    block(0, kk0, vv0, jnp.logical_and(band, jnp.logical_or(kj >= BAND, jt > 0)))

    def body(n, c):
        k0 = pl.multiple_of((n - 1) * BAND, BAND)
        block(n, kc_ref[0, pl.ds(k0, 2 * BAND), :], vc_ref[0, pl.ds(k0, 2 * BAND), :], band)
        return c

    lax.fori_loop(1, rows // BAND, body, 0)


def _band(hb, g, prev):
    bsz, seq, _ = hb.shape
    dil = DILATIONS[g]
    length = seq // dil
    rows = min(length, 2048)
    nrt = length // rows
    tail = rows // BAND
    ncb = HB_COLS // LANES
    first, last = prev is None, g == len(DILATIONS) - 1
    hv = hb.reshape(bsz, length, dil * HB_COLS)

    def cur(cb):
        return pl.BlockSpec((1, rows, LANES), lambda b, p, r, j: (b, j, r * ncb + cb + p))

    def prv(cb):
        return pl.BlockSpec((1, BAND, LANES), lambda b, p, r, j: (b, jnp.maximum(j * tail - 1, 0), r * ncb + cb + p))

    acc_spec = pl.BlockSpec((1, rows, LANES), lambda b, p, r, j: (b, j, r * PAIRS + p))
    acc_shape = jax.ShapeDtypeStruct((bsz, length, dil * WIDTH), F32)
    in_specs = [cur(CB_BQ + 3 * g), cur(CB_BK + 3 * g), prv(CB_BK + 3 * g), cur(CB_BV), prv(CB_BV)]
    args = [hv] * 5
    if not first:
        in_specs += [acc_spec] * 3
        args += [a.reshape(bsz, length, dil * WIDTH) for a in prev]
    out = pl.pallas_call(
        functools.partial(_band_kernel, rows=rows, first=first, last=last),
        grid=(bsz, PAIRS, dil, nrt),
        in_specs=in_specs,
        out_specs=acc_spec if last else [acc_spec] * 3,
        out_shape=acc_shape if last else [acc_shape] * 3,
        compiler_params=_cparams(("parallel", "parallel", "parallel", "arbitrary")),
        name=f"band{g}",
    )(*args)
    if last:
        return out.reshape(bsz, seq, WIDTH)
    return tuple(a.reshape(bsz, seq, WIDTH) for a in out)


def _dilated(hb):
    acc = None
    for g in range(len(DILATIONS)):
        acc = _band(hb, g, acc)
    return acc


def _neumann_inverse(m):
    c = m.shape[0]
    p = -m
    t = jnp.where(_iota((c, c), 0) == _iota((c, c), 1), 1.0, 0.0) + p
    for _ in range(int(np.log2(c)) - 1):
        pb = p.astype(BF16)
        p = _dot(pb, pb)
        t = t + _dot(t.astype(BF16), p.astype(BF16))
    return t


def _gdn_kernel(h_ref, cw_ref, alog_ref, dtb_ref, nw_ref, o_ref,
                xe_ref, q_s, k_s, v_s, b_s, g_s, o_s, st_ref, *, tile):
    c64 = GDN_CHUNK
    t = pl.program_id(1)

    @pl.when(t == 0)
    def _():
        xe_ref[0:8, :] = jnp.zeros((8, 3 * WIDTH), F32)
        st_ref[...] = jnp.zeros_like(st_ref)

    x = h_ref[0, :, LANES:]
    xe_ref[8:8 + tile, :] = x
    y = jnp.zeros((tile, 3 * WIDTH), F32)
    for j in range(CONV_K):
        y = y + cw_ref[j:j + 1, :] * xe_ref[8 - (CONV_K - 1) + j:8 - (CONV_K - 1) + j + tile, :]
    xe_ref[0:8, :] = x[tile - 8:tile, :]
    y = y * jax.nn.sigmoid(y)

    rw = _iota((WIDTH, WIDTH), 0) // HEAD_DIM
    cl = _iota((WIDTH, WIDTH), 1) // HEAD_DIM
    bd = (rw == cl).astype(F32)
    q = y[:, 0:WIDTH]
    k = y[:, WIDTH:2 * WIDTH]
    q_s[...] = q * lax.rsqrt(_dot_hi(q * q, bd) + RMS_EPS) * (HEAD_DIM ** -0.5)
    k_s[...] = k * lax.rsqrt(_dot_hi(k * k, bd) + RMS_EPS)
    v_s[...] = y[:, 2 * WIDTH:]

    hs = h_ref[0, :, 0:LANES]
    er = _iota((LANES, WIDTH), 0)
    ec = _iota((LANES, WIDTH), 1) // HEAD_DIM
    e_beta = (er == ec + HEADS).astype(F32)
    e_dec = (er == ec + 2 * HEADS).astype(F32)
    b_s[...] = jax.nn.sigmoid(_dot_hi(hs, e_beta))
    g_s[...] = -jnp.exp(alog_ref[...]) * jax.nn.softplus(_dot_hi(hs, e_dec) + dtb_ref[...])

    tri = (_iota((c64, c64), 0) >= _iota((c64, c64), 1)).astype(F32)
    ri = _iota((c64, c64), 0)
    ci = _iota((c64, c64), 1)
    incl = ri >= ci
    strict = ri > ci
    sel = (_iota((8, WIDTH), 1) == _iota((8, WIDTH), 0) * HEAD_DIM).astype(F32)
    hm = _head_masks(c64)
    bdiag = (_iota((LANES, LANES), 0) // HEAD_DIM) == (_iota((LANES, LANES), 1) // HEAD_DIM)

    def chunk(c, carry):
        rows = pl.ds(pl.multiple_of(c * c64, c64), c64)
        gc = _dot_hi(tri, g_s[rows, :])
        gct = _dot_nt_hi(sel, gc)
        glast = gc[c64 - 1:c64, :]
        eg = jnp.exp(gc)
        egl = jnp.exp(glast)
        ekd = jnp.exp(glast - gc)
        qc, kc, vc, bc = q_s[rows, :], k_s[rows, :], v_s[rows, :], b_s[rows, :]
        kb = kc * bc
        vb = vc * bc
        wb = kb * eg
        qd = qc * eg
        kd = kc * ekd
        for p in range(PAIRS):
            sl = slice(p * LANES, (p + 1) * LANES)
            kp = kc[:, sl].astype(BF16)
            kbp, qp = kb[:, sl], qc[:, sl]
            rhs = jnp.concatenate([vb[:, sl], wb[:, sl]], axis=1).astype(BF16)
            per_head = []
            for h in range(2):
                hh = 2 * p + h
                keep = hm if h == 0 else jnp.logical_not(hm)
                diff = gc[:, hh * HEAD_DIM:hh * HEAD_DIM + 1] - gct[hh:hh + 1, :]
                decay = jnp.where(incl, jnp.exp(jnp.where(incl, diff, 0.0)), 0.0)
                kk = _dot_nt(jnp.where(keep, kbp, 0.0).astype(BF16), kp)
                mm = jnp.where(strict, kk * decay, 0.0)
                qk = _dot_nt(jnp.where(keep, qp, 0.0).astype(BF16), kp) * decay
                tinv = _neumann_inverse(mm)
                per_head.append((_dot(tinv.astype(BF16), rhs), qk))
            (r0, qk0), (r1, qk1) = per_head
            u = jnp.where(hm, r0[:, :LANES], r1[:, :LANES])
            w = jnp.where(hm, r0[:, LANES:], r1[:, LANES:])
            st = st_ref[p]
            stb = st.astype(BF16)
            v_new = u - _dot(w.astype(BF16), stb)
            vnb = v_new.astype(BF16)
            o = _dot(qd[:, sl].astype(BF16), stb) + jnp.where(hm, _dot(qk0.astype(BF16), vnb), _dot(qk1.astype(BF16), vnb))
            upd = lax.dot_general(kd[:, sl].astype(BF16), vnb, (((0,), (0,)), ((), ())), preferred_element_type=F32)
            st_ref[p] = st * egl[:, sl] + jnp.where(bdiag, upd, 0.0)
            o_s[rows, sl] = o
        return carry

    lax.fori_loop(0, tile // c64, chunk, 0)
    o = o_s[...]
    ms = _dot_hi(o * o, bd) * (1.0 / HEAD_DIM)
    o_ref[0] = o * lax.rsqrt(ms + RMS_EPS) * nw_ref[...]


def _gdn(hf, conv_w, a_log, dt_bias, norm_w, tile=512):
    bsz, seq, _ = hf.shape
    rep = lambda a: jnp.repeat(a.astype(F32), HEAD_DIM)[None, :]
    wide = pltpu.VMEM((tile, WIDTH), F32)
    return pl.pallas_call(
        functools.partial(_gdn_kernel, tile=tile),
        grid=(bsz, seq // tile),
        in_specs=[
            pl.BlockSpec((1, tile, HF_COLS), lambda b, t: (b, t, 0)),
            pl.BlockSpec((CONV_K, 3 * WIDTH), lambda b, t: (0, 0)),
            pl.BlockSpec((1, WIDTH), lambda b, t: (0, 0)),
            pl.BlockSpec((1, WIDTH), lambda b, t: (0, 0)),
            pl.BlockSpec((1, WIDTH), lambda b, t: (0, 0)),
        ],
        out_specs=pl.BlockSpec((1, tile, WIDTH), lambda b, t: (b, t, 0)),
        out_shape=jax.ShapeDtypeStruct((bsz, seq, WIDTH), F32),
        scratch_shapes=[pltpu.VMEM((tile + 8, 3 * WIDTH), F32), wide, wide, wide, wide, wide, wide,
                        pltpu.VMEM((PAIRS, LANES, LANES), F32)],
        compiler_params=_cparams(("parallel", "arbitrary")),
        name="gdn",
    )(hf, conv_w.astype(F32), rep(a_log), rep(dt_bias), jnp.tile(norm_w.astype(F32), HEADS)[None, :])


def _memattn_kernel(q_ref, kv_ref, o_ref):
    tq = q_ref.shape[1]
    q = q_ref[0]
    k = kv_ref[0, :, 0:WIDTH]
    v = kv_ref[0, :, WIDTH:2 * WIDTH]
    head = _iota((tq, WIDTH), 1) // MEM_HEAD_DIM
    zero = jnp.zeros_like(q)
    out = jnp.zeros((tq, WIDTH), F32)
    for h in range(MEM_HEADS):
        s = _dot_nt(jnp.where(head == h, q, zero), k)
        m = jnp.max(s, axis=-1, keepdims=True)
        p = jnp.exp(s - m)
        o = _dot(p.astype(BF16), v) / jnp.sum(p, axis=-1, keepdims=True)
        out = jnp.where(head == h, o, out)
    o_ref[0] = out


def _memattn(hb, kv, layer, tq=512):
    bsz, seq, _ = hb.shape
    mlen = kv.shape[1]
    return pl.pallas_call(
        _memattn_kernel,
        grid=(bsz, seq // tq),
        in_specs=[
            pl.BlockSpec((1, tq, WIDTH), lambda b, i: (b, i, CB_E // PAIRS)),
            pl.BlockSpec((1, mlen, 2 * WIDTH), lambda b, i: (b, 0, layer)),
        ],
        out_specs=pl.BlockSpec((1, tq, WIDTH), lambda b, i: (b, i, 0)),
        out_shape=jax.ShapeDtypeStruct((bsz, seq, WIDTH), F32),
        compiler_params=_cparams(("parallel", "parallel")),
        name="memattn",
    )(hb, kv)


def _out_kernel(oa, ob, oc, od, oe, z_ref, ml_ref, x_ref, wb_ref, wo_ref, g_ref, b_ref, y_ref, *, alpha):
    tm = x_ref.shape[0]
    acc = jnp.zeros((tm, D_MODEL), F32)
    for n, o_ref in enumerate((oa, ob, oc, od, oe)):
        z = z_ref[:, n * WIDTH:(n + 1) * WIDTH]
        gated = o_ref[...] * (z * jax.nn.sigmoid(z))
        yn = _dot(gated.astype(BF16), wb_ref[n])
        acc = acc + jax.nn.sigmoid(ml_ref[:, n * D_MODEL:(n + 1) * D_MODEL]) * yn
    r = alpha * x_ref[...] + _dot(acc.astype(BF16), wo_ref[...])
    mu = jnp.mean(r, axis=-1, keepdims=True)
    rc = r - mu
    var = jnp.mean(rc * rc, axis=-1, keepdims=True)
    y_ref[...] = rc * lax.rsqrt(var + LN_EPS) * g_ref[...] + b_ref[...]


def _out(branches, z, ml, x, w_branch, w_out, ln_g, ln_b, alpha, tm=256):
    m = x.shape[0]
    row = lambda n: pl.BlockSpec((tm, n), lambda i: (i, 0))
    return pl.pallas_call(
        functools.partial(_out_kernel, alpha=alpha),
        grid=(m // tm,),
        in_specs=[row(WIDTH)] * N_BRANCH + [row(N_BRANCH * WIDTH), row(N_BRANCH * D_MODEL), row(D_MODEL),
                  pl.BlockSpec((N_BRANCH, WIDTH, D_MODEL), lambda i: (0, 0, 0)),
                  pl.BlockSpec((D_MODEL, D_MODEL), lambda i: (0, 0)),
                  pl.BlockSpec((1, D_MODEL), lambda i: (0, 0)),
                  pl.BlockSpec((1, D_MODEL), lambda i: (0, 0))],
        out_specs=row(D_MODEL),
        out_shape=jax.ShapeDtypeStruct((m, D_MODEL), F32),
        compiler_params=_cparams(("parallel",)),
        name="out",
    )(*branches, z, ml, x, w_branch, w_out, ln_g, ln_b)


def _split_weights(w_in, b_in):
    cols = lambda o, n: np.arange(o, o + n)
    hb_cols = np.concatenate([cols(O_E, WIDTH), cols(O_A, 3 * WIDTH), cols(O_BQK, 6 * WIDTH), cols(O_BV, WIDTH), cols(O_C, 3 * WIDTH)])
    scale = np.ones((HB_COLS,), np.float32)
    scale[CB_E * LANES:CB_E * LANES + WIDTH] = MEM_HEAD_DIM ** -0.5
    for cb in (CB_AQ, CB_CQ):
        scale[cb * LANES:cb * LANES + WIDTH] = HEAD_DIM ** -0.5
    scale[CB_BQ * LANES:CB_BQ * LANES + 3 * WIDTH] = HEAD_DIM ** -0.5
    hf_cols = np.concatenate([cols(O_CF, HEADS), cols(O_DBETA, HEADS), cols(O_DDECAY, HEADS)])
    pad = LANES - 3 * HEADS
    w_hf = jnp.concatenate([w_in[:, hf_cols], jnp.zeros((D_MODEL, pad), w_in.dtype), w_in[:, O_D:O_D + 3 * WIDTH]], axis=1)
    b_hf = jnp.concatenate([b_in[hf_cols], jnp.zeros((pad,), b_in.dtype), b_in[O_D:O_D + 3 * WIDTH]])
    groups = {
        "hb": (w_in[:, hb_cols], b_in[hb_cols], jnp.asarray(scale)),
        "hf": (w_hf, b_hf, None),
        "z": (w_in[:, O_Z:O_Z + N_BRANCH * WIDTH], b_in[O_Z:O_Z + N_BRANCH * WIDTH], None),
        "ml": (w_in[:, O_MERGE:], b_in[O_MERGE:], None),
    }
    out = {}
    for name, (w, b, s) in groups.items():
        s = jnp.ones((w.shape[1],), F32) if s is None else s
        out[name] = (w.astype(BF16), b.astype(F32)[None, :], s[None, :])
    return out


def _layer(x, kv, layer, w_in, b_in, conv_w, a_log, dt_bias, gdn_norm_w, w_branch, w_out, ln_g, ln_b, alpha):
    bsz, seq, d = x.shape
    m = bsz * seq
    xf = x.reshape(m, d)
    xb = xf.astype(BF16)
    gw = _split_weights(w_in, b_in)
    hb = _proj(xb, *gw["hb"], BF16, 1024, 768).reshape(bsz, seq, HB_COLS)
    hf = _proj(xb, *gw["hf"], F32, 1024, 640).reshape(bsz, seq, HF_COLS)
    z = _proj(xb, *gw["z"], F32, 1024, 640)
    ml = _proj(xb, *gw["ml"], F32, 1024, 1024)

    o_a = _moba(hb)
    o_b = _dilated(hb)
    frow = _fcum(hf)[:, :HEADS, :].reshape(bsz, PAIRS, 2, seq)
    o_c = _fox(hb, frow)
    o_d = _gdn(hf, conv_w, a_log, dt_bias, gdn_norm_w)
    o_e = _memattn(hb, kv, layer)
    branches = [o.reshape(m, WIDTH) for o in (o_a, o_b, o_c, o_d, o_e)]
    y = _out(branches, z, ml, xf, w_branch.astype(BF16), w_out.astype(BF16),
             ln_g.astype(F32)[None, :], ln_b.astype(F32)[None, :], alpha)
    return y.reshape(bsz, seq, d)


def kernel(x, mem, mem_ln_g, mem_ln_b, w_in, b_in, conv_w, a_log, dt_bias, gdn_norm_w, w_mem_kv, w_branch, w_out, ln_g, ln_b):
    depth = w_in.shape[0]
    alpha = float((2 * depth) ** 0.25)
    w_kv = jnp.concatenate([w_mem_kv[l] for l in range(depth)], axis=1).astype(BF16)
    kv = _memkv(mem.astype(F32), mem_ln_g.astype(F32)[None, :], mem_ln_b.astype(F32)[None, :], w_kv)
    x = x.astype(F32)
    for l in range(depth):
        x = _layer(x, kv, l, w_in[l], b_in[l], conv_w[l], a_log[l], dt_bias[l], gdn_norm_w[l],
                   w_branch[l], w_out[l], ln_g[l], ln_b[l], alpha)
    return x
```

```python
import functools

import jax
import jax.numpy as jnp
import numpy as np
from jax import lax
from jax.experimental import pallas as pl
from jax.experimental.pallas import tpu as pltpu

F32 = jnp.float32
BF16 = jnp.bfloat16
HI = lax.Precision.HIGHEST

D_MODEL = 1024
HEAD_DIM = 64
HEADS = 6
WIDTH = HEADS * HEAD_DIM
N_BRANCH = 5
LANES = 128
PAIRS = WIDTH // LANES
MOBA_BLOCK = 256
MOBA_TOPK = 3
DILATIONS = (1, 4, 16)
BAND = 128
GDN_CHUNK = 128
CONV_K = 4
MEM_HEADS = 4
MEM_HEAD_DIM = WIDTH // MEM_HEADS
NEG = -1e30
LN_EPS = 1e-5
RMS_EPS = 1e-6

_SPLIT = (3 * WIDTH, 6 * WIDTH, WIDTH, 3 * WIDTH, HEADS, 3 * WIDTH, HEADS, HEADS, WIDTH, N_BRANCH * WIDTH, N_BRANCH * D_MODEL)
_OFF = tuple(int(v) for v in np.concatenate([[0], np.cumsum(_SPLIT)]))
(O_A, O_BQK, O_BV, O_C, O_CF, O_D, O_DBETA, O_DDECAY, O_E, O_Z, O_MERGE, _) = _OFF

CB_E = 0
CB_AQ, CB_AK, CB_AV = 3, 6, 9
CB_BQ, CB_BK, CB_BV = 12, 21, 30
CB_CQ, CB_CK, CB_CV = 33, 36, 39
HB_COLS = 42 * LANES
HF_COLS = LANES + 3 * WIDTH

VMEM_LIMIT = 56 * 1024 * 1024


def _cparams(sem):
    return pltpu.CompilerParams(dimension_semantics=sem, vmem_limit_bytes=VMEM_LIMIT)


def _dot(a, b):
    return jnp.dot(a, b, preferred_element_type=F32)


def _dot_nt(a, b):
    return lax.dot_general(a, b, (((1,), (1,)), ((), ())), preferred_element_type=F32)


def _dot_hi(a, b):
    return jnp.dot(a, b, preferred_element_type=F32, precision=HI)


def _dot_nt_hi(a, b):
    return lax.dot_general(a, b, (((1,), (1,)), ((), ())), preferred_element_type=F32, precision=HI)


def _iota(shape, dim):
    return lax.broadcasted_iota(jnp.int32, shape, dim)


def _proj_kernel(x_ref, w_ref, b_ref, s_ref, o_ref):
    acc = _dot(x_ref[...], w_ref[...])
    o_ref[...] = ((acc + b_ref[...]) * s_ref[...]).astype(o_ref.dtype)


def _proj(xb, w, b, scale, out_dtype, tm, tn):
    m, k = xb.shape
    n = w.shape[1]
    assert m % tm == 0 and n % tn == 0
    return pl.pallas_call(
        _proj_kernel,
        grid=(m // tm, n // tn),
        in_specs=[
            pl.BlockSpec((tm, k), lambda i, j: (i, 0)),
            pl.BlockSpec((k, tn), lambda i, j: (0, j)),
            pl.BlockSpec((1, tn), lambda i, j: (0, j)),
            pl.BlockSpec((1, tn), lambda i, j: (0, j)),
        ],
        out_specs=pl.BlockSpec((tm, tn), lambda i, j: (i, j)),
        out_shape=jax.ShapeDtypeStruct((m, n), out_dtype),
        compiler_params=_cparams(("parallel", "arbitrary")),
        name="proj",
    )(xb, w, b, scale)


def _memkv_kernel(mem_ref, g_ref, b_ref, w_ref, o_ref):
    x = mem_ref[0]
    mu = jnp.mean(x, axis=-1, keepdims=True)
    xc = x - mu
    var = jnp.mean(xc * xc, axis=-1, keepdims=True)
    y = xc * lax.rsqrt(var + LN_EPS) * g_ref[...] + b_ref[...]
    o_ref[0] = _dot(y.astype(BF16), w_ref[...]).astype(o_ref.dtype)


def _memkv(mem, g, b, w):
    bsz, mlen, d = mem.shape
    n = w.shape[1]
    return pl.pallas_call(
        _memkv_kernel,
        grid=(bsz,),
        in_specs=[
            pl.BlockSpec((1, mlen, d), lambda i: (i, 0, 0)),
            pl.BlockSpec((1, d), lambda i: (0, 0)),
            pl.BlockSpec((1, d), lambda i: (0, 0)),
            pl.BlockSpec((d, n), lambda i: (0, 0)),
        ],
        out_specs=pl.BlockSpec((1, mlen, n), lambda i: (i, 0, 0)),
        out_shape=jax.ShapeDtypeStruct((bsz, mlen, n), BF16),
        compiler_params=_cparams(("parallel",)),
        name="memkv",
    )(mem, g, b, w)


def _fcum_kernel(h_ref, o_ref, *, blk):
    seq = h_ref.shape[1]
    tri = (_iota((blk, blk), 0) >= _iota((blk, blk), 1)).astype(F32)

    def body(i, carry):
        rows = pl.ds(pl.multiple_of(i * blk, blk), blk)
        logf = jax.nn.log_sigmoid(h_ref[0, rows, :])
        c = _dot_hi(tri, logf) + carry
        o_ref[0, rows, :] = c
        return c[blk - 1:blk, :]

    lax.fori_loop(0, seq // blk, body, jnp.zeros((1, LANES), F32))


def _fcum(hf):
    bsz, seq, _ = hf.shape
    return pl.pallas_call(
        functools.partial(_fcum_kernel, blk=LANES),
        grid=(bsz,),
        in_specs=[pl.BlockSpec((1, seq, LANES), lambda b: (b, 0, 0))],
        out_specs=pl.BlockSpec((1, seq, LANES), lambda b: (b, 0, 0)),
        out_shape=jax.ShapeDtypeStruct((bsz, seq, LANES), F32),
        compiler_params=_cparams(("parallel",)),
        name="fcum",
    )(hf)


def _head_masks(rows):
    lane = _iota((rows, LANES), 1)
    return lane < HEAD_DIM


def _split3(x):
    hi = x.astype(BF16).astype(F32)
    r = x - hi
    mid = r.astype(BF16).astype(F32)
    return hi, mid, r - mid


def _flash_step(qx, kx, v, m, l, acc_ref, mask=None):
    s = _dot_nt(qx, kx)
    if mask is not None:
        s = jnp.where(mask, s, NEG)
    m_new = jnp.maximum(m, jnp.max(s, axis=-1, keepdims=True))
    alpha = jnp.exp(m - m_new)
    p = jnp.exp(s - m_new)
    l = alpha * l + jnp.sum(p, axis=-1, keepdims=True)
    acc_ref[...] = alpha * acc_ref[...] + _dot(p.astype(BF16), v)
    return m_new, l


def _flash_causal(qx, kx_ref, v_ref, acc_ref, i, tq):
    acc_ref[...] = jnp.zeros_like(acc_ref)
    m = jnp.full((2 * tq, 1), NEG, F32)
    l = jnp.zeros((2 * tq, 1), F32)

    def body(kt, carry):
        rows = pl.ds(pl.multiple_of(kt * tq, tq), tq)
        return _flash_step(qx, kx_ref[rows, :], v_ref[0, rows, :], *carry, acc_ref)

    m, l = lax.fori_loop(0, i, body, (m, l))
    rows = pl.ds(pl.multiple_of(i * tq, tq), tq)
    r = _iota((2 * tq, tq), 0)
    causal = jnp.where(r >= tq, r - tq, r) >= _iota((2 * tq, tq), 1)
    m, l = _flash_step(qx, kx_ref[rows, :], v_ref[0, rows, :], m, l, acc_ref, causal)
    o = acc_ref[...] / l
    return jnp.where(_head_masks(tq), o[0:tq], o[tq:])


def _fox_kernel(q_ref, k_ref, v_ref, f_ref, o_ref, kx_ref, acc_ref, *, tq):
    p = pl.program_id(1)
    i = pl.program_id(2)
    seq = k_ref.shape[1]
    bt = 512

    @pl.when(i == 0)
    def _():
        src = _iota((LANES, LANES), 0)
        dst = _iota((LANES, LANES), 1)
        pm = jnp.logical_or(jnp.logical_and(src == 2 * p, dst < 3),
                            jnp.logical_and(src == 2 * p + 1, jnp.logical_and(dst >= 3, dst < 6))).astype(F32)
        sub = _iota((bt, LANES), 1) % 3

        def build(c, carry):
            rows = pl.ds(pl.multiple_of(c * bt, bt), bt)
            hi, mid, lo = _split3(_dot_hi(f_ref[0, rows, :], pm))
            kx_ref[rows, 0:LANES] = k_ref[0, rows, :]
            kx_ref[rows, LANES:] = jnp.where(sub == 0, hi, jnp.where(sub == 1, mid, lo)).astype(BF16)
            return carry

        lax.fori_loop(0, seq // bt, build, 0)

    q = q_ref[0].astype(F32)
    lane = _iota((tq, LANES), 1)
    hm = lane < HEAD_DIM
    top = jnp.concatenate([jnp.where(hm, q, 0.0), jnp.where(lane < 3, -1.0, 0.0)], axis=1)
    bot = jnp.concatenate([jnp.where(hm, 0.0, q), jnp.where(jnp.logical_and(lane >= 3, lane < 6), -1.0, 0.0)], axis=1)
    qx = jnp.concatenate([top, bot], axis=0).astype(BF16)
    o_ref[0] = _flash_causal(qx, kx_ref, v_ref, acc_ref, i, tq)


def _fox(hb, fcol, tq=512):
    bsz, seq, _ = hb.shape
    assert seq % tq == 0
    return pl.pallas_call(
        functools.partial(_fox_kernel, tq=tq),
        grid=(bsz, PAIRS, seq // tq),
        in_specs=[
            pl.BlockSpec((1, tq, LANES), lambda b, p, i: (b, i, CB_CQ + p)),
            pl.BlockSpec((1, seq, LANES), lambda b, p, i: (b, 0, CB_CK + p)),
            pl.BlockSpec((1, seq, LANES), lambda b, p, i: (b, 0, CB_CV + p)),
            pl.BlockSpec((1, seq, LANES), lambda b, p, i: (b, 0, 0)),
        ],
        out_specs=pl.BlockSpec((1, tq, LANES), lambda b, p, i: (b, i, p)),
        out_shape=jax.ShapeDtypeStruct((bsz, seq, WIDTH), F32),
        scratch_shapes=[pltpu.VMEM((seq, 2 * LANES), BF16), pltpu.VMEM((2 * tq, LANES), F32)],
        compiler_params=_cparams(("parallel", "parallel", "arbitrary")),
        name="fox",
    )(hb, hb, hb, fcol)


def _moba_kernel(q_ref, k_ref, v_ref, o_ref, kx_ref, kmean_ref, acc_ref, *, tq):
    i = pl.program_id(2)
    seq = k_ref.shape[1]
    blk = MOBA_BLOCK

    @pl.when(i == 0)
    def _():
        kmean_ref[...] = jnp.zeros_like(kmean_ref)
        lane = _iota((blk, LANES), 1)

        def build(n, carry):
            rows = pl.ds(pl.multiple_of(n * blk, blk), blk)
            k = k_ref[0, rows, :]
            kx_ref[rows, 0:LANES] = k
            kx_ref[rows, LANES:] = jnp.where(lane == n, 1.0, 0.0).astype(BF16)
            kmean_ref[pl.ds(n, 1), :] = jnp.sum(k.astype(F32), axis=0, keepdims=True) * (1.0 / blk)
            return carry

        lax.fori_loop(0, seq // blk, build, 0)

    q = q_ref[0].astype(F32)
    lane = _iota((tq, LANES), 1)
    hm = lane < HEAD_DIM
    own = i * (tq // blk) + _iota((tq, LANES), 0) // blk
    colf = lane.astype(F32)
    valid = lane < own
    halves = []
    for h in range(2):
        qh = jnp.where(hm, q, 0.0) if h == 0 else jnp.where(hm, 0.0, q)
        g = jnp.where(valid, _dot_nt_hi(qh, kmean_ref[...]), NEG)
        bias = jnp.where(lane == own, 0.0, NEG)
        for _ in range(MOBA_TOPK):
            mx = jnp.max(g, axis=-1, keepdims=True)
            first = jnp.min(jnp.where(g == mx, colf, 1e9), axis=-1, keepdims=True)
            pick = colf == first
            bias = jnp.where(jnp.logical_and(pick, valid), 0.0, bias)
            g = jnp.where(pick, -3e38, g)
        halves.append(jnp.concatenate([qh, bias], axis=1))
    qx = jnp.concatenate(halves, axis=0).astype(BF16)
    o_ref[0] = _flash_causal(qx, kx_ref, v_ref, acc_ref, i, tq)


def _moba(hb, tq=512):
    bsz, seq, _ = hb.shape
    assert seq % tq == 0 and tq % MOBA_BLOCK == 0 and seq // MOBA_BLOCK <= LANES
    return pl.pallas_call(
        functools.partial(_moba_kernel, tq=tq),
        grid=(bsz, PAIRS, seq // tq),
        in_specs=[
            pl.BlockSpec((1, tq, LANES), lambda b, p, i: (b, i, CB_AQ + p)),
            pl.BlockSpec((1, seq, LANES), lambda b, p, i: (b, 0, CB_AK + p)),
            pl.BlockSpec((1, seq, LANES), lambda b, p, i: (b, 0, CB_AV + p)),
        ],
        out_specs=pl.BlockSpec((1, tq, LANES), lambda b, p, i: (b, i, p)),
        out_shape=jax.ShapeDtypeStruct((bsz, seq, WIDTH), F32),
        scratch_shapes=[pltpu.VMEM((seq, 2 * LANES), BF16), pltpu.VMEM((LANES, LANES), F32),
                        pltpu.VMEM((2 * tq, LANES), F32)],
        compiler_params=_cparams(("parallel", "parallel", "arbitrary")),
        name="moba",
    )(hb, hb, hb)


def _band_kernel(*refs, rows, first, last):
    if first:
        q_ref, kc_ref, kp_ref, vc_ref, vp_ref = refs[:5]
        rest = refs[5:]
    else:
        q_ref, kc_ref, kp_ref, vc_ref, vp_ref, pn_ref, pm_ref, ps_ref = refs[:8]
        rest = refs[8:]
    jt = pl.program_id(3)
    hm = _head_masks(BAND)
    qi = _iota((BAND, 2 * BAND), 0)
    kj = _iota((BAND, 2 * BAND), 1)
    dist = BAND + qi - kj
    band = jnp.logical_and(dist >= 0, dist <= BAND)

    def block(n, kk, vv, valid):
        r0 = pl.multiple_of(n * BAND, BAND)
        q = q_ref[0, pl.ds(r0, BAND), :]
        zero = jnp.zeros_like(q)
        qs = (jnp.where(hm, q, zero), jnp.where(hm, zero, q))
        res = []
        for h in range(2):
            s = jnp.where(valid, _dot_nt(qs[h], kk), NEG)
            m = jnp.max(s, axis=-1, keepdims=True)
            p = jnp.exp(s - m)
            res.append((m, jnp.sum(p, axis=-1, keepdims=True), _dot(p.astype(BF16), vv)))
        (m0, s0, n0), (m1, s1, n1) = res
        m = jnp.where(hm, m0, m1)
        s = jnp.where(hm, s0, s1)
        num = jnp.where(hm, n0, n1)
        if not first:
            pm = pm_ref[0, pl.ds(r0, BAND), :]
            m_new = jnp.maximum(pm, m)
            wp = jnp.exp(pm - m_new)
            wc = jnp.exp(m - m_new)
            num = pn_ref[0, pl.ds(r0, BAND), :] * wp + num * wc
            s = ps_ref[0, pl.ds(r0, BAND), :] * wp + s * wc
            m = m_new
        if last:
            rest[0][0, pl.ds(r0, BAND), :] = num / s
        else:
            rest[0][0, pl.ds(r0, BAND), :] = num
            rest[1][0, pl.ds(r0, BAND), :] = m
            rest[2][0, pl.ds(r0, BAND), :] = s

    kk0 = jnp.concatenate([kp_ref[0], kc_ref[0, 0:BAND, :]], axis=0)
    vv0 = jnp.concatenate([vp_ref[0], vc_ref[0, 0:BAND, :]], axis=0)
    block(0, kk0, vv0, jnp.logical_and(band, jnp.logical_or(kj >= BAND, jt > 0)))

    def body(n, c):
        k0 = pl.multiple_of((n - 1) * BAND, BAND)
        block(n, kc_ref[0, pl.ds(k0, 2 * BAND), :], vc_ref[0, pl.ds(k0, 2 * BAND), :], band)
        return c

    lax.fori_loop(1, rows // BAND, body, 0)


def _band(hb, g, prev):
    bsz, seq, _ = hb.shape
    dil = DILATIONS[g]
    length = seq // dil
    rows = min(length, 2048)
    nrt = length // rows
    tail = rows // BAND
    ncb = HB_COLS // LANES
    first, last = prev is None, g == len(DILATIONS) - 1
    hv = hb.reshape(bsz, length, dil * HB_COLS)

    def cur(cb):
        return pl.BlockSpec((1, rows, LANES), lambda b, p, r, j: (b, j, r * ncb + cb + p))

    def prv(cb):
        return pl.BlockSpec((1, BAND, LANES), lambda b, p, r, j: (b, jnp.maximum(j * tail - 1, 0), r * ncb + cb + p))

    acc_spec = pl.BlockSpec((1, rows, LANES), lambda b, p, r, j: (b, j, r * PAIRS + p))
    acc_shape = jax.ShapeDtypeStruct((bsz, length, dil * WIDTH), F32)
    in_specs = [cur(CB_BQ + 3 * g), cur(CB_BK + 3 * g), prv(CB_BK + 3 * g), cur(CB_BV), prv(CB_BV)]
    args = [hv] * 5
    if not first:
        in_specs += [acc_spec] * 3
        args += [a.reshape(bsz, length, dil * WIDTH) for a in prev]
    out = pl.pallas_call(
        functools.partial(_band_kernel, rows=rows, first=first, last=last),
        grid=(bsz, PAIRS, dil, nrt),
        in_specs=in_specs,
        out_specs=acc_spec if last else [acc_spec] * 3,
        out_shape=acc_shape if last else [acc_shape] * 3,
        compiler_params=_cparams(("parallel", "parallel", "parallel", "arbitrary")),
        name=f"band{g}",
    )(*args)
    if last:
        return out.reshape(bsz, seq, WIDTH)
    return tuple(a.reshape(bsz, seq, WIDTH) for a in out)


def _dilated(hb):
    acc = None
    for g in range(len(DILATIONS)):
        acc = _band(hb, g, acc)
    return acc


def _dot01(a, b, nt=False, pieces=3):
    f = _dot_nt if nt else _dot
    if a.dtype == BF16:
        return sum(f(a, piece.astype(BF16)) for piece in _split3(b)[:pieces])
    return sum(f(piece.astype(BF16), b) for piece in _split3(a)[:pieces])


def _tri_inverse_all(ms):
    c = ms[0].shape[0]
    ri = _iota((c, c), 0)
    ci = _iota((c, c), 1)
    base = 16
    inblk = ri // base == ci // base
    eye = jnp.where(ri == ci, 1.0, 0.0)
    ps = [jnp.where(inblk, -m, 0.0) for m in ms]
    ts = [eye + p for p in ps]
    for _ in range(3):
        pbs = [p.astype(BF16) for p in ps]
        ps = [_dot(pb, pb) for pb in pbs]
        ts = [t + _dot(t.astype(BF16), p.astype(BF16)) for t, p in zip(ts, ps)]
    size = base
    while size < c:
        lower = jnp.logical_and(ri // (2 * size) == ci // (2 * size), ri // size != ci // size)
        tbs = [t.astype(BF16) for t in ts]
        xs = [_dot(tb, jnp.where(lower, m, 0.0).astype(BF16)) for tb, m in zip(tbs, ms)]
        ts = [t - _dot(x.astype(BF16), tb) for t, x, tb in zip(ts, xs, tbs)]
        size *= 2
    return ts


def _gdn_kernel(h_ref, cw_ref, alog_ref, dtb_ref, nw_ref, o_ref,
                xe_ref, u_s, w_s, qd_s, kd_s, qk_s, egl_s, o_s, st_ref, *, tile):
    cc = GDN_CHUNK
    nc = tile // cc
    t = pl.program_id(1)

    @pl.when(t == 0)
    def _():
        xe_ref[0:8, :] = jnp.zeros((8, 3 * WIDTH), F32)
        st_ref[...] = jnp.zeros_like(st_ref)

    x = h_ref[0, :, LANES:]
    xe_ref[8:8 + tile, :] = x
    y = jnp.zeros((tile, 3 * WIDTH), F32)
    for j in range(CONV_K):
        y = y + cw_ref[j:j + 1, :] * xe_ref[8 - (CONV_K - 1) + j:8 - (CONV_K - 1) + j + tile, :]
    xe_ref[0:8, :] = x[tile - 8:tile, :]
    y = y * jax.nn.sigmoid(y)

    bd = ((_iota((WIDTH, WIDTH), 0) // HEAD_DIM) == (_iota((WIDTH, WIDTH), 1) // HEAD_DIM)).astype(BF16)
    q = y[:, 0:WIDTH]
    k = y[:, WIDTH:2 * WIDTH]
    v = y[:, 2 * WIDTH:]
    q = q * lax.rsqrt(_dot01(q * q, bd, pieces=2) + RMS_EPS) * (HEAD_DIM ** -0.5)
    k = k * lax.rsqrt(_dot01(k * k, bd, pieces=2) + RMS_EPS)

    hs = h_ref[0, :, 0:LANES]
    er = _iota((LANES, WIDTH), 0)
    ec = _iota((LANES, WIDTH), 1) // HEAD_DIM
    beta = jax.nn.sigmoid(_dot01(hs, (er == ec + HEADS).astype(BF16)))
    g = -jnp.exp(alog_ref[...]) * jax.nn.softplus(_dot01(hs, (er == ec + 2 * HEADS).astype(BF16)) + dtb_ref[...])

    tr = _iota((tile, tile), 0)
    tc = _iota((tile, tile), 1)
    tri = jnp.logical_and(tr // cc == tc // cc, tr >= tc).astype(BF16)
    gc = _dot01(tri, g)
    glast = jnp.broadcast_to(gc.reshape(nc, cc, WIDTH)[:, cc - 1:cc, :], (nc, cc, WIDTH)).reshape(tile, WIDTH)
    eg = jnp.exp(gc)
    kb = k * beta
    vb = v * beta
    wb = kb * eg
    qd_s[...] = q * eg
    kd_s[...] = k * jnp.exp(glast - gc)
    egl_s[...] = jnp.exp(glast)

    ri = _iota((cc, cc), 0)
    ci = _iota((cc, cc), 1)
    incl = ri >= ci
    strict = ri > ci
    lane = _iota((cc, LANES), 1)
    hm = lane < HEAD_DIM
    sls = [slice(p * LANES, (p + 1) * LANES) for p in range(PAIRS)]
    keeps = [hm, jnp.logical_not(hm)]
    heads = [(p, h) for p in range(PAIRS) for h in range(2)]
    chains = [(p, h, c) for p, h in heads for c in range(nc)]
    rws = [slice(c * cc, (c + 1) * cc) for c in range(nc)]
    kps = [k[:, sl].astype(BF16) for sl in sls]
    rhss = [jnp.concatenate([vb[:, sl], wb[:, sl]], axis=1).astype(BF16) for sl in sls]
    rowf = {(p, h): _dot01((lane == h * HEAD_DIM).astype(BF16), gc[:, sls[p]], nt=True) for p, h in heads}
    kbm = {(p, h): jnp.where(jnp.tile(keeps[h], (nc, 1)), kb[:, sls[p]], 0.0).astype(BF16) for p, h in heads}
    qm = {(p, h): jnp.where(jnp.tile(keeps[h], (nc, 1)), q[:, sls[p]], 0.0).astype(BF16) for p, h in heads}
    kks = [_dot_nt(kbm[p, h][rws[c]], kps[p][rws[c]]) for p, h, c in chains]
    qks = [_dot_nt(qm[p, h][rws[c]], kps[p][rws[c]]) for p, h, c in chains]
    mms = []
    for (p, h, c), kk, qk in zip(chains, kks, qks):
        col = p * LANES + h * HEAD_DIM
        diff = gc[rws[c], col:col + 1] - rowf[p, h][:, rws[c]]
        decay = jnp.where(incl, jnp.exp(jnp.where(incl, diff, 0.0)), 0.0)
        mms.append(jnp.where(strict, kk * decay, 0.0))
        qk_s[2 * p + h, rws[c], :] = qk * decay
    tinvs = _tri_inverse_all(mms)
    rs = {ch: _dot(tinv.astype(BF16), rhss[ch[0]][rws[ch[2]]]) for ch, tinv in zip(chains, tinvs)}
    for p in range(PAIRS):
        for c in range(nc):
            r0, r1 = rs[p, 0, c], rs[p, 1, c]
            u_s[rws[c], sls[p]] = jnp.where(hm, r0[:, :LANES], r1[:, :LANES])
            w_s[rws[c], sls[p]] = jnp.where(hm, r0[:, LANES:], r1[:, LANES:])

    bdiag = (_iota((LANES, LANES), 0) // HEAD_DIM) == (_iota((LANES, LANES), 1) // HEAD_DIM)
    for c in range(nc):
        rows = rws[c]
        sts = [st_ref[p] for p in range(PAIRS)]
        stbs = [st.astype(BF16) for st in sts]
        wss = [_dot(w_s[rows, sl].astype(BF16), stb) for sl, stb in zip(sls, stbs)]
        qss = [_dot(qd_s[rows, sl].astype(BF16), stb) for sl, stb in zip(sls, stbs)]
        vnbs = [(u_s[rows, sl] - ws).astype(BF16) for sl, ws in zip(sls, wss)]
        upds = [lax.dot_general(kd_s[rows, sl].astype(BF16), vnb, (((0,), (0,)), ((), ())), preferred_element_type=F32)
                for sl, vnb in zip(sls, vnbs)]
        intra = [(_dot(qk_s[2 * p, rows, :].astype(BF16), vnbs[p]), _dot(qk_s[2 * p + 1, rows, :].astype(BF16), vnbs[p]))
                 for p in range(PAIRS)]
        for p in range(PAIRS):
            st_ref[p] = sts[p] * egl_s[c * cc:c * cc + 1, sls[p]] + jnp.where(bdiag, upds[p], 0.0)
            o_s[rows, sls[p]] = qss[p] + jnp.where(hm, intra[p][0], intra[p][1])

    o = o_s[...]
    ms = _dot01(o * o, bd, pieces=2) * (1.0 / HEAD_DIM)
    o_ref[0] = o * lax.rsqrt(ms + RMS_EPS) * nw_ref[...]


def _gdn(hf, conv_w, a_log, dt_bias, norm_w, tile=512):
    bsz, seq, _ = hf.shape
    rep = lambda a: jnp.repeat(a.astype(F32), HEAD_DIM)[None, :]
    wide = pltpu.VMEM((tile, WIDTH), F32)
    return pl.pallas_call(
        functools.partial(_gdn_kernel, tile=tile),
        grid=(bsz, seq // tile),
        in_specs=[
            pl.BlockSpec((1, tile, HF_COLS), lambda b, t: (b, t, 0)),
            pl.BlockSpec((CONV_K, 3 * WIDTH), lambda b, t: (0, 0)),
            pl.BlockSpec((1, WIDTH), lambda b, t: (0, 0)),
            pl.BlockSpec((1, WIDTH), lambda b, t: (0, 0)),
            pl.BlockSpec((1, WIDTH), lambda b, t: (0, 0)),
        ],
        out_specs=pl.BlockSpec((1, tile, WIDTH), lambda b, t: (b, t, 0)),
        out_shape=jax.ShapeDtypeStruct((bsz, seq, WIDTH), F32),
        scratch_shapes=[pltpu.VMEM((tile + 8, 3 * WIDTH), F32), wide, wide, wide, wide,
                        pltpu.VMEM((HEADS, tile, LANES), F32), wide, wide,
                        pltpu.VMEM((PAIRS, LANES, LANES), F32)],
        compiler_params=_cparams(("parallel", "arbitrary")),
        name="gdn",
    )(hf, conv_w.astype(F32), rep(a_log), rep(dt_bias), jnp.tile(norm_w.astype(F32), HEADS)[None, :])


def _memattn_kernel(q_ref, kv_ref, o_ref):
    tq = q_ref.shape[1]
    q = q_ref[0]
    k = kv_ref[0, :, 0:WIDTH]
    v = kv_ref[0, :, WIDTH:2 * WIDTH]
    head = _iota((tq, WIDTH), 1) // MEM_HEAD_DIM
    zero = jnp.zeros_like(q)
    out = jnp.zeros((tq, WIDTH), F32)
    for h in range(MEM_HEADS):
        s = _dot_nt(jnp.where(head == h, q, zero), k)
        m = jnp.max(s, axis=-1, keepdims=True)
        p = jnp.exp(s - m)
        o = _dot(p.astype(BF16), v) / jnp.sum(p, axis=-1, keepdims=True)
        out = jnp.where(head == h, o, out)
    o_ref[0] = out


def _memattn(hb, kv, layer, tq=512):
    bsz, seq, _ = hb.shape
    mlen = kv.shape[1]
    return pl.pallas_call(
        _memattn_kernel,
        grid=(bsz, seq // tq),
        in_specs=[
            pl.BlockSpec((1, tq, WIDTH), lambda b, i: (b, i, CB_E // PAIRS)),
            pl.BlockSpec((1, mlen, 2 * WIDTH), lambda b, i: (b, 0, layer)),
        ],
        out_specs=pl.BlockSpec((1, tq, WIDTH), lambda b, i: (b, i, 0)),
        out_shape=jax.ShapeDtypeStruct((bsz, seq, WIDTH), F32),
        compiler_params=_cparams(("parallel", "parallel")),
        name="memattn",
    )(hb, kv)


def _out_kernel(oa, ob, oc, od, oe, z_ref, ml_ref, x_ref, wb_ref, wo_ref, g_ref, b_ref, y_ref, *, alpha):
    tm = x_ref.shape[0]
    acc = jnp.zeros((tm, D_MODEL), F32)
    for n, o_ref in enumerate((oa, ob, oc, od, oe)):
        z = z_ref[:, n * WIDTH:(n + 1) * WIDTH]
        gated = o_ref[...] * (z * jax.nn.sigmoid(z))
        yn = _dot(gated.astype(BF16), wb_ref[n])
        acc = acc + jax.nn.sigmoid(ml_ref[:, n * D_MODEL:(n + 1) * D_MODEL]) * yn
    r = alpha * x_ref[...] + _dot(acc.astype(BF16), wo_ref[...])
    mu = jnp.mean(r, axis=-1, keepdims=True)
    rc = r - mu
    var = jnp.mean(rc * rc, axis=-1, keepdims=True)
    y_ref[...] = rc * lax.rsqrt(var + LN_EPS) * g_ref[...] + b_ref[...]


def _out(branches, z, ml, x, w_branch, w_out, ln_g, ln_b, alpha, tm=256):
    m = x.shape[0]
    row = lambda n: pl.BlockSpec((tm, n), lambda i: (i, 0))
    return pl.pallas_call(
        functools.partial(_out_kernel, alpha=alpha),
        grid=(m // tm,),
        in_specs=[row(WIDTH)] * N_BRANCH + [row(N_BRANCH * WIDTH), row(N_BRANCH * D_MODEL), row(D_MODEL),
                  pl.BlockSpec((N_BRANCH, WIDTH, D_MODEL), lambda i: (0, 0, 0)),
                  pl.BlockSpec((D_MODEL, D_MODEL), lambda i: (0, 0)),
                  pl.BlockSpec((1, D_MODEL), lambda i: (0, 0)),
                  pl.BlockSpec((1, D_MODEL), lambda i: (0, 0))],
        out_specs=row(D_MODEL),
        out_shape=jax.ShapeDtypeStruct((m, D_MODEL), F32),
        compiler_params=_cparams(("parallel",)),
        name="out",
    )(*branches, z, ml, x, w_branch, w_out, ln_g, ln_b)


def _split_weights(w_in, b_in):
    cols = lambda o, n: np.arange(o, o + n)
    hb_cols = np.concatenate([cols(O_E, WIDTH), cols(O_A, 3 * WIDTH), cols(O_BQK, 6 * WIDTH), cols(O_BV, WIDTH), cols(O_C, 3 * WIDTH)])
    scale = np.ones((HB_COLS,), np.float32)
    scale[CB_E * LANES:CB_E * LANES + WIDTH] = MEM_HEAD_DIM ** -0.5
    for cb in (CB_AQ, CB_CQ):
        scale[cb * LANES:cb * LANES + WIDTH] = HEAD_DIM ** -0.5
    scale[CB_BQ * LANES:CB_BQ * LANES + 3 * WIDTH] = HEAD_DIM ** -0.5
    hf_cols = np.concatenate([cols(O_CF, HEADS), cols(O_DBETA, HEADS), cols(O_DDECAY, HEADS)])
    pad = LANES - 3 * HEADS
    w_hf = jnp.concatenate([w_in[:, hf_cols], jnp.zeros((D_MODEL, pad), w_in.dtype), w_in[:, O_D:O_D + 3 * WIDTH]], axis=1)
    b_hf = jnp.concatenate([b_in[hf_cols], jnp.zeros((pad,), b_in.dtype), b_in[O_D:O_D + 3 * WIDTH]])
    groups = {
        "hb": (w_in[:, hb_cols], b_in[hb_cols], jnp.asarray(scale)),
        "hf": (w_hf, b_hf, None),
        "z": (w_in[:, O_Z:O_Z + N_BRANCH * WIDTH], b_in[O_Z:O_Z + N_BRANCH * WIDTH], None),
        "ml": (w_in[:, O_MERGE:], b_in[O_MERGE:], None),
    }
    out = {}
    for name, (w, b, s) in groups.items():
        s = jnp.ones((w.shape[1],), F32) if s is None else s
        out[name] = (w.astype(BF16), b.astype(F32)[None, :], s[None, :])
    return out


def _layer(x, kv, layer, w_in, b_in, conv_w, a_log, dt_bias, gdn_norm_w, w_branch, w_out, ln_g, ln_b, alpha):
    bsz, seq, d = x.shape
    m = bsz * seq
    xf = x.reshape(m, d)
    xb = xf.astype(BF16)
    gw = _split_weights(w_in, b_in)
    hb = _proj(xb, *gw["hb"], BF16, 1024, 768).reshape(bsz, seq, HB_COLS)
    hf = _proj(xb, *gw["hf"], F32, 1024, 640).reshape(bsz, seq, HF_COLS)
    z = _proj(xb, *gw["z"], F32, 1024, 640)
    ml = _proj(xb, *gw["ml"], F32, 1024, 1024)

    o_a = _moba(hb)
    o_b = _dilated(hb)
    o_c = _fox(hb, _fcum(hf))
    o_d = _gdn(hf, conv_w, a_log, dt_bias, gdn_norm_w)
    o_e = _memattn(hb, kv, layer)
    branches = [o.reshape(m, WIDTH) for o in (o_a, o_b, o_c, o_d, o_e)]
    y = _out(branches, z, ml, xf, w_branch.astype(BF16), w_out.astype(BF16),
             ln_g.astype(F32)[None, :], ln_b.astype(F32)[None, :], alpha)
    return y.reshape(bsz, seq, d)


def kernel(x, mem, mem_ln_g, mem_ln_b, w_in, b_in, conv_w, a_log, dt_bias, gdn_norm_w, w_mem_kv, w_branch, w_out, ln_g, ln_b):
    depth = w_in.shape[0]
    alpha = float((2 * depth) ** 0.25)
    w_kv = jnp.concatenate([w_mem_kv[l] for l in range(depth)], axis=1).astype(BF16)
    kv = _memkv(mem.astype(F32), mem_ln_g.astype(F32)[None, :], mem_ln_b.astype(F32)[None, :], w_kv)
    x = x.astype(F32)
    for l in range(depth):
        x = _layer(x, kv, l, w_in[l], b_in[l], conv_w[l], a_log[l], dt_bias[l], gdn_norm_w[l],
                   w_branch[l], w_out[l], ln_g[l], ln_b[l], alpha)
    return x
```

```python
import functools

import jax
import jax.numpy as jnp
import numpy as np
from jax import lax
from jax.experimental import pallas as pl
from jax.experimental.pallas import tpu as pltpu

F32 = jnp.float32
BF16 = jnp.bfloat16
HI = lax.Precision.HIGHEST

D_MODEL = 1024
HEAD_DIM = 64
HEADS = 6
WIDTH = HEADS * HEAD_DIM
N_BRANCH = 5
LANES = 128
PAIRS = WIDTH // LANES
MOBA_BLOCK = 256
MOBA_TOPK = 3
DILATIONS = (1, 4, 16)
BAND = 128
BAND_TILE = 2048
BAND_GROUP = 4
GDN_CHUNK = 128
CONV_K = 4
MEM_HEADS = 4
MEM_HEAD_DIM = WIDTH // MEM_HEADS
NEG = -1e30
LN_EPS = 1e-5
RMS_EPS = 1e-6

_SPLIT = (3 * WIDTH, 6 * WIDTH, WIDTH, 3 * WIDTH, HEADS, 3 * WIDTH, HEADS, HEADS, WIDTH, N_BRANCH * WIDTH, N_BRANCH * D_MODEL)
_OFF = tuple(int(v) for v in np.concatenate([[0], np.cumsum(_SPLIT)]))
(O_A, O_BQK, O_BV, O_C, O_CF, O_D, O_DBETA, O_DDECAY, O_E, O_Z, O_MERGE, _) = _OFF

CB_E = 0
CB_AQ, CB_AK, CB_AV = 3, 6, 9
CB_BQ, CB_BK, CB_BV = 12, 21, 30
CB_CQ, CB_CK, CB_CV = 33, 36, 39
HB_COLS = 42 * LANES
HF_COLS = LANES + 3 * WIDTH

VMEM_LIMIT = 56 * 1024 * 1024


def _cparams(sem):
    return pltpu.CompilerParams(dimension_semantics=sem, vmem_limit_bytes=VMEM_LIMIT)


def _dot(a, b):
    return jnp.dot(a, b, preferred_element_type=F32)


def _dot_nt(a, b):
    return lax.dot_general(a, b, (((1,), (1,)), ((), ())), preferred_element_type=F32)


def _dot_hi(a, b):
    return jnp.dot(a, b, preferred_element_type=F32, precision=HI)


def _dot_nt_hi(a, b):
    return lax.dot_general(a, b, (((1,), (1,)), ((), ())), preferred_element_type=F32, precision=HI)


def _iota(shape, dim):
    return lax.broadcasted_iota(jnp.int32, shape, dim)


def _proj_kernel(x_ref, w_ref, b_ref, s_ref, o_ref):
    acc = _dot(x_ref[...], w_ref[...])
    o_ref[...] = ((acc + b_ref[...]) * s_ref[...]).astype(o_ref.dtype)


def _proj(xb, w, b, scale, out_dtype, tm, tn):
    m, k = xb.shape
    n = w.shape[1]
    assert m % tm == 0 and n % tn == 0
    return pl.pallas_call(
        _proj_kernel,
        grid=(m // tm, n // tn),
        in_specs=[
            pl.BlockSpec((tm, k), lambda i, j: (i, 0)),
            pl.BlockSpec((k, tn), lambda i, j: (0, j)),
            pl.BlockSpec((1, tn), lambda i, j: (0, j)),
            pl.BlockSpec((1, tn), lambda i, j: (0, j)),
        ],
        out_specs=pl.BlockSpec((tm, tn), lambda i, j: (i, j)),
        out_shape=jax.ShapeDtypeStruct((m, n), out_dtype),
        compiler_params=_cparams(("parallel", "arbitrary")),
        name="proj",
    )(xb, w, b, scale)


def _memkv_kernel(mem_ref, g_ref, b_ref, w_ref, o_ref):
    x = mem_ref[0]
    mu = jnp.mean(x, axis=-1, keepdims=True)
    xc = x - mu
    var = jnp.mean(xc * xc, axis=-1, keepdims=True)
    y = xc * lax.rsqrt(var + LN_EPS) * g_ref[...] + b_ref[...]
    o_ref[0] = _dot(y.astype(BF16), w_ref[...]).astype(o_ref.dtype)


def _memkv(mem, g, b, w):
    bsz, mlen, d = mem.shape
    n = w.shape[1]
    return pl.pallas_call(
        _memkv_kernel,
        grid=(bsz,),
        in_specs=[
            pl.BlockSpec((1, mlen, d), lambda i: (i, 0, 0)),
            pl.BlockSpec((1, d), lambda i: (0, 0)),
            pl.BlockSpec((1, d), lambda i: (0, 0)),
            pl.BlockSpec((d, n), lambda i: (0, 0)),
        ],
        out_specs=pl.BlockSpec((1, mlen, n), lambda i: (i, 0, 0)),
        out_shape=jax.ShapeDtypeStruct((bsz, mlen, n), BF16),
        compiler_params=_cparams(("parallel",)),
        name="memkv",
    )(mem, g, b, w)


def _fcum_kernel(h_ref, o_ref, *, blk):
    seq = h_ref.shape[1]
    tri = (_iota((blk, blk), 0) >= _iota((blk, blk), 1)).astype(F32)

    def body(i, carry):
        rows = pl.ds(pl.multiple_of(i * blk, blk), blk)
        logf = jax.nn.log_sigmoid(h_ref[0, rows, :])
        c = _dot_hi(tri, logf) + carry
        o_ref[0, rows, :] = c
        return c[blk - 1:blk, :]

    lax.fori_loop(0, seq // blk, body, jnp.zeros((1, LANES), F32))


def _fcum(hf):
    bsz, seq, _ = hf.shape
    return pl.pallas_call(
        functools.partial(_fcum_kernel, blk=LANES),
        grid=(bsz,),
        in_specs=[pl.BlockSpec((1, seq, LANES), lambda b: (b, 0, 0))],
        out_specs=pl.BlockSpec((1, seq, LANES), lambda b: (b, 0, 0)),
        out_shape=jax.ShapeDtypeStruct((bsz, seq, LANES), F32),
        compiler_params=_cparams(("parallel",)),
        name="fcum",
    )(hf)


def _head_masks(rows):
    lane = _iota((rows, LANES), 1)
    return lane < HEAD_DIM


def _split3(x):
    hi = x.astype(BF16).astype(F32)
    r = x - hi
    mid = r.astype(BF16).astype(F32)
    return hi, mid, r - mid


def _flash_scratch(tq):
    return [pltpu.VMEM((2 * tq, LANES), F32), pltpu.VMEM((2 * tq, tq), F32), pltpu.VMEM((2 * tq, tq), F32),
            pltpu.VMEM((2 * tq, 1), F32), pltpu.VMEM((2 * tq, 1), F32)]


def _flash_causal(qx, kx_ref, v_ref, scratch, i, tq):
    acc_ref, sa_ref, sb_ref, m_ref, l_ref = scratch

    def logits(kt, dst):
        rows = pl.ds(pl.multiple_of(kt * tq, tq), tq)
        dst[...] = _dot_nt(qx, kx_ref[rows, :])

    r = _iota((2 * tq, tq), 0)
    causal = jnp.where(r >= tq, r - tq, r) >= _iota((2 * tq, tq), 1)

    def finish(src, kt, masked=False):
        rows = pl.ds(pl.multiple_of(kt * tq, tq), tq)
        s = src[...]
        if masked:
            s = jnp.where(causal, s, NEG)
        m = m_ref[...]
        m_new = jnp.maximum(m, jnp.max(s, axis=-1, keepdims=True))
        alpha = jnp.exp(m - m_new)
        p = jnp.exp((s - m_new).astype(BF16))
        l_ref[...] = alpha * l_ref[...] + jnp.sum(p.astype(F32), axis=-1, keepdims=True)
        m_ref[...] = m_new
        acc_ref[...] = alpha * acc_ref[...] + _dot(p, v_ref[0, rows, :])

    acc_ref[...] = jnp.zeros_like(acc_ref)
    m_ref[...] = jnp.full((2 * tq, 1), NEG, F32)
    l_ref[...] = jnp.zeros((2 * tq, 1), F32)
    logits(0, sa_ref)

    def body(u, carry):
        logits(2 * u + 1, sb_ref)
        finish(sa_ref, 2 * u)
        logits(2 * u + 2, sa_ref)
        finish(sb_ref, 2 * u + 1)
        return carry

    lax.fori_loop(0, i // 2, body, 0)

    @pl.when(i % 2 == 0)
    def _():
        finish(sa_ref, i, masked=True)

    @pl.when(i % 2 == 1)
    def _():
        logits(i, sb_ref)
        finish(sa_ref, i - 1)
        finish(sb_ref, i, masked=True)

    o = acc_ref[...] / l_ref[...]
    return jnp.where(_head_masks(tq), o[0:tq], o[tq:])


def _fox_kernel(q_ref, k_ref, v_ref, f_ref, o_ref, kx_ref, *flash, tq):
    p = pl.program_id(1)
    i = pl.program_id(2)
    seq = k_ref.shape[1]
    bt = 512

    @pl.when(i == 0)
    def _():
        src = _iota((LANES, LANES), 0)
        dst = _iota((LANES, LANES), 1)
        pm = jnp.logical_or(jnp.logical_and(src == 2 * p, dst < 3),
                            jnp.logical_and(src == 2 * p + 1, jnp.logical_and(dst >= 3, dst < 6))).astype(F32)
        sub = _iota((bt, LANES), 1) % 3

        def build(c, carry):
            rows = pl.ds(pl.multiple_of(c * bt, bt), bt)
            hi, mid, lo = _split3(_dot_hi(f_ref[0, rows, :], pm))
            kx_ref[rows, 0:LANES] = k_ref[0, rows, :]
            kx_ref[rows, LANES:] = jnp.where(sub == 0, hi, jnp.where(sub == 1, mid, lo)).astype(BF16)
            return carry

        lax.fori_loop(0, seq // bt, build, 0)

    q = q_ref[0].astype(F32)
    lane = _iota((tq, LANES), 1)
    hm = lane < HEAD_DIM
    top = jnp.concatenate([jnp.where(hm, q, 0.0), jnp.where(lane < 3, -1.0, 0.0)], axis=1)
    bot = jnp.concatenate([jnp.where(hm, 0.0, q), jnp.where(jnp.logical_and(lane >= 3, lane < 6), -1.0, 0.0)], axis=1)
    qx = jnp.concatenate([top, bot], axis=0).astype(BF16)
    o_ref[0] = _flash_causal(qx, kx_ref, v_ref, flash, i, tq).astype(o_ref.dtype)


def _fox(hb, fcol, tq=512):
    bsz, seq, _ = hb.shape
    assert seq % tq == 0
    return pl.pallas_call(
        functools.partial(_fox_kernel, tq=tq),
        grid=(bsz, PAIRS, seq // tq),
        in_specs=[
            pl.BlockSpec((1, tq, LANES), lambda b, p, i: (b, i, CB_CQ + p)),
            pl.BlockSpec((1, seq, LANES), lambda b, p, i: (b, 0, CB_CK + p)),
            pl.BlockSpec((1, seq, LANES), lambda b, p, i: (b, 0, CB_CV + p)),
            pl.BlockSpec((1, seq, LANES), lambda b, p, i: (b, 0, 0)),
        ],
        out_specs=pl.BlockSpec((1, tq, LANES), lambda b, p, i: (b, i, p)),
        out_shape=jax.ShapeDtypeStruct((bsz, seq, WIDTH), BF16),
        scratch_shapes=[pltpu.VMEM((seq, 2 * LANES), BF16)] + _flash_scratch(tq),
        compiler_params=_cparams(("parallel", "parallel", "arbitrary")),
        name="fox",
    )(hb, hb, hb, fcol)


def _moba_kernel(q_ref, k_ref, v_ref, o_ref, kx_ref, kmean_ref, *flash, tq):
    i = pl.program_id(2)
    seq = k_ref.shape[1]
    blk = MOBA_BLOCK

    @pl.when(i == 0)
    def _():
        kmean_ref[...] = jnp.zeros_like(kmean_ref)
        lane = _iota((blk, LANES), 1)

        def build(n, carry):
            rows = pl.ds(pl.multiple_of(n * blk, blk), blk)
            k = k_ref[0, rows, :]
            kx_ref[rows, 0:LANES] = k
            kx_ref[rows, LANES:] = jnp.where(lane == n, 1.0, 0.0).astype(BF16)
            kmean_ref[pl.ds(n, 1), :] = jnp.sum(k.astype(F32), axis=0, keepdims=True) * (1.0 / blk)
            return carry

        lax.fori_loop(0, seq // blk, build, 0)

    q = q_ref[0].astype(F32)
    lane = _iota((tq, LANES), 1)
    hm = lane < HEAD_DIM
    own = i * (tq // blk) + _iota((tq, LANES), 0) // blk
    colf = lane.astype(F32)
    valid = lane < own
    halves = []
    for h in range(2):
        qh = jnp.where(hm, q, 0.0) if h == 0 else jnp.where(hm, 0.0, q)
        g = jnp.where(valid, _dot_nt_hi(qh, kmean_ref[...]), NEG)
        bias = jnp.where(lane == own, 0.0, NEG)
        for _ in range(MOBA_TOPK):
            mx = jnp.max(g, axis=-1, keepdims=True)
            first = jnp.min(jnp.where(g == mx, colf, 1e9), axis=-1, keepdims=True)
            pick = colf == first
            bias = jnp.where(jnp.logical_and(pick, valid), 0.0, bias)
            g = jnp.where(pick, -3e38, g)
        halves.append(jnp.concatenate([qh, bias], axis=1))
    qx = jnp.concatenate(halves, axis=0).astype(BF16)
    o_ref[0] = _flash_causal(qx, kx_ref, v_ref, flash, i, tq).astype(o_ref.dtype)


def _moba(hb, tq=512):
    bsz, seq, _ = hb.shape
    assert seq % tq == 0 and tq % MOBA_BLOCK == 0 and seq // MOBA_BLOCK <= LANES
    return pl.pallas_call(
        functools.partial(_moba_kernel, tq=tq),
        grid=(bsz, PAIRS, seq // tq),
        in_specs=[
            pl.BlockSpec((1, tq, LANES), lambda b, p, i: (b, i, CB_AQ + p)),
            pl.BlockSpec((1, seq, LANES), lambda b, p, i: (b, 0, CB_AK + p)),
            pl.BlockSpec((1, seq, LANES), lambda b, p, i: (b, 0, CB_AV + p)),
        ],
        out_specs=pl.BlockSpec((1, tq, LANES), lambda b, p, i: (b, i, p)),
        out_shape=jax.ShapeDtypeStruct((bsz, seq, WIDTH), BF16),
        scratch_shapes=[pltpu.VMEM((seq, 2 * LANES), BF16), pltpu.VMEM((LANES, LANES), F32)] + _flash_scratch(tq),
        compiler_params=_cparams(("parallel", "parallel", "arbitrary")),
        name="moba",
    )(hb, hb, hb)


def _band_kernel(q0, q1, q2, k0c, k0p, k1c, k1p, k2c, k2p, vc, vp, o_ref, qb, kb, vb, nb, mb, sb, *, tile):
    jt = pl.program_id(2)
    for g, (qr, kc, kp) in enumerate(((q0, k0c, k0p), (q1, k1c, k1p), (q2, k2c, k2p))):
        qb[g] = qr[0].astype(F32)
        kb[g, 0:tile, :] = kp[0].astype(F32)
        kb[g, tile:, :] = kc[0].astype(F32)
    vb[0:tile, :] = vp[0].astype(F32)
    vb[tile:, :] = vc[0].astype(F32)

    hm = _head_masks(BAND)
    qi = _iota((BAND, 2 * BAND), 0)
    kj = _iota((BAND, 2 * BAND), 1)
    dist = BAND + qi - kj
    band = jnp.logical_and(dist >= 0, dist <= BAND)
    cur_half = kj >= BAND
    nblocks = tile // BAND

    for g, dil in enumerate(DILATIONS):
        per_stream = nblocks // dil

        def step(it, carry, g=g, dil=dil, per_stream=per_stream):
            starts, kstarts, valids = [], [], []
            for j in range(BAND_GROUP):
                idx = it * BAND_GROUP + j
                n = idx % per_stream
                start = idx // per_stream + n * (BAND * dil)
                starts.append(start)
                kstarts.append(tile + start - BAND * dil)
                valids.append(jnp.logical_and(band, jnp.logical_or(cur_half, jnp.logical_or(jt > 0, n > 0))))
            qs = [qb[g, pl.ds(s, BAND, stride=dil), :] for s in starts]
            kks = [kb[g, pl.ds(s, 2 * BAND, stride=dil), :].astype(BF16) for s in kstarts]
            vvs = [vb[pl.ds(s, 2 * BAND, stride=dil), :].astype(BF16) for s in kstarts]
            logits = []
            for q, kk, valid in zip(qs, kks, valids):
                for h in range(2):
                    qh = (jnp.where(hm, q, 0.0) if h == 0 else jnp.where(hm, 0.0, q)).astype(BF16)
                    logits.append(jnp.where(valid, _dot_nt(qh, kk), NEG))
            ms = [jnp.max(s, axis=-1, keepdims=True) for s in logits]
            ps = [jnp.exp(s - m) for s, m in zip(logits, ms)]
            ss = [jnp.sum(p, axis=-1, keepdims=True) for p in ps]
            nums = [_dot(p.astype(BF16), vvs[i // 2]) for i, p in enumerate(ps)]
            for j, s in enumerate(starts):
                rows = pl.ds(s, BAND, stride=dil)
                nb[g, rows, :] = jnp.where(hm, nums[2 * j], nums[2 * j + 1])
                mb[g, rows, :] = jnp.where(hm, ms[2 * j], ms[2 * j + 1])
                sb[g, rows, :] = jnp.where(hm, ss[2 * j], ss[2 * j + 1])
            return carry

        lax.fori_loop(0, nblocks // BAND_GROUP, step, 0)

    ch = 256

    def merge(i, carry):
        rows = pl.ds(pl.multiple_of(i * ch, ch), ch)
        m_all = jnp.maximum(jnp.maximum(mb[0, rows, :], mb[1, rows, :]), mb[2, rows, :])
        num = jnp.zeros((ch, LANES), F32)
        den = jnp.zeros((ch, LANES), F32)
        for g in range(len(DILATIONS)):
            w = jnp.exp(mb[g, rows, :] - m_all)
            num = num + nb[g, rows, :] * w
            den = den + sb[g, rows, :] * w
        o_ref[0, rows, :] = (num / den).astype(o_ref.dtype)
        return carry

    lax.fori_loop(0, tile // ch, merge, 0)


def _dilated(hb, tile=BAND_TILE):
    bsz, seq, _ = hb.shape
    ng = len(DILATIONS)
    nblocks = tile // BAND
    assert seq % tile == 0 and nblocks % BAND_GROUP == 0 and all(nblocks % d == 0 for d in DILATIONS)
    cur = lambda cb: pl.BlockSpec((1, tile, LANES), lambda b, p, j: (b, j, cb + p))
    prv = lambda cb: pl.BlockSpec((1, tile, LANES), lambda b, p, j: (b, jnp.maximum(j - 1, 0), cb + p))
    in_specs = [cur(CB_BQ + PAIRS * g) for g in range(ng)]
    for g in range(ng):
        in_specs += [cur(CB_BK + PAIRS * g), prv(CB_BK + PAIRS * g)]
    in_specs += [cur(CB_BV), prv(CB_BV)]
    acc = pltpu.VMEM((ng, tile, LANES), F32)
    return pl.pallas_call(
        functools.partial(_band_kernel, tile=tile),
        grid=(bsz, PAIRS, seq // tile),
        in_specs=in_specs,
        out_specs=pl.BlockSpec((1, tile, LANES), lambda b, p, j: (b, j, p)),
        out_shape=jax.ShapeDtypeStruct((bsz, seq, WIDTH), BF16),
        scratch_shapes=[acc, pltpu.VMEM((ng, 2 * tile, LANES), F32), pltpu.VMEM((2 * tile, LANES), F32), acc, acc, acc],
        compiler_params=_cparams(("parallel", "parallel", "arbitrary")),
        name="band",
    )(*([hb] * len(in_specs)))


def _dot01(a, b, nt=False, pieces=3):
    f = _dot_nt if nt else _dot
    if a.dtype == BF16:
        return sum(f(a, piece.astype(BF16)) for piece in _split3(b)[:pieces])
    return sum(f(piece.astype(BF16), b) for piece in _split3(a)[:pieces])


def _tri_inverse_all(ms):
    c = ms[0].shape[0]
    ri = _iota((c, c), 0)
    ci = _iota((c, c), 1)
    base = 16
    inblk = ri // base == ci // base
    eye = jnp.where(ri == ci, 1.0, 0.0)
    ps = [jnp.where(inblk, -m, 0.0) for m in ms]
    ts = [eye + p for p in ps]
    for _ in range(3):
        pbs = [p.astype(BF16) for p in ps]
        ps = [_dot(pb, pb) for pb in pbs]
        ts = [t + _dot(t.astype(BF16), p.astype(BF16)) for t, p in zip(ts, ps)]
    size = base
    while size < c:
        lower = jnp.logical_and(ri // (2 * size) == ci // (2 * size), ri // size != ci // size)
        tbs = [t.astype(BF16) for t in ts]
        xs = [_dot(tb, jnp.where(lower, m, 0.0).astype(BF16)) for tb, m in zip(tbs, ms)]
        ts = [t - _dot(x.astype(BF16), tb) for t, x, tb in zip(ts, xs, tbs)]
        size *= 2
    return ts


def _gdn_kernel(h_ref, cw_ref, alog_ref, dtb_ref, nw_ref, o_ref,
                xe_ref, u_s, w_s, qd_s, kd_s, qk_s, egl_s, o_s, st_ref, *, tile):
    cc = GDN_CHUNK
    nc = tile // cc
    t = pl.program_id(1)

    @pl.when(t == 0)
    def _():
        xe_ref[0:8, :] = jnp.zeros((8, 3 * WIDTH), F32)
        st_ref[...] = jnp.zeros_like(st_ref)

    x = h_ref[0, :, LANES:]
    xe_ref[8:8 + tile, :] = x
    y = jnp.zeros((tile, 3 * WIDTH), F32)
    for j in range(CONV_K):
        y = y + cw_ref[j:j + 1, :] * xe_ref[8 - (CONV_K - 1) + j:8 - (CONV_K - 1) + j + tile, :]
    xe_ref[0:8, :] = x[tile - 8:tile, :]
    y = y * jax.nn.sigmoid(y)

    bd = ((_iota((WIDTH, WIDTH), 0) // HEAD_DIM) == (_iota((WIDTH, WIDTH), 1) // HEAD_DIM)).astype(BF16)
    q = y[:, 0:WIDTH]
    k = y[:, WIDTH:2 * WIDTH]
    v = y[:, 2 * WIDTH:]
    q = q * lax.rsqrt(_dot01(q * q, bd, pieces=2) + RMS_EPS) * (HEAD_DIM ** -0.5)
    k = k * lax.rsqrt(_dot01(k * k, bd, pieces=2) + RMS_EPS)

    hs = h_ref[0, :, 0:LANES]
    er = _iota((LANES, WIDTH), 0)
    ec = _iota((LANES, WIDTH), 1) // HEAD_DIM
    beta = jax.nn.sigmoid(_dot01(hs, (er == ec + HEADS).astype(BF16)))
    g = -jnp.exp(alog_ref[...]) * jax.nn.softplus(_dot01(hs, (er == ec + 2 * HEADS).astype(BF16)) + dtb_ref[...])

    tr = _iota((tile, tile), 0)
    tc = _iota((tile, tile), 1)
    tri = jnp.logical_and(tr // cc == tc // cc, tr >= tc).astype(BF16)
    gc = _dot01(tri, g)
    glast = jnp.broadcast_to(gc.reshape(nc, cc, WIDTH)[:, cc - 1:cc, :], (nc, cc, WIDTH)).reshape(tile, WIDTH)
    eg = jnp.exp(gc)
    kb = k * beta
    vb = v * beta
    wb = kb * eg
    qd_s[...] = q * eg
    kd_s[...] = k * jnp.exp(glast - gc)
    egl_s[...] = jnp.exp(glast)

    ri = _iota((cc, cc), 0)
    ci = _iota((cc, cc), 1)
    incl = ri >= ci
    strict = ri > ci
    lane = _iota((cc, LANES), 1)
    hm = lane < HEAD_DIM
    sls = [slice(p * LANES, (p + 1) * LANES) for p in range(PAIRS)]
    keeps = [hm, jnp.logical_not(hm)]
    heads = [(p, h) for p in range(PAIRS) for h in range(2)]
    chains = [(p, h, c) for p, h in heads for c in range(nc)]
    rws = [slice(c * cc, (c + 1) * cc) for c in range(nc)]
    kps = [k[:, sl].astype(BF16) for sl in sls]
    rhss = [jnp.concatenate([vb[:, sl], wb[:, sl]], axis=1).astype(BF16) for sl in sls]
    rowf = {(p, h): _dot01((lane == h * HEAD_DIM).astype(BF16), gc[:, sls[p]], nt=True) for p, h in heads}
    kbm = {(p, h): jnp.where(jnp.tile(keeps[h], (nc, 1)), kb[:, sls[p]], 0.0).astype(BF16) for p, h in heads}
    qm = {(p, h): jnp.where(jnp.tile(keeps[h], (nc, 1)), q[:, sls[p]], 0.0).astype(BF16) for p, h in heads}
    kks = [_dot_nt(kbm[p, h][rws[c]], kps[p][rws[c]]) for p, h, c in chains]
    qks = [_dot_nt(qm[p, h][rws[c]], kps[p][rws[c]]) for p, h, c in chains]
    mms = []
    for (p, h, c), kk, qk in zip(chains, kks, qks):
        col = p * LANES + h * HEAD_DIM
        diff = gc[rws[c], col:col + 1] - rowf[p, h][:, rws[c]]
        decay = jnp.where(incl, jnp.exp(jnp.where(incl, diff, 0.0)), 0.0)
        mms.append(jnp.where(strict, kk * decay, 0.0))
        qk_s[2 * p + h, rws[c], :] = qk * decay
    tinvs = _tri_inverse_all(mms)
    rs = {ch: _dot(tinv.astype(BF16), rhss[ch[0]][rws[ch[2]]]) for ch, tinv in zip(chains, tinvs)}
    for p in range(PAIRS):
        for c in range(nc):
            r0, r1 = rs[p, 0, c], rs[p, 1, c]
            u_s[rws[c], sls[p]] = jnp.where(hm, r0[:, :LANES], r1[:, :LANES])
            w_s[rws[c], sls[p]] = jnp.where(hm, r0[:, LANES:], r1[:, LANES:])

    bdiag = (_iota((LANES, LANES), 0) // HEAD_DIM) == (_iota((LANES, LANES), 1) // HEAD_DIM)
    for c in range(nc):
        rows = rws[c]
        sts = [st_ref[p] for p in range(PAIRS)]
        stbs = [st.astype(BF16) for st in sts]
        wss = [_dot(w_s[rows, sl].astype(BF16), stb) for sl, stb in zip(sls, stbs)]
        qss = [_dot(qd_s[rows, sl].astype(BF16), stb) for sl, stb in zip(sls, stbs)]
        vnbs = [(u_s[rows, sl] - ws).astype(BF16) for sl, ws in zip(sls, wss)]
        upds = [lax.dot_general(kd_s[rows, sl].astype(BF16), vnb, (((0,), (0,)), ((), ())), preferred_element_type=F32)
                for sl, vnb in zip(sls, vnbs)]
        intra = [(_dot(qk_s[2 * p, rows, :].astype(BF16), vnbs[p]), _dot(qk_s[2 * p + 1, rows, :].astype(BF16), vnbs[p]))
                 for p in range(PAIRS)]
        for p in range(PAIRS):
            st_ref[p] = sts[p] * egl_s[c * cc:c * cc + 1, sls[p]] + jnp.where(bdiag, upds[p], 0.0)
            o_s[rows, sls[p]] = qss[p] + jnp.where(hm, intra[p][0], intra[p][1])

    o = o_s[...]
    ms = _dot01(o * o, bd, pieces=2) * (1.0 / HEAD_DIM)
    o_ref[0] = (o * lax.rsqrt(ms + RMS_EPS) * nw_ref[...]).astype(o_ref.dtype)


def _gdn(hf, conv_w, a_log, dt_bias, norm_w, tile=512):
    bsz, seq, _ = hf.shape
    rep = lambda a: jnp.repeat(a.astype(F32), HEAD_DIM)[None, :]
    wide = pltpu.VMEM((tile, WIDTH), F32)
    return pl.pallas_call(
        functools.partial(_gdn_kernel, tile=tile),
        grid=(bsz, seq // tile),
        in_specs=[
            pl.BlockSpec((1, tile, HF_COLS), lambda b, t: (b, t, 0)),
            pl.BlockSpec((CONV_K, 3 * WIDTH), lambda b, t: (0, 0)),
            pl.BlockSpec((1, WIDTH), lambda b, t: (0, 0)),
            pl.BlockSpec((1, WIDTH), lambda b, t: (0, 0)),
            pl.BlockSpec((1, WIDTH), lambda b, t: (0, 0)),
        ],
        out_specs=pl.BlockSpec((1, tile, WIDTH), lambda b, t: (b, t, 0)),
        out_shape=jax.ShapeDtypeStruct((bsz, seq, WIDTH), BF16),
        scratch_shapes=[pltpu.VMEM((tile + 8, 3 * WIDTH), F32), wide, wide, wide, wide,
                        pltpu.VMEM((HEADS, tile, LANES), F32), wide, wide,
                        pltpu.VMEM((PAIRS, LANES, LANES), F32)],
        compiler_params=_cparams(("parallel", "arbitrary")),
        name="gdn",
    )(hf, conv_w.astype(F32), rep(a_log), rep(dt_bias), jnp.tile(norm_w.astype(F32), HEADS)[None, :])


def _memattn_kernel(q_ref, kv_ref, o_ref):
    tq = q_ref.shape[1]
    q = q_ref[0]
    k = kv_ref[0, :, 0:WIDTH]
    v = kv_ref[0, :, WIDTH:2 * WIDTH]
    head = _iota((tq, WIDTH), 1) // MEM_HEAD_DIM
    zero = jnp.zeros_like(q)
    out = jnp.zeros((tq, WIDTH), F32)
    for h in range(MEM_HEADS):
        s = _dot_nt(jnp.where(head == h, q, zero), k)
        m = jnp.max(s, axis=-1, keepdims=True)
        p = jnp.exp(s - m)
        o = _dot(p.astype(BF16), v) / jnp.sum(p, axis=-1, keepdims=True)
        out = jnp.where(head == h, o, out)
    o_ref[0] = out.astype(o_ref.dtype)


def _memattn(hb, kv, layer, tq=512):
    bsz, seq, _ = hb.shape
    mlen = kv.shape[1]
    return pl.pallas_call(
        _memattn_kernel,
        grid=(bsz, seq // tq),
        in_specs=[
            pl.BlockSpec((1, tq, WIDTH), lambda b, i: (b, i, CB_E // PAIRS)),
            pl.BlockSpec((1, mlen, 2 * WIDTH), lambda b, i: (b, 0, layer)),
        ],
        out_specs=pl.BlockSpec((1, tq, WIDTH), lambda b, i: (b, i, 0)),
        out_shape=jax.ShapeDtypeStruct((bsz, seq, WIDTH), BF16),
        compiler_params=_cparams(("parallel", "parallel")),
        name="memattn",
    )(hb, kv)


def _out_kernel(oa, ob, oc, od, oe, z_ref, ml_ref, x_ref, wb_ref, wo_ref, g_ref, b_ref, y_ref, yb_ref, *, alpha):
    tm = x_ref.shape[0]
    acc = jnp.zeros((tm, D_MODEL), F32)
    for n, o_ref in enumerate((oa, ob, oc, od, oe)):
        z = z_ref[:, n * WIDTH:(n + 1) * WIDTH].astype(F32)
        gated = o_ref[...].astype(F32) * (z * jax.nn.sigmoid(z))
        yn = _dot(gated.astype(BF16), wb_ref[n])
        acc = acc + jax.nn.sigmoid(ml_ref[:, n * D_MODEL:(n + 1) * D_MODEL].astype(F32)) * yn
    r = alpha * x_ref[...] + _dot(acc.astype(BF16), wo_ref[...])
    mu = jnp.mean(r, axis=-1, keepdims=True)
    rc = r - mu
    var = jnp.mean(rc * rc, axis=-1, keepdims=True)
    y = rc * lax.rsqrt(var + LN_EPS) * g_ref[...] + b_ref[...]
    y_ref[...] = y
    yb_ref[...] = y.astype(BF16)


def _out(branches, z, ml, x, w_branch, w_out, ln_g, ln_b, alpha, tm=256):
    m = x.shape[0]
    row = lambda n: pl.BlockSpec((tm, n), lambda i: (i, 0))
    return pl.pallas_call(
        functools.partial(_out_kernel, alpha=alpha),
        grid=(m // tm,),
        in_specs=[row(WIDTH)] * N_BRANCH + [row(N_BRANCH * WIDTH), row(N_BRANCH * D_MODEL), row(D_MODEL),
                  pl.BlockSpec((N_BRANCH, WIDTH, D_MODEL), lambda i: (0, 0, 0)),
                  pl.BlockSpec((D_MODEL, D_MODEL), lambda i: (0, 0)),
                  pl.BlockSpec((1, D_MODEL), lambda i: (0, 0)),
                  pl.BlockSpec((1, D_MODEL), lambda i: (0, 0))],
        out_specs=[row(D_MODEL), row(D_MODEL)],
        out_shape=[jax.ShapeDtypeStruct((m, D_MODEL), F32), jax.ShapeDtypeStruct((m, D_MODEL), BF16)],
        compiler_params=_cparams(("parallel",)),
        name="out",
    )(*branches, z, ml, x, w_branch, w_out, ln_g, ln_b)


def _split_weights(w_in, b_in):
    def take(a, spans):
        return jnp.concatenate([a[..., o:o + n] for o, n in spans], axis=-1)

    hb_cols = ((O_E, WIDTH), (O_A, 3 * WIDTH), (O_BQK, 6 * WIDTH), (O_BV, WIDTH), (O_C, 3 * WIDTH))
    scale = np.ones((HB_COLS,), np.float32)
    scale[CB_E * LANES:CB_E * LANES + WIDTH] = MEM_HEAD_DIM ** -0.5
    for cb in (CB_AQ, CB_CQ):
        scale[cb * LANES:cb * LANES + WIDTH] = HEAD_DIM ** -0.5
    scale[CB_BQ * LANES:CB_BQ * LANES + 3 * WIDTH] = HEAD_DIM ** -0.5
    hf_cols = ((O_CF, HEADS), (O_DBETA, HEADS), (O_DDECAY, HEADS))
    pad = LANES - 3 * HEADS
    w_hf = jnp.concatenate([take(w_in, hf_cols), jnp.zeros((D_MODEL, pad), w_in.dtype), w_in[:, O_D:O_D + 3 * WIDTH]], axis=1)
    b_hf = jnp.concatenate([take(b_in, hf_cols), jnp.zeros((pad,), b_in.dtype), b_in[O_D:O_D + 3 * WIDTH]])
    groups = {
        "hb": (take(w_in, hb_cols), take(b_in, hb_cols), jnp.asarray(scale)),
        "hf": (w_hf, b_hf, None),
        "z": (w_in[:, O_Z:O_Z + N_BRANCH * WIDTH], b_in[O_Z:O_Z + N_BRANCH * WIDTH], None),
        "ml": (w_in[:, O_MERGE:], b_in[O_MERGE:], None),
    }
    out = {}
    for name, (w, b, s) in groups.items():
        s = jnp.ones((w.shape[1],), F32) if s is None else s
        out[name] = (w.astype(BF16), b.astype(F32)[None, :], s[None, :])
    return out


def _layer(x, xb, kv, layer, w_in, b_in, conv_w, a_log, dt_bias, gdn_norm_w, w_branch, w_out, ln_g, ln_b, alpha):
    bsz, seq, d = x.shape
    m = bsz * seq
    xf = x.reshape(m, d)
    gw = _split_weights(w_in, b_in)
    hb = _proj(xb, *gw["hb"], BF16, 1024, 768).reshape(bsz, seq, HB_COLS)
    hf = _proj(xb, *gw["hf"], F32, 1024, 640).reshape(bsz, seq, HF_COLS)
    z = _proj(xb, *gw["z"], BF16, 1024, 640)
    ml = _proj(xb, *gw["ml"], BF16, 1024, 1024)

    o_a = _moba(hb)
    o_b = _dilated(hb)
    o_c = _fox(hb, _fcum(hf))
    o_d = _gdn(hf, conv_w, a_log, dt_bias, gdn_norm_w)
    o_e = _memattn(hb, kv, layer)
    branches = [o.reshape(m, WIDTH) for o in (o_a, o_b, o_c, o_d, o_e)]
    y, yb = _out(branches, z, ml, xf, w_branch.astype(BF16), w_out.astype(BF16),
                 ln_g.astype(F32)[None, :], ln_b.astype(F32)[None, :], alpha)
    return y.reshape(bsz, seq, d), yb


def kernel(x, mem, mem_ln_g, mem_ln_b, w_in, b_in, conv_w, a_log, dt_bias, gdn_norm_w, w_mem_kv, w_branch, w_out, ln_g, ln_b):
    depth = w_in.shape[0]
    alpha = float((2 * depth) ** 0.25)
    w_kv = jnp.concatenate([w_mem_kv[l] for l in range(depth)], axis=1).astype(BF16)
    kv = _memkv(mem.astype(F32), mem_ln_g.astype(F32)[None, :], mem_ln_b.astype(F32)[None, :], w_kv)
    x = x.astype(F32)
    xb = x.reshape(-1, x.shape[-1]).astype(BF16)
    for l in range(depth):
        x, xb = _layer(x, xb, kv, l, w_in[l], b_in[l], conv_w[l], a_log[l], dt_bias[l], gdn_norm_w[l],
                       w_branch[l], w_out[l], ln_g[l], ln_b[l], alpha)
    return x
```

```python
import functools

import jax
import jax.numpy as jnp
import numpy as np
from jax import lax
from jax.experimental import pallas as pl
from jax.experimental.pallas import tpu as pltpu

F32 = jnp.float32
BF16 = jnp.bfloat16
HI = lax.Precision.HIGHEST

D_MODEL = 1024
HEAD_DIM = 64
HEADS = 6
WIDTH = HEADS * HEAD_DIM
N_BRANCH = 5
LANES = 128
PAIRS = WIDTH // LANES
MOBA_BLOCK = 256
MOBA_TOPK = 3
DILATIONS = (1, 4, 16)
BAND = 128
BAND_TILE = 2048
BAND_GROUP = 4
GDN_CHUNK = 128
CONV_K = 4
MEM_HEADS = 4
MEM_HEAD_DIM = WIDTH // MEM_HEADS
NEG = -1e30
FLASH_UNDERFLOW = 110.0
LN_EPS = 1e-5
RMS_EPS = 1e-6

_SPLIT = (3 * WIDTH, 6 * WIDTH, WIDTH, 3 * WIDTH, HEADS, 3 * WIDTH, HEADS, HEADS, WIDTH, N_BRANCH * WIDTH, N_BRANCH * D_MODEL)
_OFF = tuple(int(v) for v in np.concatenate([[0], np.cumsum(_SPLIT)]))
(O_A, O_BQK, O_BV, O_C, O_CF, O_D, O_DBETA, O_DDECAY, O_E, O_Z, O_MERGE, _) = _OFF

CB_E = 0
CB_AQ, CB_AK, CB_AV = 3, 6, 9
CB_BQ, CB_BK, CB_BV = 12, 21, 30
CB_CQ, CB_CK, CB_CV = 33, 36, 39
HB_COLS = 42 * LANES
HF_COLS = LANES + 3 * WIDTH

VMEM_LIMIT = 56 * 1024 * 1024


def _cparams(sem):
    return pltpu.CompilerParams(dimension_semantics=sem, vmem_limit_bytes=VMEM_LIMIT)


def _dot(a, b):
    return jnp.dot(a, b, preferred_element_type=F32)


def _dot_nt(a, b):
    return lax.dot_general(a, b, (((1,), (1,)), ((), ())), preferred_element_type=F32)


def _dot_hi(a, b):
    return jnp.dot(a, b, preferred_element_type=F32, precision=HI)


def _dot_nt_hi(a, b):
    return lax.dot_general(a, b, (((1,), (1,)), ((), ())), preferred_element_type=F32, precision=HI)


def _iota(shape, dim):
    return lax.broadcasted_iota(jnp.int32, shape, dim)


def _proj_kernel(x_ref, w_ref, b_ref, s_ref, o_ref, wb_ref):
    @pl.when(pl.program_id(1) == 0)
    def _():
        wb_ref[...] = w_ref[...].astype(BF16)

    acc = _dot(x_ref[...], wb_ref[...])
    o_ref[...] = ((acc + b_ref[...]) * s_ref[...]).astype(o_ref.dtype)


def _proj(xb, w, b, scale, layer, out_dtype, tm, tn):
    m, k = xb.shape
    n = w.shape[-1]
    assert m % tm == 0 and n % tn == 0
    return pl.pallas_call(
        _proj_kernel,
        grid=(n // tn, m // tm),
        in_specs=[
            pl.BlockSpec((tm, k), lambda j, i: (i, 0)),
            pl.BlockSpec((None, k, tn), lambda j, i: (layer, 0, j)),
            pl.BlockSpec((None, 1, tn), lambda j, i: (layer, 0, j)),
            pl.BlockSpec((1, tn), lambda j, i: (0, j)),
        ],
        out_specs=pl.BlockSpec((tm, tn), lambda j, i: (i, j)),
        out_shape=jax.ShapeDtypeStruct((m, n), out_dtype),
        scratch_shapes=[pltpu.VMEM((k, tn), BF16)],
        compiler_params=_cparams(("parallel", "arbitrary")),
        name="proj",
    )(xb, w, b, scale)


def _memkv_kernel(mem_ref, g_ref, b_ref, w_ref, o_ref):
    x = mem_ref[0]
    mu = jnp.mean(x, axis=-1, keepdims=True)
    xc = x - mu
    var = jnp.mean(xc * xc, axis=-1, keepdims=True)
    y = xc * lax.rsqrt(var + LN_EPS) * g_ref[...] + b_ref[...]
    o_ref[0] = _dot(y.astype(BF16), w_ref[...]).astype(o_ref.dtype)


def _memkv(mem, g, b, w):
    bsz, mlen, d = mem.shape
    n = w.shape[1]
    return pl.pallas_call(
        _memkv_kernel,
        grid=(bsz,),
        in_specs=[
            pl.BlockSpec((1, mlen, d), lambda i: (i, 0, 0)),
            pl.BlockSpec((1, d), lambda i: (0, 0)),
            pl.BlockSpec((1, d), lambda i: (0, 0)),
            pl.BlockSpec((d, n), lambda i: (0, 0)),
        ],
        out_specs=pl.BlockSpec((1, mlen, n), lambda i: (i, 0, 0)),
        out_shape=jax.ShapeDtypeStruct((bsz, mlen, n), BF16),
        compiler_params=_cparams(("parallel",)),
        name="memkv",
    )(mem, g, b, w)


def _fcum_kernel(h_ref, o_ref, *, blk):
    seq = h_ref.shape[1]
    tri = (_iota((blk, blk), 0) >= _iota((blk, blk), 1)).astype(F32)

    def body(i, carry):
        rows = pl.ds(pl.multiple_of(i * blk, blk), blk)
        logf = jax.nn.log_sigmoid(h_ref[0, rows, :])
        c = _dot_hi(tri, logf) + carry
        o_ref[0, rows, :] = c
        return c[blk - 1:blk, :]

    lax.fori_loop(0, seq // blk, body, jnp.zeros((1, LANES), F32))


def _fcum(hf):
    bsz, seq, _ = hf.shape
    return pl.pallas_call(
        functools.partial(_fcum_kernel, blk=LANES),
        grid=(bsz,),
        in_specs=[pl.BlockSpec((1, seq, LANES), lambda b: (b, 0, 0))],
        out_specs=pl.BlockSpec((1, seq, LANES), lambda b: (b, 0, 0)),
        out_shape=jax.ShapeDtypeStruct((bsz, seq, LANES), F32),
        compiler_params=_cparams(("parallel",)),
        name="fcum",
    )(hf)


def _head_masks(rows):
    lane = _iota((rows, LANES), 1)
    return lane < HEAD_DIM


def _split3(x):
    hi = x.astype(BF16).astype(F32)
    r = x - hi
    mid = r.astype(BF16).astype(F32)
    return hi, mid, r - mid


def _flash_scratch(tq):
    return [pltpu.VMEM((2 * tq, LANES), F32), pltpu.VMEM((2 * tq, tq), F32), pltpu.VMEM((2 * tq, tq), F32),
            pltpu.VMEM((2 * tq, 1), F32), pltpu.VMEM((2 * tq, 1), F32)]


def _flash_causal(qx, kx_ref, v_ref, scratch, i, tq, past_tiles=None):
    acc_ref, sa_ref, sb_ref, m_ref, l_ref = scratch

    def logits(kt, dst):
        rows = pl.ds(pl.multiple_of(kt * tq, tq), tq)
        dst[...] = _dot_nt(qx, kx_ref[rows, :])

    r = _iota((2 * tq, tq), 0)
    causal = jnp.where(r >= tq, r - tq, r) >= _iota((2 * tq, tq), 1)

    def finish(src, kt, masked=False):
        rows = pl.ds(pl.multiple_of(kt * tq, tq), tq)
        s = src[...]
        if masked:
            s = jnp.where(causal, s, NEG)
        m = m_ref[...]
        m_new = jnp.maximum(m, jnp.max(s, axis=-1, keepdims=True))
        alpha = jnp.exp(m - m_new)
        p = jnp.exp((s - m_new).astype(BF16))
        l_ref[...] = alpha * l_ref[...] + jnp.sum(p.astype(F32), axis=-1, keepdims=True)
        m_ref[...] = m_new
        acc_ref[...] = alpha * acc_ref[...] + _dot(p, v_ref[0, rows, :])

    acc_ref[...] = jnp.zeros_like(acc_ref)
    m_ref[...] = jnp.full((2 * tq, 1), NEG, F32)
    l_ref[...] = jnp.zeros((2 * tq, 1), F32)
    logits(i, sa_ref)
    finish(sa_ref, i, masked=True)

    n = i if past_tiles is None else past_tiles(m_ref)
    tile = lambda u: jnp.maximum(i - 1 - u, 0)

    @pl.when(n > 0)
    def _():
        logits(tile(0), sa_ref)

    def body(v, carry):
        logits(tile(2 * v + 1), sb_ref)
        finish(sa_ref, tile(2 * v))
        logits(tile(2 * v + 2), sa_ref)
        finish(sb_ref, tile(2 * v + 1))
        return carry

    lax.fori_loop(0, n // 2, body, 0)

    @pl.when(n % 2 == 1)
    def _():
        finish(sa_ref, tile(n - 1))

    o = acc_ref[...] / l_ref[...]
    return jnp.where(_head_masks(tq), o[0:tq], o[tq:])


def _head_sumsq(x):
    hm = _head_masks(x.shape[0])
    sq = x * x
    return (jnp.sum(jnp.where(hm, sq, 0.0), axis=1, keepdims=True), jnp.sum(jnp.where(hm, 0.0, sq), axis=1, keepdims=True))


def _fox_kernel(q_ref, k_ref, v_ref, f_ref, o_ref, kx_ref, kn_ref, *flash, tq):
    p = pl.program_id(1)
    i = pl.program_id(2)
    seq = k_ref.shape[1]
    bt = 512

    @pl.when(i == 0)
    def _():
        src = _iota((LANES, LANES), 0)
        dst = _iota((LANES, LANES), 1)
        pm = jnp.logical_or(jnp.logical_and(src == 2 * p, dst < 3),
                            jnp.logical_and(src == 2 * p + 1, jnp.logical_and(dst >= 3, dst < 6))).astype(F32)
        sub = _iota((bt, LANES), 1) % 3

        def build(c, carry):
            rows = pl.ds(pl.multiple_of(c * bt, bt), bt)
            hi, mid, lo = _split3(_dot_hi(f_ref[0, rows, :], pm))
            k = k_ref[0, rows, :]
            kx_ref[rows, 0:LANES] = k
            kx_ref[rows, LANES:] = jnp.where(sub == 0, hi, jnp.where(sub == 1, mid, lo)).astype(BF16)
            return tuple(jnp.maximum(c0, jnp.max(s, axis=0, keepdims=True))
                         for c0, s in zip(carry, _head_sumsq(k.astype(F32))))

        zero = jnp.zeros((1, 1), F32)
        kn = lax.fori_loop(0, seq // bt, build, (zero, zero))
        kn_ref[0:1, :] = jnp.broadcast_to(kn[0], (1, LANES))
        kn_ref[1:2, :] = jnp.broadcast_to(kn[1], (1, LANES))

    q = q_ref[0].astype(F32)
    lane = _iota((tq, LANES), 1)
    hm = lane < HEAD_DIM
    top = jnp.concatenate([jnp.where(hm, q, 0.0), jnp.where(lane < 3, -1.0, 0.0)], axis=1)
    bot = jnp.concatenate([jnp.where(hm, 0.0, q), jnp.where(jnp.logical_and(lane >= 3, lane < 6), -1.0, 0.0)], axis=1)
    qx = jnp.concatenate([top, bot], axis=0).astype(BF16)

    def past_tiles(m_ref):
        nt = seq // tq
        fend = f_ref[0, pl.ds(tq - 1, nt, stride=tq), :]
        lane_t = _iota((nt, LANES), 1)
        tpos = _iota((nt, 1), 0)
        m = m_ref[...]
        need = tpos < 0
        for h, qn2 in enumerate(_head_sumsq(q)):
            m_min = jnp.min(m[h * tq:(h + 1) * tq], axis=0, keepdims=True)
            qk = jnp.sqrt(jnp.max(qn2, axis=0, keepdims=True) * kn_ref[h:h + 1, 0:1]) * 1.01 + 1.0
            f_h = jnp.sum(jnp.where(lane_t == 2 * p + h, fend, 0.0), axis=1, keepdims=True)
            need = jnp.logical_or(need, qk - f_h > m_min - FLASH_UNDERFLOW)
        first = jnp.min(jnp.where(jnp.logical_and(need, tpos < i), tpos, i).astype(F32))
        return i - first.astype(jnp.int32)

    o_ref[0] = _flash_causal(qx, kx_ref, v_ref, flash, i, tq, past_tiles).astype(o_ref.dtype)


def _fox(hb, fcol, tq=512):
    bsz, seq, _ = hb.shape
    assert seq % tq == 0
    return pl.pallas_call(
        functools.partial(_fox_kernel, tq=tq),
        grid=(bsz, PAIRS, seq // tq),
        in_specs=[
            pl.BlockSpec((1, tq, LANES), lambda b, p, i: (b, i, CB_CQ + p)),
            pl.BlockSpec((1, seq, LANES), lambda b, p, i: (b, 0, CB_CK + p)),
            pl.BlockSpec((1, seq, LANES), lambda b, p, i: (b, 0, CB_CV + p)),
            pl.BlockSpec((1, seq, LANES), lambda b, p, i: (b, 0, 0)),
        ],
        out_specs=pl.BlockSpec((1, tq, LANES), lambda b, p, i: (b, i, p)),
        out_shape=jax.ShapeDtypeStruct((bsz, seq, WIDTH), BF16),
        scratch_shapes=[pltpu.VMEM((seq, 2 * LANES), BF16), pltpu.VMEM((8, LANES), F32)] + _flash_scratch(tq),
        compiler_params=_cparams(("parallel", "parallel", "arbitrary")),
        name="fox",
    )(hb, hb, hb, fcol)


def _moba_kernel(q_ref, k_ref, v_ref, o_ref, kx_ref, kmean_ref, *flash, tq):
    i = pl.program_id(2)
    seq = k_ref.shape[1]
    blk = MOBA_BLOCK

    @pl.when(i == 0)
    def _():
        kmean_ref[...] = jnp.zeros_like(kmean_ref)
        lane = _iota((blk, LANES), 1)

        def build(n, carry):
            rows = pl.ds(pl.multiple_of(n * blk, blk), blk)
            k = k_ref[0, rows, :]
            kx_ref[rows, 0:LANES] = k
            kx_ref[rows, LANES:] = jnp.where(lane == n, 1.0, 0.0).astype(BF16)
            kmean_ref[pl.ds(n, 1), :] = jnp.sum(k.astype(F32), axis=0, keepdims=True) * (1.0 / blk)
            return carry

        lax.fori_loop(0, seq // blk, build, 0)

    q = q_ref[0].astype(F32)
    lane = _iota((tq, LANES), 1)
    hm = lane < HEAD_DIM
    own = i * (tq // blk) + _iota((tq, LANES), 0) // blk
    colf = lane.astype(F32)
    valid = lane < own
    halves = []
    for h in range(2):
        qh = jnp.where(hm, q, 0.0) if h == 0 else jnp.where(hm, 0.0, q)
        g = jnp.where(valid, _dot_nt_hi(qh, kmean_ref[...]), NEG)
        bias = jnp.where(lane == own, 0.0, NEG)
        for _ in range(MOBA_TOPK):
            mx = jnp.max(g, axis=-1, keepdims=True)
            first = jnp.min(jnp.where(g == mx, colf, 1e9), axis=-1, keepdims=True)
            pick = colf == first
            bias = jnp.where(jnp.logical_and(pick, valid), 0.0, bias)
            g = jnp.where(pick, -3e38, g)
        halves.append(jnp.concatenate([qh, bias], axis=1))
    qx = jnp.concatenate(halves, axis=0).astype(BF16)
    o_ref[0] = _flash_causal(qx, kx_ref, v_ref, flash, i, tq).astype(o_ref.dtype)


def _moba(hb, tq=512):
    bsz, seq, _ = hb.shape
    assert seq % tq == 0 and tq % MOBA_BLOCK == 0 and seq // MOBA_BLOCK <= LANES
    return pl.pallas_call(
        functools.partial(_moba_kernel, tq=tq),
        grid=(bsz, PAIRS, seq // tq),
        in_specs=[
            pl.BlockSpec((1, tq, LANES), lambda b, p, i: (b, i, CB_AQ + p)),
            pl.BlockSpec((1, seq, LANES), lambda b, p, i: (b, 0, CB_AK + p)),
            pl.BlockSpec((1, seq, LANES), lambda b, p, i: (b, 0, CB_AV + p)),
        ],
        out_specs=pl.BlockSpec((1, tq, LANES), lambda b, p, i: (b, i, p)),
        out_shape=jax.ShapeDtypeStruct((bsz, seq, WIDTH), BF16),
        scratch_shapes=[pltpu.VMEM((seq, 2 * LANES), BF16), pltpu.VMEM((LANES, LANES), F32)] + _flash_scratch(tq),
        compiler_params=_cparams(("parallel", "parallel", "arbitrary")),
        name="moba",
    )(hb, hb, hb)


def _band_kernel(q0, q1, q2, k0c, k0p, k1c, k1p, k2c, k2p, vc, vp, o_ref, qb, kb, vb, nb, mb, sb, *, tile):
    jt = pl.program_id(2)
    for g, (qr, kc, kp) in enumerate(((q0, k0c, k0p), (q1, k1c, k1p), (q2, k2c, k2p))):
        qb[g] = qr[0].astype(F32)
        kb[g, 0:tile, :] = kp[0].astype(F32)
        kb[g, tile:, :] = kc[0].astype(F32)
    vb[0:tile, :] = vp[0].astype(F32)
    vb[tile:, :] = vc[0].astype(F32)

    hm = _head_masks(BAND)
    qi = _iota((BAND, 2 * BAND), 0)
    kj = _iota((BAND, 2 * BAND), 1)
    dist = BAND + qi - kj
    band = jnp.logical_and(dist >= 0, dist <= BAND)
    cur_half = kj >= BAND
    nblocks = tile // BAND

    for g, dil in enumerate(DILATIONS):
        per_stream = nblocks // dil

        def step(it, carry, g=g, dil=dil, per_stream=per_stream):
            starts, kstarts, valids = [], [], []
            for j in range(BAND_GROUP):
                idx = it * BAND_GROUP + j
                n = idx % per_stream
                start = idx // per_stream + n * (BAND * dil)
                starts.append(start)
                kstarts.append(tile + start - BAND * dil)
                valids.append(jnp.logical_and(band, jnp.logical_or(cur_half, jnp.logical_or(jt > 0, n > 0))))
            qs = [qb[g, pl.ds(s, BAND, stride=dil), :] for s in starts]
            kks = [kb[g, pl.ds(s, 2 * BAND, stride=dil), :].astype(BF16) for s in kstarts]
            vvs = [vb[pl.ds(s, 2 * BAND, stride=dil), :].astype(BF16) for s in kstarts]
            logits = []
            for q, kk, valid in zip(qs, kks, valids):
                for h in range(2):
                    qh = (jnp.where(hm, q, 0.0) if h == 0 else jnp.where(hm, 0.0, q)).astype(BF16)
                    logits.append(jnp.where(valid, _dot_nt(qh, kk), NEG))
            ms = [jnp.max(s, axis=-1, keepdims=True) for s in logits]
            ps = [jnp.exp(s - m) for s, m in zip(logits, ms)]
            ss = [jnp.sum(p, axis=-1, keepdims=True) for p in ps]
            nums = [_dot(p.astype(BF16), vvs[i // 2]) for i, p in enumerate(ps)]
            for j, s in enumerate(starts):
                rows = pl.ds(s, BAND, stride=dil)
                nb[g, rows, :] = jnp.where(hm, nums[2 * j], nums[2 * j + 1])
                mb[g, rows, :] = jnp.where(hm, ms[2 * j], ms[2 * j + 1])
                sb[g, rows, :] = jnp.where(hm, ss[2 * j], ss[2 * j + 1])
            return carry

        lax.fori_loop(0, nblocks // BAND_GROUP, step, 0)

    ch = 256

    def merge(i, carry):
        rows = pl.ds(pl.multiple_of(i * ch, ch), ch)
        m_all = jnp.maximum(jnp.maximum(mb[0, rows, :], mb[1, rows, :]), mb[2, rows, :])
        num = jnp.zeros((ch, LANES), F32)
        den = jnp.zeros((ch, LANES), F32)
        for g in range(len(DILATIONS)):
            w = jnp.exp(mb[g, rows, :] - m_all)
            num = num + nb[g, rows, :] * w
            den = den + sb[g, rows, :] * w
        o_ref[0, rows, :] = (num / den).astype(o_ref.dtype)
        return carry

    lax.fori_loop(0, tile // ch, merge, 0)


def _dilated(hb, tile=BAND_TILE):
    bsz, seq, _ = hb.shape
    ng = len(DILATIONS)
    nblocks = tile // BAND
    assert seq % tile == 0 and nblocks % BAND_GROUP == 0 and all(nblocks % d == 0 for d in DILATIONS)
    cur = lambda cb: pl.BlockSpec((1, tile, LANES), lambda b, p, j: (b, j, cb + p))
    prv = lambda cb: pl.BlockSpec((1, tile, LANES), lambda b, p, j: (b, jnp.maximum(j - 1, 0), cb + p))
    in_specs = [cur(CB_BQ + PAIRS * g) for g in range(ng)]
    for g in range(ng):
        in_specs += [cur(CB_BK + PAIRS * g), prv(CB_BK + PAIRS * g)]
    in_specs += [cur(CB_BV), prv(CB_BV)]
    acc = pltpu.VMEM((ng, tile, LANES), F32)
    return pl.pallas_call(
        functools.partial(_band_kernel, tile=tile),
        grid=(bsz, PAIRS, seq // tile),
        in_specs=in_specs,
        out_specs=pl.BlockSpec((1, tile, LANES), lambda b, p, j: (b, j, p)),
        out_shape=jax.ShapeDtypeStruct((bsz, seq, WIDTH), BF16),
        scratch_shapes=[acc, pltpu.VMEM((ng, 2 * tile, LANES), F32), pltpu.VMEM((2 * tile, LANES), F32), acc, acc, acc],
        compiler_params=_cparams(("parallel", "parallel", "arbitrary")),
        name="band",
    )(*([hb] * len(in_specs)))


def _dot01(a, b, nt=False, pieces=3):
    f = _dot_nt if nt else _dot
    if a.dtype == BF16:
        return sum(f(a, piece.astype(BF16)) for piece in _split3(b)[:pieces])
    return sum(f(piece.astype(BF16), b) for piece in _split3(a)[:pieces])


def _tri_inverse_all(ms):
    c = ms[0].shape[0]
    ri = _iota((c, c), 0)
    ci = _iota((c, c), 1)
    base = 16
    inblk = ri // base == ci // base
    eye = jnp.where(ri == ci, 1.0, 0.0)
    ps = [jnp.where(inblk, -m, 0.0) for m in ms]
    ts = [eye + p for p in ps]
    for _ in range(3):
        pbs = [p.astype(BF16) for p in ps]
        ps = [_dot(pb, pb) for pb in pbs]
        ts = [t + _dot(t.astype(BF16), p.astype(BF16)) for t, p in zip(ts, ps)]
    size = base
    while size < c:
        lower = jnp.logical_and(ri // (2 * size) == ci // (2 * size), ri // size != ci // size)
        tbs = [t.astype(BF16) for t in ts]
        xs = [_dot(tb, jnp.where(lower, m, 0.0).astype(BF16)) for tb, m in zip(tbs, ms)]
        ts = [t - _dot(x.astype(BF16), tb) for t, x, tb in zip(ts, xs, tbs)]
        size *= 2
    return ts


def _gdn_kernel(h_ref, cw_ref, alog_ref, dtb_ref, nw_ref, o_ref,
                xe_ref, u_s, w_s, qd_s, kd_s, qk_s, egl_s, o_s, st_ref, *, tile):
    cc = GDN_CHUNK
    nc = tile // cc
    t = pl.program_id(1)

    @pl.when(t == 0)
    def _():
        xe_ref[0:8, :] = jnp.zeros((8, 3 * WIDTH), F32)
        st_ref[...] = jnp.zeros_like(st_ref)

    x = h_ref[0, :, LANES:]
    xe_ref[8:8 + tile, :] = x
    y = jnp.zeros((tile, 3 * WIDTH), F32)
    for j in range(CONV_K):
        y = y + cw_ref[j:j + 1, :] * xe_ref[8 - (CONV_K - 1) + j:8 - (CONV_K - 1) + j + tile, :]
    xe_ref[0:8, :] = x[tile - 8:tile, :]
    y = y * jax.nn.sigmoid(y)

    bd = ((_iota((WIDTH, WIDTH), 0) // HEAD_DIM) == (_iota((WIDTH, WIDTH), 1) // HEAD_DIM)).astype(BF16)
    q = y[:, 0:WIDTH]
    k = y[:, WIDTH:2 * WIDTH]
    v = y[:, 2 * WIDTH:]
    q = q * lax.rsqrt(_dot01(q * q, bd, pieces=2) + RMS_EPS) * (HEAD_DIM ** -0.5)
    k = k * lax.rsqrt(_dot01(k * k, bd, pieces=2) + RMS_EPS)

    hs = h_ref[0, :, 0:LANES]
    er = _iota((LANES, WIDTH), 0)
    ec = _iota((LANES, WIDTH), 1) // HEAD_DIM
    beta = jax.nn.sigmoid(_dot01(hs, (er == ec + HEADS).astype(BF16)))
    g = -jnp.exp(alog_ref[...]) * jax.nn.softplus(_dot01(hs, (er == ec + 2 * HEADS).astype(BF16)) + dtb_ref[...])

    tr = _iota((tile, tile), 0)
    tc = _iota((tile, tile), 1)
    tri = jnp.logical_and(tr // cc == tc // cc, tr >= tc).astype(BF16)
    gc = _dot01(tri, g)
    glast = jnp.broadcast_to(gc.reshape(nc, cc, WIDTH)[:, cc - 1:cc, :], (nc, cc, WIDTH)).reshape(tile, WIDTH)
    eg = jnp.exp(gc)
    kb = k * beta
    vb = v * beta
    wb = kb * eg
    qd_s[...] = q * eg
    kd_s[...] = k * jnp.exp(glast - gc)
    egl_s[...] = jnp.exp(glast)

    ri = _iota((cc, cc), 0)
    ci = _iota((cc, cc), 1)
    incl = ri >= ci
    strict = ri > ci
    lane = _iota((cc, LANES), 1)
    hm = lane < HEAD_DIM
    sls = [slice(p * LANES, (p + 1) * LANES) for p in range(PAIRS)]
    keeps = [hm, jnp.logical_not(hm)]
    heads = [(p, h) for p in range(PAIRS) for h in range(2)]
    chains = [(p, h, c) for p, h in heads for c in range(nc)]
    rws = [slice(c * cc, (c + 1) * cc) for c in range(nc)]
    kps = [k[:, sl].astype(BF16) for sl in sls]
    rhss = [jnp.concatenate([vb[:, sl], wb[:, sl]], axis=1).astype(BF16) for sl in sls]
    rowf = {(p, h): _dot01((lane == h * HEAD_DIM).astype(BF16), gc[:, sls[p]], nt=True) for p, h in heads}
    kbm = {(p, h): jnp.where(jnp.tile(keeps[h], (nc, 1)), kb[:, sls[p]], 0.0).astype(BF16) for p, h in heads}
    qm = {(p, h): jnp.where(jnp.tile(keeps[h], (nc, 1)), q[:, sls[p]], 0.0).astype(BF16) for p, h in heads}
    kks = [_dot_nt(kbm[p, h][rws[c]], kps[p][rws[c]]) for p, h, c in chains]
    qks = [_dot_nt(qm[p, h][rws[c]], kps[p][rws[c]]) for p, h, c in chains]
    mms = []
    for (p, h, c), kk, qk in zip(chains, kks, qks):
        col = p * LANES + h * HEAD_DIM
        diff = gc[rws[c], col:col + 1] - rowf[p, h][:, rws[c]]
        decay = jnp.where(incl, jnp.exp(jnp.where(incl, diff, 0.0)), 0.0)
        mms.append(jnp.where(strict, kk * decay, 0.0))
        qk_s[2 * p + h, rws[c], :] = qk * decay
    tinvs = _tri_inverse_all(mms)
    rs = {ch: _dot(tinv.astype(BF16), rhss[ch[0]][rws[ch[2]]]) for ch, tinv in zip(chains, tinvs)}
    for p in range(PAIRS):
        for c in range(nc):
            r0, r1 = rs[p, 0, c], rs[p, 1, c]
            u_s[rws[c], sls[p]] = jnp.where(hm, r0[:, :LANES], r1[:, :LANES])
            w_s[rws[c], sls[p]] = jnp.where(hm, r0[:, LANES:], r1[:, LANES:])

    bdiag = (_iota((LANES, LANES), 0) // HEAD_DIM) == (_iota((LANES, LANES), 1) // HEAD_DIM)
    for c in range(nc):
        rows = rws[c]
        sts = [st_ref[p] for p in range(PAIRS)]
        stbs = [st.astype(BF16) for st in sts]
        wss = [_dot(w_s[rows, sl].astype(BF16), stb) for sl, stb in zip(sls, stbs)]
        qss = [_dot(qd_s[rows, sl].astype(BF16), stb) for sl, stb in zip(sls, stbs)]
        vnbs = [(u_s[rows, sl] - ws).astype(BF16) for sl, ws in zip(sls, wss)]
        upds = [lax.dot_general(kd_s[rows, sl].astype(BF16), vnb, (((0,), (0,)), ((), ())), preferred_element_type=F32)
                for sl, vnb in zip(sls, vnbs)]
        intra = [(_dot(qk_s[2 * p, rows, :].astype(BF16), vnbs[p]), _dot(qk_s[2 * p + 1, rows, :].astype(BF16), vnbs[p]))
                 for p in range(PAIRS)]
        for p in range(PAIRS):
            st_ref[p] = sts[p] * egl_s[c * cc:c * cc + 1, sls[p]] + jnp.where(bdiag, upds[p], 0.0)
            o_s[rows, sls[p]] = qss[p] + jnp.where(hm, intra[p][0], intra[p][1])

    o = o_s[...]
    ms = _dot01(o * o, bd, pieces=2) * (1.0 / HEAD_DIM)
    o_ref[0] = (o * lax.rsqrt(ms + RMS_EPS) * nw_ref[...]).astype(o_ref.dtype)


def _gdn(hf, conv_w, a_log, dt_bias, norm_w, tile=512):
    bsz, seq, _ = hf.shape
    rep = lambda a: jnp.repeat(a.astype(F32), HEAD_DIM)[None, :]
    wide = pltpu.VMEM((tile, WIDTH), F32)
    return pl.pallas_call(
        functools.partial(_gdn_kernel, tile=tile),
        grid=(bsz, seq // tile),
        in_specs=[
            pl.BlockSpec((1, tile, HF_COLS), lambda b, t: (b, t, 0)),
            pl.BlockSpec((CONV_K, 3 * WIDTH), lambda b, t: (0, 0)),
            pl.BlockSpec((1, WIDTH), lambda b, t: (0, 0)),
            pl.BlockSpec((1, WIDTH), lambda b, t: (0, 0)),
            pl.BlockSpec((1, WIDTH), lambda b, t: (0, 0)),
        ],
        out_specs=pl.BlockSpec((1, tile, WIDTH), lambda b, t: (b, t, 0)),
        out_shape=jax.ShapeDtypeStruct((bsz, seq, WIDTH), BF16),
        scratch_shapes=[pltpu.VMEM((tile + 8, 3 * WIDTH), F32), wide, wide, wide, wide,
                        pltpu.VMEM((HEADS, tile, LANES), F32), wide, wide,
                        pltpu.VMEM((PAIRS, LANES, LANES), F32)],
        compiler_params=_cparams(("parallel", "arbitrary")),
        name="gdn",
    )(hf, conv_w.astype(F32), rep(a_log), rep(dt_bias), jnp.tile(norm_w.astype(F32), HEADS)[None, :])


def _memattn_kernel(q_ref, kv_ref, o_ref):
    tq = q_ref.shape[1]
    q = q_ref[0]
    k = kv_ref[0, :, 0:WIDTH]
    v = kv_ref[0, :, WIDTH:2 * WIDTH]
    head = _iota((tq, WIDTH), 1) // MEM_HEAD_DIM
    zero = jnp.zeros_like(q)
    out = jnp.zeros((tq, WIDTH), F32)
    for h in range(MEM_HEADS):
        s = _dot_nt(jnp.where(head == h, q, zero), k)
        m = jnp.max(s, axis=-1, keepdims=True)
        p = jnp.exp(s - m)
        o = _dot(p.astype(BF16), v) / jnp.sum(p, axis=-1, keepdims=True)
        out = jnp.where(head == h, o, out)
    o_ref[0] = out.astype(o_ref.dtype)


def _memattn(hb, kv, layer, tq=512):
    bsz, seq, _ = hb.shape
    mlen = kv.shape[1]
    return pl.pallas_call(
        _memattn_kernel,
        grid=(bsz, seq // tq),
        in_specs=[
            pl.BlockSpec((1, tq, WIDTH), lambda b, i: (b, i, CB_E // PAIRS)),
            pl.BlockSpec((1, mlen, 2 * WIDTH), lambda b, i: (b, 0, layer)),
        ],
        out_specs=pl.BlockSpec((1, tq, WIDTH), lambda b, i: (b, i, 0)),
        out_shape=jax.ShapeDtypeStruct((bsz, seq, WIDTH), BF16),
        compiler_params=_cparams(("parallel", "parallel")),
        name="memattn",
    )(hb, kv)


def _out_kernel(oa, ob, oc, od, oe, z_ref, ml_ref, x_ref, wb_ref, wo_ref, g_ref, b_ref, y_ref, yb_ref, *, alpha):
    tm = x_ref.shape[0]
    acc = jnp.zeros((tm, D_MODEL), F32)
    for n, o_ref in enumerate((oa, ob, oc, od, oe)):
        z = z_ref[:, n * WIDTH:(n + 1) * WIDTH].astype(F32)
        gated = o_ref[...].astype(F32) * (z * jax.nn.sigmoid(z))
        yn = _dot(gated.astype(BF16), wb_ref[n])
        acc = acc + jax.nn.sigmoid(ml_ref[:, n * D_MODEL:(n + 1) * D_MODEL].astype(F32)) * yn
    r = alpha * x_ref[...] + _dot(acc.astype(BF16), wo_ref[...])
    mu = jnp.mean(r, axis=-1, keepdims=True)
    rc = r - mu
    var = jnp.mean(rc * rc, axis=-1, keepdims=True)
    y = rc * lax.rsqrt(var + LN_EPS) * g_ref[...] + b_ref[...]
    y_ref[...] = y
    yb_ref[...] = y.astype(BF16)


def _out(branches, z, ml, x, w_branch, w_out, ln_g, ln_b, alpha, tm=256):
    m = x.shape[0]
    row = lambda n: pl.BlockSpec((tm, n), lambda i: (i, 0))
    return pl.pallas_call(
        functools.partial(_out_kernel, alpha=alpha),
        grid=(m // tm,),
        in_specs=[row(WIDTH)] * N_BRANCH + [row(N_BRANCH * WIDTH), row(N_BRANCH * D_MODEL), row(D_MODEL),
                  pl.BlockSpec((N_BRANCH, WIDTH, D_MODEL), lambda i: (0, 0, 0)),
                  pl.BlockSpec((D_MODEL, D_MODEL), lambda i: (0, 0)),
                  pl.BlockSpec((1, D_MODEL), lambda i: (0, 0)),
                  pl.BlockSpec((1, D_MODEL), lambda i: (0, 0))],
        out_specs=[row(D_MODEL), row(D_MODEL)],
        out_shape=[jax.ShapeDtypeStruct((m, D_MODEL), F32), jax.ShapeDtypeStruct((m, D_MODEL), BF16)],
        compiler_params=_cparams(("parallel",)),
        name="out",
    )(*branches, z, ml, x, w_branch, w_out, ln_g, ln_b)


def _split_weights(w_in, b_in):
    def take(a, spans):
        return a[..., np.concatenate([np.arange(o, o + n) for o, n in spans])]

    hb_cols = ((O_E, WIDTH), (O_A, 3 * WIDTH), (O_BQK, 6 * WIDTH), (O_BV, WIDTH), (O_C, 3 * WIDTH))
    scale = np.ones((HB_COLS,), np.float32)
    scale[CB_E * LANES:CB_E * LANES + WIDTH] = MEM_HEAD_DIM ** -0.5
    for cb in (CB_AQ, CB_CQ):
        scale[cb * LANES:cb * LANES + WIDTH] = HEAD_DIM ** -0.5
    scale[CB_BQ * LANES:CB_BQ * LANES + 3 * WIDTH] = HEAD_DIM ** -0.5
    hf_cols = ((O_CF, HEADS), (O_DBETA, HEADS), (O_DDECAY, HEADS))
    pad = LANES - 3 * HEADS
    depth = w_in.shape[0]
    zw = jnp.zeros((depth, D_MODEL, pad), w_in.dtype)
    w_hf = jnp.concatenate([take(w_in, hf_cols), zw, w_in[..., O_D:O_D + 3 * WIDTH]], axis=-1)
    b_hf = jnp.concatenate([take(b_in, hf_cols), zw[:, 0], b_in[..., O_D:O_D + 3 * WIDTH]], axis=-1)
    groups = {
        "hb": (take(w_in, hb_cols), take(b_in, hb_cols), jnp.asarray(scale)),
        "hf": (w_hf, b_hf, None),
        "z": (w_in[..., O_Z:O_Z + N_BRANCH * WIDTH], b_in[..., O_Z:O_Z + N_BRANCH * WIDTH], None),
        "ml": (w_in[..., O_MERGE:], b_in[..., O_MERGE:], None),
    }
    out = {}
    for name, (w, b, s) in groups.items():
        s = jnp.ones((w.shape[-1],), F32) if s is None else s
        out[name] = (w.astype(F32), b.astype(F32)[:, None, :], s[None, :])
    return out


def _layer(x, xb, kv, layer, gw, conv_w, a_log, dt_bias, gdn_norm_w, w_branch, w_out, ln_g, ln_b, alpha):
    bsz, seq, d = x.shape
    m = bsz * seq
    xf = x.reshape(m, d)
    hb = _proj(xb, *gw["hb"], layer, BF16, 1024, 768).reshape(bsz, seq, HB_COLS)
    hf = _proj(xb, *gw["hf"], layer, F32, 1024, 640).reshape(bsz, seq, HF_COLS)
    z = _proj(xb, *gw["z"], layer, BF16, 1024, 640)
    ml = _proj(xb, *gw["ml"], layer, BF16, 1024, 1024)

    o_a = _moba(hb)
    o_b = _dilated(hb)
    o_c = _fox(hb, _fcum(hf))
    o_d = _gdn(hf, conv_w, a_log, dt_bias, gdn_norm_w)
    o_e = _memattn(hb, kv, layer)
    branches = [o.reshape(m, WIDTH) for o in (o_a, o_b, o_c, o_d, o_e)]
    y, yb = _out(branches, z, ml, xf, w_branch.astype(BF16), w_out.astype(BF16),
                 ln_g.astype(F32)[None, :], ln_b.astype(F32)[None, :], alpha)
    return y.reshape(bsz, seq, d), yb


def kernel(x, mem, mem_ln_g, mem_ln_b, w_in, b_in, conv_w, a_log, dt_bias, gdn_norm_w, w_mem_kv, w_branch, w_out, ln_g, ln_b):
    depth = w_in.shape[0]
    alpha = float((2 * depth) ** 0.25)
    w_kv = jnp.concatenate([w_mem_kv[l] for l in range(depth)], axis=1).astype(BF16)
    kv = _memkv(mem.astype(F32), mem_ln_g.astype(F32)[None, :], mem_ln_b.astype(F32)[None, :], w_kv)
    x = x.astype(F32)
    xb = x.reshape(-1, x.shape[-1]).astype(BF16)
    gw = _split_weights(w_in, b_in)
    for l in range(depth):
        x, xb = _layer(x, xb, kv, l, gw, conv_w[l], a_log[l], dt_bias[l], gdn_norm_w[l],
                       w_branch[l], w_out[l], ln_g[l], ln_b[l], alpha)
    return x
```

```python
import functools

import jax
import jax.numpy as jnp
import numpy as np
from jax import lax
from jax.experimental import pallas as pl
from jax.experimental.pallas import tpu as pltpu

F32 = jnp.float32
BF16 = jnp.bfloat16
HI = lax.Precision.HIGHEST

D_MODEL = 1024
HEAD_DIM = 64
HEADS = 6
WIDTH = HEADS * HEAD_DIM
N_BRANCH = 5
LANES = 128
PAIRS = WIDTH // LANES
MOBA_BLOCK = 256
MOBA_TOPK = 3
DILATIONS = (1, 4, 16)
BAND = 128
BAND_TILE = 2048
BAND_GROUP = 4
GDN_CHUNK = 128
CONV_K = 4
MEM_HEADS = 4
MEM_HEAD_DIM = WIDTH // MEM_HEADS
NEG = -1e30
FLASH_UNDERFLOW = 110.0
LN_EPS = 1e-5
RMS_EPS = 1e-6

_SPLIT = (3 * WIDTH, 6 * WIDTH, WIDTH, 3 * WIDTH, HEADS, 3 * WIDTH, HEADS, HEADS, WIDTH, N_BRANCH * WIDTH, N_BRANCH * D_MODEL)
_OFF = tuple(int(v) for v in np.concatenate([[0], np.cumsum(_SPLIT)]))
(O_A, O_BQK, O_BV, O_C, O_CF, O_D, O_DBETA, O_DDECAY, O_E, O_Z, O_MERGE, _) = _OFF

CB_E = 0
CB_AQ, CB_AK, CB_AV = 3, 6, 9
CB_BQ, CB_BK, CB_BV = 12, 21, 30
CB_CQ, CB_CK, CB_CV = 33, 36, 39
HB_COLS = 42 * LANES
HF_COLS = LANES + 3 * WIDTH

VMEM_LIMIT = 56 * 1024 * 1024


def _cparams(sem):
    return pltpu.CompilerParams(dimension_semantics=sem, vmem_limit_bytes=VMEM_LIMIT)


def _dot(a, b):
    return jnp.dot(a, b, preferred_element_type=F32)


def _dot_nt(a, b):
    return lax.dot_general(a, b, (((1,), (1,)), ((), ())), preferred_element_type=F32)


def _dot_hi(a, b):
    return jnp.dot(a, b, preferred_element_type=F32, precision=HI)


def _dot_nt_hi(a, b):
    return lax.dot_general(a, b, (((1,), (1,)), ((), ())), preferred_element_type=F32, precision=HI)


def _iota(shape, dim):
    return lax.broadcasted_iota(jnp.int32, shape, dim)


def _proj_kernel(x_ref, w_ref, b_ref, s_ref, o_ref, wb_ref):
    @pl.when(pl.program_id(1) == 0)
    def _():
        wb_ref[...] = w_ref[...].astype(BF16)

    acc = _dot(x_ref[...], wb_ref[...])
    o_ref[...] = ((acc + b_ref[...]) * s_ref[...]).astype(o_ref.dtype)


def _proj(xb, w, b, scale, layer, out_dtype, tm, tn):
    m, k = xb.shape
    n = w.shape[-1]
    assert m % tm == 0 and n % tn == 0
    return pl.pallas_call(
        _proj_kernel,
        grid=(n // tn, m // tm),
        in_specs=[
            pl.BlockSpec((tm, k), lambda j, i: (i, 0)),
            pl.BlockSpec((None, k, tn), lambda j, i: (layer, 0, j)),
            pl.BlockSpec((None, 1, tn), lambda j, i: (layer, 0, j)),
            pl.BlockSpec((1, tn), lambda j, i: (0, j)),
        ],
        out_specs=pl.BlockSpec((tm, tn), lambda j, i: (i, j)),
        out_shape=jax.ShapeDtypeStruct((m, n), out_dtype),
        scratch_shapes=[pltpu.VMEM((k, tn), BF16)],
        compiler_params=_cparams(("parallel", "arbitrary")),
        name="proj",
    )(xb, w, b, scale)


def _memkv_kernel(mem_ref, g_ref, b_ref, w_ref, o_ref):
    x = mem_ref[0]
    mu = jnp.mean(x, axis=-1, keepdims=True)
    xc = x - mu
    var = jnp.mean(xc * xc, axis=-1, keepdims=True)
    y = xc * lax.rsqrt(var + LN_EPS) * g_ref[...] + b_ref[...]
    o_ref[0] = _dot(y.astype(BF16), w_ref[...]).astype(o_ref.dtype)


def _memkv(mem, g, b, w):
    bsz, mlen, d = mem.shape
    n = w.shape[1]
    return pl.pallas_call(
        _memkv_kernel,
        grid=(bsz,),
        in_specs=[
            pl.BlockSpec((1, mlen, d), lambda i: (i, 0, 0)),
            pl.BlockSpec((1, d), lambda i: (0, 0)),
            pl.BlockSpec((1, d), lambda i: (0, 0)),
            pl.BlockSpec((d, n), lambda i: (0, 0)),
        ],
        out_specs=pl.BlockSpec((1, mlen, n), lambda i: (i, 0, 0)),
        out_shape=jax.ShapeDtypeStruct((bsz, mlen, n), BF16),
        compiler_params=_cparams(("parallel",)),
        name="memkv",
    )(mem, g, b, w)


def _fcum_kernel(h_ref, o_ref, *, blk):
    seq = h_ref.shape[1]
    tri = (_iota((blk, blk), 0) >= _iota((blk, blk), 1)).astype(F32)

    def body(i, carry):
        rows = pl.ds(pl.multiple_of(i * blk, blk), blk)
        logf = jax.nn.log_sigmoid(h_ref[0, rows, :])
        c = _dot_hi(tri, logf) + carry
        o_ref[0, rows, :] = c
        return c[blk - 1:blk, :]

    lax.fori_loop(0, seq // blk, body, jnp.zeros((1, LANES), F32))


def _fcum(hf):
    bsz, seq, _ = hf.shape
    return pl.pallas_call(
        functools.partial(_fcum_kernel, blk=LANES),
        grid=(bsz,),
        in_specs=[pl.BlockSpec((1, seq, LANES), lambda b: (b, 0, 0))],
        out_specs=pl.BlockSpec((1, seq, LANES), lambda b: (b, 0, 0)),
        out_shape=jax.ShapeDtypeStruct((bsz, seq, LANES), F32),
        compiler_params=_cparams(("parallel",)),
        name="fcum",
    )(hf)


def _head_masks(rows):
    lane = _iota((rows, LANES), 1)
    return lane < HEAD_DIM


def _split3(x):
    hi = x.astype(BF16).astype(F32)
    r = x - hi
    mid = r.astype(BF16).astype(F32)
    return hi, mid, r - mid


def _flash_scratch(tq):
    return [pltpu.VMEM((2 * tq, LANES), F32), pltpu.VMEM((2 * tq, tq), F32), pltpu.VMEM((2 * tq, tq), F32),
            pltpu.VMEM((2 * tq, 1), F32), pltpu.VMEM((2 * tq, 1), F32)]


def _flash_causal(qx, kx_ref, v_ref, scratch, i, tq, past_tiles=None):
    acc_ref, sa_ref, sb_ref, m_ref, l_ref = scratch

    def logits(kt, dst):
        rows = pl.ds(pl.multiple_of(kt * tq, tq), tq)
        dst[...] = _dot_nt(qx, kx_ref[rows, :])

    r = _iota((2 * tq, tq), 0)
    causal = jnp.where(r >= tq, r - tq, r) >= _iota((2 * tq, tq), 1)

    def finish(src, kt, masked=False):
        rows = pl.ds(pl.multiple_of(kt * tq, tq), tq)
        s = src[...]
        if masked:
            s = jnp.where(causal, s, NEG)
        m = m_ref[...]
        m_new = jnp.maximum(m, jnp.max(s, axis=-1, keepdims=True))
        alpha = jnp.exp(m - m_new)
        p = jnp.exp((s - m_new).astype(BF16))
        l_ref[...] = alpha * l_ref[...] + jnp.sum(p.astype(F32), axis=-1, keepdims=True)
        m_ref[...] = m_new
        acc_ref[...] = alpha * acc_ref[...] + _dot(p, v_ref[0, rows, :])

    acc_ref[...] = jnp.zeros_like(acc_ref)
    m_ref[...] = jnp.full((2 * tq, 1), NEG, F32)
    l_ref[...] = jnp.zeros((2 * tq, 1), F32)
    tile = lambda u: jnp.maximum(i - 1 - u, 0)
    logits(i, sa_ref)
    logits(tile(0), sb_ref)
    finish(sa_ref, i, masked=True)
    n = i if past_tiles is None else past_tiles(m_ref)

    def body(v, carry):
        logits(tile(2 * v + 1), sa_ref)
        finish(sb_ref, tile(2 * v))
        logits(tile(2 * v + 2), sb_ref)
        finish(sa_ref, tile(2 * v + 1))
        return carry

    lax.fori_loop(0, n // 2, body, 0)

    @pl.when(n % 2 == 1)
    def _():
        finish(sb_ref, tile(n - 1))

    o = acc_ref[...] / l_ref[...]
    return jnp.where(_head_masks(tq), o[0:tq], o[tq:])


def _head_sumsq(x):
    hm = _head_masks(x.shape[0])
    sq = x * x
    return (jnp.sum(jnp.where(hm, sq, 0.0), axis=1, keepdims=True), jnp.sum(jnp.where(hm, 0.0, sq), axis=1, keepdims=True))


def _fox_kernel(q_ref, k_ref, v_ref, f_ref, o_ref, kx_ref, kn_ref, *flash, tq):
    p = pl.program_id(1)
    i = pl.program_id(2)
    seq = k_ref.shape[1]
    bt = 512

    @pl.when(i == 0)
    def _():
        src = _iota((LANES, LANES), 0)
        dst = _iota((LANES, LANES), 1)
        pm = jnp.logical_or(jnp.logical_and(src == 2 * p, dst < 3),
                            jnp.logical_and(src == 2 * p + 1, jnp.logical_and(dst >= 3, dst < 6))).astype(F32)
        sub = _iota((bt, LANES), 1) % 3

        def build(c, carry):
            rows = pl.ds(pl.multiple_of(c * bt, bt), bt)
            hi, mid, lo = _split3(_dot_hi(f_ref[0, rows, :], pm))
            k = k_ref[0, rows, :]
            kx_ref[rows, 0:LANES] = k
            kx_ref[rows, LANES:] = jnp.where(sub == 0, hi, jnp.where(sub == 1, mid, lo)).astype(BF16)
            return tuple(jnp.maximum(c0, jnp.max(s, axis=0, keepdims=True))
                         for c0, s in zip(carry, _head_sumsq(k.astype(F32))))

        zero = jnp.zeros((1, 1), F32)
        kn = lax.fori_loop(0, seq // bt, build, (zero, zero))
        kn_ref[0:1, :] = jnp.broadcast_to(kn[0], (1, LANES))
        kn_ref[1:2, :] = jnp.broadcast_to(kn[1], (1, LANES))

    q = q_ref[0].astype(F32)
    lane = _iota((tq, LANES), 1)
    hm = lane < HEAD_DIM
    top = jnp.concatenate([jnp.where(hm, q, 0.0), jnp.where(lane < 3, -1.0, 0.0)], axis=1)
    bot = jnp.concatenate([jnp.where(hm, 0.0, q), jnp.where(jnp.logical_and(lane >= 3, lane < 6), -1.0, 0.0)], axis=1)
    qx = jnp.concatenate([top, bot], axis=0).astype(BF16)

    def past_tiles(m_ref):
        nt = seq // tq
        fend = f_ref[0, pl.ds(tq - 1, nt, stride=tq), :]
        lane_t = _iota((nt, LANES), 1)
        tpos = _iota((nt, 1), 0)
        m = m_ref[...]
        need = tpos < 0
        for h, qn2 in enumerate(_head_sumsq(q)):
            m_min = jnp.min(m[h * tq:(h + 1) * tq], axis=0, keepdims=True)
            qk = jnp.sqrt(jnp.max(qn2, axis=0, keepdims=True) * kn_ref[h:h + 1, 0:1]) * 1.01 + 1.0
            f_h = jnp.sum(jnp.where(lane_t == 2 * p + h, fend, 0.0), axis=1, keepdims=True)
            need = jnp.logical_or(need, qk - f_h > m_min - FLASH_UNDERFLOW)
        first = jnp.min(jnp.where(jnp.logical_and(need, tpos < i), tpos, i).astype(F32))
        return i - first.astype(jnp.int32)

    o_ref[0] = _flash_causal(qx, kx_ref, v_ref, flash, i, tq, past_tiles).astype(o_ref.dtype)


def _fox(hb, fcol, tq=256):
    bsz, seq, _ = hb.shape
    assert seq % tq == 0
    return pl.pallas_call(
        functools.partial(_fox_kernel, tq=tq),
        grid=(bsz, PAIRS, seq // tq),
        in_specs=[
            pl.BlockSpec((1, tq, LANES), lambda b, p, i: (b, i, CB_CQ + p)),
            pl.BlockSpec((1, seq, LANES), lambda b, p, i: (b, 0, CB_CK + p)),
            pl.BlockSpec((1, seq, LANES), lambda b, p, i: (b, 0, CB_CV + p)),
            pl.BlockSpec((1, seq, LANES), lambda b, p, i: (b, 0, 0)),
        ],
        out_specs=pl.BlockSpec((1, tq, LANES), lambda b, p, i: (b, i, p)),
        out_shape=jax.ShapeDtypeStruct((bsz, seq, WIDTH), BF16),
        scratch_shapes=[pltpu.VMEM((seq, 2 * LANES), BF16), pltpu.VMEM((8, LANES), F32)] + _flash_scratch(tq),
        compiler_params=_cparams(("parallel", "parallel", "arbitrary")),
        name="fox",
    )(hb, hb, hb, fcol)


def _moba_kernel(q_ref, k_ref, v_ref, o_ref, kx_ref, kmean_ref, *flash, tq):
    i = pl.program_id(2)
    seq = k_ref.shape[1]
    blk = MOBA_BLOCK

    @pl.when(i == 0)
    def _():
        kmean_ref[...] = jnp.zeros_like(kmean_ref)
        lane = _iota((blk, LANES), 1)

        def build(n, carry):
            rows = pl.ds(pl.multiple_of(n * blk, blk), blk)
            k = k_ref[0, rows, :]
            kx_ref[rows, 0:LANES] = k
            kx_ref[rows, LANES:] = jnp.where(lane == n, 1.0, 0.0).astype(BF16)
            kmean_ref[pl.ds(n, 1), :] = jnp.sum(k.astype(F32), axis=0, keepdims=True) * (1.0 / blk)
            return carry

        lax.fori_loop(0, seq // blk, build, 0)

    q = q_ref[0].astype(F32)
    hm = _head_masks(tq)
    nbp = -(-(seq // blk) // 8) * 8
    blk_id = _iota((nbp, tq), 0)
    blk_f = blk_id.astype(F32)
    own = i * (tq // blk) + _iota((nbp, tq), 1) // blk
    valid = blk_id < own
    kmean = kmean_ref[0:nbp, :]
    halves = []
    for h in range(2):
        qh = jnp.where(hm, q, 0.0) if h == 0 else jnp.where(hm, 0.0, q)
        g = jnp.where(valid, _dot_nt_hi(kmean, qh), NEG)
        bias = jnp.where(blk_id == own, 0.0, NEG)
        for _ in range(MOBA_TOPK):
            mx = jnp.max(g, axis=0, keepdims=True)
            first = jnp.min(jnp.where(g == mx, blk_f, 1e9), axis=0, keepdims=True)
            pick = blk_f == first
            bias = jnp.where(jnp.logical_and(pick, valid), 0.0, bias)
            g = jnp.where(pick, -3e38, g)
        bias = jnp.concatenate([bias, jnp.zeros((LANES - nbp, tq), F32)], axis=0).T
        halves.append(jnp.concatenate([qh, bias], axis=1))
    qx = jnp.concatenate(halves, axis=0).astype(BF16)
    o_ref[0] = _flash_causal(qx, kx_ref, v_ref, flash, i, tq).astype(o_ref.dtype)


def _moba(hb, tq=512):
    bsz, seq, _ = hb.shape
    assert seq % tq == 0 and tq % MOBA_BLOCK == 0 and seq // MOBA_BLOCK <= LANES
    return pl.pallas_call(
        functools.partial(_moba_kernel, tq=tq),
        grid=(bsz, PAIRS, seq // tq),
        in_specs=[
            pl.BlockSpec((1, tq, LANES), lambda b, p, i: (b, i, CB_AQ + p)),
            pl.BlockSpec((1, seq, LANES), lambda b, p, i: (b, 0, CB_AK + p)),
            pl.BlockSpec((1, seq, LANES), lambda b, p, i: (b, 0, CB_AV + p)),
        ],
        out_specs=pl.BlockSpec((1, tq, LANES), lambda b, p, i: (b, i, p)),
        out_shape=jax.ShapeDtypeStruct((bsz, seq, WIDTH), BF16),
        scratch_shapes=[pltpu.VMEM((seq, 2 * LANES), BF16), pltpu.VMEM((LANES, LANES), F32)] + _flash_scratch(tq),
        compiler_params=_cparams(("parallel", "parallel", "arbitrary")),
        name="moba",
    )(hb, hb, hb)


def _band_kernel(q0, q1, q2, k0c, k0p, k1c, k1p, k2c, k2p, vc, vp, o_ref, qb, kb, vb, nb, mb, sb, *, tile):
    jt = pl.program_id(2)
    for g, (qr, kc, kp) in enumerate(((q0, k0c, k0p), (q1, k1c, k1p), (q2, k2c, k2p))):
        qb[g] = qr[0].astype(F32)
        kb[g, 0:tile, :] = kp[0].astype(F32)
        kb[g, tile:, :] = kc[0].astype(F32)
    vb[0:tile, :] = vp[0].astype(F32)
    vb[tile:, :] = vc[0].astype(F32)

    hm = _head_masks(BAND)
    qi = _iota((BAND, 2 * BAND), 0)
    kj = _iota((BAND, 2 * BAND), 1)
    dist = BAND + qi - kj
    band = jnp.logical_and(dist >= 0, dist <= BAND)
    cur_half = kj >= BAND
    nblocks = tile // BAND

    for g, dil in enumerate(DILATIONS):
        per_stream = nblocks // dil

        def step(it, carry, g=g, dil=dil, per_stream=per_stream):
            starts, kstarts, valids = [], [], []
            for j in range(BAND_GROUP):
                idx = it * BAND_GROUP + j
                n = idx % per_stream
                start = idx // per_stream + n * (BAND * dil)
                starts.append(start)
                kstarts.append(tile + start - BAND * dil)
                valids.append(jnp.logical_and(band, jnp.logical_or(cur_half, jnp.logical_or(jt > 0, n > 0))))
            qs = [qb[g, pl.ds(s, BAND, stride=dil), :] for s in starts]
            kks = [kb[g, pl.ds(s, 2 * BAND, stride=dil), :].astype(BF16) for s in kstarts]
            vvs = [vb[pl.ds(s, 2 * BAND, stride=dil), :].astype(BF16) for s in kstarts]
            logits = []
            for q, kk, valid in zip(qs, kks, valids):
                for h in range(2):
                    qh = (jnp.where(hm, q, 0.0) if h == 0 else jnp.where(hm, 0.0, q)).astype(BF16)
                    logits.append(jnp.where(valid, _dot_nt(qh, kk), NEG))
            ms = [jnp.max(s, axis=-1, keepdims=True) for s in logits]
            ps = [jnp.exp(s - m) for s, m in zip(logits, ms)]
            ss = [jnp.sum(p, axis=-1, keepdims=True) for p in ps]
            nums = [_dot(p.astype(BF16), vvs[i // 2]) for i, p in enumerate(ps)]
            for j, s in enumerate(starts):
                rows = pl.ds(s, BAND, stride=dil)
                nb[g, rows, :] = jnp.where(hm, nums[2 * j], nums[2 * j + 1])
                mb[g, rows, :] = jnp.where(hm, ms[2 * j], ms[2 * j + 1])
                sb[g, rows, :] = jnp.where(hm, ss[2 * j], ss[2 * j + 1])
            return carry

        lax.fori_loop(0, nblocks // BAND_GROUP, step, 0)

    ch = 256

    def merge(i, carry):
        rows = pl.ds(pl.multiple_of(i * ch, ch), ch)
        m_all = jnp.maximum(jnp.maximum(mb[0, rows, :], mb[1, rows, :]), mb[2, rows, :])
        num = jnp.zeros((ch, LANES), F32)
        den = jnp.zeros((ch, LANES), F32)
        for g in range(len(DILATIONS)):
            w = jnp.exp(mb[g, rows, :] - m_all)
            num = num + nb[g, rows, :] * w
            den = den + sb[g, rows, :] * w
        o_ref[0, rows, :] = (num / den).astype(o_ref.dtype)
        return carry

    lax.fori_loop(0, tile // ch, merge, 0)


def _dilated(hb, tile=BAND_TILE):
    bsz, seq, _ = hb.shape
    ng = len(DILATIONS)
    nblocks = tile // BAND
    assert seq % tile == 0 and nblocks % BAND_GROUP == 0 and all(nblocks % d == 0 for d in DILATIONS)
    cur = lambda cb: pl.BlockSpec((1, tile, LANES), lambda b, p, j: (b, j, cb + p))
    prv = lambda cb: pl.BlockSpec((1, tile, LANES), lambda b, p, j: (b, jnp.maximum(j - 1, 0), cb + p))
    in_specs = [cur(CB_BQ + PAIRS * g) for g in range(ng)]
    for g in range(ng):
        in_specs += [cur(CB_BK + PAIRS * g), prv(CB_BK + PAIRS * g)]
    in_specs += [cur(CB_BV), prv(CB_BV)]
    acc = pltpu.VMEM((ng, tile, LANES), F32)
    return pl.pallas_call(
        functools.partial(_band_kernel, tile=tile),
        grid=(bsz, PAIRS, seq // tile),
        in_specs=in_specs,
        out_specs=pl.BlockSpec((1, tile, LANES), lambda b, p, j: (b, j, p)),
        out_shape=jax.ShapeDtypeStruct((bsz, seq, WIDTH), BF16),
        scratch_shapes=[acc, pltpu.VMEM((ng, 2 * tile, LANES), F32), pltpu.VMEM((2 * tile, LANES), F32), acc, acc, acc],
        compiler_params=_cparams(("parallel", "parallel", "arbitrary")),
        name="band",
    )(*([hb] * len(in_specs)))


def _dot01(a, b, nt=False, pieces=3):
    f = _dot_nt if nt else _dot
    if a.dtype == BF16:
        return sum(f(a, piece.astype(BF16)) for piece in _split3(b)[:pieces])
    return sum(f(piece.astype(BF16), b) for piece in _split3(a)[:pieces])


def _tri_inverse_all(ms):
    c = ms[0].shape[0]
    ri = _iota((c, c), 0)
    ci = _iota((c, c), 1)
    base = 16
    inblk = ri // base == ci // base
    eye = jnp.where(ri == ci, 1.0, 0.0)
    ps = [jnp.where(inblk, -m, 0.0) for m in ms]
    ts = [eye + p for p in ps]
    for _ in range(3):
        pbs = [p.astype(BF16) for p in ps]
        ps = [_dot(pb, pb) for pb in pbs]
        ts = [t + _dot(t.astype(BF16), p.astype(BF16)) for t, p in zip(ts, ps)]
    size = base
    while size < c:
        lower = jnp.logical_and(ri // (2 * size) == ci // (2 * size), ri // size != ci // size)
        tbs = [t.astype(BF16) for t in ts]
        xs = [_dot(tb, jnp.where(lower, m, 0.0).astype(BF16)) for tb, m in zip(tbs, ms)]
        ts = [t - _dot(x.astype(BF16), tb) for t, x, tb in zip(ts, xs, tbs)]
        size *= 2
    return ts


def _gdn_kernel(h_ref, cw_ref, alog_ref, dtb_ref, nw_ref, o_ref,
                xe_ref, u_s, w_s, qd_s, kd_s, qk_s, egl_s, o_s, st_ref, *, tile):
    cc = GDN_CHUNK
    nc = tile // cc
    t = pl.program_id(1)

    @pl.when(t == 0)
    def _():
        xe_ref[0:8, :] = jnp.zeros((8, 3 * WIDTH), F32)
        st_ref[...] = jnp.zeros_like(st_ref)

    x = h_ref[0, :, LANES:]
    xe_ref[8:8 + tile, :] = x
    y = jnp.zeros((tile, 3 * WIDTH), F32)
    for j in range(CONV_K):
        y = y + cw_ref[j:j + 1, :] * xe_ref[8 - (CONV_K - 1) + j:8 - (CONV_K - 1) + j + tile, :]
    xe_ref[0:8, :] = x[tile - 8:tile, :]
    y = y * jax.nn.sigmoid(y)

    bd = ((_iota((WIDTH, WIDTH), 0) // HEAD_DIM) == (_iota((WIDTH, WIDTH), 1) // HEAD_DIM)).astype(BF16)
    q = y[:, 0:WIDTH]
    k = y[:, WIDTH:2 * WIDTH]
    v = y[:, 2 * WIDTH:]
    q = q * lax.rsqrt(_dot01(q * q, bd, pieces=2) + RMS_EPS) * (HEAD_DIM ** -0.5)
    k = k * lax.rsqrt(_dot01(k * k, bd, pieces=2) + RMS_EPS)

    hs = h_ref[0, :, 0:LANES]
    er = _iota((LANES, WIDTH), 0)
    ec = _iota((LANES, WIDTH), 1) // HEAD_DIM
    beta = jax.nn.sigmoid(_dot01(hs, (er == ec + HEADS).astype(BF16)))
    g = -jnp.exp(alog_ref[...]) * jax.nn.softplus(_dot01(hs, (er == ec + 2 * HEADS).astype(BF16)) + dtb_ref[...])

    tr = _iota((tile, tile), 0)
    tc = _iota((tile, tile), 1)
    tri = jnp.logical_and(tr // cc == tc // cc, tr >= tc).astype(BF16)
    gc = _dot01(tri, g)
    glast = jnp.broadcast_to(gc.reshape(nc, cc, WIDTH)[:, cc - 1:cc, :], (nc, cc, WIDTH)).reshape(tile, WIDTH)
    eg = jnp.exp(gc)
    kb = k * beta
    vb = v * beta
    wb = kb * eg
    qd_s[...] = q * eg
    kd_s[...] = k * jnp.exp(glast - gc)
    egl_s[...] = jnp.exp(glast)

    ri = _iota((cc, cc), 0)
    ci = _iota((cc, cc), 1)
    incl = ri >= ci
    strict = ri > ci
    lane = _iota((cc, LANES), 1)
    hm = lane < HEAD_DIM
    sls = [slice(p * LANES, (p + 1) * LANES) for p in range(PAIRS)]
    keeps = [hm, jnp.logical_not(hm)]
    heads = [(p, h) for p in range(PAIRS) for h in range(2)]
    chains = [(p, h, c) for p, h in heads for c in range(nc)]
    rws = [slice(c * cc, (c + 1) * cc) for c in range(nc)]
    kps = [k[:, sl].astype(BF16) for sl in sls]
    rhss = [jnp.concatenate([vb[:, sl], wb[:, sl]], axis=1).astype(BF16) for sl in sls]
    rowf = {(p, h): _dot01((lane == h * HEAD_DIM).astype(BF16), gc[:, sls[p]], nt=True) for p, h in heads}
    kbm = {(p, h): jnp.where(jnp.tile(keeps[h], (nc, 1)), kb[:, sls[p]], 0.0).astype(BF16) for p, h in heads}
    qm = {(p, h): jnp.where(jnp.tile(keeps[h], (nc, 1)), q[:, sls[p]], 0.0).astype(BF16) for p, h in heads}
    kks = [_dot_nt(kbm[p, h][rws[c]], kps[p][rws[c]]) for p, h, c in chains]
    qks = [_dot_nt(qm[p, h][rws[c]], kps[p][rws[c]]) for p, h, c in chains]
    mms = []
    for (p, h, c), kk, qk in zip(chains, kks, qks):
        col = p * LANES + h * HEAD_DIM
        diff = gc[rws[c], col:col + 1] - rowf[p, h][:, rws[c]]
        decay = jnp.where(incl, jnp.exp(jnp.where(incl, diff, 0.0)), 0.0)
        mms.append(jnp.where(strict, kk * decay, 0.0))
        qk_s[2 * p + h, rws[c], :] = qk * decay
    tinvs = _tri_inverse_all(mms)
    rs = {ch: _dot(tinv.astype(BF16), rhss[ch[0]][rws[ch[2]]]) for ch, tinv in zip(chains, tinvs)}
    for p in range(PAIRS):
        for c in range(nc):
            r0, r1 = rs[p, 0, c], rs[p, 1, c]
            u_s[rws[c], sls[p]] = jnp.where(hm, r0[:, :LANES], r1[:, :LANES])
            w_s[rws[c], sls[p]] = jnp.where(hm, r0[:, LANES:], r1[:, LANES:])

    bdiag = (_iota((LANES, LANES), 0) // HEAD_DIM) == (_iota((LANES, LANES), 1) // HEAD_DIM)
    for c in range(nc):
        rows = rws[c]
        sts = [st_ref[p] for p in range(PAIRS)]
        stbs = [st.astype(BF16) for st in sts]
        wss = [_dot(w_s[rows, sl].astype(BF16), stb) for sl, stb in zip(sls, stbs)]
        qss = [_dot(qd_s[rows, sl].astype(BF16), stb) for sl, stb in zip(sls, stbs)]
        vnbs = [(u_s[rows, sl] - ws).astype(BF16) for sl, ws in zip(sls, wss)]
        upds = [lax.dot_general(kd_s[rows, sl].astype(BF16), vnb, (((0,), (0,)), ((), ())), preferred_element_type=F32)
                for sl, vnb in zip(sls, vnbs)]
        intra = [(_dot(qk_s[2 * p, rows, :].astype(BF16), vnbs[p]), _dot(qk_s[2 * p + 1, rows, :].astype(BF16), vnbs[p]))
                 for p in range(PAIRS)]
        for p in range(PAIRS):
            st_ref[p] = sts[p] * egl_s[c * cc:c * cc + 1, sls[p]] + jnp.where(bdiag, upds[p], 0.0)
            o_s[rows, sls[p]] = qss[p] + jnp.where(hm, intra[p][0], intra[p][1])

    o = o_s[...]
    ms = _dot01(o * o, bd, pieces=2) * (1.0 / HEAD_DIM)
    o_ref[0] = (o * lax.rsqrt(ms + RMS_EPS) * nw_ref[...]).astype(o_ref.dtype)


def _gdn(hf, conv_w, a_log, dt_bias, norm_w, tile=512):
    bsz, seq, _ = hf.shape
    rep = lambda a: jnp.repeat(a.astype(F32), HEAD_DIM)[None, :]
    wide = pltpu.VMEM((tile, WIDTH), F32)
    return pl.pallas_call(
        functools.partial(_gdn_kernel, tile=tile),
        grid=(bsz, seq // tile),
        in_specs=[
            pl.BlockSpec((1, tile, HF_COLS), lambda b, t: (b, t, 0)),
            pl.BlockSpec((CONV_K, 3 * WIDTH), lambda b, t: (0, 0)),
            pl.BlockSpec((1, WIDTH), lambda b, t: (0, 0)),
            pl.BlockSpec((1, WIDTH), lambda b, t: (0, 0)),
            pl.BlockSpec((1, WIDTH), lambda b, t: (0, 0)),
        ],
        out_specs=pl.BlockSpec((1, tile, WIDTH), lambda b, t: (b, t, 0)),
        out_shape=jax.ShapeDtypeStruct((bsz, seq, WIDTH), BF16),
        scratch_shapes=[pltpu.VMEM((tile + 8, 3 * WIDTH), F32), wide, wide, wide, wide,
                        pltpu.VMEM((HEADS, tile, LANES), F32), wide, wide,
                        pltpu.VMEM((PAIRS, LANES, LANES), F32)],
        compiler_params=_cparams(("parallel", "arbitrary")),
        name="gdn",
    )(hf, conv_w.astype(F32), rep(a_log), rep(dt_bias), jnp.tile(norm_w.astype(F32), HEADS)[None, :])


def _memattn_kernel(q_ref, kv_ref, o_ref):
    tq = q_ref.shape[1]
    q = q_ref[0]
    k = kv_ref[0, :, 0:WIDTH]
    v = kv_ref[0, :, WIDTH:2 * WIDTH]
    head = _iota((tq, WIDTH), 1) // MEM_HEAD_DIM
    zero = jnp.zeros_like(q)
    out = jnp.zeros((tq, WIDTH), F32)
    for h in range(MEM_HEADS):
        s = _dot_nt(jnp.where(head == h, q, zero), k)
        m = jnp.max(s, axis=-1, keepdims=True)
        p = jnp.exp(s - m)
        o = _dot(p.astype(BF16), v) / jnp.sum(p, axis=-1, keepdims=True)
        out = jnp.where(head == h, o, out)
    o_ref[0] = out.astype(o_ref.dtype)


def _memattn(hb, kv, layer, tq=512):
    bsz, seq, _ = hb.shape
    mlen = kv.shape[1]
    return pl.pallas_call(
        _memattn_kernel,
        grid=(bsz, seq // tq),
        in_specs=[
            pl.BlockSpec((1, tq, WIDTH), lambda b, i: (b, i, CB_E // PAIRS)),
            pl.BlockSpec((1, mlen, 2 * WIDTH), lambda b, i: (b, 0, layer)),
        ],
        out_specs=pl.BlockSpec((1, tq, WIDTH), lambda b, i: (b, i, 0)),
        out_shape=jax.ShapeDtypeStruct((bsz, seq, WIDTH), BF16),
        compiler_params=_cparams(("parallel", "parallel")),
        name="memattn",
    )(hb, kv)


def _out_kernel(oa, ob, oc, od, oe, z_ref, ml_ref, x_ref, wb_ref, wo_ref, g_ref, b_ref, y_ref, yb_ref, *, alpha):
    tm = x_ref.shape[0]
    acc = jnp.zeros((tm, D_MODEL), F32)
    for n, o_ref in enumerate((oa, ob, oc, od, oe)):
        z = z_ref[:, n * WIDTH:(n + 1) * WIDTH].astype(F32)
        gated = o_ref[...].astype(F32) * (z * jax.nn.sigmoid(z))
        yn = _dot(gated.astype(BF16), wb_ref[n])
        acc = acc + jax.nn.sigmoid(ml_ref[:, n * D_MODEL:(n + 1) * D_MODEL].astype(F32)) * yn
    r = alpha * x_ref[...] + _dot(acc.astype(BF16), wo_ref[...])
    mu = jnp.mean(r, axis=-1, keepdims=True)
    rc = r - mu
    var = jnp.mean(rc * rc, axis=-1, keepdims=True)
    y = rc * lax.rsqrt(var + LN_EPS) * g_ref[...] + b_ref[...]
    y_ref[...] = y
    yb_ref[...] = y.astype(BF16)


def _out(branches, z, ml, x, w_branch, w_out, ln_g, ln_b, alpha, tm=256):
    m = x.shape[0]
    row = lambda n: pl.BlockSpec((tm, n), lambda i: (i, 0))
    return pl.pallas_call(
        functools.partial(_out_kernel, alpha=alpha),
        grid=(m // tm,),
        in_specs=[row(WIDTH)] * N_BRANCH + [row(N_BRANCH * WIDTH), row(N_BRANCH * D_MODEL), row(D_MODEL),
                  pl.BlockSpec((N_BRANCH, WIDTH, D_MODEL), lambda i: (0, 0, 0)),
                  pl.BlockSpec((D_MODEL, D_MODEL), lambda i: (0, 0)),
                  pl.BlockSpec((1, D_MODEL), lambda i: (0, 0)),
                  pl.BlockSpec((1, D_MODEL), lambda i: (0, 0))],
        out_specs=[row(D_MODEL), row(D_MODEL)],
        out_shape=[jax.ShapeDtypeStruct((m, D_MODEL), F32), jax.ShapeDtypeStruct((m, D_MODEL), BF16)],
        compiler_params=_cparams(("parallel",)),
        name="out",
    )(*branches, z, ml, x, w_branch, w_out, ln_g, ln_b)


def _split_weights(w_in, b_in):
    def take(a, spans):
        return a[..., np.concatenate([np.arange(o, o + n) for o, n in spans])]

    hb_cols = ((O_E, WIDTH), (O_A, 3 * WIDTH), (O_BQK, 6 * WIDTH), (O_BV, WIDTH), (O_C, 3 * WIDTH))
    scale = np.ones((HB_COLS,), np.float32)
    scale[CB_E * LANES:CB_E * LANES + WIDTH] = MEM_HEAD_DIM ** -0.5
    for cb in (CB_AQ, CB_CQ):
        scale[cb * LANES:cb * LANES + WIDTH] = HEAD_DIM ** -0.5
    scale[CB_BQ * LANES:CB_BQ * LANES + 3 * WIDTH] = HEAD_DIM ** -0.5
    hf_cols = ((O_CF, HEADS), (O_DBETA, HEADS), (O_DDECAY, HEADS))
    pad = LANES - 3 * HEADS
    depth = w_in.shape[0]
    zw = jnp.zeros((depth, D_MODEL, pad), w_in.dtype)
    w_hf = jnp.concatenate([take(w_in, hf_cols), zw, w_in[..., O_D:O_D + 3 * WIDTH]], axis=-1)
    b_hf = jnp.concatenate([take(b_in, hf_cols), zw[:, 0], b_in[..., O_D:O_D + 3 * WIDTH]], axis=-1)
    groups = {
        "hb": (take(w_in, hb_cols), take(b_in, hb_cols), jnp.asarray(scale)),
        "hf": (w_hf, b_hf, None),
        "z": (w_in[..., O_Z:O_Z + N_BRANCH * WIDTH], b_in[..., O_Z:O_Z + N_BRANCH * WIDTH], None),
        "ml": (w_in[..., O_MERGE:], b_in[..., O_MERGE:], None),
    }
    out = {}
    for name, (w, b, s) in groups.items():
        s = jnp.ones((w.shape[-1],), F32) if s is None else s
        out[name] = (w.astype(F32), b.astype(F32)[:, None, :], s[None, :])
    return out


def _layer(x, xb, kv, layer, gw, conv_w, a_log, dt_bias, gdn_norm_w, w_branch, w_out, ln_g, ln_b, alpha):
    bsz, seq, d = x.shape
    m = bsz * seq
    xf = x.reshape(m, d)
    hb = _proj(xb, *gw["hb"], layer, BF16, 1024, 768).reshape(bsz, seq, HB_COLS)
    hf = _proj(xb, *gw["hf"], layer, F32, 1024, 640).reshape(bsz, seq, HF_COLS)
    z = _proj(xb, *gw["z"], layer, BF16, 1024, 640)
    ml = _proj(xb, *gw["ml"], layer, BF16, 1024, 1024)

    o_a = _moba(hb)
    o_b = _dilated(hb)
    o_c = _fox(hb, _fcum(hf))
    o_d = _gdn(hf, conv_w, a_log, dt_bias, gdn_norm_w)
    o_e = _memattn(hb, kv, layer)
    branches = [o.reshape(m, WIDTH) for o in (o_a, o_b, o_c, o_d, o_e)]
    y, yb = _out(branches, z, ml, xf, w_branch.astype(BF16), w_out.astype(BF16),
                 ln_g.astype(F32)[None, :], ln_b.astype(F32)[None, :], alpha)
    return y.reshape(bsz, seq, d), yb


def kernel(x, mem, mem_ln_g, mem_ln_b, w_in, b_in, conv_w, a_log, dt_bias, gdn_norm_w, w_mem_kv, w_branch, w_out, ln_g, ln_b):
    depth = w_in.shape[0]
    alpha = float((2 * depth) ** 0.25)
    w_kv = jnp.concatenate([w_mem_kv[l] for l in range(depth)], axis=1).astype(BF16)
    kv = _memkv(mem.astype(F32), mem_ln_g.astype(F32)[None, :], mem_ln_b.astype(F32)[None, :], w_kv)
    x = x.astype(F32)
    xb = x.reshape(-1, x.shape[-1]).astype(BF16)
    gw = _split_weights(w_in, b_in)
    for l in range(depth):
        x, xb = _layer(x, xb, kv, l, gw, conv_w[l], a_log[l], dt_bias[l], gdn_norm_w[l],
                       w_branch[l], w_out[l], ln_g[l], ln_b[l], alpha)
    return x
```

```python
import functools

import jax
import jax.numpy as jnp
import numpy as np
from jax import lax
from jax.experimental import pallas as pl
from jax.experimental.pallas import tpu as pltpu

F32 = jnp.float32
BF16 = jnp.bfloat16
HI = lax.Precision.HIGHEST

D_MODEL = 1024
HEAD_DIM = 64
HEADS = 6
WIDTH = HEADS * HEAD_DIM
N_BRANCH = 5
LANES = 128
PAIRS = WIDTH // LANES
MOBA_BLOCK = 256
MOBA_TOPK = 3
DILATIONS = (1, 4, 16)
BAND = 128
BAND_TILE = 2048
BAND_GROUP = 4
GDN_CHUNK = 128
CONV_K = 4
MEM_HEADS = 4
MEM_HEAD_DIM = WIDTH // MEM_HEADS
NEG = -1e30
FLASH_UNDERFLOW = 110.0
LN_EPS = 1e-5
RMS_EPS = 1e-6

_SPLIT = (3 * WIDTH, 6 * WIDTH, WIDTH, 3 * WIDTH, HEADS, 3 * WIDTH, HEADS, HEADS, WIDTH, N_BRANCH * WIDTH, N_BRANCH * D_MODEL)
_OFF = tuple(int(v) for v in np.concatenate([[0], np.cumsum(_SPLIT)]))
(O_A, O_BQK, O_BV, O_C, O_CF, O_D, O_DBETA, O_DDECAY, O_E, O_Z, O_MERGE, _) = _OFF

CB_E = 0
CB_AQ, CB_AK, CB_AV = 3, 6, 9
CB_BQ, CB_BK, CB_BV = 12, 21, 30
CB_CQ, CB_CK, CB_CV = 33, 36, 39
HB_COLS = 42 * LANES
HF_COLS = LANES + 3 * WIDTH

VMEM_LIMIT = 56 * 1024 * 1024


def _cparams(sem):
    return pltpu.CompilerParams(dimension_semantics=sem, vmem_limit_bytes=VMEM_LIMIT)


def _dot(a, b):
    return jnp.dot(a, b, preferred_element_type=F32)


def _dot_nt(a, b):
    return lax.dot_general(a, b, (((1,), (1,)), ((), ())), preferred_element_type=F32)


def _dot_hi(a, b):
    return jnp.dot(a, b, preferred_element_type=F32, precision=HI)


def _dot_nt_hi(a, b):
    return lax.dot_general(a, b, (((1,), (1,)), ((), ())), preferred_element_type=F32, precision=HI)


def _iota(shape, dim):
    return lax.broadcasted_iota(jnp.int32, shape, dim)


def _proj_kernel(x_ref, w_ref, b_ref, s_ref, o_ref, wb_ref):
    @pl.when(pl.program_id(1) == 0)
    def _():
        wb_ref[...] = w_ref[...].astype(BF16)

    acc = _dot(x_ref[...], wb_ref[...])
    o_ref[...] = ((acc + b_ref[...]) * s_ref[...]).astype(o_ref.dtype)


def _proj(xb, w, b, scale, layer, out_dtype, tm, tn):
    m, k = xb.shape
    n = w.shape[-1]
    assert m % tm == 0 and n % tn == 0
    return pl.pallas_call(
        _proj_kernel,
        grid=(n // tn, m // tm),
        in_specs=[
            pl.BlockSpec((tm, k), lambda j, i: (i, 0)),
            pl.BlockSpec((None, k, tn), lambda j, i: (layer, 0, j)),
            pl.BlockSpec((None, 1, tn), lambda j, i: (layer, 0, j)),
            pl.BlockSpec((1, tn), lambda j, i: (0, j)),
        ],
        out_specs=pl.BlockSpec((tm, tn), lambda j, i: (i, j)),
        out_shape=jax.ShapeDtypeStruct((m, n), out_dtype),
        scratch_shapes=[pltpu.VMEM((k, tn), BF16)],
        compiler_params=_cparams(("parallel", "arbitrary")),
        name="proj",
    )(xb, w, b, scale)


def _memkv_kernel(mem_ref, g_ref, b_ref, w_ref, o_ref):
    x = mem_ref[0]
    mu = jnp.mean(x, axis=-1, keepdims=True)
    xc = x - mu
    var = jnp.mean(xc * xc, axis=-1, keepdims=True)
    y = xc * lax.rsqrt(var + LN_EPS) * g_ref[...] + b_ref[...]
    o_ref[0] = _dot(y.astype(BF16), w_ref[...]).astype(o_ref.dtype)


def _memkv(mem, g, b, w):
    bsz, mlen, d = mem.shape
    n = w.shape[1]
    return pl.pallas_call(
        _memkv_kernel,
        grid=(bsz,),
        in_specs=[
            pl.BlockSpec((1, mlen, d), lambda i: (i, 0, 0)),
            pl.BlockSpec((1, d), lambda i: (0, 0)),
            pl.BlockSpec((1, d), lambda i: (0, 0)),
            pl.BlockSpec((d, n), lambda i: (0, 0)),
        ],
        out_specs=pl.BlockSpec((1, mlen, n), lambda i: (i, 0, 0)),
        out_shape=jax.ShapeDtypeStruct((bsz, mlen, n), BF16),
        compiler_params=_cparams(("parallel",)),
        name="memkv",
    )(mem, g, b, w)


def _fcum_kernel(h_ref, o_ref, *, blk):
    seq = h_ref.shape[1]
    tri = (_iota((blk, blk), 0) >= _iota((blk, blk), 1)).astype(F32)

    def body(i, carry):
        rows = pl.ds(pl.multiple_of(i * blk, blk), blk)
        logf = jax.nn.log_sigmoid(h_ref[0, rows, :])
        c = _dot_hi(tri, logf) + carry
        o_ref[0, rows, :] = c
        return c[blk - 1:blk, :]

    lax.fori_loop(0, seq // blk, body, jnp.zeros((1, LANES), F32))


def _fcum(hf):
    bsz, seq, _ = hf.shape
    return pl.pallas_call(
        functools.partial(_fcum_kernel, blk=LANES),
        grid=(bsz,),
        in_specs=[pl.BlockSpec((1, seq, LANES), lambda b: (b, 0, 0))],
        out_specs=pl.BlockSpec((1, seq, LANES), lambda b: (b, 0, 0)),
        out_shape=jax.ShapeDtypeStruct((bsz, seq, LANES), F32),
        compiler_params=_cparams(("parallel",)),
        name="fcum",
    )(hf)


def _head_masks(rows):
    lane = _iota((rows, LANES), 1)
    return lane < HEAD_DIM


def _split3(x):
    hi = x.astype(BF16).astype(F32)
    r = x - hi
    mid = r.astype(BF16).astype(F32)
    return hi, mid, r - mid


def _flash_scratch(tq):
    return [pltpu.VMEM((2 * tq, LANES), F32), pltpu.VMEM((2 * tq, tq), F32), pltpu.VMEM((2 * tq, tq), F32),
            pltpu.VMEM((2 * tq, 1), F32), pltpu.VMEM((2 * tq, 1), F32)]


def _augment_values(vx_ref, rows, v):
    vx_ref[rows, 0:LANES] = v
    vx_ref[rows, LANES:] = jnp.where(_iota((v.shape[0], LANES), 1) == 0, 1.0, 0.0).astype(BF16)


def _flash_causal(qx, kx_ref, vx_ref, scratch, i, tq, past_tiles=None):
    acc_ref, sa_ref, sb_ref, m_ref, l_ref = scratch

    def logits(kt, dst):
        rows = pl.ds(pl.multiple_of(kt * tq, tq), tq)
        dst[...] = _dot_nt(qx, kx_ref[rows, :])

    r = _iota((2 * tq, tq), 0)
    causal = jnp.where(r >= tq, r - tq, r) >= _iota((2 * tq, tq), 1)

    def finish(src, kt, masked=False):
        rows = pl.ds(pl.multiple_of(kt * tq, tq), tq)
        s = src[...]
        if masked:
            s = jnp.where(causal, s, NEG)
        m = m_ref[...]
        m_new = jnp.maximum(m, jnp.max(s, axis=-1, keepdims=True))
        alpha = jnp.exp(m - m_new)
        p = jnp.exp((s - m_new).astype(BF16))
        pv = _dot(p, vx_ref[rows, :])
        l_ref[...] = alpha * l_ref[...] + pv[:, LANES:LANES + 1]
        m_ref[...] = m_new
        acc_ref[...] = alpha * acc_ref[...] + pv[:, 0:LANES]

    acc_ref[...] = jnp.zeros_like(acc_ref)
    m_ref[...] = jnp.full((2 * tq, 1), NEG, F32)
    l_ref[...] = jnp.zeros((2 * tq, 1), F32)
    tile = lambda u: jnp.maximum(i - 1 - u, 0)
    logits(i, sa_ref)
    logits(tile(0), sb_ref)
    finish(sa_ref, i, masked=True)
    n = i if past_tiles is None else past_tiles(m_ref)

    def body(v, carry):
        logits(tile(2 * v + 1), sa_ref)
        finish(sb_ref, tile(2 * v))
        logits(tile(2 * v + 2), sb_ref)
        finish(sa_ref, tile(2 * v + 1))
        return carry

    lax.fori_loop(0, n // 2, body, 0)

    @pl.when(n % 2 == 1)
    def _():
        finish(sb_ref, tile(n - 1))

    o = acc_ref[...] / l_ref[...]
    return jnp.where(_head_masks(tq), o[0:tq], o[tq:])


def _head_sumsq(x):
    hm = _head_masks(x.shape[0])
    sq = x * x
    return (jnp.sum(jnp.where(hm, sq, 0.0), axis=1, keepdims=True), jnp.sum(jnp.where(hm, 0.0, sq), axis=1, keepdims=True))


def _fox_kernel(q_ref, k_ref, v_ref, f_ref, o_ref, kx_ref, vx_ref, kn_ref, *flash, tq):
    p = pl.program_id(1)
    i = pl.program_id(2)
    seq = k_ref.shape[1]
    bt = 512

    @pl.when(i == 0)
    def _():
        src = _iota((LANES, LANES), 0)
        dst = _iota((LANES, LANES), 1)
        pm = jnp.logical_or(jnp.logical_and(src == 2 * p, dst < 3),
                            jnp.logical_and(src == 2 * p + 1, jnp.logical_and(dst >= 3, dst < 6))).astype(F32)
        sub = _iota((bt, LANES), 1) % 3

        def build(c, carry):
            rows = pl.ds(pl.multiple_of(c * bt, bt), bt)
            hi, mid, lo = _split3(_dot_hi(f_ref[0, rows, :], pm))
            k = k_ref[0, rows, :]
            kx_ref[rows, 0:LANES] = k
            kx_ref[rows, LANES:] = jnp.where(sub == 0, hi, jnp.where(sub == 1, mid, lo)).astype(BF16)
            _augment_values(vx_ref, rows, v_ref[0, rows, :])
            return tuple(jnp.maximum(c0, jnp.max(s, axis=0, keepdims=True))
                         for c0, s in zip(carry, _head_sumsq(k.astype(F32))))

        zero = jnp.zeros((1, 1), F32)
        kn = lax.fori_loop(0, seq // bt, build, (zero, zero))
        kn_ref[0:1, :] = jnp.broadcast_to(kn[0], (1, LANES))
        kn_ref[1:2, :] = jnp.broadcast_to(kn[1], (1, LANES))

    q = q_ref[0].astype(F32)
    lane = _iota((tq, LANES), 1)
    hm = lane < HEAD_DIM
    top = jnp.concatenate([jnp.where(hm, q, 0.0), jnp.where(lane < 3, -1.0, 0.0)], axis=1)
    bot = jnp.concatenate([jnp.where(hm, 0.0, q), jnp.where(jnp.logical_and(lane >= 3, lane < 6), -1.0, 0.0)], axis=1)
    qx = jnp.concatenate([top, bot], axis=0).astype(BF16)

    def past_tiles(m_ref):
        nt = seq // tq
        fend = f_ref[0, pl.ds(tq - 1, nt, stride=tq), :]
        lane_t = _iota((nt, LANES), 1)
        tpos = _iota((nt, 1), 0)
        m = m_ref[...]
        need = tpos < 0
        for h, qn2 in enumerate(_head_sumsq(q)):
            m_min = jnp.min(m[h * tq:(h + 1) * tq], axis=0, keepdims=True)
            qk = jnp.sqrt(jnp.max(qn2, axis=0, keepdims=True) * kn_ref[h:h + 1, 0:1]) * 1.01 + 1.0
            f_h = jnp.sum(jnp.where(lane_t == 2 * p + h, fend, 0.0), axis=1, keepdims=True)
            need = jnp.logical_or(need, qk - f_h > m_min - FLASH_UNDERFLOW)
        first = jnp.min(jnp.where(jnp.logical_and(need, tpos < i), tpos, i).astype(F32))
        return i - first.astype(jnp.int32)

    o_ref[0] = _flash_causal(qx, kx_ref, vx_ref, flash, i, tq, past_tiles).astype(o_ref.dtype)


def _fox(hb, fcol, tq=256):
    bsz, seq, _ = hb.shape
    assert seq % tq == 0
    return pl.pallas_call(
        functools.partial(_fox_kernel, tq=tq),
        grid=(bsz, PAIRS, seq // tq),
        in_specs=[
            pl.BlockSpec((1, tq, LANES), lambda b, p, i: (b, i, CB_CQ + p)),
            pl.BlockSpec((1, seq, LANES), lambda b, p, i: (b, 0, CB_CK + p)),
            pl.BlockSpec((1, seq, LANES), lambda b, p, i: (b, 0, CB_CV + p)),
            pl.BlockSpec((1, seq, LANES), lambda b, p, i: (b, 0, 0)),
        ],
        out_specs=pl.BlockSpec((1, tq, LANES), lambda b, p, i: (b, i, p)),
        out_shape=jax.ShapeDtypeStruct((bsz, seq, WIDTH), BF16),
        scratch_shapes=[pltpu.VMEM((seq, 2 * LANES), BF16), pltpu.VMEM((seq, 2 * LANES), BF16),
                        pltpu.VMEM((8, LANES), F32)] + _flash_scratch(tq),
        compiler_params=_cparams(("parallel", "parallel", "arbitrary")),
        name="fox",
    )(hb, hb, hb, fcol)


def _moba_kernel(q_ref, k_ref, v_ref, o_ref, kx_ref, vx_ref, kmean_ref, *flash, tq):
    i = pl.program_id(2)
    seq = k_ref.shape[1]
    blk = MOBA_BLOCK

    @pl.when(i == 0)
    def _():
        kmean_ref[...] = jnp.zeros_like(kmean_ref)
        lane = _iota((blk, LANES), 1)

        def build(n, carry):
            rows = pl.ds(pl.multiple_of(n * blk, blk), blk)
            k = k_ref[0, rows, :]
            kx_ref[rows, 0:LANES] = k
            kx_ref[rows, LANES:] = jnp.where(lane == n, 1.0, 0.0).astype(BF16)
            _augment_values(vx_ref, rows, v_ref[0, rows, :])
            kmean_ref[pl.ds(n, 1), :] = jnp.sum(k.astype(F32), axis=0, keepdims=True) * (1.0 / blk)
            return carry

        lax.fori_loop(0, seq // blk, build, 0)

    q = q_ref[0].astype(F32)
    hm = _head_masks(tq)
    nbp = -(-(seq // blk) // 8) * 8
    blk_id = _iota((nbp, tq), 0)
    blk_f = blk_id.astype(F32)
    own = i * (tq // blk) + _iota((nbp, tq), 1) // blk
    valid = blk_id < own
    kmean = kmean_ref[0:nbp, :]
    halves = []
    for h in range(2):
        qh = jnp.where(hm, q, 0.0) if h == 0 else jnp.where(hm, 0.0, q)
        g = jnp.where(valid, _dot_nt_hi(kmean, qh), NEG)
        bias = jnp.where(blk_id == own, 0.0, NEG)
        for _ in range(MOBA_TOPK):
            mx = jnp.max(g, axis=0, keepdims=True)
            first = jnp.min(jnp.where(g == mx, blk_f, 1e9), axis=0, keepdims=True)
            pick = blk_f == first
            bias = jnp.where(jnp.logical_and(pick, valid), 0.0, bias)
            g = jnp.where(pick, -3e38, g)
        bias = jnp.concatenate([bias, jnp.zeros((LANES - nbp, tq), F32)], axis=0).T
        halves.append(jnp.concatenate([qh, bias], axis=1))
    qx = jnp.concatenate(halves, axis=0).astype(BF16)
    o_ref[0] = _flash_causal(qx, kx_ref, vx_ref, flash, i, tq).astype(o_ref.dtype)


def _moba(hb, tq=512):
    bsz, seq, _ = hb.shape
    assert seq % tq == 0 and tq % MOBA_BLOCK == 0 and seq // MOBA_BLOCK <= LANES
    return pl.pallas_call(
        functools.partial(_moba_kernel, tq=tq),
        grid=(bsz, PAIRS, seq // tq),
        in_specs=[
            pl.BlockSpec((1, tq, LANES), lambda b, p, i: (b, i, CB_AQ + p)),
            pl.BlockSpec((1, seq, LANES), lambda b, p, i: (b, 0, CB_AK + p)),
            pl.BlockSpec((1, seq, LANES), lambda b, p, i: (b, 0, CB_AV + p)),
        ],
        out_specs=pl.BlockSpec((1, tq, LANES), lambda b, p, i: (b, i, p)),
        out_shape=jax.ShapeDtypeStruct((bsz, seq, WIDTH), BF16),
        scratch_shapes=[pltpu.VMEM((seq, 2 * LANES), BF16), pltpu.VMEM((seq, 2 * LANES), BF16),
                        pltpu.VMEM((LANES, LANES), F32)] + _flash_scratch(tq),
        compiler_params=_cparams(("parallel", "parallel", "arbitrary")),
        name="moba",
    )(hb, hb, hb)


def _band_kernel(q0, q1, q2, k0c, k0p, k1c, k1p, k2c, k2p, vc, vp, o_ref, qb, kb, vb, nb, mb, sb, *, tile):
    jt = pl.program_id(2)
    for g, (qr, kc, kp) in enumerate(((q0, k0c, k0p), (q1, k1c, k1p), (q2, k2c, k2p))):
        qb[g] = qr[0].astype(F32)
        kb[g, 0:tile, :] = kp[0].astype(F32)
        kb[g, tile:, :] = kc[0].astype(F32)
    vb[0:tile, :] = vp[0].astype(F32)
    vb[tile:, :] = vc[0].astype(F32)

    hm = _head_masks(BAND)
    qi = _iota((BAND, 2 * BAND), 0)
    kj = _iota((BAND, 2 * BAND), 1)
    dist = BAND + qi - kj
    band = jnp.logical_and(dist >= 0, dist <= BAND)
    cur_half = kj >= BAND
    nblocks = tile // BAND

    for g, dil in enumerate(DILATIONS):
        per_stream = nblocks // dil

        def step(it, carry, g=g, dil=dil, per_stream=per_stream):
            starts, kstarts, valids = [], [], []
            for j in range(BAND_GROUP):
                idx = it * BAND_GROUP + j
                n = idx % per_stream
                start = idx // per_stream + n * (BAND * dil)
                starts.append(start)
                kstarts.append(tile + start - BAND * dil)
                valids.append(jnp.logical_and(band, jnp.logical_or(cur_half, jnp.logical_or(jt > 0, n > 0))))
            qs = [qb[g, pl.ds(s, BAND, stride=dil), :] for s in starts]
            kks = [kb[g, pl.ds(s, 2 * BAND, stride=dil), :].astype(BF16) for s in kstarts]
            vvs = [vb[pl.ds(s, 2 * BAND, stride=dil), :].astype(BF16) for s in kstarts]
            logits = []
            for q, kk, valid in zip(qs, kks, valids):
                for h in range(2):
                    qh = (jnp.where(hm, q, 0.0) if h == 0 else jnp.where(hm, 0.0, q)).astype(BF16)
                    logits.append(jnp.where(valid, _dot_nt(qh, kk), NEG))
            ms = [jnp.max(s, axis=-1, keepdims=True) for s in logits]
            ps = [jnp.exp(s - m) for s, m in zip(logits, ms)]
            ss = [jnp.sum(p, axis=-1, keepdims=True) for p in ps]
            nums = [_dot(p.astype(BF16), vvs[i // 2]) for i, p in enumerate(ps)]
            for j, s in enumerate(starts):
                rows = pl.ds(s, BAND, stride=dil)
                nb[g, rows, :] = jnp.where(hm, nums[2 * j], nums[2 * j + 1])
                mb[g, rows, :] = jnp.where(hm, ms[2 * j], ms[2 * j + 1])
                sb[g, rows, :] = jnp.where(hm, ss[2 * j], ss[2 * j + 1])
            return carry

        lax.fori_loop(0, nblocks // BAND_GROUP, step, 0)

    ch = 256

    def merge(i, carry):
        rows = pl.ds(pl.multiple_of(i * ch, ch), ch)
        m_all = jnp.maximum(jnp.maximum(mb[0, rows, :], mb[1, rows, :]), mb[2, rows, :])
        num = jnp.zeros((ch, LANES), F32)
        den = jnp.zeros((ch, LANES), F32)
        for g in range(len(DILATIONS)):
            w = jnp.exp(mb[g, rows, :] - m_all)
            num = num + nb[g, rows, :] * w
            den = den + sb[g, rows, :] * w
        o_ref[0, rows, :] = (num / den).astype(o_ref.dtype)
        return carry

    lax.fori_loop(0, tile // ch, merge, 0)


def _dilated(hb, tile=BAND_TILE):
    bsz, seq, _ = hb.shape
    ng = len(DILATIONS)
    nblocks = tile // BAND
    assert seq % tile == 0 and nblocks % BAND_GROUP == 0 and all(nblocks % d == 0 for d in DILATIONS)
    cur = lambda cb: pl.BlockSpec((1, tile, LANES), lambda b, p, j: (b, j, cb + p))
    prv = lambda cb: pl.BlockSpec((1, tile, LANES), lambda b, p, j: (b, jnp.maximum(j - 1, 0), cb + p))
    in_specs = [cur(CB_BQ + PAIRS * g) for g in range(ng)]
    for g in range(ng):
        in_specs += [cur(CB_BK + PAIRS * g), prv(CB_BK + PAIRS * g)]
    in_specs += [cur(CB_BV), prv(CB_BV)]
    acc = pltpu.VMEM((ng, tile, LANES), F32)
    return pl.pallas_call(
        functools.partial(_band_kernel, tile=tile),
        grid=(bsz, PAIRS, seq // tile),
        in_specs=in_specs,
        out_specs=pl.BlockSpec((1, tile, LANES), lambda b, p, j: (b, j, p)),
        out_shape=jax.ShapeDtypeStruct((bsz, seq, WIDTH), BF16),
        scratch_shapes=[acc, pltpu.VMEM((ng, 2 * tile, LANES), F32), pltpu.VMEM((2 * tile, LANES), F32), acc, acc, acc],
        compiler_params=_cparams(("parallel", "parallel", "arbitrary")),
        name="band",
    )(*([hb] * len(in_specs)))


def _dot01(a, b, nt=False, pieces=3):
    f = _dot_nt if nt else _dot
    if a.dtype == BF16:
        return sum(f(a, piece.astype(BF16)) for piece in _split3(b)[:pieces])
    return sum(f(piece.astype(BF16), b) for piece in _split3(a)[:pieces])


def _tri_inverse_all(ms):
    c = ms[0].shape[0]
    ri = _iota((c, c), 0)
    ci = _iota((c, c), 1)
    base = 16
    inblk = ri // base == ci // base
    eye = jnp.where(ri == ci, 1.0, 0.0)
    ps = [jnp.where(inblk, -m, 0.0) for m in ms]
    ts = [eye + p for p in ps]
    for _ in range(3):
        pbs = [p.astype(BF16) for p in ps]
        ps = [_dot(pb, pb) for pb in pbs]
        ts = [t + _dot(t.astype(BF16), p.astype(BF16)) for t, p in zip(ts, ps)]
    size = base
    while size < c:
        lower = jnp.logical_and(ri // (2 * size) == ci // (2 * size), ri // size != ci // size)
        tbs = [t.astype(BF16) for t in ts]
        xs = [_dot(tb, jnp.where(lower, m, 0.0).astype(BF16)) for tb, m in zip(tbs, ms)]
        ts = [t - _dot(x.astype(BF16), tb) for t, x, tb in zip(ts, xs, tbs)]
        size *= 2
    return ts


def _gdn_kernel(h_ref, cw_ref, alog_ref, dtb_ref, nw_ref, o_ref,
                xe_ref, u_s, w_s, qd_s, kd_s, qk_s, egl_s, o_s, st_ref, *, tile):
    cc = GDN_CHUNK
    nc = tile // cc
    t = pl.program_id(1)

    @pl.when(t == 0)
    def _():
        xe_ref[0:8, :] = jnp.zeros((8, 3 * WIDTH), F32)
        st_ref[...] = jnp.zeros_like(st_ref)

    x = h_ref[0, :, LANES:]
    xe_ref[8:8 + tile, :] = x
    y = jnp.zeros((tile, 3 * WIDTH), F32)
    for j in range(CONV_K):
        y = y + cw_ref[j:j + 1, :] * xe_ref[8 - (CONV_K - 1) + j:8 - (CONV_K - 1) + j + tile, :]
    xe_ref[0:8, :] = x[tile - 8:tile, :]
    y = y * jax.nn.sigmoid(y)

    bd = ((_iota((WIDTH, WIDTH), 0) // HEAD_DIM) == (_iota((WIDTH, WIDTH), 1) // HEAD_DIM)).astype(BF16)
    q = y[:, 0:WIDTH]
    k = y[:, WIDTH:2 * WIDTH]
    v = y[:, 2 * WIDTH:]
    q = q * lax.rsqrt(_dot01(q * q, bd, pieces=2) + RMS_EPS) * (HEAD_DIM ** -0.5)
    k = k * lax.rsqrt(_dot01(k * k, bd, pieces=2) + RMS_EPS)

    hs = h_ref[0, :, 0:LANES]
    er = _iota((LANES, WIDTH), 0)
    ec = _iota((LANES, WIDTH), 1) // HEAD_DIM
    beta = jax.nn.sigmoid(_dot01(hs, (er == ec + HEADS).astype(BF16)))
    g = -jnp.exp(alog_ref[...]) * jax.nn.softplus(_dot01(hs, (er == ec + 2 * HEADS).astype(BF16)) + dtb_ref[...])

    tri = (_iota((cc, cc), 0) >= _iota((cc, cc), 1)).astype(BF16)
    gc = jnp.concatenate([_dot01(tri, g[c * cc:(c + 1) * cc]) for c in range(nc)], axis=0)
    glast = jnp.broadcast_to(gc.reshape(nc, cc, WIDTH)[:, cc - 1:cc, :], (nc, cc, WIDTH)).reshape(tile, WIDTH)
    eg = jnp.exp(gc)
    kb = k * beta
    vb = v * beta
    wb = kb * eg
    qd_s[...] = q * eg
    kd_s[...] = k * jnp.exp(glast - gc)
    egl_s[...] = jnp.exp(glast)

    ri = _iota((cc, cc), 0)
    ci = _iota((cc, cc), 1)
    incl = ri >= ci
    strict = ri > ci
    lane = _iota((cc, LANES), 1)
    hm = lane < HEAD_DIM
    sls = [slice(p * LANES, (p + 1) * LANES) for p in range(PAIRS)]
    keeps = [hm, jnp.logical_not(hm)]
    heads = [(p, h) for p in range(PAIRS) for h in range(2)]
    chains = [(p, h, c) for p, h in heads for c in range(nc)]
    rws = [slice(c * cc, (c + 1) * cc) for c in range(nc)]
    kps = [k[:, sl].astype(BF16) for sl in sls]
    rhss = [jnp.concatenate([vb[:, sl], wb[:, sl]], axis=1).astype(BF16) for sl in sls]
    gct = gc.T
    rowf = {(p, h): gct[p * LANES + h * HEAD_DIM:p * LANES + h * HEAD_DIM + 1, :] for p, h in heads}
    kbm = {(p, h): jnp.where(jnp.tile(keeps[h], (nc, 1)), kb[:, sls[p]], 0.0).astype(BF16) for p, h in heads}
    qm = {(p, h): jnp.where(jnp.tile(keeps[h], (nc, 1)), q[:, sls[p]], 0.0).astype(BF16) for p, h in heads}
    kks = [_dot_nt(kbm[p, h][rws[c]], kps[p][rws[c]]) for p, h, c in chains]
    qks = [_dot_nt(qm[p, h][rws[c]], kps[p][rws[c]]) for p, h, c in chains]
    mms = []
    for (p, h, c), kk, qk in zip(chains, kks, qks):
        col = p * LANES + h * HEAD_DIM
        diff = gc[rws[c], col:col + 1] - rowf[p, h][:, rws[c]]
        decay = jnp.where(incl, jnp.exp(jnp.where(incl, diff, 0.0)), 0.0)
        mms.append(jnp.where(strict, kk * decay, 0.0))
        qk_s[2 * p + h, rws[c], :] = qk * decay
    tinvs = _tri_inverse_all(mms)
    rs = {ch: _dot(tinv.astype(BF16), rhss[ch[0]][rws[ch[2]]]) for ch, tinv in zip(chains, tinvs)}
    for p in range(PAIRS):
        for c in range(nc):
            r0, r1 = rs[p, 0, c], rs[p, 1, c]
            u_s[rws[c], sls[p]] = jnp.where(hm, r0[:, :LANES], r1[:, :LANES])
            w_s[rws[c], sls[p]] = jnp.where(hm, r0[:, LANES:], r1[:, LANES:])

    bdiag = (_iota((LANES, LANES), 0) // HEAD_DIM) == (_iota((LANES, LANES), 1) // HEAD_DIM)
    for c in range(nc):
        rows = rws[c]
        sts = [st_ref[p] for p in range(PAIRS)]
        stbs = [st.astype(BF16) for st in sts]
        wss = [_dot(w_s[rows, sl].astype(BF16), stb) for sl, stb in zip(sls, stbs)]
        qss = [_dot(qd_s[rows, sl].astype(BF16), stb) for sl, stb in zip(sls, stbs)]
        vnbs = [(u_s[rows, sl] - ws).astype(BF16) for sl, ws in zip(sls, wss)]
        upds = [lax.dot_general(kd_s[rows, sl].astype(BF16), vnb, (((0,), (0,)), ((), ())), preferred_element_type=F32)
                for sl, vnb in zip(sls, vnbs)]
        intra = [(_dot(qk_s[2 * p, rows, :].astype(BF16), vnbs[p]), _dot(qk_s[2 * p + 1, rows, :].astype(BF16), vnbs[p]))
                 for p in range(PAIRS)]
        for p in range(PAIRS):
            st_ref[p] = sts[p] * egl_s[c * cc:c * cc + 1, sls[p]] + jnp.where(bdiag, upds[p], 0.0)
            o_s[rows, sls[p]] = qss[p] + jnp.where(hm, intra[p][0], intra[p][1])

    o = o_s[...]
    ms = _dot01(o * o, bd, pieces=2) * (1.0 / HEAD_DIM)
    o_ref[0] = (o * lax.rsqrt(ms + RMS_EPS) * nw_ref[...]).astype(o_ref.dtype)


def _gdn(hf, conv_w, a_log, dt_bias, norm_w, tile=512):
    bsz, seq, _ = hf.shape
    rep = lambda a: jnp.repeat(a.astype(F32), HEAD_DIM)[None, :]
    wide = pltpu.VMEM((tile, WIDTH), F32)
    return pl.pallas_call(
        functools.partial(_gdn_kernel, tile=tile),
        grid=(bsz, seq // tile),
        in_specs=[
            pl.BlockSpec((1, tile, HF_COLS), lambda b, t: (b, t, 0)),
            pl.BlockSpec((CONV_K, 3 * WIDTH), lambda b, t: (0, 0)),
            pl.BlockSpec((1, WIDTH), lambda b, t: (0, 0)),
            pl.BlockSpec((1, WIDTH), lambda b, t: (0, 0)),
            pl.BlockSpec((1, WIDTH), lambda b, t: (0, 0)),
        ],
        out_specs=pl.BlockSpec((1, tile, WIDTH), lambda b, t: (b, t, 0)),
        out_shape=jax.ShapeDtypeStruct((bsz, seq, WIDTH), BF16),
        scratch_shapes=[pltpu.VMEM((tile + 8, 3 * WIDTH), F32), wide, wide, wide, wide,
                        pltpu.VMEM((HEADS, tile, LANES), F32), wide, wide,
                        pltpu.VMEM((PAIRS, LANES, LANES), F32)],
        compiler_params=_cparams(("parallel", "arbitrary")),
        name="gdn",
    )(hf, conv_w.astype(F32), rep(a_log), rep(dt_bias), jnp.tile(norm_w.astype(F32), HEADS)[None, :])


def _memattn_kernel(q_ref, kv_ref, o_ref):
    tq = q_ref.shape[1]
    q = q_ref[0]
    k = kv_ref[0, :, 0:WIDTH]
    v = kv_ref[0, :, WIDTH:2 * WIDTH]
    head = _iota((tq, WIDTH), 1) // MEM_HEAD_DIM
    zero = jnp.zeros_like(q)
    out = jnp.zeros((tq, WIDTH), F32)
    for h in range(MEM_HEADS):
        s = _dot_nt(jnp.where(head == h, q, zero), k)
        m = jnp.max(s, axis=-1, keepdims=True)
        p = jnp.exp(s - m)
        o = _dot(p.astype(BF16), v) / jnp.sum(p, axis=-1, keepdims=True)
        out = jnp.where(head == h, o, out)
    o_ref[0] = out.astype(o_ref.dtype)


def _memattn(hb, kv, layer, tq=512):
    bsz, seq, _ = hb.shape
    mlen = kv.shape[1]
    return pl.pallas_call(
        _memattn_kernel,
        grid=(bsz, seq // tq),
        in_specs=[
            pl.BlockSpec((1, tq, WIDTH), lambda b, i: (b, i, CB_E // PAIRS)),
            pl.BlockSpec((1, mlen, 2 * WIDTH), lambda b, i: (b, 0, layer)),
        ],
        out_specs=pl.BlockSpec((1, tq, WIDTH), lambda b, i: (b, i, 0)),
        out_shape=jax.ShapeDtypeStruct((bsz, seq, WIDTH), BF16),
        compiler_params=_cparams(("parallel", "parallel")),
        name="memattn",
    )(hb, kv)


def _out_kernel(oa, ob, oc, od, oe, z_ref, ml_ref, x_ref, wb_ref, wo_ref, g_ref, b_ref, y_ref, yb_ref, *, alpha):
    tm = x_ref.shape[0]
    acc = jnp.zeros((tm, D_MODEL), F32)
    for n, o_ref in enumerate((oa, ob, oc, od, oe)):
        z = z_ref[:, n * WIDTH:(n + 1) * WIDTH]
        gated = o_ref[...] * (z * jax.nn.sigmoid(z))
        yn = _dot(gated, wb_ref[n])
        acc = acc + jax.nn.sigmoid(ml_ref[:, n * D_MODEL:(n + 1) * D_MODEL]).astype(F32) * yn
    r = alpha * x_ref[...] + _dot(acc.astype(BF16), wo_ref[...])
    mu = jnp.mean(r, axis=-1, keepdims=True)
    rc = r - mu
    var = jnp.mean(rc * rc, axis=-1, keepdims=True)
    y = rc * lax.rsqrt(var + LN_EPS) * g_ref[...] + b_ref[...]
    y_ref[...] = y
    yb_ref[...] = y.astype(BF16)


def _out(branches, z, ml, x, w_branch, w_out, ln_g, ln_b, alpha, tm=256):
    m = x.shape[0]
    row = lambda n: pl.BlockSpec((tm, n), lambda i: (i, 0))
    return pl.pallas_call(
        functools.partial(_out_kernel, alpha=alpha),
        grid=(m // tm,),
        in_specs=[row(WIDTH)] * N_BRANCH + [row(N_BRANCH * WIDTH), row(N_BRANCH * D_MODEL), row(D_MODEL),
                  pl.BlockSpec((N_BRANCH, WIDTH, D_MODEL), lambda i: (0, 0, 0)),
                  pl.BlockSpec((D_MODEL, D_MODEL), lambda i: (0, 0)),
                  pl.BlockSpec((1, D_MODEL), lambda i: (0, 0)),
                  pl.BlockSpec((1, D_MODEL), lambda i: (0, 0))],
        out_specs=[row(D_MODEL), row(D_MODEL)],
        out_shape=[jax.ShapeDtypeStruct((m, D_MODEL), F32), jax.ShapeDtypeStruct((m, D_MODEL), BF16)],
        compiler_params=_cparams(("parallel",)),
        name="out",
    )(*branches, z, ml, x, w_branch, w_out, ln_g, ln_b)


def _split_weights(w_in, b_in):
    def take(a, spans):
        return a[..., np.concatenate([np.arange(o, o + n) for o, n in spans])]

    hb_cols = ((O_E, WIDTH), (O_A, 3 * WIDTH), (O_BQK, 6 * WIDTH), (O_BV, WIDTH), (O_C, 3 * WIDTH))
    scale = np.ones((HB_COLS,), np.float32)
    scale[CB_E * LANES:CB_E * LANES + WIDTH] = MEM_HEAD_DIM ** -0.5
    for cb in (CB_AQ, CB_CQ):
        scale[cb * LANES:cb * LANES + WIDTH] = HEAD_DIM ** -0.5
    scale[CB_BQ * LANES:CB_BQ * LANES + 3 * WIDTH] = HEAD_DIM ** -0.5
    hf_cols = ((O_CF, HEADS), (O_DBETA, HEADS), (O_DDECAY, HEADS))
    pad = LANES - 3 * HEADS
    depth = w_in.shape[0]
    zw = jnp.zeros((depth, D_MODEL, pad), w_in.dtype)
    w_hf = jnp.concatenate([take(w_in, hf_cols), zw, w_in[..., O_D:O_D + 3 * WIDTH]], axis=-1)
    b_hf = jnp.concatenate([take(b_in, hf_cols), zw[:, 0], b_in[..., O_D:O_D + 3 * WIDTH]], axis=-1)
    groups = {
        "hb": (take(w_in, hb_cols), take(b_in, hb_cols), jnp.asarray(scale)),
        "hf": (w_hf, b_hf, None),
        "z": (w_in[..., O_Z:O_Z + N_BRANCH * WIDTH], b_in[..., O_Z:O_Z + N_BRANCH * WIDTH], None),
        "ml": (w_in[..., O_MERGE:], b_in[..., O_MERGE:], None),
    }
    out = {}
    for name, (w, b, s) in groups.items():
        s = jnp.ones((w.shape[-1],), F32) if s is None else s
        out[name] = (w.astype(F32), b.astype(F32)[:, None, :], s[None, :])
    return out


def _layer(x, xb, kv, layer, gw, conv_w, a_log, dt_bias, gdn_norm_w, w_branch, w_out, ln_g, ln_b, alpha):
    bsz, seq, d = x.shape
    m = bsz * seq
    xf = x.reshape(m, d)
    hb = _proj(xb, *gw["hb"], layer, BF16, 1024, 768).reshape(bsz, seq, HB_COLS)
    hf = _proj(xb, *gw["hf"], layer, F32, 1024, 640).reshape(bsz, seq, HF_COLS)
    z = _proj(xb, *gw["z"], layer, BF16, 1024, 640)
    ml = _proj(xb, *gw["ml"], layer, BF16, 1024, 1024)

    o_a = _moba(hb)
    o_b = _dilated(hb)
    o_c = _fox(hb, _fcum(hf))
    o_d = _gdn(hf, conv_w, a_log, dt_bias, gdn_norm_w)
    o_e = _memattn(hb, kv, layer)
    branches = [o.reshape(m, WIDTH) for o in (o_a, o_b, o_c, o_d, o_e)]
    y, yb = _out(branches, z, ml, xf, w_branch.astype(BF16), w_out.astype(BF16),
                 ln_g.astype(F32)[None, :], ln_b.astype(F32)[None, :], alpha)
    return y.reshape(bsz, seq, d), yb


def kernel(x, mem, mem_ln_g, mem_ln_b, w_in, b_in, conv_w, a_log, dt_bias, gdn_norm_w, w_mem_kv, w_branch, w_out, ln_g, ln_b):
    depth = w_in.shape[0]
    alpha = float((2 * depth) ** 0.25)
    w_kv = jnp.concatenate([w_mem_kv[l] for l in range(depth)], axis=1).astype(BF16)
    kv = _memkv(mem.astype(F32), mem_ln_g.astype(F32)[None, :], mem_ln_b.astype(F32)[None, :], w_kv)
    x = x.astype(F32)
    xb = x.reshape(-1, x.shape[-1]).astype(BF16)
    gw = _split_weights(w_in, b_in)
    for l in range(depth):
        x, xb = _layer(x, xb, kv, l, gw, conv_w[l], a_log[l], dt_bias[l], gdn_norm_w[l],
                       w_branch[l], w_out[l], ln_g[l], ln_b[l], alpha)
    return x
```

```python
import functools

import jax
import jax.numpy as jnp
import numpy as np
from jax import lax
from jax.experimental import pallas as pl
from jax.experimental.pallas import tpu as pltpu

F32 = jnp.float32
BF16 = jnp.bfloat16
HI = lax.Precision.HIGHEST

D_MODEL = 1024
HEAD_DIM = 64
HEADS = 6
WIDTH = HEADS * HEAD_DIM
N_BRANCH = 5
LANES = 128
PAIRS = WIDTH // LANES
MOBA_BLOCK = 256
MOBA_TOPK = 3
DILATIONS = (1, 4, 16)
BAND = 128
BAND_TILE = 2048
BAND_GROUP = 4
GDN_CHUNK = 128
CONV_K = 4
MEM_HEADS = 4
MEM_HEAD_DIM = WIDTH // MEM_HEADS
NEG = -1e30
FLASH_UNDERFLOW = 110.0
LN_EPS = 1e-5
RMS_EPS = 1e-6

_SPLIT = (3 * WIDTH, 6 * WIDTH, WIDTH, 3 * WIDTH, HEADS, 3 * WIDTH, HEADS, HEADS, WIDTH, N_BRANCH * WIDTH, N_BRANCH * D_MODEL)
_OFF = tuple(int(v) for v in np.concatenate([[0], np.cumsum(_SPLIT)]))
(O_A, O_BQK, O_BV, O_C, O_CF, O_D, O_DBETA, O_DDECAY, O_E, O_Z, O_MERGE, _) = _OFF

CB_AQ, CB_AK, CB_AV = 0, 3, 6
CB_BQ, CB_BK, CB_BV = 9, 18, 27
CB_CQ, CB_CK, CB_CV = 30, 33, 36
HB_COLS = 39 * LANES
assert (O_A, O_BQK, O_BV, O_C, O_CF) == tuple(LANES * c for c in (CB_AQ, CB_BQ, CB_BV, CB_CQ, 39))
HF_COLS = LANES + 3 * WIDTH

VMEM_LIMIT = 56 * 1024 * 1024


def _cparams(sem):
    return pltpu.CompilerParams(dimension_semantics=sem, vmem_limit_bytes=VMEM_LIMIT)


def _dot(a, b):
    return jnp.dot(a, b, preferred_element_type=F32)


def _dot_nt(a, b):
    return lax.dot_general(a, b, (((1,), (1,)), ((), ())), preferred_element_type=F32)


def _dot_hi(a, b):
    return jnp.dot(a, b, preferred_element_type=F32, precision=HI)


def _dot_nt_hi(a, b):
    return lax.dot_general(a, b, (((1,), (1,)), ((), ())), preferred_element_type=F32, precision=HI)


def _iota(shape, dim):
    return lax.broadcasted_iota(jnp.int32, shape, dim)


def _proj_kernel(x_ref, w_ref, b_ref, s_ref, o_ref, wb_ref):
    @pl.when(pl.program_id(1) == 0)
    def _():
        wb_ref[...] = w_ref[...].astype(BF16)

    acc = _dot(x_ref[...], wb_ref[...])
    o_ref[...] = ((acc + b_ref[...]) * s_ref[...]).astype(o_ref.dtype)


def _proj(xb, w, b, scale, layer, out_dtype, tm, tn):
    m, k = xb.shape
    n = b.shape[-1]
    assert m % tm == 0 and n % tn == 0
    return pl.pallas_call(
        _proj_kernel,
        grid=(n // tn, m // tm),
        in_specs=[
            pl.BlockSpec((tm, k), lambda j, i: (i, 0)),
            pl.BlockSpec((None, k, tn), lambda j, i: (layer, 0, j)),
            pl.BlockSpec((None, 1, tn), lambda j, i: (layer, 0, j)),
            pl.BlockSpec((1, tn), lambda j, i: (0, j)),
        ],
        out_specs=pl.BlockSpec((tm, tn), lambda j, i: (i, j)),
        out_shape=jax.ShapeDtypeStruct((m, n), out_dtype),
        scratch_shapes=[pltpu.VMEM((k, tn), BF16)],
        compiler_params=_cparams(("parallel", "arbitrary")),
        name="proj",
    )(xb, w, b, scale)


def _memkv_kernel(mem_ref, g_ref, b_ref, w_ref, o_ref):
    x = mem_ref[0]
    mu = jnp.mean(x, axis=-1, keepdims=True)
    xc = x - mu
    var = jnp.mean(xc * xc, axis=-1, keepdims=True)
    y = xc * lax.rsqrt(var + LN_EPS) * g_ref[...] + b_ref[...]
    o_ref[0] = _dot(y.astype(BF16), w_ref[...]).astype(o_ref.dtype)


def _memkv(mem, g, b, w):
    bsz, mlen, d = mem.shape
    n = w.shape[1]
    return pl.pallas_call(
        _memkv_kernel,
        grid=(bsz,),
        in_specs=[
            pl.BlockSpec((1, mlen, d), lambda i: (i, 0, 0)),
            pl.BlockSpec((1, d), lambda i: (0, 0)),
            pl.BlockSpec((1, d), lambda i: (0, 0)),
            pl.BlockSpec((d, n), lambda i: (0, 0)),
        ],
        out_specs=pl.BlockSpec((1, mlen, n), lambda i: (i, 0, 0)),
        out_shape=jax.ShapeDtypeStruct((bsz, mlen, n), BF16),
        compiler_params=_cparams(("parallel",)),
        name="memkv",
    )(mem, g, b, w)


def _fcum_kernel(h_ref, o_ref, *, blk):
    seq = h_ref.shape[1]
    tri = (_iota((blk, blk), 0) >= _iota((blk, blk), 1)).astype(F32)

    def body(i, carry):
        rows = pl.ds(pl.multiple_of(i * blk, blk), blk)
        logf = jax.nn.log_sigmoid(h_ref[0, rows, :])
        c = _dot_hi(tri, logf) + carry
        o_ref[0, rows, :] = c
        return c[blk - 1:blk, :]

    lax.fori_loop(0, seq // blk, body, jnp.zeros((1, LANES), F32))


def _fcum(hf):
    bsz, seq, _ = hf.shape
    return pl.pallas_call(
        functools.partial(_fcum_kernel, blk=LANES),
        grid=(bsz,),
        in_specs=[pl.BlockSpec((1, seq, LANES), lambda b: (b, 0, 0))],
        out_specs=pl.BlockSpec((1, seq, LANES), lambda b: (b, 0, 0)),
        out_shape=jax.ShapeDtypeStruct((bsz, seq, LANES), F32),
        compiler_params=_cparams(("parallel",)),
        name="fcum",
    )(hf)


def _head_masks(rows):
    lane = _iota((rows, LANES), 1)
    return lane < HEAD_DIM


def _split3(x):
    hi = x.astype(BF16).astype(F32)
    r = x - hi
    mid = r.astype(BF16).astype(F32)
    return hi, mid, r - mid


def _flash_scratch(tq):
    return [pltpu.VMEM((2 * tq, LANES), F32), pltpu.VMEM((2 * tq, tq), F32), pltpu.VMEM((2 * tq, tq), F32),
            pltpu.VMEM((2 * tq, LANES), F32), pltpu.VMEM((2 * tq, LANES), F32)]


def _augment_values(vx_ref, rows, v):
    vx_ref[rows, 0:LANES] = v
    vx_ref[rows, LANES:] = jnp.ones((v.shape[0], LANES), BF16)


def _flash_causal(qx, kx_ref, vx_ref, scratch, i, tq, past_tiles=None):
    acc_ref, sa_ref, sb_ref, m_ref, l_ref = scratch

    def logits(kt, dst):
        rows = pl.ds(pl.multiple_of(kt * tq, tq), tq)
        dst[...] = _dot_nt(qx, kx_ref[rows, :])

    r = _iota((2 * tq, tq), 0)
    causal = jnp.where(r >= tq, r - tq, r) >= _iota((2 * tq, tq), 1)

    def finish(src, kt, masked=False):
        rows = pl.ds(pl.multiple_of(kt * tq, tq), tq)
        s = src[...]
        if masked:
            s = jnp.where(causal, s, NEG)
        m = m_ref[...]
        m_new = jnp.maximum(m, jnp.max(s, axis=-1, keepdims=True))
        alpha = jnp.exp(m - m_new)
        p = jnp.exp((s - jnp.tile(m_new, (1, tq // LANES))).astype(BF16))
        pv = _dot(p, vx_ref[rows, :])
        l_ref[...] = alpha * l_ref[...] + pv[:, LANES:]
        m_ref[...] = m_new
        acc_ref[...] = alpha * acc_ref[...] + pv[:, 0:LANES]

    acc_ref[...] = jnp.zeros_like(acc_ref)
    m_ref[...] = jnp.full((2 * tq, LANES), NEG, F32)
    l_ref[...] = jnp.zeros((2 * tq, LANES), F32)
    tile = lambda u: jnp.maximum(i - 1 - u, 0)
    logits(i, sa_ref)
    logits(tile(0), sb_ref)
    finish(sa_ref, i, masked=True)
    n = i if past_tiles is None else past_tiles(m_ref)

    def body(v, carry):
        logits(tile(2 * v + 1), sa_ref)
        finish(sb_ref, tile(2 * v))
        logits(tile(2 * v + 2), sb_ref)
        finish(sa_ref, tile(2 * v + 1))
        return carry

    lax.fori_loop(0, n // 2, body, 0)

    @pl.when(n % 2 == 1)
    def _():
        finish(sb_ref, tile(n - 1))

    o = acc_ref[...] / l_ref[...]
    return jnp.where(_head_masks(tq), o[0:tq], o[tq:])


def _head_sumsq(x):
    hm = _head_masks(x.shape[0])
    sq = x * x
    return (jnp.sum(jnp.where(hm, sq, 0.0), axis=1, keepdims=True), jnp.sum(jnp.where(hm, 0.0, sq), axis=1, keepdims=True))


def _fox_kernel(q_ref, k_ref, v_ref, f_ref, o_ref, kx_ref, vx_ref, kn_ref, *flash, tq):
    p = pl.program_id(1)
    i = pl.program_id(2)
    seq = k_ref.shape[1]
    bt = 512

    @pl.when(i == 0)
    def _():
        src = _iota((LANES, LANES), 0)
        dst = _iota((LANES, LANES), 1)
        pm = jnp.logical_or(jnp.logical_and(src == 2 * p, dst < 3),
                            jnp.logical_and(src == 2 * p + 1, jnp.logical_and(dst >= 3, dst < 6))).astype(F32)
        sub = _iota((bt, LANES), 1) % 3

        def build(c, carry):
            rows = pl.ds(pl.multiple_of(c * bt, bt), bt)
            hi, mid, lo = _split3(_dot_hi(f_ref[0, rows, :], pm))
            k = k_ref[0, rows, :]
            kx_ref[rows, 0:LANES] = k
            kx_ref[rows, LANES:] = jnp.where(sub == 0, hi, jnp.where(sub == 1, mid, lo)).astype(BF16)
            _augment_values(vx_ref, rows, v_ref[0, rows, :])
            return tuple(jnp.maximum(c0, jnp.max(s, axis=0, keepdims=True))
                         for c0, s in zip(carry, _head_sumsq(k.astype(F32))))

        zero = jnp.zeros((1, 1), F32)
        kn = lax.fori_loop(0, seq // bt, build, (zero, zero))
        kn_ref[0:1, :] = jnp.broadcast_to(kn[0], (1, LANES))
        kn_ref[1:2, :] = jnp.broadcast_to(kn[1], (1, LANES))

    q = q_ref[0].astype(F32)
    lane = _iota((tq, LANES), 1)
    hm = lane < HEAD_DIM
    top = jnp.concatenate([jnp.where(hm, q, 0.0), jnp.where(lane < 3, -1.0, 0.0)], axis=1)
    bot = jnp.concatenate([jnp.where(hm, 0.0, q), jnp.where(jnp.logical_and(lane >= 3, lane < 6), -1.0, 0.0)], axis=1)
    qx = jnp.concatenate([top, bot], axis=0).astype(BF16)

    def past_tiles(m_ref):
        nt = seq // tq
        fend = f_ref[0, pl.ds(tq - 1, nt, stride=tq), :]
        lane_t = _iota((nt, LANES), 1)
        tpos = _iota((nt, 1), 0)
        m = m_ref[...]
        need = tpos < 0
        for h, qn2 in enumerate(_head_sumsq(q)):
            m_min = jnp.min(m[h * tq:(h + 1) * tq, 0:1], axis=0, keepdims=True)
            qk = jnp.sqrt(jnp.max(qn2, axis=0, keepdims=True) * kn_ref[h:h + 1, 0:1]) * 1.01 + 1.0
            f_h = jnp.sum(jnp.where(lane_t == 2 * p + h, fend, 0.0), axis=1, keepdims=True)
            need = jnp.logical_or(need, qk - f_h > m_min - FLASH_UNDERFLOW)
        first = jnp.min(jnp.where(jnp.logical_and(need, tpos < i), tpos, i).astype(F32))
        return i - first.astype(jnp.int32)

    o_ref[0] = _flash_causal(qx, kx_ref, vx_ref, flash, i, tq, past_tiles).astype(o_ref.dtype)


def _fox(hb, fcol, tq=256):
    bsz, seq, _ = hb.shape
    assert seq % tq == 0
    return pl.pallas_call(
        functools.partial(_fox_kernel, tq=tq),
        grid=(bsz, PAIRS, seq // tq),
        in_specs=[
            pl.BlockSpec((1, tq, LANES), lambda b, p, i: (b, i, CB_CQ + p)),
            pl.BlockSpec((1, seq, LANES), lambda b, p, i: (b, 0, CB_CK + p)),
            pl.BlockSpec((1, seq, LANES), lambda b, p, i: (b, 0, CB_CV + p)),
            pl.BlockSpec((1, seq, LANES), lambda b, p, i: (b, 0, 0)),
        ],
        out_specs=pl.BlockSpec((1, tq, LANES), lambda b, p, i: (b, i, p)),
        out_shape=jax.ShapeDtypeStruct((bsz, seq, WIDTH), BF16),
        scratch_shapes=[pltpu.VMEM((seq, 2 * LANES), BF16), pltpu.VMEM((seq, 2 * LANES), BF16),
                        pltpu.VMEM((8, LANES), F32)] + _flash_scratch(tq),
        compiler_params=_cparams(("parallel", "parallel", "arbitrary")),
        name="fox",
    )(hb, hb, hb, fcol)


def _moba_kernel(q_ref, k_ref, v_ref, o_ref, kx_ref, vx_ref, kmean_ref, *flash, tq):
    i = pl.program_id(2)
    seq = k_ref.shape[1]
    blk = MOBA_BLOCK

    @pl.when(i == 0)
    def _():
        kmean_ref[...] = jnp.zeros_like(kmean_ref)
        lane = _iota((blk, LANES), 1)

        def build(n, carry):
            rows = pl.ds(pl.multiple_of(n * blk, blk), blk)
            k = k_ref[0, rows, :]
            kx_ref[rows, 0:LANES] = k
            kx_ref[rows, LANES:] = jnp.where(lane == n, 1.0, 0.0).astype(BF16)
            _augment_values(vx_ref, rows, v_ref[0, rows, :])
            kmean_ref[pl.ds(n, 1), :] = jnp.sum(k.astype(F32), axis=0, keepdims=True) * (1.0 / blk)
            return carry

        lax.fori_loop(0, seq // blk, build, 0)

    q = q_ref[0].astype(F32)
    hm = _head_masks(tq)
    nbp = -(-(seq // blk) // 8) * 8
    blk_id = _iota((nbp, tq), 0)
    blk_f = blk_id.astype(F32)
    own = i * (tq // blk) + _iota((nbp, tq), 1) // blk
    valid = blk_id < own
    kmean = kmean_ref[0:nbp, :]
    halves = []
    for h in range(2):
        qh = jnp.where(hm, q, 0.0) if h == 0 else jnp.where(hm, 0.0, q)
        g = jnp.where(valid, _dot_nt_hi(kmean, qh), NEG)
        bias = jnp.where(blk_id == own, 0.0, NEG)
        for _ in range(MOBA_TOPK):
            mx = jnp.max(g, axis=0, keepdims=True)
            first = jnp.min(jnp.where(g == mx, blk_f, 1e9), axis=0, keepdims=True)
            pick = blk_f == first
            bias = jnp.where(jnp.logical_and(pick, valid), 0.0, bias)
            g = jnp.where(pick, -3e38, g)
        bias = jnp.concatenate([bias, jnp.zeros((LANES - nbp, tq), F32)], axis=0).T
        halves.append(jnp.concatenate([qh, bias], axis=1))
    qx = jnp.concatenate(halves, axis=0).astype(BF16)
    o_ref[0] = _flash_causal(qx, kx_ref, vx_ref, flash, i, tq).astype(o_ref.dtype)


def _moba(hb, tq=512):
    bsz, seq, _ = hb.shape
    assert seq % tq == 0 and tq % MOBA_BLOCK == 0 and seq // MOBA_BLOCK <= LANES
    return pl.pallas_call(
        functools.partial(_moba_kernel, tq=tq),
        grid=(bsz, PAIRS, seq // tq),
        in_specs=[
            pl.BlockSpec((1, tq, LANES), lambda b, p, i: (b, i, CB_AQ + p)),
            pl.BlockSpec((1, seq, LANES), lambda b, p, i: (b, 0, CB_AK + p)),
            pl.BlockSpec((1, seq, LANES), lambda b, p, i: (b, 0, CB_AV + p)),
        ],
        out_specs=pl.BlockSpec((1, tq, LANES), lambda b, p, i: (b, i, p)),
        out_shape=jax.ShapeDtypeStruct((bsz, seq, WIDTH), BF16),
        scratch_shapes=[pltpu.VMEM((seq, 2 * LANES), BF16), pltpu.VMEM((seq, 2 * LANES), BF16),
                        pltpu.VMEM((LANES, LANES), F32)] + _flash_scratch(tq),
        compiler_params=_cparams(("parallel", "parallel", "arbitrary")),
        name="moba",
    )(hb, hb, hb)


def _band_kernel(q0, q1, q2, k0c, k0p, k1c, k1p, k2c, k2p, vc, vp, o_ref, qb, kb, vb, nb, mb, sb, *, tile):
    jt = pl.program_id(2)
    for g, (qr, kc, kp) in enumerate(((q0, k0c, k0p), (q1, k1c, k1p), (q2, k2c, k2p))):
        qb[g] = qr[0].astype(F32)
        kb[g, 0:tile, :] = kp[0].astype(F32)
        kb[g, tile:, :] = kc[0].astype(F32)
    vb[0:tile, :] = vp[0].astype(F32)
    vb[tile:, :] = vc[0].astype(F32)

    hm = _head_masks(BAND)
    qi = _iota((BAND, 2 * BAND), 0)
    kj = _iota((BAND, 2 * BAND), 1)
    dist = BAND + qi - kj
    band = jnp.logical_and(dist >= 0, dist <= BAND)
    cur_half = kj >= BAND
    nblocks = tile // BAND

    for g, dil in enumerate(DILATIONS):
        per_stream = nblocks // dil

        def step(it, carry, g=g, dil=dil, per_stream=per_stream):
            starts, kstarts, valids = [], [], []
            for j in range(BAND_GROUP):
                idx = it * BAND_GROUP + j
                n = idx % per_stream
                start = idx // per_stream + n * (BAND * dil)
                starts.append(start)
                kstarts.append(tile + start - BAND * dil)
                valids.append(jnp.logical_and(band, jnp.logical_or(cur_half, jnp.logical_or(jt > 0, n > 0))))
            qs = [qb[g, pl.ds(s, BAND, stride=dil), :] for s in starts]
            kks = [kb[g, pl.ds(s, 2 * BAND, stride=dil), :].astype(BF16) for s in kstarts]
            vvs = [vb[pl.ds(s, 2 * BAND, stride=dil), :].astype(BF16) for s in kstarts]
            logits = []
            for q, kk, valid in zip(qs, kks, valids):
                for h in range(2):
                    qh = (jnp.where(hm, q, 0.0) if h == 0 else jnp.where(hm, 0.0, q)).astype(BF16)
                    logits.append(jnp.where(valid, _dot_nt(qh, kk), NEG))
            ms = [jnp.max(s, axis=-1, keepdims=True) for s in logits]
            ps = [jnp.exp(s - m) for s, m in zip(logits, ms)]
            ss = [jnp.sum(p, axis=-1, keepdims=True) for p in ps]
            nums = [_dot(p.astype(BF16), vvs[i // 2]) for i, p in enumerate(ps)]
            for j, s in enumerate(starts):
                rows = pl.ds(s, BAND, stride=dil)
                nb[g, rows, :] = jnp.where(hm, nums[2 * j], nums[2 * j + 1])
                mb[g, rows, :] = jnp.where(hm, ms[2 * j], ms[2 * j + 1])
                sb[g, rows, :] = jnp.where(hm, ss[2 * j], ss[2 * j + 1])
            return carry

        lax.fori_loop(0, nblocks // BAND_GROUP, step, 0)

    ch = 256

    def merge(i, carry):
        rows = pl.ds(pl.multiple_of(i * ch, ch), ch)
        m_all = jnp.maximum(jnp.maximum(mb[0, rows, :], mb[1, rows, :]), mb[2, rows, :])
        num = jnp.zeros((ch, LANES), F32)
        den = jnp.zeros((ch, LANES), F32)
        for g in range(len(DILATIONS)):
            w = jnp.exp(mb[g, rows, :] - m_all)
            num = num + nb[g, rows, :] * w
            den = den + sb[g, rows, :] * w
        o_ref[0, rows, :] = (num / den).astype(o_ref.dtype)
        return carry

    lax.fori_loop(0, tile // ch, merge, 0)


def _dilated(hb, tile=BAND_TILE):
    bsz, seq, _ = hb.shape
    ng = len(DILATIONS)
    nblocks = tile // BAND
    assert seq % tile == 0 and nblocks % BAND_GROUP == 0 and all(nblocks % d == 0 for d in DILATIONS)
    cur = lambda cb: pl.BlockSpec((1, tile, LANES), lambda b, p, j: (b, j, cb + p))
    prv = lambda cb: pl.BlockSpec((1, tile, LANES), lambda b, p, j: (b, jnp.maximum(j - 1, 0), cb + p))
    in_specs = [cur(CB_BQ + PAIRS * g) for g in range(ng)]
    for g in range(ng):
        in_specs += [cur(CB_BK + PAIRS * g), prv(CB_BK + PAIRS * g)]
    in_specs += [cur(CB_BV), prv(CB_BV)]
    acc = pltpu.VMEM((ng, tile, LANES), F32)
    return pl.pallas_call(
        functools.partial(_band_kernel, tile=tile),
        grid=(bsz, PAIRS, seq // tile),
        in_specs=in_specs,
        out_specs=pl.BlockSpec((1, tile, LANES), lambda b, p, j: (b, j, p)),
        out_shape=jax.ShapeDtypeStruct((bsz, seq, WIDTH), BF16),
        scratch_shapes=[acc, pltpu.VMEM((ng, 2 * tile, LANES), F32), pltpu.VMEM((2 * tile, LANES), F32), acc, acc, acc],
        compiler_params=_cparams(("parallel", "parallel", "arbitrary")),
        name="band",
    )(*([hb] * len(in_specs)))


def _dot01(a, b, nt=False, pieces=3):
    f = _dot_nt if nt else _dot
    if a.dtype == BF16:
        return sum(f(a, piece.astype(BF16)) for piece in _split3(b)[:pieces])
    return sum(f(piece.astype(BF16), b) for piece in _split3(a)[:pieces])


def _tri_inverse_all(ms):
    c = ms[0].shape[0]
    ri = _iota((c, c), 0)
    ci = _iota((c, c), 1)
    base = 16
    inblk = ri // base == ci // base
    eye = jnp.where(ri == ci, 1.0, 0.0)
    ps = [jnp.where(inblk, -m, 0.0) for m in ms]
    ts = [eye + p for p in ps]
    for _ in range(3):
        pbs = [p.astype(BF16) for p in ps]
        ps = [_dot(pb, pb) for pb in pbs]
        ts = [t + _dot(t.astype(BF16), p.astype(BF16)) for t, p in zip(ts, ps)]
    size = base
    while size < c:
        lower = jnp.logical_and(ri // (2 * size) == ci // (2 * size), ri // size != ci // size)
        tbs = [t.astype(BF16) for t in ts]
        xs = [_dot(tb, jnp.where(lower, m, 0.0).astype(BF16)) for tb, m in zip(tbs, ms)]
        ts = [t - _dot(x.astype(BF16), tb) for t, x, tb in zip(ts, xs, tbs)]
        size *= 2
    return ts


def _gdn_kernel(h_ref, cw_ref, alog_ref, dtb_ref, nw_ref, o_ref,
                xe_ref, u_s, w_s, qd_s, kd_s, qk_s, egl_s, o_s, st_ref, *, tile):
    cc = GDN_CHUNK
    nc = tile // cc
    t = pl.program_id(1)

    @pl.when(t == 0)
    def _():
        xe_ref[0:8, :] = jnp.zeros((8, 3 * WIDTH), F32)
        st_ref[...] = jnp.zeros_like(st_ref)

    x = h_ref[0, :, LANES:]
    xe_ref[8:8 + tile, :] = x
    y = jnp.zeros((tile, 3 * WIDTH), F32)
    for j in range(CONV_K):
        y = y + cw_ref[j:j + 1, :] * xe_ref[8 - (CONV_K - 1) + j:8 - (CONV_K - 1) + j + tile, :]
    xe_ref[0:8, :] = x[tile - 8:tile, :]
    y = y * jax.nn.sigmoid(y)

    bd = ((_iota((WIDTH, WIDTH), 0) // HEAD_DIM) == (_iota((WIDTH, WIDTH), 1) // HEAD_DIM)).astype(BF16)
    q = y[:, 0:WIDTH]
    k = y[:, WIDTH:2 * WIDTH]
    v = y[:, 2 * WIDTH:]
    q = q * lax.rsqrt(_dot01(q * q, bd, pieces=2) + RMS_EPS) * (HEAD_DIM ** -0.5)
    k = k * lax.rsqrt(_dot01(k * k, bd, pieces=2) + RMS_EPS)

    hs = h_ref[0, :, 0:LANES]
    er = _iota((LANES, WIDTH), 0)
    ec = _iota((LANES, WIDTH), 1) // HEAD_DIM
    beta = jax.nn.sigmoid(_dot01(hs, (er == ec + HEADS).astype(BF16)))
    g = -jnp.exp(alog_ref[...]) * jax.nn.softplus(_dot01(hs, (er == ec + 2 * HEADS).astype(BF16)) + dtb_ref[...])

    tri = (_iota((cc, cc), 0) >= _iota((cc, cc), 1)).astype(BF16)
    gc = jnp.concatenate([_dot01(tri, g[c * cc:(c + 1) * cc]) for c in range(nc)], axis=0)
    glast = jnp.broadcast_to(gc.reshape(nc, cc, WIDTH)[:, cc - 1:cc, :], (nc, cc, WIDTH)).reshape(tile, WIDTH)
    eg = jnp.exp(gc)
    kb = k * beta
    vb = v * beta
    wb = kb * eg
    qd_s[...] = q * eg
    kd_s[...] = k * jnp.exp(glast - gc)
    egl_s[...] = jnp.exp(glast)

    ri = _iota((cc, cc), 0)
    ci = _iota((cc, cc), 1)
    incl = ri >= ci
    strict = ri > ci
    lane = _iota((cc, LANES), 1)
    hm = lane < HEAD_DIM
    sls = [slice(p * LANES, (p + 1) * LANES) for p in range(PAIRS)]
    keeps = [hm, jnp.logical_not(hm)]
    heads = [(p, h) for p in range(PAIRS) for h in range(2)]
    chains = [(p, h, c) for p, h in heads for c in range(nc)]
    rws = [slice(c * cc, (c + 1) * cc) for c in range(nc)]
    kps = [k[:, sl].astype(BF16) for sl in sls]
    rhss = [jnp.concatenate([vb[:, sl], wb[:, sl]], axis=1).astype(BF16) for sl in sls]
    gct = gc.T
    rowf = {(p, h): gct[p * LANES + h * HEAD_DIM:p * LANES + h * HEAD_DIM + 1, :] for p, h in heads}
    kbm = {(p, h): jnp.where(jnp.tile(keeps[h], (nc, 1)), kb[:, sls[p]], 0.0).astype(BF16) for p, h in heads}
    qm = {(p, h): jnp.where(jnp.tile(keeps[h], (nc, 1)), q[:, sls[p]], 0.0).astype(BF16) for p, h in heads}
    kks = [_dot_nt(kbm[p, h][rws[c]], kps[p][rws[c]]) for p, h, c in chains]
    qks = [_dot_nt(qm[p, h][rws[c]], kps[p][rws[c]]) for p, h, c in chains]
    mms = []
    for (p, h, c), kk, qk in zip(chains, kks, qks):
        col = p * LANES + h * HEAD_DIM
        diff = gc[rws[c], col:col + 1] - rowf[p, h][:, rws[c]]
        decay = jnp.where(incl, jnp.exp(jnp.where(incl, diff, 0.0)), 0.0)
        mms.append(jnp.where(strict, kk * decay, 0.0))
        qk_s[2 * p + h, rws[c], :] = qk * decay
    tinvs = _tri_inverse_all(mms)
    rs = {ch: _dot(tinv.astype(BF16), rhss[ch[0]][rws[ch[2]]]) for ch, tinv in zip(chains, tinvs)}
    for p in range(PAIRS):
        for c in range(nc):
            r0, r1 = rs[p, 0, c], rs[p, 1, c]
            u_s[rws[c], sls[p]] = jnp.where(hm, r0[:, :LANES], r1[:, :LANES])
            w_s[rws[c], sls[p]] = jnp.where(hm, r0[:, LANES:], r1[:, LANES:])

    bdiag = (_iota((LANES, LANES), 0) // HEAD_DIM) == (_iota((LANES, LANES), 1) // HEAD_DIM)
    for c in range(nc):
        rows = rws[c]
        sts = [st_ref[p] for p in range(PAIRS)]
        stbs = [st.astype(BF16) for st in sts]
        wss = [_dot(w_s[rows, sl].astype(BF16), stb) for sl, stb in zip(sls, stbs)]
        qss = [_dot(qd_s[rows, sl].astype(BF16), stb) for sl, stb in zip(sls, stbs)]
        vnbs = [(u_s[rows, sl] - ws).astype(BF16) for sl, ws in zip(sls, wss)]
        upds = [lax.dot_general(kd_s[rows, sl].astype(BF16), vnb, (((0,), (0,)), ((), ())), preferred_element_type=F32)
                for sl, vnb in zip(sls, vnbs)]
        intra = [(_dot(qk_s[2 * p, rows, :].astype(BF16), vnbs[p]), _dot(qk_s[2 * p + 1, rows, :].astype(BF16), vnbs[p]))
                 for p in range(PAIRS)]
        for p in range(PAIRS):
            st_ref[p] = sts[p] * egl_s[c * cc:c * cc + 1, sls[p]] + jnp.where(bdiag, upds[p], 0.0)
            o_s[rows, sls[p]] = qss[p] + jnp.where(hm, intra[p][0], intra[p][1])

    o = o_s[...]
    ms = _dot01(o * o, bd, pieces=2) * (1.0 / HEAD_DIM)
    o_ref[0] = (o * lax.rsqrt(ms + RMS_EPS) * nw_ref[...]).astype(o_ref.dtype)


def _gdn(hf, conv_w, a_log, dt_bias, norm_w, tile=512):
    bsz, seq, _ = hf.shape
    rep = lambda a: jnp.repeat(a.astype(F32), HEAD_DIM)[None, :]
    wide = pltpu.VMEM((tile, WIDTH), F32)
    return pl.pallas_call(
        functools.partial(_gdn_kernel, tile=tile),
        grid=(bsz, seq // tile),
        in_specs=[
            pl.BlockSpec((1, tile, HF_COLS), lambda b, t: (b, t, 0)),
            pl.BlockSpec((CONV_K, 3 * WIDTH), lambda b, t: (0, 0)),
            pl.BlockSpec((1, WIDTH), lambda b, t: (0, 0)),
            pl.BlockSpec((1, WIDTH), lambda b, t: (0, 0)),
            pl.BlockSpec((1, WIDTH), lambda b, t: (0, 0)),
        ],
        out_specs=pl.BlockSpec((1, tile, WIDTH), lambda b, t: (b, t, 0)),
        out_shape=jax.ShapeDtypeStruct((bsz, seq, WIDTH), BF16),
        scratch_shapes=[pltpu.VMEM((tile + 8, 3 * WIDTH), F32), wide, wide, wide, wide,
                        pltpu.VMEM((HEADS, tile, LANES), F32), wide, wide,
                        pltpu.VMEM((PAIRS, LANES, LANES), F32)],
        compiler_params=_cparams(("parallel", "arbitrary")),
        name="gdn",
    )(hf, conv_w.astype(F32), rep(a_log), rep(dt_bias), jnp.tile(norm_w.astype(F32), HEADS)[None, :])


def _memattn_kernel(q_ref, kv_ref, o_ref):
    tq = q_ref.shape[1]
    q = q_ref[0]
    k = kv_ref[0, :, 0:WIDTH]
    v = kv_ref[0, :, WIDTH:2 * WIDTH]
    head = _iota((tq, WIDTH), 1) // MEM_HEAD_DIM
    zero = jnp.zeros_like(q)
    out = jnp.zeros((tq, WIDTH), F32)
    for h in range(MEM_HEADS):
        s = _dot_nt(jnp.where(head == h, q, zero), k)
        m = jnp.max(s, axis=-1, keepdims=True)
        p = jnp.exp(s - m)
        o = _dot(p.astype(BF16), v) / jnp.sum(p, axis=-1, keepdims=True)
        out = jnp.where(head == h, o, out)
    o_ref[0] = out.astype(o_ref.dtype)


def _memattn(he, kv, layer, tq=512):
    bsz, seq, _ = he.shape
    mlen = kv.shape[1]
    return pl.pallas_call(
        _memattn_kernel,
        grid=(bsz, seq // tq),
        in_specs=[
            pl.BlockSpec((1, tq, WIDTH), lambda b, i: (b, i, 0)),
            pl.BlockSpec((1, mlen, 2 * WIDTH), lambda b, i: (b, 0, layer)),
        ],
        out_specs=pl.BlockSpec((1, tq, WIDTH), lambda b, i: (b, i, 0)),
        out_shape=jax.ShapeDtypeStruct((bsz, seq, WIDTH), BF16),
        compiler_params=_cparams(("parallel", "parallel")),
        name="memattn",
    )(he, kv)


def _out_kernel(oa, ob, oc, od, oe, z_ref, ml_ref, x_ref, wb_ref, wo_ref, g_ref, b_ref, y_ref, yb_ref, *, alpha):
    tm = x_ref.shape[0]
    acc = jnp.zeros((tm, D_MODEL), F32)
    for n, o_ref in enumerate((oa, ob, oc, od, oe)):
        z = z_ref[:, n * WIDTH:(n + 1) * WIDTH]
        gated = o_ref[...] * (z * jax.nn.sigmoid(z))
        yn = _dot(gated, wb_ref[n])
        acc = acc + jax.nn.sigmoid(ml_ref[:, n * D_MODEL:(n + 1) * D_MODEL]).astype(F32) * yn
    r = alpha * x_ref[...] + _dot(acc.astype(BF16), wo_ref[...])
    mu = jnp.mean(r, axis=-1, keepdims=True)
    rc = r - mu
    var = jnp.mean(rc * rc, axis=-1, keepdims=True)
    y = rc * lax.rsqrt(var + LN_EPS) * g_ref[...] + b_ref[...]
    y_ref[...] = y
    yb_ref[...] = y.astype(BF16)


def _out(branches, z, ml, x, w_branch, w_out, ln_g, ln_b, alpha, tm=512):
    m = x.shape[0]
    row = lambda n: pl.BlockSpec((tm, n), lambda i: (i, 0))
    return pl.pallas_call(
        functools.partial(_out_kernel, alpha=alpha),
        grid=(m // tm,),
        in_specs=[row(WIDTH)] * N_BRANCH + [row(N_BRANCH * WIDTH), row(N_BRANCH * D_MODEL), row(D_MODEL),
                  pl.BlockSpec((N_BRANCH, WIDTH, D_MODEL), lambda i: (0, 0, 0)),
                  pl.BlockSpec((D_MODEL, D_MODEL), lambda i: (0, 0)),
                  pl.BlockSpec((1, D_MODEL), lambda i: (0, 0)),
                  pl.BlockSpec((1, D_MODEL), lambda i: (0, 0))],
        out_specs=[row(D_MODEL), row(D_MODEL)],
        out_shape=[jax.ShapeDtypeStruct((m, D_MODEL), F32), jax.ShapeDtypeStruct((m, D_MODEL), BF16)],
        compiler_params=_cparams(("parallel",)),
        name="out",
    )(*branches, z, ml, x, w_branch, w_out, ln_g, ln_b)


def _split_weights(w_in, b_in):
    def take(a, spans):
        return a[..., np.concatenate([np.arange(o, o + n) for o, n in spans])]

    scale = np.ones((HB_COLS,), np.float32)
    for cb in (CB_AQ, CB_CQ):
        scale[cb * LANES:cb * LANES + WIDTH] = HEAD_DIM ** -0.5
    scale[CB_BQ * LANES:CB_BQ * LANES + 3 * WIDTH] = HEAD_DIM ** -0.5
    hf_cols = ((O_CF, HEADS), (O_DBETA, HEADS), (O_DDECAY, HEADS))
    pad = LANES - 3 * HEADS
    depth = w_in.shape[0]
    zw = jnp.zeros((depth, D_MODEL, pad), w_in.dtype)
    w_hf = jnp.concatenate([take(w_in, hf_cols), zw, w_in[..., O_D:O_D + 3 * WIDTH]], axis=-1)
    b_hf = jnp.concatenate([take(b_in, hf_cols), zw[:, 0], b_in[..., O_D:O_D + 3 * WIDTH]], axis=-1)
    groups = {
        "hb": (w_in, b_in[..., :HB_COLS], jnp.asarray(scale)),
        "he": (w_in[..., O_E:O_E + WIDTH], b_in[..., O_E:O_E + WIDTH], jnp.full((WIDTH,), MEM_HEAD_DIM ** -0.5, F32)),
        "hf": (w_hf, b_hf, None),
        "z": (w_in[..., O_Z:O_Z + N_BRANCH * WIDTH], b_in[..., O_Z:O_Z + N_BRANCH * WIDTH], None),
        "ml": (w_in[..., O_MERGE:], b_in[..., O_MERGE:], None),
    }
    out = {}
    for name, (w, b, s) in groups.items():
        s = jnp.ones((b.shape[-1],), F32) if s is None else s
        out[name] = (w.astype(F32), b.astype(F32)[:, None, :], s[None, :])
    return out


def _layer(x, xb, kv, layer, gw, conv_w, a_log, dt_bias, gdn_norm_w, w_branch, w_out, ln_g, ln_b, alpha):
    bsz, seq, d = x.shape
    m = bsz * seq
    xf = x.reshape(m, d)
    hb = _proj(xb, *gw["hb"], layer, BF16, 1024, 1664).reshape(bsz, seq, HB_COLS)
    he = _proj(xb, *gw["he"], layer, BF16, 1024, WIDTH).reshape(bsz, seq, WIDTH)
    hf = _proj(xb, *gw["hf"], layer, F32, 1024, 640).reshape(bsz, seq, HF_COLS)
    z = _proj(xb, *gw["z"], layer, BF16, 1024, 640)
    ml = _proj(xb, *gw["ml"], layer, BF16, 1024, 1024)

    o_a = _moba(hb)
    o_b = _dilated(hb)
    o_c = _fox(hb, _fcum(hf))
    o_d = _gdn(hf, conv_w, a_log, dt_bias, gdn_norm_w)
    o_e = _memattn(he, kv, layer)
    branches = [o.reshape(m, WIDTH) for o in (o_a, o_b, o_c, o_d, o_e)]
    y, yb = _out(branches, z, ml, xf, w_branch.astype(BF16), w_out.astype(BF16),
                 ln_g.astype(F32)[None, :], ln_b.astype(F32)[None, :], alpha)
    return y.reshape(bsz, seq, d), yb


def kernel(x, mem, mem_ln_g, mem_ln_b, w_in, b_in, conv_w, a_log, dt_bias, gdn_norm_w, w_mem_kv, w_branch, w_out, ln_g, ln_b):
    depth = w_in.shape[0]
    alpha = float((2 * depth) ** 0.25)
    w_kv = jnp.concatenate([w_mem_kv[l] for l in range(depth)], axis=1).astype(BF16)
    kv = _memkv(mem.astype(F32), mem_ln_g.astype(F32)[None, :], mem_ln_b.astype(F32)[None, :], w_kv)
    x = x.astype(F32)
    xb = x.reshape(-1, x.shape[-1]).astype(BF16)
    gw = _split_weights(w_in, b_in)
    for l in range(depth):
        x, xb = _layer(x, xb, kv, l, gw, conv_w[l], a_log[l], dt_bias[l], gdn_norm_w[l],
                       w_branch[l], w_out[l], ln_g[l], ln_b[l], alpha)
    return x
```

```python
import functools

import jax
import jax.numpy as jnp
import numpy as np
from jax import lax
from jax.experimental import pallas as pl
from jax.experimental.pallas import tpu as pltpu

F32 = jnp.float32
BF16 = jnp.bfloat16
HI = lax.Precision.HIGHEST

D_MODEL = 1024
HEAD_DIM = 64
HEADS = 6
WIDTH = HEADS * HEAD_DIM
N_BRANCH = 5
LANES = 128
PAIRS = WIDTH // LANES
MOBA_BLOCK = 256
MOBA_TOPK = 3
DILATIONS = (1, 4, 16)
BAND = 128
BAND_TILE = 2048
BAND_GROUP = 8
GDN_CHUNK = 128
CONV_K = 4
MEM_HEADS = 4
MEM_HEAD_DIM = WIDTH // MEM_HEADS
NEG = -1e30
FLASH_UNDERFLOW = 110.0
LN_EPS = 1e-5
RMS_EPS = 1e-6

_SPLIT = (3 * WIDTH, 6 * WIDTH, WIDTH, 3 * WIDTH, HEADS, 3 * WIDTH, HEADS, HEADS, WIDTH, N_BRANCH * WIDTH, N_BRANCH * D_MODEL)
_OFF = tuple(int(v) for v in np.concatenate([[0], np.cumsum(_SPLIT)]))
(O_A, O_BQK, O_BV, O_C, O_CF, O_D, O_DBETA, O_DDECAY, O_E, O_Z, O_MERGE, _) = _OFF

CB_AQ, CB_AK, CB_AV = 0, 3, 6
CB_BQ, CB_BK, CB_BV = 9, 18, 27
CB_CQ, CB_CK, CB_CV = 30, 33, 36
HB_COLS = 39 * LANES
assert (O_A, O_BQK, O_BV, O_C, O_CF) == tuple(LANES * c for c in (CB_AQ, CB_BQ, CB_BV, CB_CQ, 39))
HF_COLS = LANES + 3 * WIDTH

VMEM_LIMIT = 56 * 1024 * 1024


def _cparams(sem):
    return pltpu.CompilerParams(dimension_semantics=sem, vmem_limit_bytes=VMEM_LIMIT)


def _dot(a, b):
    return jnp.dot(a, b, preferred_element_type=F32)


def _dot_nt(a, b):
    return lax.dot_general(a, b, (((1,), (1,)), ((), ())), preferred_element_type=F32)


def _dot_hi(a, b):
    return jnp.dot(a, b, preferred_element_type=F32, precision=HI)


def _dot_nt_hi(a, b):
    return lax.dot_general(a, b, (((1,), (1,)), ((), ())), preferred_element_type=F32, precision=HI)


def _iota(shape, dim):
    return lax.broadcasted_iota(jnp.int32, shape, dim)


def _proj_kernel(x_ref, w_ref, b_ref, s_ref, o_ref, wb_ref):
    @pl.when(pl.program_id(1) == 0)
    def _():
        wb_ref[...] = w_ref[...].astype(BF16)

    acc = _dot(x_ref[...], wb_ref[...])
    o_ref[...] = ((acc + b_ref[...]) * s_ref[...]).astype(o_ref.dtype)


def _proj(xb, w, b, scale, layer, out_dtype, tm, tn):
    m, k = xb.shape
    n = b.shape[-1]
    assert m % tm == 0 and n % tn == 0
    return pl.pallas_call(
        _proj_kernel,
        grid=(n // tn, m // tm),
        in_specs=[
            pl.BlockSpec((tm, k), lambda j, i: (i, 0)),
            pl.BlockSpec((None, k, tn), lambda j, i: (layer, 0, j)),
            pl.BlockSpec((None, 1, tn), lambda j, i: (layer, 0, j)),
            pl.BlockSpec((1, tn), lambda j, i: (0, j)),
        ],
        out_specs=pl.BlockSpec((tm, tn), lambda j, i: (i, j)),
        out_shape=jax.ShapeDtypeStruct((m, n), out_dtype),
        scratch_shapes=[pltpu.VMEM((k, tn), BF16)],
        compiler_params=_cparams(("parallel", "arbitrary")),
        name="proj",
    )(xb, w, b, scale)


def _memkv_kernel(mem_ref, g_ref, b_ref, w_ref, o_ref):
    x = mem_ref[0]
    mu = jnp.mean(x, axis=-1, keepdims=True)
    xc = x - mu
    var = jnp.mean(xc * xc, axis=-1, keepdims=True)
    y = xc * lax.rsqrt(var + LN_EPS) * g_ref[...] + b_ref[...]
    o_ref[0] = _dot(y.astype(BF16), w_ref[...]).astype(o_ref.dtype)


def _memkv(mem, g, b, w):
    bsz, mlen, d = mem.shape
    n = w.shape[1]
    return pl.pallas_call(
        _memkv_kernel,
        grid=(bsz,),
        in_specs=[
            pl.BlockSpec((1, mlen, d), lambda i: (i, 0, 0)),
            pl.BlockSpec((1, d), lambda i: (0, 0)),
            pl.BlockSpec((1, d), lambda i: (0, 0)),
            pl.BlockSpec((d, n), lambda i: (0, 0)),
        ],
        out_specs=pl.BlockSpec((1, mlen, n), lambda i: (i, 0, 0)),
        out_shape=jax.ShapeDtypeStruct((bsz, mlen, n), BF16),
        compiler_params=_cparams(("parallel",)),
        name="memkv",
    )(mem, g, b, w)


def _fcum_kernel(h_ref, o_ref, *, blk):
    seq = h_ref.shape[1]
    tri = (_iota((blk, blk), 0) >= _iota((blk, blk), 1)).astype(F32)

    def body(i, carry):
        rows = pl.ds(pl.multiple_of(i * blk, blk), blk)
        logf = jax.nn.log_sigmoid(h_ref[0, rows, :])
        c = _dot_hi(tri, logf) + carry
        o_ref[0, rows, :] = c
        return c[blk - 1:blk, :]

    lax.fori_loop(0, seq // blk, body, jnp.zeros((1, LANES), F32))


def _fcum(hf):
    bsz, seq, _ = hf.shape
    return pl.pallas_call(
        functools.partial(_fcum_kernel, blk=LANES),
        grid=(bsz,),
        in_specs=[pl.BlockSpec((1, seq, LANES), lambda b: (b, 0, 0))],
        out_specs=pl.BlockSpec((1, seq, LANES), lambda b: (b, 0, 0)),
        out_shape=jax.ShapeDtypeStruct((bsz, seq, LANES), F32),
        compiler_params=_cparams(("parallel",)),
        name="fcum",
    )(hf)


def _head_masks(rows):
    lane = _iota((rows, LANES), 1)
    return lane < HEAD_DIM


def _split3(x):
    hi = x.astype(BF16).astype(F32)
    r = x - hi
    mid = r.astype(BF16).astype(F32)
    return hi, mid, r - mid


def _flash_scratch(tq):
    return [pltpu.VMEM((2 * tq, LANES), F32), pltpu.VMEM((2 * tq, tq), F32), pltpu.VMEM((2 * tq, tq), F32),
            pltpu.VMEM((2 * tq, LANES), F32), pltpu.VMEM((2 * tq, LANES), F32)]


def _augment_values(vx_ref, rows, v):
    vx_ref[rows, 0:LANES] = v
    vx_ref[rows, LANES:] = jnp.ones((v.shape[0], LANES), BF16)


def _flash_causal(qx, kx_ref, vx_ref, scratch, i, tq, past_tiles=None):
    acc_ref, sa_ref, sb_ref, m_ref, l_ref = scratch

    def logits(kt, dst):
        rows = pl.ds(pl.multiple_of(kt * tq, tq), tq)
        dst[...] = _dot_nt(qx, kx_ref[rows, :])

    r = _iota((2 * tq, tq), 0)
    causal = jnp.where(r >= tq, r - tq, r) >= _iota((2 * tq, tq), 1)

    def finish(src, kt, masked=False):
        rows = pl.ds(pl.multiple_of(kt * tq, tq), tq)
        s = src[...]
        if masked:
            s = jnp.where(causal, s, NEG)
        m = m_ref[...]
        m_new = jnp.maximum(m, jnp.max(s, axis=-1, keepdims=True))
        alpha = jnp.exp(m - m_new)
        p = jnp.exp((s - jnp.tile(m_new, (1, tq // LANES))).astype(BF16))
        pv = _dot(p, vx_ref[rows, :])
        l_ref[...] = alpha * l_ref[...] + pv[:, LANES:]
        m_ref[...] = m_new
        acc_ref[...] = alpha * acc_ref[...] + pv[:, 0:LANES]

    acc_ref[...] = jnp.zeros_like(acc_ref)
    m_ref[...] = jnp.full((2 * tq, LANES), NEG, F32)
    l_ref[...] = jnp.zeros((2 * tq, LANES), F32)
    tile = lambda u: jnp.maximum(i - 1 - u, 0)
    logits(i, sa_ref)
    logits(tile(0), sb_ref)
    finish(sa_ref, i, masked=True)
    n = i if past_tiles is None else past_tiles(m_ref)

    def body(v, carry):
        logits(tile(2 * v + 1), sa_ref)
        finish(sb_ref, tile(2 * v))
        logits(tile(2 * v + 2), sb_ref)
        finish(sa_ref, tile(2 * v + 1))
        return carry

    lax.fori_loop(0, n // 2, body, 0)

    @pl.when(n % 2 == 1)
    def _():
        finish(sb_ref, tile(n - 1))

    o = acc_ref[...] / l_ref[...]
    return jnp.where(_head_masks(tq), o[0:tq], o[tq:])


def _head_sumsq(x):
    hm = _head_masks(x.shape[0])
    sq = x * x
    return (jnp.sum(jnp.where(hm, sq, 0.0), axis=1, keepdims=True), jnp.sum(jnp.where(hm, 0.0, sq), axis=1, keepdims=True))


def _fox_kernel(q_ref, k_ref, v_ref, f_ref, o_ref, kx_ref, vx_ref, kn_ref, *flash, tq):
    p = pl.program_id(1)
    i = pl.program_id(2)
    seq = k_ref.shape[1]
    bt = 512

    @pl.when(i == 0)
    def _():
        src = _iota((LANES, LANES), 0)
        dst = _iota((LANES, LANES), 1)
        pm = jnp.logical_or(jnp.logical_and(src == 2 * p, dst < 3),
                            jnp.logical_and(src == 2 * p + 1, jnp.logical_and(dst >= 3, dst < 6))).astype(F32)
        sub = _iota((bt, LANES), 1) % 3

        def build(c, carry):
            rows = pl.ds(pl.multiple_of(c * bt, bt), bt)
            hi, mid, lo = _split3(_dot_hi(f_ref[0, rows, :], pm))
            k = k_ref[0, rows, :]
            kx_ref[rows, 0:LANES] = k
            kx_ref[rows, LANES:] = jnp.where(sub == 0, hi, jnp.where(sub == 1, mid, lo)).astype(BF16)
            _augment_values(vx_ref, rows, v_ref[0, rows, :])
            return tuple(jnp.maximum(c0, jnp.max(s, axis=0, keepdims=True))
                         for c0, s in zip(carry, _head_sumsq(k.astype(F32))))

        zero = jnp.zeros((1, 1), F32)
        kn = lax.fori_loop(0, seq // bt, build, (zero, zero))
        kn_ref[0:1, :] = jnp.broadcast_to(kn[0], (1, LANES))
        kn_ref[1:2, :] = jnp.broadcast_to(kn[1], (1, LANES))

    q = q_ref[0].astype(F32)
    lane = _iota((tq, LANES), 1)
    hm = lane < HEAD_DIM
    top = jnp.concatenate([jnp.where(hm, q, 0.0), jnp.where(lane < 3, -1.0, 0.0)], axis=1)
    bot = jnp.concatenate([jnp.where(hm, 0.0, q), jnp.where(jnp.logical_and(lane >= 3, lane < 6), -1.0, 0.0)], axis=1)
    qx = jnp.concatenate([top, bot], axis=0).astype(BF16)

    def past_tiles(m_ref):
        nt = seq // tq
        fend = f_ref[0, pl.ds(tq - 1, nt, stride=tq), :]
        lane_t = _iota((nt, LANES), 1)
        tpos = _iota((nt, 1), 0)
        m = m_ref[...]
        need = tpos < 0
        for h, qn2 in enumerate(_head_sumsq(q)):
            m_min = jnp.min(m[h * tq:(h + 1) * tq, 0:1], axis=0, keepdims=True)
            qk = jnp.sqrt(jnp.max(qn2, axis=0, keepdims=True) * kn_ref[h:h + 1, 0:1]) * 1.01 + 1.0
            f_h = jnp.sum(jnp.where(lane_t == 2 * p + h, fend, 0.0), axis=1, keepdims=True)
            need = jnp.logical_or(need, qk - f_h > m_min - FLASH_UNDERFLOW)
        first = jnp.min(jnp.where(jnp.logical_and(need, tpos < i), tpos, i).astype(F32))
        return i - first.astype(jnp.int32)

    o_ref[0] = _flash_causal(qx, kx_ref, vx_ref, flash, i, tq, past_tiles).astype(o_ref.dtype)


def _fox(hb, fcol, tq=256):
    bsz, seq, _ = hb.shape
    assert seq % tq == 0
    return pl.pallas_call(
        functools.partial(_fox_kernel, tq=tq),
        grid=(bsz, PAIRS, seq // tq),
        in_specs=[
            pl.BlockSpec((1, tq, LANES), lambda b, p, i: (b, i, CB_CQ + p)),
            pl.BlockSpec((1, seq, LANES), lambda b, p, i: (b, 0, CB_CK + p)),
            pl.BlockSpec((1, seq, LANES), lambda b, p, i: (b, 0, CB_CV + p)),
            pl.BlockSpec((1, seq, LANES), lambda b, p, i: (b, 0, 0)),
        ],
        out_specs=pl.BlockSpec((1, tq, LANES), lambda b, p, i: (b, i, p)),
        out_shape=jax.ShapeDtypeStruct((bsz, seq, WIDTH), BF16),
        scratch_shapes=[pltpu.VMEM((seq, 2 * LANES), BF16), pltpu.VMEM((seq, 2 * LANES), BF16),
                        pltpu.VMEM((8, LANES), F32)] + _flash_scratch(tq),
        compiler_params=_cparams(("parallel", "parallel", "arbitrary")),
        name="fox",
    )(hb, hb, hb, fcol)


def _moba_kernel(q_ref, k_ref, v_ref, o_ref, kx_ref, vx_ref, kmean_ref, *flash, tq):
    i = pl.program_id(2)
    seq = k_ref.shape[1]
    blk = MOBA_BLOCK

    @pl.when(i == 0)
    def _():
        kmean_ref[...] = jnp.zeros_like(kmean_ref)
        lane = _iota((blk, LANES), 1)

        def build(n, carry):
            rows = pl.ds(pl.multiple_of(n * blk, blk), blk)
            k = k_ref[0, rows, :]
            kx_ref[rows, 0:LANES] = k
            kx_ref[rows, LANES:] = jnp.where(lane == n, 1.0, 0.0).astype(BF16)
            _augment_values(vx_ref, rows, v_ref[0, rows, :])
            kmean_ref[pl.ds(n, 1), :] = jnp.sum(k.astype(F32), axis=0, keepdims=True) * (1.0 / blk)
            return carry

        lax.fori_loop(0, seq // blk, build, 0)

    q = q_ref[0].astype(F32)
    hm = _head_masks(tq)
    nbp = -(-(seq // blk) // 8) * 8
    blk_id = _iota((nbp, tq), 0)
    blk_f = blk_id.astype(F32)
    own = i * (tq // blk) + _iota((nbp, tq), 1) // blk
    valid = blk_id < own
    kmean = kmean_ref[0:nbp, :]
    halves = []
    for h in range(2):
        qh = jnp.where(hm, q, 0.0) if h == 0 else jnp.where(hm, 0.0, q)
        g = jnp.where(valid, _dot_nt_hi(kmean, qh), NEG)
        bias = jnp.where(blk_id == own, 0.0, NEG)
        for _ in range(MOBA_TOPK):
            mx = jnp.max(g, axis=0, keepdims=True)
            first = jnp.min(jnp.where(g == mx, blk_f, 1e9), axis=0, keepdims=True)
            pick = blk_f == first
            bias = jnp.where(jnp.logical_and(pick, valid), 0.0, bias)
            g = jnp.where(pick, -3e38, g)
        bias = jnp.concatenate([bias, jnp.zeros((LANES - nbp, tq), F32)], axis=0).T
        halves.append(jnp.concatenate([qh, bias], axis=1))
    qx = jnp.concatenate(halves, axis=0).astype(BF16)
    o_ref[0] = _flash_causal(qx, kx_ref, vx_ref, flash, i, tq).astype(o_ref.dtype)


def _moba(hb, tq=512):
    bsz, seq, _ = hb.shape
    assert seq % tq == 0 and tq % MOBA_BLOCK == 0 and seq // MOBA_BLOCK <= LANES
    return pl.pallas_call(
        functools.partial(_moba_kernel, tq=tq),
        grid=(bsz, PAIRS, seq // tq),
        in_specs=[
            pl.BlockSpec((1, tq, LANES), lambda b, p, i: (b, i, CB_AQ + p)),
            pl.BlockSpec((1, seq, LANES), lambda b, p, i: (b, 0, CB_AK + p)),
            pl.BlockSpec((1, seq, LANES), lambda b, p, i: (b, 0, CB_AV + p)),
        ],
        out_specs=pl.BlockSpec((1, tq, LANES), lambda b, p, i: (b, i, p)),
        out_shape=jax.ShapeDtypeStruct((bsz, seq, WIDTH), BF16),
        scratch_shapes=[pltpu.VMEM((seq, 2 * LANES), BF16), pltpu.VMEM((seq, 2 * LANES), BF16),
                        pltpu.VMEM((LANES, LANES), F32)] + _flash_scratch(tq),
        compiler_params=_cparams(("parallel", "parallel", "arbitrary")),
        name="moba",
    )(hb, hb, hb)


def _band_kernel(q0, q1, q2, k0c, k0p, k1c, k1p, k2c, k2p, vc, vp, o_ref, qb, kb, vb, nb, mb, sb, *, tile):
    jt = pl.program_id(2)
    for g, (qr, kc, kp) in enumerate(((q0, k0c, k0p), (q1, k1c, k1p), (q2, k2c, k2p))):
        qb[g] = qr[0].astype(F32)
        kb[g, 0:tile, :] = kp[0].astype(F32)
        kb[g, tile:, :] = kc[0].astype(F32)
    vb[0:tile, :] = vp[0].astype(F32)
    vb[tile:, :] = vc[0].astype(F32)

    hm = _head_masks(BAND)
    qi = _iota((BAND, 2 * BAND), 0)
    kj = _iota((BAND, 2 * BAND), 1)
    dist = BAND + qi - kj
    band = jnp.logical_and(dist >= 0, dist <= BAND)
    cur_half = kj >= BAND
    nblocks = tile // BAND

    for g, dil in enumerate(DILATIONS):
        per_stream = nblocks // dil

        def step(it, carry, g=g, dil=dil, per_stream=per_stream):
            starts, kstarts, valids = [], [], []
            for j in range(BAND_GROUP):
                idx = it * BAND_GROUP + j
                n = idx % per_stream
                start = idx // per_stream + n * (BAND * dil)
                starts.append(start)
                kstarts.append(tile + start - BAND * dil)
                valids.append(jnp.logical_and(band, jnp.logical_or(cur_half, jnp.logical_or(jt > 0, n > 0))))
            qs = [qb[g, pl.ds(s, BAND, stride=dil), :] for s in starts]
            kks = [kb[g, pl.ds(s, 2 * BAND, stride=dil), :].astype(BF16) for s in kstarts]
            ones = jnp.ones((2 * BAND, LANES), BF16)
            vvs = [jnp.concatenate([vb[pl.ds(s, 2 * BAND, stride=dil), :].astype(BF16), ones], axis=1) for s in kstarts]
            logits = []
            for q, kk, valid in zip(qs, kks, valids):
                for h in range(2):
                    qh = (jnp.where(hm, q, 0.0) if h == 0 else jnp.where(hm, 0.0, q)).astype(BF16)
                    logits.append(jnp.where(valid, _dot_nt(qh, kk), NEG))
            ms = [jnp.max(s, axis=-1, keepdims=True) for s in logits]
            ps = [jnp.exp((s - m).astype(BF16)) for s, m in zip(logits, ms)]
            pvs = [_dot(p, vvs[i // 2]) for i, p in enumerate(ps)]
            nums = [pv[:, 0:LANES] for pv in pvs]
            ss = [pv[:, LANES:] for pv in pvs]
            for j, s in enumerate(starts):
                rows = pl.ds(s, BAND, stride=dil)
                nb[g, rows, :] = jnp.where(hm, nums[2 * j], nums[2 * j + 1])
                mb[g, rows, :] = jnp.where(hm, ms[2 * j], ms[2 * j + 1])
                sb[g, rows, :] = jnp.where(hm, ss[2 * j], ss[2 * j + 1])
            return carry

        lax.fori_loop(0, nblocks // BAND_GROUP, step, 0)

    ch = 256

    def merge(i, carry):
        rows = pl.ds(pl.multiple_of(i * ch, ch), ch)
        m_all = jnp.maximum(jnp.maximum(mb[0, rows, :], mb[1, rows, :]), mb[2, rows, :])
        num = jnp.zeros((ch, LANES), F32)
        den = jnp.zeros((ch, LANES), F32)
        for g in range(len(DILATIONS)):
            w = jnp.exp(mb[g, rows, :] - m_all)
            num = num + nb[g, rows, :] * w
            den = den + sb[g, rows, :] * w
        o_ref[0, rows, :] = (num / den).astype(o_ref.dtype)
        return carry

    lax.fori_loop(0, tile // ch, merge, 0)


def _dilated(hb, tile=BAND_TILE):
    bsz, seq, _ = hb.shape
    ng = len(DILATIONS)
    nblocks = tile // BAND
    assert seq % tile == 0 and nblocks % BAND_GROUP == 0 and all(nblocks % d == 0 for d in DILATIONS)
    cur = lambda cb: pl.BlockSpec((1, tile, LANES), lambda b, p, j: (b, j, cb + p))
    prv = lambda cb: pl.BlockSpec((1, tile, LANES), lambda b, p, j: (b, jnp.maximum(j - 1, 0), cb + p))
    in_specs = [cur(CB_BQ + PAIRS * g) for g in range(ng)]
    for g in range(ng):
        in_specs += [cur(CB_BK + PAIRS * g), prv(CB_BK + PAIRS * g)]
    in_specs += [cur(CB_BV), prv(CB_BV)]
    acc = pltpu.VMEM((ng, tile, LANES), F32)
    return pl.pallas_call(
        functools.partial(_band_kernel, tile=tile),
        grid=(bsz, PAIRS, seq // tile),
        in_specs=in_specs,
        out_specs=pl.BlockSpec((1, tile, LANES), lambda b, p, j: (b, j, p)),
        out_shape=jax.ShapeDtypeStruct((bsz, seq, WIDTH), BF16),
        scratch_shapes=[acc, pltpu.VMEM((ng, 2 * tile, LANES), F32), pltpu.VMEM((2 * tile, LANES), F32), acc, acc, acc],
        compiler_params=_cparams(("parallel", "parallel", "arbitrary")),
        name="band",
    )(*([hb] * len(in_specs)))


def _dot01(a, b, nt=False, pieces=3):
    f = _dot_nt if nt else _dot
    if a.dtype == BF16:
        return sum(f(a, piece.astype(BF16)) for piece in _split3(b)[:pieces])
    return sum(f(piece.astype(BF16), b) for piece in _split3(a)[:pieces])


def _tri_inverse_all(ms):
    c = ms[0].shape[0]
    ri = _iota((c, c), 0)
    ci = _iota((c, c), 1)
    base = 16
    inblk = ri // base == ci // base
    eye = jnp.where(ri == ci, 1.0, 0.0)
    ps = [jnp.where(inblk, -m, 0.0) for m in ms]
    ts = [eye + p for p in ps]
    for _ in range(3):
        pbs = [p.astype(BF16) for p in ps]
        ps = [_dot(pb, pb) for pb in pbs]
        ts = [t + _dot(t.astype(BF16), p.astype(BF16)) for t, p in zip(ts, ps)]
    size = base
    while size < c:
        lower = jnp.logical_and(ri // (2 * size) == ci // (2 * size), ri // size != ci // size)
        tbs = [t.astype(BF16) for t in ts]
        xs = [_dot(tb, jnp.where(lower, m, 0.0).astype(BF16)) for tb, m in zip(tbs, ms)]
        ts = [t - _dot(x.astype(BF16), tb) for t, x, tb in zip(ts, xs, tbs)]
        size *= 2
    return ts


def _gdn_kernel(h_ref, cw_ref, alog_ref, dtb_ref, nw_ref, o_ref,
                xe_ref, u_s, w_s, qd_s, kd_s, qk_s, egl_s, o_s, st_ref, *, tile):
    cc = GDN_CHUNK
    nc = tile // cc
    t = pl.program_id(1)

    @pl.when(t == 0)
    def _():
        xe_ref[0:8, :] = jnp.zeros((8, 3 * WIDTH), F32)
        st_ref[...] = jnp.zeros_like(st_ref)

    x = h_ref[0, :, LANES:]
    xe_ref[8:8 + tile, :] = x
    y = jnp.zeros((tile, 3 * WIDTH), F32)
    for j in range(CONV_K):
        y = y + cw_ref[j:j + 1, :] * xe_ref[8 - (CONV_K - 1) + j:8 - (CONV_K - 1) + j + tile, :]
    xe_ref[0:8, :] = x[tile - 8:tile, :]
    y = y * jax.nn.sigmoid(y)

    bd = ((_iota((WIDTH, WIDTH), 0) // HEAD_DIM) == (_iota((WIDTH, WIDTH), 1) // HEAD_DIM)).astype(BF16)
    q = y[:, 0:WIDTH]
    k = y[:, WIDTH:2 * WIDTH]
    v = y[:, 2 * WIDTH:]
    q = q * lax.rsqrt(_dot01(q * q, bd, pieces=2) + RMS_EPS) * (HEAD_DIM ** -0.5)
    k = k * lax.rsqrt(_dot01(k * k, bd, pieces=2) + RMS_EPS)

    hs = h_ref[0, :, 0:LANES]
    er = _iota((LANES, WIDTH), 0)
    ec = _iota((LANES, WIDTH), 1) // HEAD_DIM
    beta = jax.nn.sigmoid(_dot01(hs, (er == ec + HEADS).astype(BF16)))
    g = -jnp.exp(alog_ref[...]) * jax.nn.softplus(_dot01(hs, (er == ec + 2 * HEADS).astype(BF16)) + dtb_ref[...])

    tri = (_iota((cc, cc), 0) >= _iota((cc, cc), 1)).astype(BF16)
    gc = jnp.concatenate([_dot01(tri, g[c * cc:(c + 1) * cc]) for c in range(nc)], axis=0)
    glast = jnp.broadcast_to(gc.reshape(nc, cc, WIDTH)[:, cc - 1:cc, :], (nc, cc, WIDTH)).reshape(tile, WIDTH)
    eg = jnp.exp(gc)
    kb = k * beta
    vb = v * beta
    wb = kb * eg
    qd_s[...] = q * eg
    kd_s[...] = k * jnp.exp(glast - gc)
    egl_s[...] = jnp.exp(glast)

    ri = _iota((cc, cc), 0)
    ci = _iota((cc, cc), 1)
    incl = ri >= ci
    strict = ri > ci
    lane = _iota((cc, LANES), 1)
    hm = lane < HEAD_DIM
    sls = [slice(p * LANES, (p + 1) * LANES) for p in range(PAIRS)]
    keeps = [hm, jnp.logical_not(hm)]
    heads = [(p, h) for p in range(PAIRS) for h in range(2)]
    chains = [(p, h, c) for p, h in heads for c in range(nc)]
    rws = [slice(c * cc, (c + 1) * cc) for c in range(nc)]
    kps = [k[:, sl].astype(BF16) for sl in sls]
    rhss = [jnp.concatenate([vb[:, sl], wb[:, sl]], axis=1).astype(BF16) for sl in sls]
    gct = gc.T
    rowf = {(p, h): gct[p * LANES + h * HEAD_DIM:p * LANES + h * HEAD_DIM + 1, :] for p, h in heads}
    kbm = {(p, h): jnp.where(jnp.tile(keeps[h], (nc, 1)), kb[:, sls[p]], 0.0).astype(BF16) for p, h in heads}
    qm = {(p, h): jnp.where(jnp.tile(keeps[h], (nc, 1)), q[:, sls[p]], 0.0).astype(BF16) for p, h in heads}
    kks = [_dot_nt(kbm[p, h][rws[c]], kps[p][rws[c]]) for p, h, c in chains]
    qks = [_dot_nt(qm[p, h][rws[c]], kps[p][rws[c]]) for p, h, c in chains]
    mms = []
    for (p, h, c), kk, qk in zip(chains, kks, qks):
        col = p * LANES + h * HEAD_DIM
        diff = gc[rws[c], col:col + 1] - rowf[p, h][:, rws[c]]
        decay = jnp.where(incl, jnp.exp(jnp.where(incl, diff, 0.0)), 0.0)
        mms.append(jnp.where(strict, kk * decay, 0.0))
        qk_s[2 * p + h, rws[c], :] = qk * decay
    tinvs = _tri_inverse_all(mms)
    rs = {ch: _dot(tinv.astype(BF16), rhss[ch[0]][rws[ch[2]]]) for ch, tinv in zip(chains, tinvs)}
    for p in range(PAIRS):
        for c in range(nc):
            r0, r1 = rs[p, 0, c], rs[p, 1, c]
            u_s[rws[c], sls[p]] = jnp.where(hm, r0[:, :LANES], r1[:, :LANES])
            w_s[rws[c], sls[p]] = jnp.where(hm, r0[:, LANES:], r1[:, LANES:])

    bdiag = (_iota((LANES, LANES), 0) // HEAD_DIM) == (_iota((LANES, LANES), 1) // HEAD_DIM)
    for c in range(nc):
        rows = rws[c]
        sts = [st_ref[p] for p in range(PAIRS)]
        stbs = [st.astype(BF16) for st in sts]
        wss = [_dot(w_s[rows, sl].astype(BF16), stb) for sl, stb in zip(sls, stbs)]
        qss = [_dot(qd_s[rows, sl].astype(BF16), stb) for sl, stb in zip(sls, stbs)]
        vnbs = [(u_s[rows, sl] - ws).astype(BF16) for sl, ws in zip(sls, wss)]
        upds = [lax.dot_general(kd_s[rows, sl].astype(BF16), vnb, (((0,), (0,)), ((), ())), preferred_element_type=F32)
                for sl, vnb in zip(sls, vnbs)]
        intra = [(_dot(qk_s[2 * p, rows, :].astype(BF16), vnbs[p]), _dot(qk_s[2 * p + 1, rows, :].astype(BF16), vnbs[p]))
                 for p in range(PAIRS)]
        for p in range(PAIRS):
            st_ref[p] = sts[p] * egl_s[c * cc:c * cc + 1, sls[p]] + jnp.where(bdiag, upds[p], 0.0)
            o_s[rows, sls[p]] = qss[p] + jnp.where(hm, intra[p][0], intra[p][1])

    o = o_s[...]
    ms = _dot01(o * o, bd, pieces=2) * (1.0 / HEAD_DIM)
    o_ref[0] = (o * lax.rsqrt(ms + RMS_EPS) * nw_ref[...]).astype(o_ref.dtype)


def _gdn(hf, conv_w, a_log, dt_bias, norm_w, tile=512):
    bsz, seq, _ = hf.shape
    rep = lambda a: jnp.repeat(a.astype(F32), HEAD_DIM)[None, :]
    wide = pltpu.VMEM((tile, WIDTH), F32)
    return pl.pallas_call(
        functools.partial(_gdn_kernel, tile=tile),
        grid=(bsz, seq // tile),
        in_specs=[
            pl.BlockSpec((1, tile, HF_COLS), lambda b, t: (b, t, 0)),
            pl.BlockSpec((CONV_K, 3 * WIDTH), lambda b, t: (0, 0)),
            pl.BlockSpec((1, WIDTH), lambda b, t: (0, 0)),
            pl.BlockSpec((1, WIDTH), lambda b, t: (0, 0)),
            pl.BlockSpec((1, WIDTH), lambda b, t: (0, 0)),
        ],
        out_specs=pl.BlockSpec((1, tile, WIDTH), lambda b, t: (b, t, 0)),
        out_shape=jax.ShapeDtypeStruct((bsz, seq, WIDTH), BF16),
        scratch_shapes=[pltpu.VMEM((tile + 8, 3 * WIDTH), F32), wide, wide, wide, wide,
                        pltpu.VMEM((HEADS, tile, LANES), F32), wide, wide,
                        pltpu.VMEM((PAIRS, LANES, LANES), F32)],
        compiler_params=_cparams(("parallel", "arbitrary")),
        name="gdn",
    )(hf, conv_w.astype(F32), rep(a_log), rep(dt_bias), jnp.tile(norm_w.astype(F32), HEADS)[None, :])


def _memattn_kernel(q_ref, kv_ref, o_ref):
    tq = q_ref.shape[1]
    q = q_ref[0]
    k = kv_ref[0, :, 0:WIDTH]
    v = kv_ref[0, :, WIDTH:2 * WIDTH]
    head = _iota((tq, WIDTH), 1) // MEM_HEAD_DIM
    zero = jnp.zeros_like(q)
    out = jnp.zeros((tq, WIDTH), F32)
    for h in range(MEM_HEADS):
        s = _dot_nt(jnp.where(head == h, q, zero), k)
        m = jnp.max(s, axis=-1, keepdims=True)
        p = jnp.exp(s - m)
        o = _dot(p.astype(BF16), v) / jnp.sum(p, axis=-1, keepdims=True)
        out = jnp.where(head == h, o, out)
    o_ref[0] = out.astype(o_ref.dtype)


def _memattn(he, kv, layer, tq=512):
    bsz, seq, _ = he.shape
    mlen = kv.shape[1]
    return pl.pallas_call(
        _memattn_kernel,
        grid=(bsz, seq // tq),
        in_specs=[
            pl.BlockSpec((1, tq, WIDTH), lambda b, i: (b, i, 0)),
            pl.BlockSpec((1, mlen, 2 * WIDTH), lambda b, i: (b, 0, layer)),
        ],
        out_specs=pl.BlockSpec((1, tq, WIDTH), lambda b, i: (b, i, 0)),
        out_shape=jax.ShapeDtypeStruct((bsz, seq, WIDTH), BF16),
        compiler_params=_cparams(("parallel", "parallel")),
        name="memattn",
    )(he, kv)


def _out_kernel(oa, ob, oc, od, oe, z_ref, ml_ref, x_ref, wb_ref, wo_ref, g_ref, b_ref, y_ref, yb_ref, *, alpha):
    tm = x_ref.shape[0]
    acc = jnp.zeros((tm, D_MODEL), F32)
    for n, o_ref in enumerate((oa, ob, oc, od, oe)):
        z = z_ref[:, n * WIDTH:(n + 1) * WIDTH]
        gated = o_ref[...] * (z * jax.nn.sigmoid(z))
        yn = _dot(gated, wb_ref[n])
        acc = acc + jax.nn.sigmoid(ml_ref[:, n * D_MODEL:(n + 1) * D_MODEL]).astype(F32) * yn
    r = alpha * x_ref[...] + _dot(acc.astype(BF16), wo_ref[...])
    mu = jnp.mean(r, axis=-1, keepdims=True)
    rc = r - mu
    var = jnp.mean(rc * rc, axis=-1, keepdims=True)
    y = rc * lax.rsqrt(var + LN_EPS) * g_ref[...] + b_ref[...]
    y_ref[...] = y
    yb_ref[...] = y.astype(BF16)


def _out(branches, z, ml, x, w_branch, w_out, ln_g, ln_b, alpha, tm=512):
    m = x.shape[0]
    row = lambda n: pl.BlockSpec((tm, n), lambda i: (i, 0))
    return pl.pallas_call(
        functools.partial(_out_kernel, alpha=alpha),
        grid=(m // tm,),
        in_specs=[row(WIDTH)] * N_BRANCH + [row(N_BRANCH * WIDTH), row(N_BRANCH * D_MODEL), row(D_MODEL),
                  pl.BlockSpec((N_BRANCH, WIDTH, D_MODEL), lambda i: (0, 0, 0)),
                  pl.BlockSpec((D_MODEL, D_MODEL), lambda i: (0, 0)),
                  pl.BlockSpec((1, D_MODEL), lambda i: (0, 0)),
                  pl.BlockSpec((1, D_MODEL), lambda i: (0, 0))],
        out_specs=[row(D_MODEL), row(D_MODEL)],
        out_shape=[jax.ShapeDtypeStruct((m, D_MODEL), F32), jax.ShapeDtypeStruct((m, D_MODEL), BF16)],
        compiler_params=_cparams(("parallel",)),
        name="out",
    )(*branches, z, ml, x, w_branch, w_out, ln_g, ln_b)


def _split_weights(w_in, b_in):
    def take(a, spans):
        return a[..., np.concatenate([np.arange(o, o + n) for o, n in spans])]

    scale = np.ones((HB_COLS,), np.float32)
    for cb in (CB_AQ, CB_CQ):
        scale[cb * LANES:cb * LANES + WIDTH] = HEAD_DIM ** -0.5
    scale[CB_BQ * LANES:CB_BQ * LANES + 3 * WIDTH] = HEAD_DIM ** -0.5
    hf_cols = ((O_CF, HEADS), (O_DBETA, HEADS), (O_DDECAY, HEADS))
    pad = LANES - 3 * HEADS
    depth = w_in.shape[0]
    zw = jnp.zeros((depth, D_MODEL, pad), w_in.dtype)
    w_hf = jnp.concatenate([take(w_in, hf_cols), zw, w_in[..., O_D:O_D + 3 * WIDTH]], axis=-1)
    b_hf = jnp.concatenate([take(b_in, hf_cols), zw[:, 0], b_in[..., O_D:O_D + 3 * WIDTH]], axis=-1)
    groups = {
        "hb": (w_in[..., :HB_COLS], b_in[..., :HB_COLS], jnp.asarray(scale)),
        "he": (w_in[..., O_E:O_E + WIDTH], b_in[..., O_E:O_E + WIDTH], jnp.full((WIDTH,), MEM_HEAD_DIM ** -0.5, F32)),
        "hf": (w_hf, b_hf, None),
        "z": (w_in[..., O_Z:O_Z + N_BRANCH * WIDTH], b_in[..., O_Z:O_Z + N_BRANCH * WIDTH], None),
        "ml": (w_in[..., O_MERGE:], b_in[..., O_MERGE:], None),
    }
    out = {}
    for name, (w, b, s) in groups.items():
        s = jnp.ones((b.shape[-1],), F32) if s is None else s
        out[name] = (w.astype(F32), b.astype(F32)[:, None, :], s[None, :])
    return out


def _layer(x, xb, kv, layer, gw, conv_w, a_log, dt_bias, gdn_norm_w, w_branch, w_out, ln_g, ln_b, alpha):
    bsz, seq, d = x.shape
    m = bsz * seq
    xf = x.reshape(m, d)
    hb = _proj(xb, *gw["hb"], layer, BF16, 1024, 1664).reshape(bsz, seq, HB_COLS)
    he = _proj(xb, *gw["he"], layer, BF16, 1024, WIDTH).reshape(bsz, seq, WIDTH)
    hf = _proj(xb, *gw["hf"], layer, F32, 1024, 640).reshape(bsz, seq, HF_COLS)
    z = _proj(xb, *gw["z"], layer, BF16, 1024, 640)
    ml = _proj(xb, *gw["ml"], layer, BF16, 1024, 1024)

    o_a = _moba(hb)
    o_b = _dilated(hb)
    o_c = _fox(hb, _fcum(hf))
    o_d = _gdn(hf, conv_w, a_log, dt_bias, gdn_norm_w)
    o_e = _memattn(he, kv, layer)
    branches = [o.reshape(m, WIDTH) for o in (o_a, o_b, o_c, o_d, o_e)]
    y, yb = _out(branches, z, ml, xf, w_branch.astype(BF16), w_out.astype(BF16),
                 ln_g.astype(F32)[None, :], ln_b.astype(F32)[None, :], alpha)
    return y.reshape(bsz, seq, d), yb


def kernel(x, mem, mem_ln_g, mem_ln_b, w_in, b_in, conv_w, a_log, dt_bias, gdn_norm_w, w_mem_kv, w_branch, w_out, ln_g, ln_b):
    depth = w_in.shape[0]
    alpha = float((2 * depth) ** 0.25)
    w_kv = jnp.concatenate([w_mem_kv[l] for l in range(depth)], axis=1).astype(BF16)
    kv = _memkv(mem.astype(F32), mem_ln_g.astype(F32)[None, :], mem_ln_b.astype(F32)[None, :], w_kv)
    x = x.astype(F32)
    xb = x.reshape(-1, x.shape[-1]).astype(BF16)
    gw = _split_weights(w_in, b_in)
    for l in range(depth):
        x, xb = _layer(x, xb, kv, l, gw, conv_w[l], a_log[l], dt_bias[l], gdn_norm_w[l],
                       w_branch[l], w_out[l], ln_g[l], ln_b[l], alpha)
    return x
```

```python
import functools

import jax
import jax.numpy as jnp
import numpy as np
from jax import lax
from jax.experimental import pallas as pl
from jax.experimental.pallas import tpu as pltpu

F32 = jnp.float32
BF16 = jnp.bfloat16
HI = lax.Precision.HIGHEST

D_MODEL = 1024
HEAD_DIM = 64
HEADS = 6
WIDTH = HEADS * HEAD_DIM
N_BRANCH = 5
LANES = 128
PAIRS = WIDTH // LANES
MOBA_BLOCK = 256
MOBA_TOPK = 3
DILATIONS = (1, 4, 16)
BAND = 128
BAND_TILE = 2048
BAND_GROUP = 8
GDN_CHUNK = 128
CONV_K = 4
MEM_HEADS = 4
MEM_HEAD_DIM = WIDTH // MEM_HEADS
NEG = -1e30
FLASH_UNDERFLOW = 110.0
LN_EPS = 1e-5
RMS_EPS = 1e-6

_SPLIT = (3 * WIDTH, 6 * WIDTH, WIDTH, 3 * WIDTH, HEADS, 3 * WIDTH, HEADS, HEADS, WIDTH, N_BRANCH * WIDTH, N_BRANCH * D_MODEL)
_OFF = tuple(int(v) for v in np.concatenate([[0], np.cumsum(_SPLIT)]))
(O_A, O_BQK, O_BV, O_C, O_CF, O_D, O_DBETA, O_DDECAY, O_E, O_Z, O_MERGE, _) = _OFF

CB_AQ, CB_AK, CB_AV = 0, 3, 6
CB_BQ, CB_BK, CB_BV = 9, 18, 27
CB_CQ, CB_CK, CB_CV = 30, 33, 36
HB_COLS = 39 * LANES
assert (O_A, O_BQK, O_BV, O_C, O_CF) == tuple(LANES * c for c in (CB_AQ, CB_BQ, CB_BV, CB_CQ, 39))
HF_COLS = LANES + 3 * WIDTH
TAIL = {"z": (0, N_BRANCH * WIDTH, N_BRANCH * WIDTH), "hf": (2 * HF_COLS, HF_COLS, HF_COLS),
        "ml": (4096, N_BRANCH * D_MODEL, 1024), "he": (24 * WIDTH, WIDTH, WIDTH)}
TAIL_COLS = 25 * WIDTH

VMEM_LIMIT = 56 * 1024 * 1024


def _cparams(sem):
    return pltpu.CompilerParams(dimension_semantics=sem, vmem_limit_bytes=VMEM_LIMIT)


def _dot(a, b):
    return jnp.dot(a, b, preferred_element_type=F32)


def _dot_nt(a, b):
    return lax.dot_general(a, b, (((1,), (1,)), ((), ())), preferred_element_type=F32)


def _dot_hi(a, b):
    return jnp.dot(a, b, preferred_element_type=F32, precision=HI)


def _dot_nt_hi(a, b):
    return lax.dot_general(a, b, (((1,), (1,)), ((), ())), preferred_element_type=F32, precision=HI)


def _iota(shape, dim):
    return lax.broadcasted_iota(jnp.int32, shape, dim)


def _proj_kernel(x_ref, w_ref, b_ref, s_ref, o_ref, wb_ref):
    @pl.when(pl.program_id(1) == 0)
    def _():
        wb_ref[...] = w_ref[...].astype(BF16)

    acc = _dot(x_ref[...], wb_ref[...])
    o_ref[...] = ((acc + b_ref[...]) * s_ref[...]).astype(o_ref.dtype)


def _proj(xb, w, b, scale, layer, col0, n, out_dtype, tm, tn):
    m, k = xb.shape
    assert m % tm == 0 and n % tn == 0 and col0 % tn == 0
    c0 = col0 // tn
    return pl.pallas_call(
        _proj_kernel,
        grid=(n // tn, m // tm),
        in_specs=[
            pl.BlockSpec((tm, k), lambda j, i: (i, 0)),
            pl.BlockSpec((None, k, tn), lambda j, i: (layer, 0, c0 + j)),
            pl.BlockSpec((None, 1, tn), lambda j, i: (layer, 0, c0 + j)),
            pl.BlockSpec((1, tn), lambda j, i: (0, c0 + j)),
        ],
        out_specs=pl.BlockSpec((tm, tn), lambda j, i: (i, j)),
        out_shape=jax.ShapeDtypeStruct((m, n), out_dtype),
        scratch_shapes=[pltpu.VMEM((k, tn), BF16)],
        compiler_params=_cparams(("parallel", "arbitrary")),
        name="proj",
    )(xb, w, b, scale)


def _memkv_kernel(mem_ref, g_ref, b_ref, w_ref, o_ref):
    x = mem_ref[0]
    mu = jnp.mean(x, axis=-1, keepdims=True)
    xc = x - mu
    var = jnp.mean(xc * xc, axis=-1, keepdims=True)
    y = xc * lax.rsqrt(var + LN_EPS) * g_ref[...] + b_ref[...]
    o_ref[0] = _dot(y.astype(BF16), w_ref[...]).astype(o_ref.dtype)


def _memkv(mem, g, b, w):
    bsz, mlen, d = mem.shape
    n = w.shape[1]
    return pl.pallas_call(
        _memkv_kernel,
        grid=(bsz,),
        in_specs=[
            pl.BlockSpec((1, mlen, d), lambda i: (i, 0, 0)),
            pl.BlockSpec((1, d), lambda i: (0, 0)),
            pl.BlockSpec((1, d), lambda i: (0, 0)),
            pl.BlockSpec((d, n), lambda i: (0, 0)),
        ],
        out_specs=pl.BlockSpec((1, mlen, n), lambda i: (i, 0, 0)),
        out_shape=jax.ShapeDtypeStruct((bsz, mlen, n), BF16),
        compiler_params=_cparams(("parallel",)),
        name="memkv",
    )(mem, g, b, w)


def _fcum_kernel(h_ref, o_ref, *, blk):
    seq = h_ref.shape[1]
    tri = (_iota((blk, blk), 0) >= _iota((blk, blk), 1)).astype(F32)

    def body(i, carry):
        rows = pl.ds(pl.multiple_of(i * blk, blk), blk)
        logf = jax.nn.log_sigmoid(h_ref[0, rows, :])
        c = _dot_hi(tri, logf) + carry
        o_ref[0, rows, :] = c
        return c[blk - 1:blk, :]

    lax.fori_loop(0, seq // blk, body, jnp.zeros((1, LANES), F32))


def _fcum(hf):
    bsz, seq, _ = hf.shape
    return pl.pallas_call(
        functools.partial(_fcum_kernel, blk=LANES),
        grid=(bsz,),
        in_specs=[pl.BlockSpec((1, seq, LANES), lambda b: (b, 0, 0))],
        out_specs=pl.BlockSpec((1, seq, LANES), lambda b: (b, 0, 0)),
        out_shape=jax.ShapeDtypeStruct((bsz, seq, LANES), F32),
        compiler_params=_cparams(("parallel",)),
        name="fcum",
    )(hf)


def _head_masks(rows):
    lane = _iota((rows, LANES), 1)
    return lane < HEAD_DIM


def _split3(x):
    hi = x.astype(BF16).astype(F32)
    r = x - hi
    mid = r.astype(BF16).astype(F32)
    return hi, mid, r - mid


def _flash_scratch(tq):
    return [pltpu.VMEM((2 * tq, LANES), F32), pltpu.VMEM((2 * tq, tq), F32), pltpu.VMEM((2 * tq, tq), F32),
            pltpu.VMEM((2 * tq, LANES), F32), pltpu.VMEM((2 * tq, LANES), F32)]


def _augment_values(vx_ref, rows, v):
    vx_ref[rows, 0:LANES] = v
    vx_ref[rows, LANES:] = jnp.ones((v.shape[0], LANES), BF16)


def _flash_causal(qx, kx_ref, vx_ref, scratch, i, tq, past_tiles=None):
    acc_ref, sa_ref, sb_ref, m_ref, l_ref = scratch

    def logits(kt, dst):
        rows = pl.ds(pl.multiple_of(kt * tq, tq), tq)
        dst[...] = _dot_nt(qx, kx_ref[rows, :])

    r = _iota((2 * tq, tq), 0)
    causal = jnp.where(r >= tq, r - tq, r) >= _iota((2 * tq, tq), 1)

    def finish(src, kt, masked=False):
        rows = pl.ds(pl.multiple_of(kt * tq, tq), tq)
        s = src[...]
        if masked:
            s = jnp.where(causal, s, NEG)
        m = m_ref[...]
        m_new = jnp.maximum(m, jnp.max(s, axis=-1, keepdims=True))
        alpha = jnp.exp(m - m_new)
        p = jnp.exp((s - jnp.tile(m_new, (1, tq // LANES))).astype(BF16))
        pv = _dot(p, vx_ref[rows, :])
        l_ref[...] = alpha * l_ref[...] + pv[:, LANES:]
        m_ref[...] = m_new
        acc_ref[...] = alpha * acc_ref[...] + pv[:, 0:LANES]

    acc_ref[...] = jnp.zeros_like(acc_ref)
    m_ref[...] = jnp.full((2 * tq, LANES), NEG, F32)
    l_ref[...] = jnp.zeros((2 * tq, LANES), F32)
    tile = lambda u: jnp.maximum(i - 1 - u, 0)
    logits(i, sa_ref)
    logits(tile(0), sb_ref)
    finish(sa_ref, i, masked=True)
    n = i if past_tiles is None else past_tiles(m_ref)

    def body(v, carry):
        logits(tile(2 * v + 1), sa_ref)
        finish(sb_ref, tile(2 * v))
        logits(tile(2 * v + 2), sb_ref)
        finish(sa_ref, tile(2 * v + 1))
        return carry

    lax.fori_loop(0, n // 2, body, 0)

    @pl.when(n % 2 == 1)
    def _():
        finish(sb_ref, tile(n - 1))

    o = acc_ref[...] / l_ref[...]
    return jnp.where(_head_masks(tq), o[0:tq], o[tq:])


def _head_sumsq(x):
    hm = _head_masks(x.shape[0])
    sq = x * x
    return (jnp.sum(jnp.where(hm, sq, 0.0), axis=1, keepdims=True), jnp.sum(jnp.where(hm, 0.0, sq), axis=1, keepdims=True))


def _fox_kernel(q_ref, k_ref, v_ref, f_ref, o_ref, kx_ref, vx_ref, kn_ref, *flash, tq):
    p = pl.program_id(1)
    i = pl.program_id(2)
    seq = k_ref.shape[1]
    bt = 512

    @pl.when(i == 0)
    def _():
        src = _iota((LANES, LANES), 0)
        dst = _iota((LANES, LANES), 1)
        pm = jnp.logical_or(jnp.logical_and(src == 2 * p, dst < 3),
                            jnp.logical_and(src == 2 * p + 1, jnp.logical_and(dst >= 3, dst < 6))).astype(F32)
        sub = _iota((bt, LANES), 1) % 3

        def build(c, carry):
            rows = pl.ds(pl.multiple_of(c * bt, bt), bt)
            hi, mid, lo = _split3(_dot_hi(f_ref[0, rows, :], pm))
            k = k_ref[0, rows, :]
            kx_ref[rows, 0:LANES] = k
            kx_ref[rows, LANES:] = jnp.where(sub == 0, hi, jnp.where(sub == 1, mid, lo)).astype(BF16)
            _augment_values(vx_ref, rows, v_ref[0, rows, :])
            return tuple(jnp.maximum(c0, jnp.max(s, axis=0, keepdims=True))
                         for c0, s in zip(carry, _head_sumsq(k.astype(F32))))

        zero = jnp.zeros((1, 1), F32)
        kn = lax.fori_loop(0, seq // bt, build, (zero, zero))
        kn_ref[0:1, :] = jnp.broadcast_to(kn[0], (1, LANES))
        kn_ref[1:2, :] = jnp.broadcast_to(kn[1], (1, LANES))

    q = q_ref[0].astype(F32)
    lane = _iota((tq, LANES), 1)
    hm = lane < HEAD_DIM
    top = jnp.concatenate([jnp.where(hm, q, 0.0), jnp.where(lane < 3, -1.0, 0.0)], axis=1)
    bot = jnp.concatenate([jnp.where(hm, 0.0, q), jnp.where(jnp.logical_and(lane >= 3, lane < 6), -1.0, 0.0)], axis=1)
    qx = jnp.concatenate([top, bot], axis=0).astype(BF16)

    def past_tiles(m_ref):
        nt = seq // tq
        fend = f_ref[0, pl.ds(tq - 1, nt, stride=tq), :]
        lane_t = _iota((nt, LANES), 1)
        tpos = _iota((nt, 1), 0)
        m = m_ref[...]
        need = tpos < 0
        for h, qn2 in enumerate(_head_sumsq(q)):
            m_min = jnp.min(m[h * tq:(h + 1) * tq, 0:1], axis=0, keepdims=True)
            qk = jnp.sqrt(jnp.max(qn2, axis=0, keepdims=True) * kn_ref[h:h + 1, 0:1]) * 1.01 + 1.0
            f_h = jnp.sum(jnp.where(lane_t == 2 * p + h, fend, 0.0), axis=1, keepdims=True)
            need = jnp.logical_or(need, qk - f_h > m_min - FLASH_UNDERFLOW)
        first = jnp.min(jnp.where(jnp.logical_and(need, tpos < i), tpos, i).astype(F32))
        return i - first.astype(jnp.int32)

    o_ref[0] = _flash_causal(qx, kx_ref, vx_ref, flash, i, tq, past_tiles).astype(o_ref.dtype)


def _fox(hb, fcol, tq=256):
    bsz, seq, _ = hb.shape
    assert seq % tq == 0
    return pl.pallas_call(
        functools.partial(_fox_kernel, tq=tq),
        grid=(bsz, PAIRS, seq // tq),
        in_specs=[
            pl.BlockSpec((1, tq, LANES), lambda b, p, i: (b, i, CB_CQ + p)),
            pl.BlockSpec((1, seq, LANES), lambda b, p, i: (b, 0, CB_CK + p)),
            pl.BlockSpec((1, seq, LANES), lambda b, p, i: (b, 0, CB_CV + p)),
            pl.BlockSpec((1, seq, LANES), lambda b, p, i: (b, 0, 0)),
        ],
        out_specs=pl.BlockSpec((1, tq, LANES), lambda b, p, i: (b, i, p)),
        out_shape=jax.ShapeDtypeStruct((bsz, seq, WIDTH), BF16),
        scratch_shapes=[pltpu.VMEM((seq, 2 * LANES), BF16), pltpu.VMEM((seq, 2 * LANES), BF16),
                        pltpu.VMEM((8, LANES), F32)] + _flash_scratch(tq),
        compiler_params=_cparams(("parallel", "parallel", "arbitrary")),
        name="fox",
    )(hb, hb, hb, fcol)


def _moba_kernel(q_ref, k_ref, v_ref, o_ref, kx_ref, vx_ref, kmean_ref, *flash, tq):
    i = pl.program_id(2)
    seq = k_ref.shape[1]
    blk = MOBA_BLOCK

    @pl.when(i == 0)
    def _():
        kmean_ref[...] = jnp.zeros_like(kmean_ref)
        lane = _iota((blk, LANES), 1)

        def build(n, carry):
            rows = pl.ds(pl.multiple_of(n * blk, blk), blk)
            k = k_ref[0, rows, :]
            kx_ref[rows, 0:LANES] = k
            kx_ref[rows, LANES:] = jnp.where(lane == n, 1.0, 0.0).astype(BF16)
            _augment_values(vx_ref, rows, v_ref[0, rows, :])
            kmean_ref[pl.ds(n, 1), :] = jnp.sum(k.astype(F32), axis=0, keepdims=True) * (1.0 / blk)
            return carry

        lax.fori_loop(0, seq // blk, build, 0)

    q = q_ref[0].astype(F32)
    hm = _head_masks(tq)
    nbp = -(-(seq // blk) // 8) * 8
    blk_id = _iota((nbp, tq), 0)
    blk_f = blk_id.astype(F32)
    own = i * (tq // blk) + _iota((nbp, tq), 1) // blk
    valid = blk_id < own
    kmean = kmean_ref[0:nbp, :]
    halves = []
    for h in range(2):
        qh = jnp.where(hm, q, 0.0) if h == 0 else jnp.where(hm, 0.0, q)
        g = jnp.where(valid, _dot_nt_hi(kmean, qh), NEG)
        bias = jnp.where(blk_id == own, 0.0, NEG)
        for _ in range(MOBA_TOPK):
            mx = jnp.max(g, axis=0, keepdims=True)
            first = jnp.min(jnp.where(g == mx, blk_f, 1e9), axis=0, keepdims=True)
            pick = blk_f == first
            bias = jnp.where(jnp.logical_and(pick, valid), 0.0, bias)
            g = jnp.where(pick, -3e38, g)
        bias = jnp.concatenate([bias, jnp.zeros((LANES - nbp, tq), F32)], axis=0).T
        halves.append(jnp.concatenate([qh, bias], axis=1))
    qx = jnp.concatenate(halves, axis=0).astype(BF16)
    o_ref[0] = _flash_causal(qx, kx_ref, vx_ref, flash, i, tq).astype(o_ref.dtype)


def _moba(hb, tq=512):
    bsz, seq, _ = hb.shape
    assert seq % tq == 0 and tq % MOBA_BLOCK == 0 and seq // MOBA_BLOCK <= LANES
    return pl.pallas_call(
        functools.partial(_moba_kernel, tq=tq),
        grid=(bsz, PAIRS, seq // tq),
        in_specs=[
            pl.BlockSpec((1, tq, LANES), lambda b, p, i: (b, i, CB_AQ + p)),
            pl.BlockSpec((1, seq, LANES), lambda b, p, i: (b, 0, CB_AK + p)),
            pl.BlockSpec((1, seq, LANES), lambda b, p, i: (b, 0, CB_AV + p)),
        ],
        out_specs=pl.BlockSpec((1, tq, LANES), lambda b, p, i: (b, i, p)),
        out_shape=jax.ShapeDtypeStruct((bsz, seq, WIDTH), BF16),
        scratch_shapes=[pltpu.VMEM((seq, 2 * LANES), BF16), pltpu.VMEM((seq, 2 * LANES), BF16),
                        pltpu.VMEM((LANES, LANES), F32)] + _flash_scratch(tq),
        compiler_params=_cparams(("parallel", "parallel", "arbitrary")),
        name="moba",
    )(hb, hb, hb)


def _band_kernel(q0, q1, q2, k0c, k0p, k1c, k1p, k2c, k2p, vc, vp, o_ref, qb, kb, vb, nb, mb, sb, *, tile):
    jt = pl.program_id(2)
    for g, (qr, kc, kp) in enumerate(((q0, k0c, k0p), (q1, k1c, k1p), (q2, k2c, k2p))):
        qb[g] = qr[0].astype(F32)
        kb[g, 0:tile, :] = kp[0].astype(F32)
        kb[g, tile:, :] = kc[0].astype(F32)
    vb[0:tile, :] = vp[0].astype(F32)
    vb[tile:, :] = vc[0].astype(F32)

    hm = _head_masks(BAND)
    qi = _iota((BAND, 2 * BAND), 0)
    kj = _iota((BAND, 2 * BAND), 1)
    dist = BAND + qi - kj
    band = jnp.logical_and(dist >= 0, dist <= BAND)
    cur_half = kj >= BAND
    nblocks = tile // BAND

    for g, dil in enumerate(DILATIONS):
        per_stream = nblocks // dil

        def step(it, carry, g=g, dil=dil, per_stream=per_stream):
            starts, kstarts, valids = [], [], []
            for j in range(BAND_GROUP):
                idx = it * BAND_GROUP + j
                n = idx % per_stream
                start = idx // per_stream + n * (BAND * dil)
                starts.append(start)
                kstarts.append(tile + start - BAND * dil)
                valids.append(jnp.logical_and(band, jnp.logical_or(cur_half, jnp.logical_or(jt > 0, n > 0))))
            qs = [qb[g, pl.ds(s, BAND, stride=dil), :] for s in starts]
            kks = [kb[g, pl.ds(s, 2 * BAND, stride=dil), :].astype(BF16) for s in kstarts]
            ones = jnp.ones((2 * BAND, LANES), BF16)
            vvs = [jnp.concatenate([vb[pl.ds(s, 2 * BAND, stride=dil), :].astype(BF16), ones], axis=1) for s in kstarts]
            logits = []
            for q, kk, valid in zip(qs, kks, valids):
                for h in range(2):
                    qh = (jnp.where(hm, q, 0.0) if h == 0 else jnp.where(hm, 0.0, q)).astype(BF16)
                    logits.append(jnp.where(valid, _dot_nt(qh, kk), NEG))
            ms = [jnp.max(s, axis=-1, keepdims=True) for s in logits]
            ps = [jnp.exp((s - m).astype(BF16)) for s, m in zip(logits, ms)]
            pvs = [_dot(p, vvs[i // 2]) for i, p in enumerate(ps)]
            nums = [pv[:, 0:LANES] for pv in pvs]
            ss = [pv[:, LANES:] for pv in pvs]
            for j, s in enumerate(starts):
                rows = pl.ds(s, BAND, stride=dil)
                nb[g, rows, :] = jnp.where(hm, nums[2 * j], nums[2 * j + 1])
                mb[g, rows, :] = jnp.where(hm, ms[2 * j], ms[2 * j + 1])
                sb[g, rows, :] = jnp.where(hm, ss[2 * j], ss[2 * j + 1])
            return carry

        lax.fori_loop(0, nblocks // BAND_GROUP, step, 0)

    ch = 256

    def merge(i, carry):
        rows = pl.ds(pl.multiple_of(i * ch, ch), ch)
        m_all = jnp.maximum(jnp.maximum(mb[0, rows, :], mb[1, rows, :]), mb[2, rows, :])
        num = jnp.zeros((ch, LANES), F32)
        den = jnp.zeros((ch, LANES), F32)
        for g in range(len(DILATIONS)):
            w = jnp.exp(mb[g, rows, :] - m_all)
            num = num + nb[g, rows, :] * w
            den = den + sb[g, rows, :] * w
        o_ref[0, rows, :] = (num / den).astype(o_ref.dtype)
        return carry

    lax.fori_loop(0, tile // ch, merge, 0)


def _dilated(hb, tile=BAND_TILE):
    bsz, seq, _ = hb.shape
    ng = len(DILATIONS)
    nblocks = tile // BAND
    assert seq % tile == 0 and nblocks % BAND_GROUP == 0 and all(nblocks % d == 0 for d in DILATIONS)
    cur = lambda cb: pl.BlockSpec((1, tile, LANES), lambda b, p, j: (b, j, cb + p))
    prv = lambda cb: pl.BlockSpec((1, tile, LANES), lambda b, p, j: (b, jnp.maximum(j - 1, 0), cb + p))
    in_specs = [cur(CB_BQ + PAIRS * g) for g in range(ng)]
    for g in range(ng):
        in_specs += [cur(CB_BK + PAIRS * g), prv(CB_BK + PAIRS * g)]
    in_specs += [cur(CB_BV), prv(CB_BV)]
    acc = pltpu.VMEM((ng, tile, LANES), F32)
    return pl.pallas_call(
        functools.partial(_band_kernel, tile=tile),
        grid=(bsz, PAIRS, seq // tile),
        in_specs=in_specs,
        out_specs=pl.BlockSpec((1, tile, LANES), lambda b, p, j: (b, j, p)),
        out_shape=jax.ShapeDtypeStruct((bsz, seq, WIDTH), BF16),
        scratch_shapes=[acc, pltpu.VMEM((ng, 2 * tile, LANES), F32), pltpu.VMEM((2 * tile, LANES), F32), acc, acc, acc],
        compiler_params=_cparams(("parallel", "parallel", "arbitrary")),
        name="band",
    )(*([hb] * len(in_specs)))


def _dot01(a, b, nt=False, pieces=3):
    f = _dot_nt if nt else _dot
    if a.dtype == BF16:
        return sum(f(a, piece.astype(BF16)) for piece in _split3(b)[:pieces])
    return sum(f(piece.astype(BF16), b) for piece in _split3(a)[:pieces])


def _tri_inverse_all(ms):
    c = ms[0].shape[0]
    ri = _iota((c, c), 0)
    ci = _iota((c, c), 1)
    base = 16
    inblk = ri // base == ci // base
    eye = jnp.where(ri == ci, 1.0, 0.0)
    ps = [jnp.where(inblk, -m, 0.0) for m in ms]
    ts = [eye + p for p in ps]
    for _ in range(3):
        pbs = [p.astype(BF16) for p in ps]
        ps = [_dot(pb, pb) for pb in pbs]
        ts = [t + _dot(t.astype(BF16), p.astype(BF16)) for t, p in zip(ts, ps)]
    size = base
    while size < c:
        lower = jnp.logical_and(ri // (2 * size) == ci // (2 * size), ri // size != ci // size)
        tbs = [t.astype(BF16) for t in ts]
        xs = [_dot(tb, jnp.where(lower, m, 0.0).astype(BF16)) for tb, m in zip(tbs, ms)]
        ts = [t - _dot(x.astype(BF16), tb) for t, x, tb in zip(ts, xs, tbs)]
        size *= 2
    return ts


def _gdn_kernel(h_ref, cw_ref, alog_ref, dtb_ref, nw_ref, o_ref,
                xe_ref, u_s, w_s, qd_s, kd_s, qk_s, egl_s, o_s, st_ref, *, tile):
    cc = GDN_CHUNK
    nc = tile // cc
    t = pl.program_id(1)

    @pl.when(t == 0)
    def _():
        xe_ref[0:8, :] = jnp.zeros((8, 3 * WIDTH), F32)
        st_ref[...] = jnp.zeros_like(st_ref)

    x = h_ref[0, :, LANES:]
    xe_ref[8:8 + tile, :] = x
    y = jnp.zeros((tile, 3 * WIDTH), F32)
    for j in range(CONV_K):
        y = y + cw_ref[j:j + 1, :] * xe_ref[8 - (CONV_K - 1) + j:8 - (CONV_K - 1) + j + tile, :]
    xe_ref[0:8, :] = x[tile - 8:tile, :]
    y = y * jax.nn.sigmoid(y)

    bd = ((_iota((WIDTH, WIDTH), 0) // HEAD_DIM) == (_iota((WIDTH, WIDTH), 1) // HEAD_DIM)).astype(BF16)
    q = y[:, 0:WIDTH]
    k = y[:, WIDTH:2 * WIDTH]
    v = y[:, 2 * WIDTH:]
    q = q * lax.rsqrt(_dot01(q * q, bd, pieces=2) + RMS_EPS) * (HEAD_DIM ** -0.5)
    k = k * lax.rsqrt(_dot01(k * k, bd, pieces=2) + RMS_EPS)

    hs = h_ref[0, :, 0:LANES]
    er = _iota((LANES, WIDTH), 0)
    ec = _iota((LANES, WIDTH), 1) // HEAD_DIM
    beta = jax.nn.sigmoid(_dot01(hs, (er == ec + HEADS).astype(BF16)))
    g = -jnp.exp(alog_ref[...]) * jax.nn.softplus(_dot01(hs, (er == ec + 2 * HEADS).astype(BF16)) + dtb_ref[...])

    tri = (_iota((cc, cc), 0) >= _iota((cc, cc), 1)).astype(BF16)
    gc = jnp.concatenate([_dot01(tri, g[c * cc:(c + 1) * cc]) for c in range(nc)], axis=0)
    glast = jnp.broadcast_to(gc.reshape(nc, cc, WIDTH)[:, cc - 1:cc, :], (nc, cc, WIDTH)).reshape(tile, WIDTH)
    eg = jnp.exp(gc)
    kb = k * beta
    vb = v * beta
    wb = kb * eg
    qd_s[...] = q * eg
    kd_s[...] = k * jnp.exp(glast - gc)
    egl_s[...] = jnp.exp(glast)

    ri = _iota((cc, cc), 0)
    ci = _iota((cc, cc), 1)
    incl = ri >= ci
    strict = ri > ci
    lane = _iota((cc, LANES), 1)
    hm = lane < HEAD_DIM
    sls = [slice(p * LANES, (p + 1) * LANES) for p in range(PAIRS)]
    keeps = [hm, jnp.logical_not(hm)]
    heads = [(p, h) for p in range(PAIRS) for h in range(2)]
    chains = [(p, h, c) for p, h in heads for c in range(nc)]
    rws = [slice(c * cc, (c + 1) * cc) for c in range(nc)]
    kps = [k[:, sl].astype(BF16) for sl in sls]
    rhss = [jnp.concatenate([vb[:, sl], wb[:, sl]], axis=1).astype(BF16) for sl in sls]
    gct = gc.T
    rowf = {(p, h): gct[p * LANES + h * HEAD_DIM:p * LANES + h * HEAD_DIM + 1, :] for p, h in heads}
    kbm = {(p, h): jnp.where(jnp.tile(keeps[h], (nc, 1)), kb[:, sls[p]], 0.0).astype(BF16) for p, h in heads}
    qm = {(p, h): jnp.where(jnp.tile(keeps[h], (nc, 1)), q[:, sls[p]], 0.0).astype(BF16) for p, h in heads}
    kks = [_dot_nt(kbm[p, h][rws[c]], kps[p][rws[c]]) for p, h, c in chains]
    qks = [_dot_nt(qm[p, h][rws[c]], kps[p][rws[c]]) for p, h, c in chains]
    mms = []
    for (p, h, c), kk, qk in zip(chains, kks, qks):
        col = p * LANES + h * HEAD_DIM
        diff = gc[rws[c], col:col + 1] - rowf[p, h][:, rws[c]]
        decay = jnp.where(incl, jnp.exp(jnp.where(incl, diff, 0.0)), 0.0)
        mms.append(jnp.where(strict, kk * decay, 0.0))
        qk_s[2 * p + h, rws[c], :] = qk * decay
    tinvs = _tri_inverse_all(mms)
    rs = {ch: _dot(tinv.astype(BF16), rhss[ch[0]][rws[ch[2]]]) for ch, tinv in zip(chains, tinvs)}
    for p in range(PAIRS):
        for c in range(nc):
            r0, r1 = rs[p, 0, c], rs[p, 1, c]
            u_s[rws[c], sls[p]] = jnp.where(hm, r0[:, :LANES], r1[:, :LANES])
            w_s[rws[c], sls[p]] = jnp.where(hm, r0[:, LANES:], r1[:, LANES:])

    bdiag = (_iota((LANES, LANES), 0) // HEAD_DIM) == (_iota((LANES, LANES), 1) // HEAD_DIM)
    for c in range(nc):
        rows = rws[c]
        sts = [st_ref[p] for p in range(PAIRS)]
        stbs = [st.astype(BF16) for st in sts]
        wss = [_dot(w_s[rows, sl].astype(BF16), stb) for sl, stb in zip(sls, stbs)]
        qss = [_dot(qd_s[rows, sl].astype(BF16), stb) for sl, stb in zip(sls, stbs)]
        vnbs = [(u_s[rows, sl] - ws).astype(BF16) for sl, ws in zip(sls, wss)]
        upds = [lax.dot_general(kd_s[rows, sl].astype(BF16), vnb, (((0,), (0,)), ((), ())), preferred_element_type=F32)
                for sl, vnb in zip(sls, vnbs)]
        intra = [(_dot(qk_s[2 * p, rows, :].astype(BF16), vnbs[p]), _dot(qk_s[2 * p + 1, rows, :].astype(BF16), vnbs[p]))
                 for p in range(PAIRS)]
        for p in range(PAIRS):
            st_ref[p] = sts[p] * egl_s[c * cc:c * cc + 1, sls[p]] + jnp.where(bdiag, upds[p], 0.0)
            o_s[rows, sls[p]] = qss[p] + jnp.where(hm, intra[p][0], intra[p][1])

    o = o_s[...]
    ms = _dot01(o * o, bd, pieces=2) * (1.0 / HEAD_DIM)
    o_ref[0] = (o * lax.rsqrt(ms + RMS_EPS) * nw_ref[...]).astype(o_ref.dtype)


def _gdn(hf, conv_w, a_log, dt_bias, norm_w, tile=512):
    bsz, seq, _ = hf.shape
    rep = lambda a: jnp.repeat(a.astype(F32), HEAD_DIM)[None, :]
    wide = pltpu.VMEM((tile, WIDTH), F32)
    return pl.pallas_call(
        functools.partial(_gdn_kernel, tile=tile),
        grid=(bsz, seq // tile),
        in_specs=[
            pl.BlockSpec((1, tile, HF_COLS), lambda b, t: (b, t, 0)),
            pl.BlockSpec((CONV_K, 3 * WIDTH), lambda b, t: (0, 0)),
            pl.BlockSpec((1, WIDTH), lambda b, t: (0, 0)),
            pl.BlockSpec((1, WIDTH), lambda b, t: (0, 0)),
            pl.BlockSpec((1, WIDTH), lambda b, t: (0, 0)),
        ],
        out_specs=pl.BlockSpec((1, tile, WIDTH), lambda b, t: (b, t, 0)),
        out_shape=jax.ShapeDtypeStruct((bsz, seq, WIDTH), BF16),
        scratch_shapes=[pltpu.VMEM((tile + 8, 3 * WIDTH), F32), wide, wide, wide, wide,
                        pltpu.VMEM((HEADS, tile, LANES), F32), wide, wide,
                        pltpu.VMEM((PAIRS, LANES, LANES), F32)],
        compiler_params=_cparams(("parallel", "arbitrary")),
        name="gdn",
    )(hf, conv_w.astype(F32), rep(a_log), rep(dt_bias), jnp.tile(norm_w.astype(F32), HEADS)[None, :])


def _memattn_kernel(q_ref, kv_ref, o_ref):
    tq = q_ref.shape[1]
    q = q_ref[0]
    k = kv_ref[0, :, 0:WIDTH]
    v = kv_ref[0, :, WIDTH:2 * WIDTH]
    head = _iota((tq, WIDTH), 1) // MEM_HEAD_DIM
    zero = jnp.zeros_like(q)
    out = jnp.zeros((tq, WIDTH), F32)
    for h in range(MEM_HEADS):
        s = _dot_nt(jnp.where(head == h, q, zero), k)
        m = jnp.max(s, axis=-1, keepdims=True)
        p = jnp.exp(s - m)
        o = _dot(p.astype(BF16), v) / jnp.sum(p, axis=-1, keepdims=True)
        out = jnp.where(head == h, o, out)
    o_ref[0] = out.astype(o_ref.dtype)


def _memattn(he, kv, layer, tq=512):
    bsz, seq, _ = he.shape
    mlen = kv.shape[1]
    return pl.pallas_call(
        _memattn_kernel,
        grid=(bsz, seq // tq),
        in_specs=[
            pl.BlockSpec((1, tq, WIDTH), lambda b, i: (b, i, 0)),
            pl.BlockSpec((1, mlen, 2 * WIDTH), lambda b, i: (b, 0, layer)),
        ],
        out_specs=pl.BlockSpec((1, tq, WIDTH), lambda b, i: (b, i, 0)),
        out_shape=jax.ShapeDtypeStruct((bsz, seq, WIDTH), BF16),
        compiler_params=_cparams(("parallel", "parallel")),
        name="memattn",
    )(he, kv)


def _out_kernel(oa, ob, oc, od, oe, z_ref, ml_ref, x_ref, wb_ref, wo_ref, g_ref, b_ref, y_ref, yb_ref, *, alpha):
    tm = x_ref.shape[0]
    acc = jnp.zeros((tm, D_MODEL), F32)
    for n, o_ref in enumerate((oa, ob, oc, od, oe)):
        z = z_ref[:, n * WIDTH:(n + 1) * WIDTH]
        gated = o_ref[...] * (z * jax.nn.sigmoid(z))
        yn = _dot(gated, wb_ref[n])
        acc = acc + jax.nn.sigmoid(ml_ref[:, n * D_MODEL:(n + 1) * D_MODEL]).astype(F32) * yn
    r = alpha * x_ref[...] + _dot(acc.astype(BF16), wo_ref[...])
    mu = jnp.mean(r, axis=-1, keepdims=True)
    rc = r - mu
    var = jnp.mean(rc * rc, axis=-1, keepdims=True)
    y = rc * lax.rsqrt(var + LN_EPS) * g_ref[...] + b_ref[...]
    y_ref[...] = y
    yb_ref[...] = y.astype(BF16)


def _out(branches, z, ml, x, w_branch, w_out, ln_g, ln_b, alpha, tm=512):
    m = x.shape[0]
    row = lambda n: pl.BlockSpec((tm, n), lambda i: (i, 0))
    return pl.pallas_call(
        functools.partial(_out_kernel, alpha=alpha),
        grid=(m // tm,),
        in_specs=[row(WIDTH)] * N_BRANCH + [row(N_BRANCH * WIDTH), row(N_BRANCH * D_MODEL), row(D_MODEL),
                  pl.BlockSpec((N_BRANCH, WIDTH, D_MODEL), lambda i: (0, 0, 0)),
                  pl.BlockSpec((D_MODEL, D_MODEL), lambda i: (0, 0)),
                  pl.BlockSpec((1, D_MODEL), lambda i: (0, 0)),
                  pl.BlockSpec((1, D_MODEL), lambda i: (0, 0))],
        out_specs=[row(D_MODEL), row(D_MODEL)],
        out_shape=[jax.ShapeDtypeStruct((m, D_MODEL), F32), jax.ShapeDtypeStruct((m, D_MODEL), BF16)],
        compiler_params=_cparams(("parallel",)),
        name="out",
    )(*branches, z, ml, x, w_branch, w_out, ln_g, ln_b)


def _split_weights(w_in, b_in):
    scale = np.ones((HB_COLS,), np.float32)
    for cb in (CB_AQ, CB_CQ):
        scale[cb * LANES:cb * LANES + WIDTH] = HEAD_DIM ** -0.5
    scale[CB_BQ * LANES:CB_BQ * LANES + 3 * WIDTH] = HEAD_DIM ** -0.5
    idx = np.zeros((TAIL_COLS,), np.int32)
    tscale = np.ones((TAIL_COLS,), np.float32)
    put = lambda name, off, cols: idx.__setitem__(slice(TAIL[name][0] + off, TAIL[name][0] + off + len(cols)), cols)
    put("z", 0, np.arange(O_Z, O_Z + N_BRANCH * WIDTH))
    for g, o in enumerate((O_CF, O_DBETA, O_DDECAY)):
        put("hf", g * HEADS, np.arange(o, o + HEADS))
    put("hf", LANES, np.arange(O_D, O_D + 3 * WIDTH))
    put("ml", 0, np.arange(O_MERGE, O_MERGE + N_BRANCH * D_MODEL))
    put("he", 0, np.arange(O_E, O_E + WIDTH))
    tscale[TAIL["he"][0]:TAIL["he"][0] + WIDTH] = MEM_HEAD_DIM ** -0.5
    prep = lambda w, b, s: (w.astype(F32), b.astype(F32)[:, None, :], jnp.asarray(s)[None, :])
    return {"hb": prep(w_in[..., :HB_COLS], b_in[..., :HB_COLS], scale),
            "tail": prep(w_in[..., idx], b_in[..., idx], tscale)}


def _layer(x, xb, kv, layer, gw, conv_w, a_log, dt_bias, gdn_norm_w, w_branch, w_out, ln_g, ln_b, alpha):
    bsz, seq, d = x.shape
    m = bsz * seq
    xf = x.reshape(m, d)
    hb = _proj(xb, *gw["hb"], layer, 0, HB_COLS, BF16, 1024, 1664).reshape(bsz, seq, HB_COLS)
    tail = lambda name, dtype: _proj(xb, *gw["tail"], layer, *TAIL[name][:2], dtype, 1024, TAIL[name][2])
    he = tail("he", BF16).reshape(bsz, seq, WIDTH)
    hf = tail("hf", F32).reshape(bsz, seq, HF_COLS)
    z = tail("z", BF16)
    ml = tail("ml", BF16)

    o_a = _moba(hb)
    o_b = _dilated(hb)
    o_c = _fox(hb, _fcum(hf))
    o_d = _gdn(hf, conv_w, a_log, dt_bias, gdn_norm_w)
    o_e = _memattn(he, kv, layer)
    branches = [o.reshape(m, WIDTH) for o in (o_a, o_b, o_c, o_d, o_e)]
    y, yb = _out(branches, z, ml, xf, w_branch.astype(BF16), w_out.astype(BF16),
                 ln_g.astype(F32)[None, :], ln_b.astype(F32)[None, :], alpha)
    return y.reshape(bsz, seq, d), yb


def kernel(x, mem, mem_ln_g, mem_ln_b, w_in, b_in, conv_w, a_log, dt_bias, gdn_norm_w, w_mem_kv, w_branch, w_out, ln_g, ln_b):
    depth = w_in.shape[0]
    alpha = float((2 * depth) ** 0.25)
    w_kv = jnp.concatenate([w_mem_kv[l] for l in range(depth)], axis=1).astype(BF16)
    kv = _memkv(mem.astype(F32), mem_ln_g.astype(F32)[None, :], mem_ln_b.astype(F32)[None, :], w_kv)
    x = x.astype(F32)
    xb = x.reshape(-1, x.shape[-1]).astype(BF16)
    gw = _split_weights(w_in, b_in)
    for l in range(depth):
        x, xb = _layer(x, xb, kv, l, gw, conv_w[l], a_log[l], dt_bias[l], gdn_norm_w[l],
                       w_branch[l], w_out[l], ln_g[l], ln_b[l], alpha)
    return x
```

```python
import functools

import jax
import jax.numpy as jnp
import numpy as np
from jax import lax
from jax.experimental import pallas as pl
from jax.experimental.pallas import tpu as pltpu

F32 = jnp.float32
BF16 = jnp.bfloat16
HI = lax.Precision.HIGHEST

D_MODEL = 1024
HEAD_DIM = 64
HEADS = 6
WIDTH = HEADS * HEAD_DIM
N_BRANCH = 5
LANES = 128
PAIRS = WIDTH // LANES
MOBA_BLOCK = 256
MOBA_TOPK = 3
DILATIONS = (1, 4, 16)
BAND = 128
BAND_TILE = 2048
BAND_GROUP = 8
GDN_CHUNK = 128
CONV_K = 4
MEM_HEADS = 4
MEM_HEAD_DIM = WIDTH // MEM_HEADS
NEG = -1e30
FLASH_UNDERFLOW = 110.0
LN_EPS = 1e-5
RMS_EPS = 1e-6

_SPLIT = (3 * WIDTH, 6 * WIDTH, WIDTH, 3 * WIDTH, HEADS, 3 * WIDTH, HEADS, HEADS, WIDTH, N_BRANCH * WIDTH, N_BRANCH * D_MODEL)
_OFF = tuple(int(v) for v in np.concatenate([[0], np.cumsum(_SPLIT)]))
(O_A, O_BQK, O_BV, O_C, O_CF, O_D, O_DBETA, O_DDECAY, O_E, O_Z, O_MERGE, _) = _OFF

CB_AQ, CB_AK, CB_AV = 0, 3, 6
CB_BQ, CB_BK, CB_BV = 9, 18, 27
CB_CQ, CB_CK, CB_CV = 30, 33, 36
HB_COLS = 39 * LANES
assert (O_A, O_BQK, O_BV, O_C, O_CF) == tuple(LANES * c for c in (CB_AQ, CB_BQ, CB_BV, CB_CQ, 39))
HF_COLS = LANES + 3 * WIDTH

VMEM_LIMIT = 56 * 1024 * 1024


def _cparams(sem):
    return pltpu.CompilerParams(dimension_semantics=sem, vmem_limit_bytes=VMEM_LIMIT)


def _dot(a, b):
    return jnp.dot(a, b, preferred_element_type=F32)


def _dot_nt(a, b):
    return lax.dot_general(a, b, (((1,), (1,)), ((), ())), preferred_element_type=F32)


def _dot_hi(a, b):
    return jnp.dot(a, b, preferred_element_type=F32, precision=HI)


def _dot_nt_hi(a, b):
    return lax.dot_general(a, b, (((1,), (1,)), ((), ())), preferred_element_type=F32, precision=HI)


def _iota(shape, dim):
    return lax.broadcasted_iota(jnp.int32, shape, dim)


def _proj_kernel(x_ref, w_ref, b_ref, s_ref, o_ref, wb_ref):
    @pl.when(pl.program_id(1) == 0)
    def _():
        wb_ref[...] = w_ref[...].astype(BF16)

    acc = _dot(x_ref[...], wb_ref[...])
    o_ref[...] = ((acc + b_ref[...]) * s_ref[...]).astype(o_ref.dtype)


def _proj(xb, w, b, scale, layer, out_dtype, tm, tn):
    m, k = xb.shape
    n = b.shape[-1]
    assert m % tm == 0 and n % tn == 0
    return pl.pallas_call(
        _proj_kernel,
        grid=(n // tn, m // tm),
        in_specs=[
            pl.BlockSpec((tm, k), lambda j, i: (i, 0)),
            pl.BlockSpec((None, k, tn), lambda j, i: (layer, 0, j)),
            pl.BlockSpec((None, 1, tn), lambda j, i: (layer, 0, j)),
            pl.BlockSpec((1, tn), lambda j, i: (0, j)),
        ],
        out_specs=pl.BlockSpec((tm, tn), lambda j, i: (i, j)),
        out_shape=jax.ShapeDtypeStruct((m, n), out_dtype),
        scratch_shapes=[pltpu.VMEM((k, tn), BF16)],
        compiler_params=_cparams(("parallel", "arbitrary")),
        name="proj",
    )(xb, w, b, scale)


def _memkv_kernel(mem_ref, g_ref, b_ref, w_ref, o_ref):
    x = mem_ref[0]
    mu = jnp.mean(x, axis=-1, keepdims=True)
    xc = x - mu
    var = jnp.mean(xc * xc, axis=-1, keepdims=True)
    y = xc * lax.rsqrt(var + LN_EPS) * g_ref[...] + b_ref[...]
    o_ref[0] = _dot(y.astype(BF16), w_ref[...]).astype(o_ref.dtype)


def _memkv(mem, g, b, w):
    bsz, mlen, d = mem.shape
    n = w.shape[1]
    return pl.pallas_call(
        _memkv_kernel,
        grid=(bsz,),
        in_specs=[
            pl.BlockSpec((1, mlen, d), lambda i: (i, 0, 0)),
            pl.BlockSpec((1, d), lambda i: (0, 0)),
            pl.BlockSpec((1, d), lambda i: (0, 0)),
            pl.BlockSpec((d, n), lambda i: (0, 0)),
        ],
        out_specs=pl.BlockSpec((1, mlen, n), lambda i: (i, 0, 0)),
        out_shape=jax.ShapeDtypeStruct((bsz, mlen, n), BF16),
        compiler_params=_cparams(("parallel",)),
        name="memkv",
    )(mem, g, b, w)


def _fcum_kernel(h_ref, o_ref, *, blk):
    seq = h_ref.shape[1]
    tri = (_iota((blk, blk), 0) >= _iota((blk, blk), 1)).astype(F32)

    def body(i, carry):
        rows = pl.ds(pl.multiple_of(i * blk, blk), blk)
        logf = jax.nn.log_sigmoid(h_ref[0, rows, :])
        c = _dot_hi(tri, logf) + carry
        o_ref[0, rows, :] = c
        return c[blk - 1:blk, :]

    lax.fori_loop(0, seq // blk, body, jnp.zeros((1, LANES), F32))


def _fcum(hf):
    bsz, seq, _ = hf.shape
    return pl.pallas_call(
        functools.partial(_fcum_kernel, blk=LANES),
        grid=(bsz,),
        in_specs=[pl.BlockSpec((1, seq, LANES), lambda b: (b, 0, 0))],
        out_specs=pl.BlockSpec((1, seq, LANES), lambda b: (b, 0, 0)),
        out_shape=jax.ShapeDtypeStruct((bsz, seq, LANES), F32),
        compiler_params=_cparams(("parallel",)),
        name="fcum",
    )(hf)


def _head_masks(rows):
    lane = _iota((rows, LANES), 1)
    return lane < HEAD_DIM


def _split3(x):
    hi = x.astype(BF16).astype(F32)
    r = x - hi
    mid = r.astype(BF16).astype(F32)
    return hi, mid, r - mid


def _flash_scratch(tq):
    return [pltpu.VMEM((2 * tq, LANES), F32), pltpu.VMEM((2 * tq, tq), F32), pltpu.VMEM((2 * tq, tq), F32),
            pltpu.VMEM((2 * tq, LANES), F32), pltpu.VMEM((2 * tq, LANES), F32)]


def _augment_values(vx_ref, rows, v):
    vx_ref[rows, 0:LANES] = v
    vx_ref[rows, LANES:] = jnp.ones((v.shape[0], LANES), BF16)


def _flash_causal(qx, kx_ref, vx_ref, scratch, i, tq, past_tiles=None):
    acc_ref, sa_ref, sb_ref, m_ref, l_ref = scratch

    def logits(kt, dst):
        rows = pl.ds(pl.multiple_of(kt * tq, tq), tq)
        dst[...] = _dot_nt(qx, kx_ref[rows, :])

    r = _iota((2 * tq, tq), 0)
    causal = jnp.where(r >= tq, r - tq, r) >= _iota((2 * tq, tq), 1)

    def finish(src, kt, masked=False):
        rows = pl.ds(pl.multiple_of(kt * tq, tq), tq)
        s = src[...]
        if masked:
            s = jnp.where(causal, s, NEG)
        m = m_ref[...]
        m_new = jnp.maximum(m, jnp.max(s, axis=-1, keepdims=True))
        alpha = jnp.exp(m - m_new)
        p = jnp.exp((s - jnp.tile(m_new, (1, tq // LANES))).astype(BF16))
        pv = _dot(p, vx_ref[rows, :])
        l_ref[...] = alpha * l_ref[...] + pv[:, LANES:]
        m_ref[...] = m_new
        acc_ref[...] = alpha * acc_ref[...] + pv[:, 0:LANES]

    acc_ref[...] = jnp.zeros_like(acc_ref)
    m_ref[...] = jnp.full((2 * tq, LANES), NEG, F32)
    l_ref[...] = jnp.zeros((2 * tq, LANES), F32)
    tile = lambda u: jnp.maximum(i - 1 - u, 0)
    logits(i, sa_ref)
    logits(tile(0), sb_ref)
    finish(sa_ref, i, masked=True)
    n = i if past_tiles is None else past_tiles(m_ref)

    def body(v, carry):
        logits(tile(2 * v + 1), sa_ref)
        finish(sb_ref, tile(2 * v))
        logits(tile(2 * v + 2), sb_ref)
        finish(sa_ref, tile(2 * v + 1))
        return carry

    lax.fori_loop(0, n // 2, body, 0)

    @pl.when(n % 2 == 1)
    def _():
        finish(sb_ref, tile(n - 1))

    o = acc_ref[...] / l_ref[...]
    return jnp.where(_head_masks(tq), o[0:tq], o[tq:])


def _head_sumsq(x):
    hm = _head_masks(x.shape[0])
    sq = x * x
    return (jnp.sum(jnp.where(hm, sq, 0.0), axis=1, keepdims=True), jnp.sum(jnp.where(hm, 0.0, sq), axis=1, keepdims=True))


def _fox_kernel(q_ref, k_ref, v_ref, f_ref, o_ref, kx_ref, vx_ref, kn_ref, *flash, tq):
    p = pl.program_id(1)
    i = pl.program_id(2)
    seq = k_ref.shape[1]
    bt = 512

    @pl.when(i == 0)
    def _():
        src = _iota((LANES, LANES), 0)
        dst = _iota((LANES, LANES), 1)
        pm = jnp.logical_or(jnp.logical_and(src == 2 * p, dst < 3),
                            jnp.logical_and(src == 2 * p + 1, jnp.logical_and(dst >= 3, dst < 6))).astype(F32)
        sub = _iota((bt, LANES), 1) % 3

        def build(c, carry):
            rows = pl.ds(pl.multiple_of(c * bt, bt), bt)
            hi, mid, lo = _split3(_dot_hi(f_ref[0, rows, :], pm))
            k = k_ref[0, rows, :]
            kx_ref[rows, 0:LANES] = k
            kx_ref[rows, LANES:] = jnp.where(sub == 0, hi, jnp.where(sub == 1, mid, lo)).astype(BF16)
            _augment_values(vx_ref, rows, v_ref[0, rows, :])
            return tuple(jnp.maximum(c0, jnp.max(s, axis=0, keepdims=True))
                         for c0, s in zip(carry, _head_sumsq(k.astype(F32))))

        zero = jnp.zeros((1, 1), F32)
        kn = lax.fori_loop(0, seq // bt, build, (zero, zero))
        kn_ref[0:1, :] = jnp.broadcast_to(kn[0], (1, LANES))
        kn_ref[1:2, :] = jnp.broadcast_to(kn[1], (1, LANES))

    q = q_ref[0].astype(F32)
    lane = _iota((tq, LANES), 1)
    hm = lane < HEAD_DIM
    top = jnp.concatenate([jnp.where(hm, q, 0.0), jnp.where(lane < 3, -1.0, 0.0)], axis=1)
    bot = jnp.concatenate([jnp.where(hm, 0.0, q), jnp.where(jnp.logical_and(lane >= 3, lane < 6), -1.0, 0.0)], axis=1)
    qx = jnp.concatenate([top, bot], axis=0).astype(BF16)

    def past_tiles(m_ref):
        nt = seq // tq
        fend = f_ref[0, pl.ds(tq - 1, nt, stride=tq), :]
        lane_t = _iota((nt, LANES), 1)
        tpos = _iota((nt, 1), 0)
        m = m_ref[...]
        need = tpos < 0
        for h, qn2 in enumerate(_head_sumsq(q)):
            m_min = jnp.min(m[h * tq:(h + 1) * tq, 0:1], axis=0, keepdims=True)
            qk = jnp.sqrt(jnp.max(qn2, axis=0, keepdims=True) * kn_ref[h:h + 1, 0:1]) * 1.01 + 1.0
            f_h = jnp.sum(jnp.where(lane_t == 2 * p + h, fend, 0.0), axis=1, keepdims=True)
            need = jnp.logical_or(need, qk - f_h > m_min - FLASH_UNDERFLOW)
        first = jnp.min(jnp.where(jnp.logical_and(need, tpos < i), tpos, i).astype(F32))
        return i - first.astype(jnp.int32)

    o_ref[0] = _flash_causal(qx, kx_ref, vx_ref, flash, i, tq, past_tiles).astype(o_ref.dtype)


def _fox(hb, fcol, tq=256):
    bsz, seq, _ = hb.shape
    assert seq % tq == 0
    return pl.pallas_call(
        functools.partial(_fox_kernel, tq=tq),
        grid=(bsz, PAIRS, seq // tq),
        in_specs=[
            pl.BlockSpec((1, tq, LANES), lambda b, p, i: (b, i, CB_CQ + p)),
            pl.BlockSpec((1, seq, LANES), lambda b, p, i: (b, 0, CB_CK + p)),
            pl.BlockSpec((1, seq, LANES), lambda b, p, i: (b, 0, CB_CV + p)),
            pl.BlockSpec((1, seq, LANES), lambda b, p, i: (b, 0, 0)),
        ],
        out_specs=pl.BlockSpec((1, tq, LANES), lambda b, p, i: (b, i, p)),
        out_shape=jax.ShapeDtypeStruct((bsz, seq, WIDTH), BF16),
        scratch_shapes=[pltpu.VMEM((seq, 2 * LANES), BF16), pltpu.VMEM((seq, 2 * LANES), BF16),
                        pltpu.VMEM((8, LANES), F32)] + _flash_scratch(tq),
        compiler_params=_cparams(("parallel", "parallel", "arbitrary")),
        name="fox",
    )(hb, hb, hb, fcol)


def _moba_kernel(q_ref, k_ref, v_ref, o_ref, kx_ref, vx_ref, kmean_ref, *flash, tq):
    i = pl.program_id(2)
    seq = k_ref.shape[1]
    blk = MOBA_BLOCK

    @pl.when(i == 0)
    def _():
        kmean_ref[...] = jnp.zeros_like(kmean_ref)
        lane = _iota((blk, LANES), 1)

        def build(n, carry):
            rows = pl.ds(pl.multiple_of(n * blk, blk), blk)
            k = k_ref[0, rows, :]
            kx_ref[rows, 0:LANES] = k
            kx_ref[rows, LANES:] = jnp.where(lane == n, 1.0, 0.0).astype(BF16)
            _augment_values(vx_ref, rows, v_ref[0, rows, :])
            kmean_ref[pl.ds(n, 1), :] = jnp.sum(k.astype(F32), axis=0, keepdims=True) * (1.0 / blk)
            return carry

        lax.fori_loop(0, seq // blk, build, 0)

    q = q_ref[0].astype(F32)
    hm = _head_masks(tq)
    nbp = -(-(seq // blk) // 8) * 8
    blk_id = _iota((nbp, tq), 0)
    blk_f = blk_id.astype(F32)
    own = i * (tq // blk) + _iota((nbp, tq), 1) // blk
    valid = blk_id < own
    kmean = kmean_ref[0:nbp, :]
    halves = []
    for h in range(2):
        qh = jnp.where(hm, q, 0.0) if h == 0 else jnp.where(hm, 0.0, q)
        g = jnp.where(valid, _dot_nt_hi(kmean, qh), NEG)
        bias = jnp.where(blk_id == own, 0.0, NEG)
        for _ in range(MOBA_TOPK):
            mx = jnp.max(g, axis=0, keepdims=True)
            first = jnp.min(jnp.where(g == mx, blk_f, 1e9), axis=0, keepdims=True)
            pick = blk_f == first
            bias = jnp.where(jnp.logical_and(pick, valid), 0.0, bias)
            g = jnp.where(pick, -3e38, g)
        bias = jnp.concatenate([bias, jnp.zeros((LANES - nbp, tq), F32)], axis=0).T
        halves.append(jnp.concatenate([qh, bias], axis=1))
    qx = jnp.concatenate(halves, axis=0).astype(BF16)
    o_ref[0] = _flash_causal(qx, kx_ref, vx_ref, flash, i, tq).astype(o_ref.dtype)


def _moba(hb, tq=512):
    bsz, seq, _ = hb.shape
    assert seq % tq == 0 and tq % MOBA_BLOCK == 0 and seq // MOBA_BLOCK <= LANES
    return pl.pallas_call(
        functools.partial(_moba_kernel, tq=tq),
        grid=(bsz, PAIRS, seq // tq),
        in_specs=[
            pl.BlockSpec((1, tq, LANES), lambda b, p, i: (b, i, CB_AQ + p)),
            pl.BlockSpec((1, seq, LANES), lambda b, p, i: (b, 0, CB_AK + p)),
            pl.BlockSpec((1, seq, LANES), lambda b, p, i: (b, 0, CB_AV + p)),
        ],
        out_specs=pl.BlockSpec((1, tq, LANES), lambda b, p, i: (b, i, p)),
        out_shape=jax.ShapeDtypeStruct((bsz, seq, WIDTH), BF16),
        scratch_shapes=[pltpu.VMEM((seq, 2 * LANES), BF16), pltpu.VMEM((seq, 2 * LANES), BF16),
                        pltpu.VMEM((LANES, LANES), F32)] + _flash_scratch(tq),
        compiler_params=_cparams(("parallel", "parallel", "arbitrary")),
        name="moba",
    )(hb, hb, hb)


def _band_kernel(q0, q1, q2, k0c, k0p, k1c, k1p, k2c, k2p, vc, vp, o_ref, qb, kb, vb, nb, mb, sb, *, tile):
    jt = pl.program_id(2)
    for g, (qr, kc, kp) in enumerate(((q0, k0c, k0p), (q1, k1c, k1p), (q2, k2c, k2p))):
        qb[g] = qr[0].astype(F32)
        kb[g, 0:tile, :] = kp[0].astype(F32)
        kb[g, tile:, :] = kc[0].astype(F32)
    vb[0:tile, :] = vp[0].astype(F32)
    vb[tile:, :] = vc[0].astype(F32)

    hm = _head_masks(BAND)
    qi = _iota((BAND, 2 * BAND), 0)
    kj = _iota((BAND, 2 * BAND), 1)
    dist = BAND + qi - kj
    band = jnp.logical_and(dist >= 0, dist <= BAND)
    cur_half = kj >= BAND
    nblocks = tile // BAND

    for g, dil in enumerate(DILATIONS):
        per_stream = nblocks // dil

        def step(it, carry, g=g, dil=dil, per_stream=per_stream):
            starts, kstarts, valids = [], [], []
            for j in range(BAND_GROUP):
                idx = it * BAND_GROUP + j
                n = idx % per_stream
                start = idx // per_stream + n * (BAND * dil)
                starts.append(start)
                kstarts.append(tile + start - BAND * dil)
                valids.append(jnp.logical_and(band, jnp.logical_or(cur_half, jnp.logical_or(jt > 0, n > 0))))
            qs = [qb[g, pl.ds(s, BAND, stride=dil), :] for s in starts]
            kks = [kb[g, pl.ds(s, 2 * BAND, stride=dil), :].astype(BF16) for s in kstarts]
            ones = jnp.ones((2 * BAND, LANES), BF16)
            vvs = [jnp.concatenate([vb[pl.ds(s, 2 * BAND, stride=dil), :].astype(BF16), ones], axis=1) for s in kstarts]
            logits = []
            for q, kk, valid in zip(qs, kks, valids):
                for h in range(2):
                    qh = (jnp.where(hm, q, 0.0) if h == 0 else jnp.where(hm, 0.0, q)).astype(BF16)
                    logits.append(jnp.where(valid, _dot_nt(qh, kk), NEG))
            ms = [jnp.max(s, axis=-1, keepdims=True) for s in logits]
            ps = [jnp.exp((s - m).astype(BF16)) for s, m in zip(logits, ms)]
            pvs = [_dot(p, vvs[i // 2]) for i, p in enumerate(ps)]
            nums = [pv[:, 0:LANES] for pv in pvs]
            ss = [pv[:, LANES:] for pv in pvs]
            for j, s in enumerate(starts):
                rows = pl.ds(s, BAND, stride=dil)
                nb[g, rows, :] = jnp.where(hm, nums[2 * j], nums[2 * j + 1])
                mb[g, rows, :] = jnp.where(hm, ms[2 * j], ms[2 * j + 1])
                sb[g, rows, :] = jnp.where(hm, ss[2 * j], ss[2 * j + 1])
            return carry

        lax.fori_loop(0, nblocks // BAND_GROUP, step, 0)

    ch = 256

    def merge(i, carry):
        rows = pl.ds(pl.multiple_of(i * ch, ch), ch)
        m_all = jnp.maximum(jnp.maximum(mb[0, rows, :], mb[1, rows, :]), mb[2, rows, :])
        num = jnp.zeros((ch, LANES), F32)
        den = jnp.zeros((ch, LANES), F32)
        for g in range(len(DILATIONS)):
            w = jnp.exp(mb[g, rows, :] - m_all)
            num = num + nb[g, rows, :] * w
            den = den + sb[g, rows, :] * w
        o_ref[0, rows, :] = (num / den).astype(o_ref.dtype)
        return carry

    lax.fori_loop(0, tile // ch, merge, 0)


def _dilated(hb, tile=BAND_TILE):
    bsz, seq, _ = hb.shape
    ng = len(DILATIONS)
    nblocks = tile // BAND
    assert seq % tile == 0 and nblocks % BAND_GROUP == 0 and all(nblocks % d == 0 for d in DILATIONS)
    cur = lambda cb: pl.BlockSpec((1, tile, LANES), lambda b, p, j: (b, j, cb + p))
    prv = lambda cb: pl.BlockSpec((1, tile, LANES), lambda b, p, j: (b, jnp.maximum(j - 1, 0), cb + p))
    in_specs = [cur(CB_BQ + PAIRS * g) for g in range(ng)]
    for g in range(ng):
        in_specs += [cur(CB_BK + PAIRS * g), prv(CB_BK + PAIRS * g)]
    in_specs += [cur(CB_BV), prv(CB_BV)]
    acc = pltpu.VMEM((ng, tile, LANES), F32)
    return pl.pallas_call(
        functools.partial(_band_kernel, tile=tile),
        grid=(bsz, PAIRS, seq // tile),
        in_specs=in_specs,
        out_specs=pl.BlockSpec((1, tile, LANES), lambda b, p, j: (b, j, p)),
        out_shape=jax.ShapeDtypeStruct((bsz, seq, WIDTH), BF16),
        scratch_shapes=[acc, pltpu.VMEM((ng, 2 * tile, LANES), F32), pltpu.VMEM((2 * tile, LANES), F32), acc, acc, acc],
        compiler_params=_cparams(("parallel", "parallel", "arbitrary")),
        name="band",
    )(*([hb] * len(in_specs)))


def _dot01(a, b, nt=False, pieces=3):
    f = _dot_nt if nt else _dot
    if a.dtype == BF16:
        return sum(f(a, piece.astype(BF16)) for piece in _split3(b)[:pieces])
    return sum(f(piece.astype(BF16), b) for piece in _split3(a)[:pieces])


def _tri_inverse_all(ms):
    c = ms[0].shape[0]
    ri = _iota((c, c), 0)
    ci = _iota((c, c), 1)
    base = 16
    inblk = ri // base == ci // base
    eye = jnp.where(ri == ci, 1.0, 0.0)
    ps = [jnp.where(inblk, -m, 0.0) for m in ms]
    ts = [eye + p for p in ps]
    for _ in range(3):
        pbs = [p.astype(BF16) for p in ps]
        ps = [_dot(pb, pb) for pb in pbs]
        ts = [t + _dot(t.astype(BF16), p.astype(BF16)) for t, p in zip(ts, ps)]
    size = base
    while size < c:
        lower = jnp.logical_and(ri // (2 * size) == ci // (2 * size), ri // size != ci // size)
        tbs = [t.astype(BF16) for t in ts]
        xs = [_dot(tb, jnp.where(lower, m, 0.0).astype(BF16)) for tb, m in zip(tbs, ms)]
        ts = [t - _dot(x.astype(BF16), tb) for t, x, tb in zip(ts, xs, tbs)]
        size *= 2
    return ts


def _gdn_kernel(h_ref, cw_ref, alog_ref, dtb_ref, nw_ref, o_ref,
                xe_ref, u_s, w_s, qd_s, kd_s, qk_s, egl_s, o_s, st_ref, *, tile):
    cc = GDN_CHUNK
    nc = tile // cc
    t = pl.program_id(1)

    @pl.when(t == 0)
    def _():
        xe_ref[0:8, :] = jnp.zeros((8, 3 * WIDTH), F32)
        st_ref[...] = jnp.zeros_like(st_ref)

    x = h_ref[0, :, LANES:]
    xe_ref[8:8 + tile, :] = x
    y = jnp.zeros((tile, 3 * WIDTH), F32)
    for j in range(CONV_K):
        y = y + cw_ref[j:j + 1, :] * xe_ref[8 - (CONV_K - 1) + j:8 - (CONV_K - 1) + j + tile, :]
    xe_ref[0:8, :] = x[tile - 8:tile, :]
    y = y * jax.nn.sigmoid(y)

    bd = ((_iota((WIDTH, WIDTH), 0) // HEAD_DIM) == (_iota((WIDTH, WIDTH), 1) // HEAD_DIM)).astype(BF16)
    q = y[:, 0:WIDTH]
    k = y[:, WIDTH:2 * WIDTH]
    v = y[:, 2 * WIDTH:]
    q = q * lax.rsqrt(_dot01(q * q, bd, pieces=2) + RMS_EPS) * (HEAD_DIM ** -0.5)
    k = k * lax.rsqrt(_dot01(k * k, bd, pieces=2) + RMS_EPS)

    hs = h_ref[0, :, 0:LANES]
    er = _iota((LANES, WIDTH), 0)
    ec = _iota((LANES, WIDTH), 1) // HEAD_DIM
    beta = jax.nn.sigmoid(_dot01(hs, (er == ec + HEADS).astype(BF16)))
    g = -jnp.exp(alog_ref[...]) * jax.nn.softplus(_dot01(hs, (er == ec + 2 * HEADS).astype(BF16)) + dtb_ref[...])

    tri = (_iota((cc, cc), 0) >= _iota((cc, cc), 1)).astype(BF16)
    gc = jnp.concatenate([_dot01(tri, g[c * cc:(c + 1) * cc]) for c in range(nc)], axis=0)
    glast = jnp.broadcast_to(gc.reshape(nc, cc, WIDTH)[:, cc - 1:cc, :], (nc, cc, WIDTH)).reshape(tile, WIDTH)
    eg = jnp.exp(gc)
    kb = k * beta
    vb = v * beta
    wb = kb * eg
    qd_s[...] = q * eg
    kd_s[...] = k * jnp.exp(glast - gc)
    egl_s[...] = jnp.exp(glast)

    ri = _iota((cc, cc), 0)
    ci = _iota((cc, cc), 1)
    incl = ri >= ci
    strict = ri > ci
    lane = _iota((cc, LANES), 1)
    hm = lane < HEAD_DIM
    sls = [slice(p * LANES, (p + 1) * LANES) for p in range(PAIRS)]
    keeps = [hm, jnp.logical_not(hm)]
    heads = [(p, h) for p in range(PAIRS) for h in range(2)]
    chains = [(p, h, c) for p, h in heads for c in range(nc)]
    rws = [slice(c * cc, (c + 1) * cc) for c in range(nc)]
    kps = [k[:, sl].astype(BF16) for sl in sls]
    rhss = [jnp.concatenate([vb[:, sl], wb[:, sl]], axis=1).astype(BF16) for sl in sls]
    gct = gc.T
    rowf = {(p, h): gct[p * LANES + h * HEAD_DIM:p * LANES + h * HEAD_DIM + 1, :] for p, h in heads}
    kbm = {(p, h): jnp.where(jnp.tile(keeps[h], (nc, 1)), kb[:, sls[p]], 0.0).astype(BF16) for p, h in heads}
    qm = {(p, h): jnp.where(jnp.tile(keeps[h], (nc, 1)), q[:, sls[p]], 0.0).astype(BF16) for p, h in heads}
    kks = [_dot_nt(kbm[p, h][rws[c]], kps[p][rws[c]]) for p, h, c in chains]
    qks = [_dot_nt(qm[p, h][rws[c]], kps[p][rws[c]]) for p, h, c in chains]
    mms = []
    for (p, h, c), kk, qk in zip(chains, kks, qks):
        col = p * LANES + h * HEAD_DIM
        diff = gc[rws[c], col:col + 1] - rowf[p, h][:, rws[c]]
        decay = jnp.where(incl, jnp.exp(jnp.where(incl, diff, 0.0)), 0.0)
        mms.append(jnp.where(strict, kk * decay, 0.0))
        qk_s[2 * p + h, rws[c], :] = qk * decay
    tinvs = _tri_inverse_all(mms)
    rs = {ch: _dot(tinv.astype(BF16), rhss[ch[0]][rws[ch[2]]]) for ch, tinv in zip(chains, tinvs)}
    for p in range(PAIRS):
        for c in range(nc):
            r0, r1 = rs[p, 0, c], rs[p, 1, c]
            u_s[rws[c], sls[p]] = jnp.where(hm, r0[:, :LANES], r1[:, :LANES])
            w_s[rws[c], sls[p]] = jnp.where(hm, r0[:, LANES:], r1[:, LANES:])

    bdiag = (_iota((LANES, LANES), 0) // HEAD_DIM) == (_iota((LANES, LANES), 1) // HEAD_DIM)
    for c in range(nc):
        rows = rws[c]
        sts = [st_ref[p] for p in range(PAIRS)]
        stbs = [st.astype(BF16) for st in sts]
        wss = [_dot(w_s[rows, sl].astype(BF16), stb) for sl, stb in zip(sls, stbs)]
        qss = [_dot(qd_s[rows, sl].astype(BF16), stb) for sl, stb in zip(sls, stbs)]
        vnbs = [(u_s[rows, sl] - ws).astype(BF16) for sl, ws in zip(sls, wss)]
        upds = [lax.dot_general(kd_s[rows, sl].astype(BF16), vnb, (((0,), (0,)), ((), ())), preferred_element_type=F32)
                for sl, vnb in zip(sls, vnbs)]
        intra = [(_dot(qk_s[2 * p, rows, :].astype(BF16), vnbs[p]), _dot(qk_s[2 * p + 1, rows, :].astype(BF16), vnbs[p]))
                 for p in range(PAIRS)]
        for p in range(PAIRS):
            st_ref[p] = sts[p] * egl_s[c * cc:c * cc + 1, sls[p]] + jnp.where(bdiag, upds[p], 0.0)
            o_s[rows, sls[p]] = qss[p] + jnp.where(hm, intra[p][0], intra[p][1])

    o = o_s[...]
    ms = _dot01(o * o, bd, pieces=2) * (1.0 / HEAD_DIM)
    o_ref[0] = (o * lax.rsqrt(ms + RMS_EPS) * nw_ref[...]).astype(o_ref.dtype)


def _gdn(hf, conv_w, a_log, dt_bias, norm_w, tile=512):
    bsz, seq, _ = hf.shape
    rep = lambda a: jnp.repeat(a.astype(F32), HEAD_DIM)[None, :]
    wide = pltpu.VMEM((tile, WIDTH), F32)
    return pl.pallas_call(
        functools.partial(_gdn_kernel, tile=tile),
        grid=(bsz, seq // tile),
        in_specs=[
            pl.BlockSpec((1, tile, HF_COLS), lambda b, t: (b, t, 0)),
            pl.BlockSpec((CONV_K, 3 * WIDTH), lambda b, t: (0, 0)),
            pl.BlockSpec((1, WIDTH), lambda b, t: (0, 0)),
            pl.BlockSpec((1, WIDTH), lambda b, t: (0, 0)),
            pl.BlockSpec((1, WIDTH), lambda b, t: (0, 0)),
        ],
        out_specs=pl.BlockSpec((1, tile, WIDTH), lambda b, t: (b, t, 0)),
        out_shape=jax.ShapeDtypeStruct((bsz, seq, WIDTH), BF16),
        scratch_shapes=[pltpu.VMEM((tile + 8, 3 * WIDTH), F32), wide, wide, wide, wide,
                        pltpu.VMEM((HEADS, tile, LANES), F32), wide, wide,
                        pltpu.VMEM((PAIRS, LANES, LANES), F32)],
        compiler_params=_cparams(("parallel", "arbitrary")),
        name="gdn",
    )(hf, conv_w.astype(F32), rep(a_log), rep(dt_bias), jnp.tile(norm_w.astype(F32), HEADS)[None, :])


def _memattn_kernel(q_ref, kv_ref, o_ref):
    tq = q_ref.shape[1]
    q = q_ref[0]
    k = kv_ref[0, :, 0:WIDTH]
    v = kv_ref[0, :, WIDTH:2 * WIDTH]
    head = _iota((tq, WIDTH), 1) // MEM_HEAD_DIM
    zero = jnp.zeros_like(q)
    out = jnp.zeros((tq, WIDTH), F32)
    for h in range(MEM_HEADS):
        s = _dot_nt(jnp.where(head == h, q, zero), k)
        m = jnp.max(s, axis=-1, keepdims=True)
        p = jnp.exp(s - m)
        o = _dot(p.astype(BF16), v) / jnp.sum(p, axis=-1, keepdims=True)
        out = jnp.where(head == h, o, out)
    o_ref[0] = out.astype(o_ref.dtype)


def _memattn(he, kv, layer, tq=512):
    bsz, seq, _ = he.shape
    mlen = kv.shape[1]
    return pl.pallas_call(
        _memattn_kernel,
        grid=(bsz, seq // tq),
        in_specs=[
            pl.BlockSpec((1, tq, WIDTH), lambda b, i: (b, i, 0)),
            pl.BlockSpec((1, mlen, 2 * WIDTH), lambda b, i: (b, 0, layer)),
        ],
        out_specs=pl.BlockSpec((1, tq, WIDTH), lambda b, i: (b, i, 0)),
        out_shape=jax.ShapeDtypeStruct((bsz, seq, WIDTH), BF16),
        compiler_params=_cparams(("parallel", "parallel")),
        name="memattn",
    )(he, kv)


def _out_kernel(oa, ob, oc, od, oe, z_ref, ml_ref, x_ref, wb_ref, wo_ref, g_ref, b_ref, y_ref, yb_ref, *, alpha):
    tm = x_ref.shape[0]
    acc = jnp.zeros((tm, D_MODEL), F32)
    for n, o_ref in enumerate((oa, ob, oc, od, oe)):
        z = z_ref[:, n * WIDTH:(n + 1) * WIDTH]
        gated = o_ref[...] * (z * jax.nn.sigmoid(z))
        yn = _dot(gated, wb_ref[n])
        acc = acc + jax.nn.sigmoid(ml_ref[:, n * D_MODEL:(n + 1) * D_MODEL]).astype(F32) * yn
    r = alpha * x_ref[...] + _dot(acc.astype(BF16), wo_ref[...])
    mu = jnp.mean(r, axis=-1, keepdims=True)
    rc = r - mu
    var = jnp.mean(rc * rc, axis=-1, keepdims=True)
    y = rc * lax.rsqrt(var + LN_EPS) * g_ref[...] + b_ref[...]
    y_ref[...] = y
    yb_ref[...] = y.astype(BF16)


def _out(branches, z, ml, x, w_branch, w_out, ln_g, ln_b, alpha, tm=512):
    m = x.shape[0]
    row = lambda n: pl.BlockSpec((tm, n), lambda i: (i, 0))
    return pl.pallas_call(
        functools.partial(_out_kernel, alpha=alpha),
        grid=(m // tm,),
        in_specs=[row(WIDTH)] * N_BRANCH + [row(N_BRANCH * WIDTH), row(N_BRANCH * D_MODEL), row(D_MODEL),
                  pl.BlockSpec((N_BRANCH, WIDTH, D_MODEL), lambda i: (0, 0, 0)),
                  pl.BlockSpec((D_MODEL, D_MODEL), lambda i: (0, 0)),
                  pl.BlockSpec((1, D_MODEL), lambda i: (0, 0)),
                  pl.BlockSpec((1, D_MODEL), lambda i: (0, 0))],
        out_specs=[row(D_MODEL), row(D_MODEL)],
        out_shape=[jax.ShapeDtypeStruct((m, D_MODEL), F32), jax.ShapeDtypeStruct((m, D_MODEL), BF16)],
        compiler_params=_cparams(("parallel",)),
        name="out",
    )(*branches, z, ml, x, w_branch, w_out, ln_g, ln_b)


def _split_weights(w_in, b_in):
    scale = np.ones((HB_COLS,), np.float32)
    for cb in (CB_AQ, CB_CQ):
        scale[cb * LANES:cb * LANES + WIDTH] = HEAD_DIM ** -0.5
    scale[CB_BQ * LANES:CB_BQ * LANES + 3 * WIDTH] = HEAD_DIM ** -0.5
    sc = np.concatenate([np.arange(o, o + HEADS) for o in (O_CF, O_DBETA, O_DDECAY)])
    pad = jnp.zeros(w_in.shape[:2] + (LANES - 3 * HEADS,), w_in.dtype)
    w_hf = jnp.concatenate([w_in[..., sc], pad, w_in[..., O_D:O_D + 3 * WIDTH]], axis=-1)
    b_hf = jnp.concatenate([b_in[..., sc], pad[:, 0], b_in[..., O_D:O_D + 3 * WIDTH]], axis=-1)
    span = lambda a, o, n: a[..., o:o + n]
    groups = {
        "hb": (span(w_in, 0, HB_COLS), span(b_in, 0, HB_COLS), scale),
        "he": (span(w_in, O_E, WIDTH), span(b_in, O_E, WIDTH), np.full((WIDTH,), MEM_HEAD_DIM ** -0.5, np.float32)),
        "hf": (w_hf, b_hf, np.ones((HF_COLS,), np.float32)),
        "z": (span(w_in, O_Z, N_BRANCH * WIDTH), span(b_in, O_Z, N_BRANCH * WIDTH), np.ones((N_BRANCH * WIDTH,), np.float32)),
        "ml": (span(w_in, O_MERGE, N_BRANCH * D_MODEL), span(b_in, O_MERGE, N_BRANCH * D_MODEL),
               np.ones((N_BRANCH * D_MODEL,), np.float32)),
    }
    return {k: (w.astype(F32), b.astype(F32)[:, None, :], jnp.asarray(s)[None, :]) for k, (w, b, s) in groups.items()}


def _layer(x, xb, kv, layer, gw, conv_w, a_log, dt_bias, gdn_norm_w, w_branch, w_out, ln_g, ln_b, alpha):
    bsz, seq, d = x.shape
    m = bsz * seq
    xf = x.reshape(m, d)
    hb = _proj(xb, *gw["hb"], layer, BF16, 1024, 1664).reshape(bsz, seq, HB_COLS)
    he = _proj(xb, *gw["he"], layer, BF16, 1024, WIDTH).reshape(bsz, seq, WIDTH)
    hf = _proj(xb, *gw["hf"], layer, F32, 1024, HF_COLS).reshape(bsz, seq, HF_COLS)
    z = _proj(xb, *gw["z"], layer, BF16, 1024, N_BRANCH * WIDTH)
    ml = _proj(xb, *gw["ml"], layer, BF16, 1024, 1024)

    o_a = _moba(hb)
    o_b = _dilated(hb)
    o_c = _fox(hb, _fcum(hf))
    o_d = _gdn(hf, conv_w, a_log, dt_bias, gdn_norm_w)
    o_e = _memattn(he, kv, layer)
    branches = [o.reshape(m, WIDTH) for o in (o_a, o_b, o_c, o_d, o_e)]
    y, yb = _out(branches, z, ml, xf, w_branch.astype(BF16), w_out.astype(BF16),
                 ln_g.astype(F32)[None, :], ln_b.astype(F32)[None, :], alpha)
    return y.reshape(bsz, seq, d), yb


def kernel(x, mem, mem_ln_g, mem_ln_b, w_in, b_in, conv_w, a_log, dt_bias, gdn_norm_w, w_mem_kv, w_branch, w_out, ln_g, ln_b):
    depth = w_in.shape[0]
    alpha = float((2 * depth) ** 0.25)
    w_kv = jnp.concatenate([w_mem_kv[l] for l in range(depth)], axis=1).astype(BF16)
    kv = _memkv(mem.astype(F32), mem_ln_g.astype(F32)[None, :], mem_ln_b.astype(F32)[None, :], w_kv)
    x = x.astype(F32)
    xb = x.reshape(-1, x.shape[-1]).astype(BF16)
    gw = _split_weights(w_in, b_in)
    for l in range(depth):
        x, xb = _layer(x, xb, kv, l, gw, conv_w[l], a_log[l], dt_bias[l], gdn_norm_w[l],
                       w_branch[l], w_out[l], ln_g[l], ln_b[l], alpha)
    return x
```

```python
import functools

import jax
import jax.numpy as jnp
import numpy as np
from jax import lax
from jax.experimental import pallas as pl
from jax.experimental.pallas import tpu as pltpu

F32 = jnp.float32
BF16 = jnp.bfloat16
HI = lax.Precision.HIGHEST

D_MODEL = 1024
HEAD_DIM = 64
HEADS = 6
WIDTH = HEADS * HEAD_DIM
N_BRANCH = 5
LANES = 128
PAIRS = WIDTH // LANES
MOBA_BLOCK = 256
MOBA_TOPK = 3
DILATIONS = (1, 4, 16)
BAND = 128
BAND_TILE = 2048
BAND_GROUP = 8
GDN_CHUNK = 128
CONV_K = 4
MEM_HEADS = 4
MEM_HEAD_DIM = WIDTH // MEM_HEADS
NEG = -1e30
LOG2E = float(np.log2(np.e))
FLASH_UNDERFLOW = 160.0
LN_EPS = 1e-5
RMS_EPS = 1e-6

_SPLIT = (3 * WIDTH, 6 * WIDTH, WIDTH, 3 * WIDTH, HEADS, 3 * WIDTH, HEADS, HEADS, WIDTH, N_BRANCH * WIDTH, N_BRANCH * D_MODEL)
_OFF = tuple(int(v) for v in np.concatenate([[0], np.cumsum(_SPLIT)]))
(O_A, O_BQK, O_BV, O_C, O_CF, O_D, O_DBETA, O_DDECAY, O_E, O_Z, O_MERGE, _) = _OFF

CB_AQ, CB_AK, CB_AV = 0, 3, 6
CB_BQ, CB_BK, CB_BV = 9, 18, 27
CB_CQ, CB_CK, CB_CV = 30, 33, 36
HB_COLS = 39 * LANES
assert (O_A, O_BQK, O_BV, O_C, O_CF) == tuple(LANES * c for c in (CB_AQ, CB_BQ, CB_BV, CB_CQ, 39))
HF_COLS = LANES + 3 * WIDTH

VMEM_LIMIT = 56 * 1024 * 1024


def _cparams(sem):
    return pltpu.CompilerParams(dimension_semantics=sem, vmem_limit_bytes=VMEM_LIMIT)


def _dot(a, b):
    return jnp.dot(a, b, preferred_element_type=F32)


def _dot_nt(a, b):
    return lax.dot_general(a, b, (((1,), (1,)), ((), ())), preferred_element_type=F32)


def _dot_hi(a, b):
    return jnp.dot(a, b, preferred_element_type=F32, precision=HI)


def _dot_nt_hi(a, b):
    return lax.dot_general(a, b, (((1,), (1,)), ((), ())), preferred_element_type=F32, precision=HI)


def _iota(shape, dim):
    return lax.broadcasted_iota(jnp.int32, shape, dim)


def _proj_kernel(x_ref, w_ref, b_ref, s_ref, o_ref, wb_ref):
    @pl.when(pl.program_id(1) == 0)
    def _():
        wb_ref[...] = w_ref[...].astype(BF16)

    acc = _dot(x_ref[...], wb_ref[...])
    o_ref[...] = ((acc + b_ref[...]) * s_ref[...]).astype(o_ref.dtype)


def _proj(xb, w, b, scale, layer, out_dtype, tm, tn):
    m, k = xb.shape
    n = b.shape[-1]
    assert m % tm == 0 and n % tn == 0
    return pl.pallas_call(
        _proj_kernel,
        grid=(n // tn, m // tm),
        in_specs=[
            pl.BlockSpec((tm, k), lambda j, i: (i, 0)),
            pl.BlockSpec((None, k, tn), lambda j, i: (layer, 0, j)),
            pl.BlockSpec((None, 1, tn), lambda j, i: (layer, 0, j)),
            pl.BlockSpec((1, tn), lambda j, i: (0, j)),
        ],
        out_specs=pl.BlockSpec((tm, tn), lambda j, i: (i, j)),
        out_shape=jax.ShapeDtypeStruct((m, n), out_dtype),
        scratch_shapes=[pltpu.VMEM((k, tn), BF16)],
        compiler_params=_cparams(("parallel", "arbitrary")),
        name="proj",
    )(xb, w, b, scale)


def _memkv_kernel(mem_ref, g_ref, b_ref, w_ref, o_ref):
    x = mem_ref[0]
    mu = jnp.mean(x, axis=-1, keepdims=True)
    xc = x - mu
    var = jnp.mean(xc * xc, axis=-1, keepdims=True)
    y = xc * lax.rsqrt(var + LN_EPS) * g_ref[...] + b_ref[...]
    o_ref[0] = _dot(y.astype(BF16), w_ref[...]).astype(o_ref.dtype)


def _memkv(mem, g, b, w):
    bsz, mlen, d = mem.shape
    n = w.shape[1]
    return pl.pallas_call(
        _memkv_kernel,
        grid=(bsz,),
        in_specs=[
            pl.BlockSpec((1, mlen, d), lambda i: (i, 0, 0)),
            pl.BlockSpec((1, d), lambda i: (0, 0)),
            pl.BlockSpec((1, d), lambda i: (0, 0)),
            pl.BlockSpec((d, n), lambda i: (0, 0)),
        ],
        out_specs=pl.BlockSpec((1, mlen, n), lambda i: (i, 0, 0)),
        out_shape=jax.ShapeDtypeStruct((bsz, mlen, n), BF16),
        compiler_params=_cparams(("parallel",)),
        name="memkv",
    )(mem, g, b, w)


def _fcum_kernel(h_ref, o_ref, *, blk):
    seq = h_ref.shape[1]
    tri = (_iota((blk, blk), 0) >= _iota((blk, blk), 1)).astype(F32)

    def body(i, carry):
        rows = pl.ds(pl.multiple_of(i * blk, blk), blk)
        logf = jax.nn.log_sigmoid(h_ref[0, rows, :])
        c = _dot_hi(tri, logf) + carry
        o_ref[0, rows, :] = c * LOG2E
        return c[blk - 1:blk, :]

    lax.fori_loop(0, seq // blk, body, jnp.zeros((1, LANES), F32))


def _fcum(hf):
    bsz, seq, _ = hf.shape
    return pl.pallas_call(
        functools.partial(_fcum_kernel, blk=LANES),
        grid=(bsz,),
        in_specs=[pl.BlockSpec((1, seq, LANES), lambda b: (b, 0, 0))],
        out_specs=pl.BlockSpec((1, seq, LANES), lambda b: (b, 0, 0)),
        out_shape=jax.ShapeDtypeStruct((bsz, seq, LANES), F32),
        compiler_params=_cparams(("parallel",)),
        name="fcum",
    )(hf)


def _head_masks(rows):
    lane = _iota((rows, LANES), 1)
    return lane < HEAD_DIM


def _split3(x):
    hi = x.astype(BF16).astype(F32)
    r = x - hi
    mid = r.astype(BF16).astype(F32)
    return hi, mid, r - mid


def _flash_scratch(tq):
    return [pltpu.VMEM((2 * tq, LANES), F32), pltpu.VMEM((2 * tq, tq), F32), pltpu.VMEM((2 * tq, tq), F32),
            pltpu.VMEM((2 * tq, LANES), F32), pltpu.VMEM((2 * tq, LANES), F32)]


def _augment_values(vx_ref, rows, v):
    vx_ref[rows, 0:LANES] = v
    vx_ref[rows, LANES:] = jnp.ones((v.shape[0], LANES), BF16)


def _flash_causal(qx, kx_ref, vx_ref, scratch, i, tq, past_tiles=None):
    acc_ref, sa_ref, sb_ref, m_ref, l_ref = scratch

    def logits(kt, dst):
        rows = pl.ds(pl.multiple_of(kt * tq, tq), tq)
        dst[...] = _dot_nt(qx, kx_ref[rows, :])

    r = _iota((2 * tq, tq), 0)
    causal = jnp.where(r >= tq, r - tq, r) >= _iota((2 * tq, tq), 1)

    def finish(src, kt, masked=False):
        rows = pl.ds(pl.multiple_of(kt * tq, tq), tq)
        s = src[...]
        if masked:
            s = jnp.where(causal, s, NEG)
        m = m_ref[...]
        m_new = jnp.maximum(m, jnp.max(s, axis=-1, keepdims=True))
        alpha = jnp.exp2(m - m_new)
        p = jnp.exp2((s - jnp.tile(m_new, (1, tq // LANES))).astype(BF16))
        pv = _dot(p, vx_ref[rows, :])
        l_ref[...] = alpha * l_ref[...] + pv[:, LANES:]
        m_ref[...] = m_new
        acc_ref[...] = alpha * acc_ref[...] + pv[:, 0:LANES]

    acc_ref[...] = jnp.zeros_like(acc_ref)
    m_ref[...] = jnp.full((2 * tq, LANES), NEG, F32)
    l_ref[...] = jnp.zeros((2 * tq, LANES), F32)
    tile = lambda u: jnp.maximum(i - 1 - u, 0)
    logits(i, sa_ref)
    logits(tile(0), sb_ref)
    finish(sa_ref, i, masked=True)
    n = i if past_tiles is None else past_tiles(m_ref)

    def body(v, carry):
        logits(tile(2 * v + 1), sa_ref)
        finish(sb_ref, tile(2 * v))
        logits(tile(2 * v + 2), sb_ref)
        finish(sa_ref, tile(2 * v + 1))
        return carry

    lax.fori_loop(0, n // 2, body, 0)

    @pl.when(n % 2 == 1)
    def _():
        finish(sb_ref, tile(n - 1))

    o = acc_ref[...] / l_ref[...]
    return jnp.where(_head_masks(tq), o[0:tq], o[tq:])


def _head_sumsq(x):
    hm = _head_masks(x.shape[0])
    sq = x * x
    return (jnp.sum(jnp.where(hm, sq, 0.0), axis=1, keepdims=True), jnp.sum(jnp.where(hm, 0.0, sq), axis=1, keepdims=True))


def _fox_kernel(q_ref, k_ref, v_ref, f_ref, o_ref, kx_ref, vx_ref, kn_ref, *flash, tq):
    p = pl.program_id(1)
    i = pl.program_id(2)
    seq = k_ref.shape[1]
    bt = 512

    @pl.when(i == 0)
    def _():
        src = _iota((LANES, LANES), 0)
        dst = _iota((LANES, LANES), 1)
        pm = jnp.logical_or(jnp.logical_and(src == 2 * p, dst < 3),
                            jnp.logical_and(src == 2 * p + 1, jnp.logical_and(dst >= 3, dst < 6))).astype(F32)
        sub = _iota((bt, LANES), 1) % 3

        def build(c, carry):
            rows = pl.ds(pl.multiple_of(c * bt, bt), bt)
            hi, mid, lo = _split3(_dot_hi(f_ref[0, rows, :], pm))
            k = k_ref[0, rows, :]
            kx_ref[rows, 0:LANES] = k
            kx_ref[rows, LANES:] = jnp.where(sub == 0, hi, jnp.where(sub == 1, mid, lo)).astype(BF16)
            _augment_values(vx_ref, rows, v_ref[0, rows, :])
            return tuple(jnp.maximum(c0, jnp.max(s, axis=0, keepdims=True))
                         for c0, s in zip(carry, _head_sumsq(k.astype(F32))))

        zero = jnp.zeros((1, 1), F32)
        kn = lax.fori_loop(0, seq // bt, build, (zero, zero))
        kn_ref[0:1, :] = jnp.broadcast_to(kn[0], (1, LANES))
        kn_ref[1:2, :] = jnp.broadcast_to(kn[1], (1, LANES))

    q = q_ref[0].astype(F32)
    lane = _iota((tq, LANES), 1)
    hm = lane < HEAD_DIM
    top = jnp.concatenate([jnp.where(hm, q, 0.0), jnp.where(lane < 3, -1.0, 0.0)], axis=1)
    bot = jnp.concatenate([jnp.where(hm, 0.0, q), jnp.where(jnp.logical_and(lane >= 3, lane < 6), -1.0, 0.0)], axis=1)
    qx = jnp.concatenate([top, bot], axis=0).astype(BF16)

    def past_tiles(m_ref):
        nt = seq // tq
        fend = f_ref[0, pl.ds(tq - 1, nt, stride=tq), :]
        lane_t = _iota((nt, LANES), 1)
        tpos = _iota((nt, 1), 0)
        m = m_ref[...]
        need = tpos < 0
        for h, qn2 in enumerate(_head_sumsq(q)):
            m_min = jnp.min(m[h * tq:(h + 1) * tq, 0:1], axis=0, keepdims=True)
            qk = jnp.sqrt(jnp.max(qn2, axis=0, keepdims=True) * kn_ref[h:h + 1, 0:1]) * 1.01 + 1.0
            f_h = jnp.sum(jnp.where(lane_t == 2 * p + h, fend, 0.0), axis=1, keepdims=True)
            need = jnp.logical_or(need, qk - f_h > m_min - FLASH_UNDERFLOW)
        first = jnp.min(jnp.where(jnp.logical_and(need, tpos < i), tpos, i).astype(F32))
        return i - first.astype(jnp.int32)

    o_ref[0] = _flash_causal(qx, kx_ref, vx_ref, flash, i, tq, past_tiles).astype(o_ref.dtype)


def _fox(hb, fcol, tq=256):
    bsz, seq, _ = hb.shape
    assert seq % tq == 0
    return pl.pallas_call(
        functools.partial(_fox_kernel, tq=tq),
        grid=(bsz, PAIRS, seq // tq),
        in_specs=[
            pl.BlockSpec((1, tq, LANES), lambda b, p, i: (b, i, CB_CQ + p)),
            pl.BlockSpec((1, seq, LANES), lambda b, p, i: (b, 0, CB_CK + p)),
            pl.BlockSpec((1, seq, LANES), lambda b, p, i: (b, 0, CB_CV + p)),
            pl.BlockSpec((1, seq, LANES), lambda b, p, i: (b, 0, 0)),
        ],
        out_specs=pl.BlockSpec((1, tq, LANES), lambda b, p, i: (b, i, p)),
        out_shape=jax.ShapeDtypeStruct((bsz, seq, WIDTH), BF16),
        scratch_shapes=[pltpu.VMEM((seq, 2 * LANES), BF16), pltpu.VMEM((seq, 2 * LANES), BF16),
                        pltpu.VMEM((8, LANES), F32)] + _flash_scratch(tq),
        compiler_params=_cparams(("parallel", "parallel", "arbitrary")),
        name="fox",
    )(hb, hb, hb, fcol)


def _moba_kernel(q_ref, k_ref, v_ref, o_ref, kx_ref, vx_ref, kmean_ref, *flash, tq):
    i = pl.program_id(2)
    seq = k_ref.shape[1]
    blk = MOBA_BLOCK

    @pl.when(i == 0)
    def _():
        kmean_ref[...] = jnp.zeros_like(kmean_ref)
        lane = _iota((blk, LANES), 1)

        def build(n, carry):
            rows = pl.ds(pl.multiple_of(n * blk, blk), blk)
            k = k_ref[0, rows, :]
            kx_ref[rows, 0:LANES] = k
            kx_ref[rows, LANES:] = jnp.where(lane == n, 1.0, 0.0).astype(BF16)
            _augment_values(vx_ref, rows, v_ref[0, rows, :])
            kmean_ref[pl.ds(n, 1), :] = jnp.sum(k.astype(F32), axis=0, keepdims=True) * (1.0 / blk)
            return carry

        lax.fori_loop(0, seq // blk, build, 0)

    q = q_ref[0].astype(F32)
    hm = _head_masks(tq)
    nbp = -(-(seq // blk) // 8) * 8
    blk_id = _iota((nbp, tq), 0)
    blk_f = blk_id.astype(F32)
    own = i * (tq // blk) + _iota((nbp, tq), 1) // blk
    valid = blk_id < own
    kmean = kmean_ref[0:nbp, :]
    halves = []
    for h in range(2):
        qh = jnp.where(hm, q, 0.0) if h == 0 else jnp.where(hm, 0.0, q)
        g = jnp.where(valid, _dot_nt_hi(kmean, qh), NEG)
        bias = jnp.where(blk_id == own, 0.0, NEG)
        for _ in range(MOBA_TOPK):
            mx = jnp.max(g, axis=0, keepdims=True)
            first = jnp.min(jnp.where(g == mx, blk_f, 1e9), axis=0, keepdims=True)
            pick = blk_f == first
            bias = jnp.where(jnp.logical_and(pick, valid), 0.0, bias)
            g = jnp.where(pick, -3e38, g)
        bias = jnp.concatenate([bias, jnp.zeros((LANES - nbp, tq), F32)], axis=0).T
        halves.append(jnp.concatenate([qh, bias], axis=1))
    qx = jnp.concatenate(halves, axis=0).astype(BF16)
    o_ref[0] = _flash_causal(qx, kx_ref, vx_ref, flash, i, tq).astype(o_ref.dtype)


def _moba(hb, tq=512):
    bsz, seq, _ = hb.shape
    assert seq % tq == 0 and tq % MOBA_BLOCK == 0 and seq // MOBA_BLOCK <= LANES
    return pl.pallas_call(
        functools.partial(_moba_kernel, tq=tq),
        grid=(bsz, PAIRS, seq // tq),
        in_specs=[
            pl.BlockSpec((1, tq, LANES), lambda b, p, i: (b, i, CB_AQ + p)),
            pl.BlockSpec((1, seq, LANES), lambda b, p, i: (b, 0, CB_AK + p)),
            pl.BlockSpec((1, seq, LANES), lambda b, p, i: (b, 0, CB_AV + p)),
        ],
        out_specs=pl.BlockSpec((1, tq, LANES), lambda b, p, i: (b, i, p)),
        out_shape=jax.ShapeDtypeStruct((bsz, seq, WIDTH), BF16),
        scratch_shapes=[pltpu.VMEM((seq, 2 * LANES), BF16), pltpu.VMEM((seq, 2 * LANES), BF16),
                        pltpu.VMEM((LANES, LANES), F32)] + _flash_scratch(tq),
        compiler_params=_cparams(("parallel", "parallel", "arbitrary")),
        name="moba",
    )(hb, hb, hb)


def _band_kernel(q0, q1, q2, k0c, k0p, k1c, k1p, k2c, k2p, vc, vp, o_ref, qb, kb, vb, nb, mb, sb, *, tile):
    jt = pl.program_id(2)
    for g, (qr, kc, kp) in enumerate(((q0, k0c, k0p), (q1, k1c, k1p), (q2, k2c, k2p))):
        qb[g] = qr[0].astype(F32)
        kb[g, 0:tile, :] = kp[0].astype(F32)
        kb[g, tile:, :] = kc[0].astype(F32)
    vb[0:tile, :] = vp[0].astype(F32)
    vb[tile:, :] = vc[0].astype(F32)

    hm = _head_masks(BAND)
    qi = _iota((BAND, 2 * BAND), 0)
    kj = _iota((BAND, 2 * BAND), 1)
    dist = BAND + qi - kj
    band = jnp.logical_and(dist >= 0, dist <= BAND)
    cur_half = kj >= BAND
    nblocks = tile // BAND

    for g, dil in enumerate(DILATIONS):
        per_stream = nblocks // dil

        def step(it, carry, g=g, dil=dil, per_stream=per_stream):
            starts, kstarts, valids = [], [], []
            for j in range(BAND_GROUP):
                idx = it * BAND_GROUP + j
                n = idx % per_stream
                start = idx // per_stream + n * (BAND * dil)
                starts.append(start)
                kstarts.append(tile + start - BAND * dil)
                valids.append(jnp.logical_and(band, jnp.logical_or(cur_half, jnp.logical_or(jt > 0, n > 0))))
            qs = [qb[g, pl.ds(s, BAND, stride=dil), :] for s in starts]
            kks = [kb[g, pl.ds(s, 2 * BAND, stride=dil), :].astype(BF16) for s in kstarts]
            ones = jnp.ones((2 * BAND, LANES), BF16)
            vvs = [jnp.concatenate([vb[pl.ds(s, 2 * BAND, stride=dil), :].astype(BF16), ones], axis=1) for s in kstarts]
            logits = []
            for q, kk, valid in zip(qs, kks, valids):
                for h in range(2):
                    qh = (jnp.where(hm, q, 0.0) if h == 0 else jnp.where(hm, 0.0, q)).astype(BF16)
                    logits.append(jnp.where(valid, _dot_nt(qh, kk), NEG))
            ms = [jnp.max(s, axis=-1, keepdims=True) for s in logits]
            ps = [jnp.exp((s - m).astype(BF16)) for s, m in zip(logits, ms)]
            pvs = [_dot(p, vvs[i // 2]) for i, p in enumerate(ps)]
            nums = [pv[:, 0:LANES] for pv in pvs]
            ss = [pv[:, LANES:] for pv in pvs]
            for j, s in enumerate(starts):
                rows = pl.ds(s, BAND, stride=dil)
                nb[g, rows, :] = jnp.where(hm, nums[2 * j], nums[2 * j + 1])
                mb[g, rows, :] = jnp.where(hm, ms[2 * j], ms[2 * j + 1])
                sb[g, rows, :] = jnp.where(hm, ss[2 * j], ss[2 * j + 1])
            return carry

        lax.fori_loop(0, nblocks // BAND_GROUP, step, 0)

    ch = 256

    def merge(i, carry):
        rows = pl.ds(pl.multiple_of(i * ch, ch), ch)
        m_all = jnp.maximum(jnp.maximum(mb[0, rows, :], mb[1, rows, :]), mb[2, rows, :])
        num = jnp.zeros((ch, LANES), F32)
        den = jnp.zeros((ch, LANES), F32)
        for g in range(len(DILATIONS)):
            w = jnp.exp(mb[g, rows, :] - m_all)
            num = num + nb[g, rows, :] * w
            den = den + sb[g, rows, :] * w
        o_ref[0, rows, :] = (num / den).astype(o_ref.dtype)
        return carry

    lax.fori_loop(0, tile // ch, merge, 0)


def _dilated(hb, tile=BAND_TILE):
    bsz, seq, _ = hb.shape
    ng = len(DILATIONS)
    nblocks = tile // BAND
    assert seq % tile == 0 and nblocks % BAND_GROUP == 0 and all(nblocks % d == 0 for d in DILATIONS)
    cur = lambda cb: pl.BlockSpec((1, tile, LANES), lambda b, p, j: (b, j, cb + p))
    prv = lambda cb: pl.BlockSpec((1, tile, LANES), lambda b, p, j: (b, jnp.maximum(j - 1, 0), cb + p))
    in_specs = [cur(CB_BQ + PAIRS * g) for g in range(ng)]
    for g in range(ng):
        in_specs += [cur(CB_BK + PAIRS * g), prv(CB_BK + PAIRS * g)]
    in_specs += [cur(CB_BV), prv(CB_BV)]
    acc = pltpu.VMEM((ng, tile, LANES), F32)
    return pl.pallas_call(
        functools.partial(_band_kernel, tile=tile),
        grid=(bsz, PAIRS, seq // tile),
        in_specs=in_specs,
        out_specs=pl.BlockSpec((1, tile, LANES), lambda b, p, j: (b, j, p)),
        out_shape=jax.ShapeDtypeStruct((bsz, seq, WIDTH), BF16),
        scratch_shapes=[acc, pltpu.VMEM((ng, 2 * tile, LANES), F32), pltpu.VMEM((2 * tile, LANES), F32), acc, acc, acc],
        compiler_params=_cparams(("parallel", "parallel", "arbitrary")),
        name="band",
    )(*([hb] * len(in_specs)))


def _dot01(a, b, nt=False, pieces=3):
    f = _dot_nt if nt else _dot
    if a.dtype == BF16:
        return sum(f(a, piece.astype(BF16)) for piece in _split3(b)[:pieces])
    return sum(f(piece.astype(BF16), b) for piece in _split3(a)[:pieces])


def _tri_inverse_all(ms):
    c = ms[0].shape[0]
    ri = _iota((c, c), 0)
    ci = _iota((c, c), 1)
    base = 16
    inblk = ri // base == ci // base
    eye = jnp.where(ri == ci, 1.0, 0.0)
    ps = [jnp.where(inblk, -m, 0.0) for m in ms]
    ts = [eye + p for p in ps]
    for _ in range(3):
        pbs = [p.astype(BF16) for p in ps]
        ps = [_dot(pb, pb) for pb in pbs]
        ts = [t + _dot(t.astype(BF16), p.astype(BF16)) for t, p in zip(ts, ps)]
    size = base
    while size < c:
        lower = jnp.logical_and(ri // (2 * size) == ci // (2 * size), ri // size != ci // size)
        tbs = [t.astype(BF16) for t in ts]
        xs = [_dot(tb, jnp.where(lower, m, 0.0).astype(BF16)) for tb, m in zip(tbs, ms)]
        ts = [t - _dot(x.astype(BF16), tb) for t, x, tb in zip(ts, xs, tbs)]
        size *= 2
    return ts


def _gdn_kernel(h_ref, cw_ref, alog_ref, dtb_ref, nw_ref, o_ref,
                xe_ref, u_s, w_s, qd_s, kd_s, qk_s, egl_s, o_s, st_ref, *, tile):
    cc = GDN_CHUNK
    nc = tile // cc
    t = pl.program_id(1)

    @pl.when(t == 0)
    def _():
        xe_ref[0:8, :] = jnp.zeros((8, 3 * WIDTH), F32)
        st_ref[...] = jnp.zeros_like(st_ref)

    x = h_ref[0, :, LANES:]
    xe_ref[8:8 + tile, :] = x
    y = jnp.zeros((tile, 3 * WIDTH), F32)
    for j in range(CONV_K):
        y = y + cw_ref[j:j + 1, :] * xe_ref[8 - (CONV_K - 1) + j:8 - (CONV_K - 1) + j + tile, :]
    xe_ref[0:8, :] = x[tile - 8:tile, :]
    y = y * _sigmoid(y)

    bd = ((_iota((WIDTH, WIDTH), 0) // HEAD_DIM) == (_iota((WIDTH, WIDTH), 1) // HEAD_DIM)).astype(BF16)
    q = y[:, 0:WIDTH]
    k = y[:, WIDTH:2 * WIDTH]
    v = y[:, 2 * WIDTH:]
    q = q * lax.rsqrt(_dot01(q * q, bd, pieces=2) + RMS_EPS) * (HEAD_DIM ** -0.5)
    k = k * lax.rsqrt(_dot01(k * k, bd, pieces=2) + RMS_EPS)

    hs = h_ref[0, :, 0:LANES]
    er = _iota((LANES, WIDTH), 0)
    ec = _iota((LANES, WIDTH), 1) // HEAD_DIM
    beta = _sigmoid(_dot01(hs, (er == ec + HEADS).astype(BF16), pieces=2))
    g = -jnp.exp(alog_ref[...]) * jax.nn.softplus(_dot01(hs, (er == ec + 2 * HEADS).astype(BF16), pieces=2) + dtb_ref[...])

    tri = (_iota((cc, cc), 0) >= _iota((cc, cc), 1)).astype(BF16)
    gc = jnp.concatenate([_dot01(tri, g[c * cc:(c + 1) * cc]) for c in range(nc)], axis=0)
    glast = jnp.broadcast_to(gc.reshape(nc, cc, WIDTH)[:, cc - 1:cc, :], (nc, cc, WIDTH)).reshape(tile, WIDTH)
    eg = jnp.exp(gc)
    kb = k * beta
    vb = v * beta
    wb = kb * eg
    qd_s[...] = q * eg
    kd_s[...] = k * jnp.exp(glast - gc)
    egl_s[...] = jnp.exp(glast)

    ri = _iota((cc, cc), 0)
    ci = _iota((cc, cc), 1)
    incl = ri >= ci
    strict = ri > ci
    lane = _iota((cc, LANES), 1)
    hm = lane < HEAD_DIM
    sls = [slice(p * LANES, (p + 1) * LANES) for p in range(PAIRS)]
    keeps = [hm, jnp.logical_not(hm)]
    heads = [(p, h) for p in range(PAIRS) for h in range(2)]
    chains = [(p, h, c) for p, h in heads for c in range(nc)]
    rws = [slice(c * cc, (c + 1) * cc) for c in range(nc)]
    kps = [k[:, sl].astype(BF16) for sl in sls]
    rhss = [jnp.concatenate([vb[:, sl], wb[:, sl]], axis=1).astype(BF16) for sl in sls]
    gct = gc.T
    rowf = {(p, h): gct[p * LANES + h * HEAD_DIM:p * LANES + h * HEAD_DIM + 1, :] for p, h in heads}
    kbm = {(p, h): jnp.where(jnp.tile(keeps[h], (nc, 1)), kb[:, sls[p]], 0.0).astype(BF16) for p, h in heads}
    qm = {(p, h): jnp.where(jnp.tile(keeps[h], (nc, 1)), q[:, sls[p]], 0.0).astype(BF16) for p, h in heads}
    kks = [_dot_nt(kbm[p, h][rws[c]], kps[p][rws[c]]) for p, h, c in chains]
    qks = [_dot_nt(qm[p, h][rws[c]], kps[p][rws[c]]) for p, h, c in chains]
    mms = []
    for (p, h, c), kk, qk in zip(chains, kks, qks):
        col = p * LANES + h * HEAD_DIM
        diff = gc[rws[c], col:col + 1] - rowf[p, h][:, rws[c]]
        decay = jnp.where(incl, jnp.exp(jnp.where(incl, diff, 0.0)), 0.0)
        mms.append(jnp.where(strict, kk * decay, 0.0))
        qk_s[2 * p + h, rws[c], :] = qk * decay
    tinvs = _tri_inverse_all(mms)
    rs = {ch: _dot(tinv.astype(BF16), rhss[ch[0]][rws[ch[2]]]) for ch, tinv in zip(chains, tinvs)}
    for p in range(PAIRS):
        for c in range(nc):
            r0, r1 = rs[p, 0, c], rs[p, 1, c]
            u_s[rws[c], sls[p]] = jnp.where(hm, r0[:, :LANES], r1[:, :LANES])
            w_s[rws[c], sls[p]] = jnp.where(hm, r0[:, LANES:], r1[:, LANES:])

    bdiag = (_iota((LANES, LANES), 0) // HEAD_DIM) == (_iota((LANES, LANES), 1) // HEAD_DIM)
    for c in range(nc):
        rows = rws[c]
        sts = [st_ref[p] for p in range(PAIRS)]
        stbs = [st.astype(BF16) for st in sts]
        wss = [_dot(w_s[rows, sl].astype(BF16), stb) for sl, stb in zip(sls, stbs)]
        qss = [_dot(qd_s[rows, sl].astype(BF16), stb) for sl, stb in zip(sls, stbs)]
        vnbs = [(u_s[rows, sl] - ws).astype(BF16) for sl, ws in zip(sls, wss)]
        upds = [lax.dot_general(kd_s[rows, sl].astype(BF16), vnb, (((0,), (0,)), ((), ())), preferred_element_type=F32)
                for sl, vnb in zip(sls, vnbs)]
        intra = [(_dot(qk_s[2 * p, rows, :].astype(BF16), vnbs[p]), _dot(qk_s[2 * p + 1, rows, :].astype(BF16), vnbs[p]))
                 for p in range(PAIRS)]
        for p in range(PAIRS):
            st_ref[p] = sts[p] * egl_s[c * cc:c * cc + 1, sls[p]] + jnp.where(bdiag, upds[p], 0.0)
            o_s[rows, sls[p]] = qss[p] + jnp.where(hm, intra[p][0], intra[p][1])

    o = o_s[...]
    ms = _dot01(o * o, bd, pieces=2) * (1.0 / HEAD_DIM)
    o_ref[0] = (o * lax.rsqrt(ms + RMS_EPS) * nw_ref[...]).astype(o_ref.dtype)


def _gdn(hf, conv_w, a_log, dt_bias, norm_w, tile=512):
    bsz, seq, _ = hf.shape
    rep = lambda a: jnp.repeat(a.astype(F32), HEAD_DIM)[None, :]
    wide = pltpu.VMEM((tile, WIDTH), F32)
    return pl.pallas_call(
        functools.partial(_gdn_kernel, tile=tile),
        grid=(bsz, seq // tile),
        in_specs=[
            pl.BlockSpec((1, tile, HF_COLS), lambda b, t: (b, t, 0)),
            pl.BlockSpec((CONV_K, 3 * WIDTH), lambda b, t: (0, 0)),
            pl.BlockSpec((1, WIDTH), lambda b, t: (0, 0)),
            pl.BlockSpec((1, WIDTH), lambda b, t: (0, 0)),
            pl.BlockSpec((1, WIDTH), lambda b, t: (0, 0)),
        ],
        out_specs=pl.BlockSpec((1, tile, WIDTH), lambda b, t: (b, t, 0)),
        out_shape=jax.ShapeDtypeStruct((bsz, seq, WIDTH), BF16),
        scratch_shapes=[pltpu.VMEM((tile + 8, 3 * WIDTH), F32), wide, wide, wide, wide,
                        pltpu.VMEM((HEADS, tile, LANES), F32), wide, wide,
                        pltpu.VMEM((PAIRS, LANES, LANES), F32)],
        compiler_params=_cparams(("parallel", "arbitrary")),
        name="gdn",
    )(hf, conv_w.astype(F32), rep(a_log), rep(dt_bias), jnp.tile(norm_w.astype(F32), HEADS)[None, :])


def _memattn_kernel(q_ref, kv_ref, o_ref):
    tq = q_ref.shape[1]
    q = q_ref[0]
    k = kv_ref[0, :, 0:WIDTH]
    v = kv_ref[0, :, WIDTH:2 * WIDTH]
    head = _iota((tq, WIDTH), 1) // MEM_HEAD_DIM
    zero = jnp.zeros_like(q)
    out = jnp.zeros((tq, WIDTH), F32)
    for h in range(MEM_HEADS):
        s = _dot_nt(jnp.where(head == h, q, zero), k)
        m = jnp.max(s, axis=-1, keepdims=True)
        p = jnp.exp(s - m)
        o = _dot(p.astype(BF16), v) / jnp.sum(p, axis=-1, keepdims=True)
        out = jnp.where(head == h, o, out)
    o_ref[0] = out.astype(o_ref.dtype)


def _memattn(he, kv, layer, tq=512):
    bsz, seq, _ = he.shape
    mlen = kv.shape[1]
    return pl.pallas_call(
        _memattn_kernel,
        grid=(bsz, seq // tq),
        in_specs=[
            pl.BlockSpec((1, tq, WIDTH), lambda b, i: (b, i, 0)),
            pl.BlockSpec((1, mlen, 2 * WIDTH), lambda b, i: (b, 0, layer)),
        ],
        out_specs=pl.BlockSpec((1, tq, WIDTH), lambda b, i: (b, i, 0)),
        out_shape=jax.ShapeDtypeStruct((bsz, seq, WIDTH), BF16),
        compiler_params=_cparams(("parallel", "parallel")),
        name="memattn",
    )(he, kv)


def _sigmoid(x):
    return 0.5 * jnp.tanh(0.5 * x) + 0.5


def _out_kernel(oa, ob, oc, od, oe, z_ref, ml_ref, x_ref, wb_ref, wo_ref, g_ref, b_ref, y_ref, yb_ref, *, alpha):
    tm = x_ref.shape[0]
    acc = jnp.zeros((tm, D_MODEL), F32)
    for n, o_ref in enumerate((oa, ob, oc, od, oe)):
        z = z_ref[:, n * WIDTH:(n + 1) * WIDTH]
        gated = o_ref[...] * (z * _sigmoid(z))
        yn = _dot(gated, wb_ref[n])
        acc = acc + _sigmoid(ml_ref[:, n * D_MODEL:(n + 1) * D_MODEL]).astype(F32) * yn
    r = alpha * x_ref[...] + _dot(acc.astype(BF16), wo_ref[...])
    mu = jnp.mean(r, axis=-1, keepdims=True)
    rc = r - mu
    var = jnp.mean(rc * rc, axis=-1, keepdims=True)
    y = rc * lax.rsqrt(var + LN_EPS) * g_ref[...] + b_ref[...]
    y_ref[...] = y
    yb_ref[...] = y.astype(BF16)


def _out(branches, z, ml, x, w_branch, w_out, ln_g, ln_b, alpha, tm=512):
    m = x.shape[0]
    row = lambda n: pl.BlockSpec((tm, n), lambda i: (i, 0))
    return pl.pallas_call(
        functools.partial(_out_kernel, alpha=alpha),
        grid=(m // tm,),
        in_specs=[row(WIDTH)] * N_BRANCH + [row(N_BRANCH * WIDTH), row(N_BRANCH * D_MODEL), row(D_MODEL),
                  pl.BlockSpec((N_BRANCH, WIDTH, D_MODEL), lambda i: (0, 0, 0)),
                  pl.BlockSpec((D_MODEL, D_MODEL), lambda i: (0, 0)),
                  pl.BlockSpec((1, D_MODEL), lambda i: (0, 0)),
                  pl.BlockSpec((1, D_MODEL), lambda i: (0, 0))],
        out_specs=[row(D_MODEL), row(D_MODEL)],
        out_shape=[jax.ShapeDtypeStruct((m, D_MODEL), F32), jax.ShapeDtypeStruct((m, D_MODEL), BF16)],
        compiler_params=_cparams(("parallel",)),
        name="out",
    )(*branches, z, ml, x, w_branch, w_out, ln_g, ln_b)


def _split_weights(w_in, b_in):
    scale = np.ones((HB_COLS,), np.float32)
    for cb in (CB_AQ, CB_CQ):
        scale[cb * LANES:cb * LANES + WIDTH] = HEAD_DIM ** -0.5 * LOG2E
    scale[CB_BQ * LANES:CB_BQ * LANES + 3 * WIDTH] = HEAD_DIM ** -0.5
    sc = np.concatenate([np.arange(o, o + HEADS) for o in (O_CF, O_DBETA, O_DDECAY)])
    pad = jnp.zeros(w_in.shape[:2] + (LANES - 3 * HEADS,), w_in.dtype)
    w_hf = jnp.concatenate([w_in[..., sc], pad, w_in[..., O_D:O_D + 3 * WIDTH]], axis=-1)
    b_hf = jnp.concatenate([b_in[..., sc], pad[:, 0], b_in[..., O_D:O_D + 3 * WIDTH]], axis=-1)
    span = lambda a, o, n: a[..., o:o + n]
    groups = {
        "hb": (span(w_in, 0, HB_COLS), span(b_in, 0, HB_COLS), scale),
        "he": (span(w_in, O_E, WIDTH), span(b_in, O_E, WIDTH), np.full((WIDTH,), MEM_HEAD_DIM ** -0.5, np.float32)),
        "hf": (w_hf, b_hf, np.ones((HF_COLS,), np.float32)),
        "z": (span(w_in, O_Z, N_BRANCH * WIDTH), span(b_in, O_Z, N_BRANCH * WIDTH), np.ones((N_BRANCH * WIDTH,), np.float32)),
        "ml": (span(w_in, O_MERGE, N_BRANCH * D_MODEL), span(b_in, O_MERGE, N_BRANCH * D_MODEL),
               np.ones((N_BRANCH * D_MODEL,), np.float32)),
    }
    return {k: (w.astype(F32), b.astype(F32)[:, None, :], jnp.asarray(s)[None, :]) for k, (w, b, s) in groups.items()}


def _layer(x, xb, kv, layer, gw, conv_w, a_log, dt_bias, gdn_norm_w, w_branch, w_out, ln_g, ln_b, alpha):
    bsz, seq, d = x.shape
    m = bsz * seq
    xf = x.reshape(m, d)
    hb = _proj(xb, *gw["hb"], layer, BF16, 1024, 1664).reshape(bsz, seq, HB_COLS)
    he = _proj(xb, *gw["he"], layer, BF16, 1024, WIDTH).reshape(bsz, seq, WIDTH)
    hf = _proj(xb, *gw["hf"], layer, F32, 1024, HF_COLS).reshape(bsz, seq, HF_COLS)
    z = _proj(xb, *gw["z"], layer, BF16, 1024, N_BRANCH * WIDTH)
    ml = _proj(xb, *gw["ml"], layer, BF16, 1024, 1024)

    o_a = _moba(hb)
    o_b = _dilated(hb)
    o_c = _fox(hb, _fcum(hf))
    o_d = _gdn(hf, conv_w, a_log, dt_bias, gdn_norm_w)
    o_e = _memattn(he, kv, layer)
    branches = [o.reshape(m, WIDTH) for o in (o_a, o_b, o_c, o_d, o_e)]
    y, yb = _out(branches, z, ml, xf, w_branch.astype(BF16), w_out.astype(BF16),
                 ln_g.astype(F32)[None, :], ln_b.astype(F32)[None, :], alpha)
    return y.reshape(bsz, seq, d), yb


def kernel(x, mem, mem_ln_g, mem_ln_b, w_in, b_in, conv_w, a_log, dt_bias, gdn_norm_w, w_mem_kv, w_branch, w_out, ln_g, ln_b):
    depth = w_in.shape[0]
    alpha = float((2 * depth) ** 0.25)
    w_kv = jnp.concatenate([w_mem_kv[l] for l in range(depth)], axis=1).astype(BF16)
    kv = _memkv(mem.astype(F32), mem_ln_g.astype(F32)[None, :], mem_ln_b.astype(F32)[None, :], w_kv)
    x = x.astype(F32)
    xb = x.reshape(-1, x.shape[-1]).astype(BF16)
    gw = _split_weights(w_in, b_in)
    for l in range(depth):
        x, xb = _layer(x, xb, kv, l, gw, conv_w[l], a_log[l], dt_bias[l], gdn_norm_w[l],
                       w_branch[l], w_out[l], ln_g[l], ln_b[l], alpha)
    return x
```

```python
import functools

import jax
import jax.numpy as jnp
import numpy as np
from jax import lax
from jax.experimental import pallas as pl
from jax.experimental.pallas import tpu as pltpu

F32 = jnp.float32
BF16 = jnp.bfloat16
HI = lax.Precision.HIGHEST

D_MODEL = 1024
HEAD_DIM = 64
HEADS = 6
WIDTH = HEADS * HEAD_DIM
N_BRANCH = 5
LANES = 128
PAIRS = WIDTH // LANES
MOBA_BLOCK = 256
MOBA_TOPK = 3
DILATIONS = (1, 4, 16)
BAND = 128
BAND_TILE = 2048
BAND_GROUP = 8
GDN_CHUNK = 128
CONV_K = 4
MEM_HEADS = 4
MEM_HEAD_DIM = WIDTH // MEM_HEADS
NEG = -1e30
LOG2E = float(np.log2(np.e))
FLASH_UNDERFLOW = 160.0
LN_EPS = 1e-5
RMS_EPS = 1e-6

_SPLIT = (3 * WIDTH, 6 * WIDTH, WIDTH, 3 * WIDTH, HEADS, 3 * WIDTH, HEADS, HEADS, WIDTH, N_BRANCH * WIDTH, N_BRANCH * D_MODEL)
_OFF = tuple(int(v) for v in np.concatenate([[0], np.cumsum(_SPLIT)]))
(O_A, O_BQK, O_BV, O_C, O_CF, O_D, O_DBETA, O_DDECAY, O_E, O_Z, O_MERGE, _) = _OFF

CB_AQ, CB_AK, CB_AV = 0, 3, 6
CB_BQ, CB_BK, CB_BV = 9, 18, 27
CB_CQ, CB_CK, CB_CV = 30, 33, 36
HB_COLS = 39 * LANES
assert (O_A, O_BQK, O_BV, O_C, O_CF) == tuple(LANES * c for c in (CB_AQ, CB_BQ, CB_BV, CB_CQ, 39))
HF_COLS = LANES + 3 * WIDTH

VMEM_LIMIT = 56 * 1024 * 1024

PROJ_ROWS = 2048
PROJ_COLS = {"hb": HB_COLS // 3, "he": WIDTH, "hf": HF_COLS, "z": N_BRANCH * WIDTH, "ml": 1024}
FOX_ROWS = 256
MOBA_ROWS = 2 * MOBA_BLOCK
FLASH_BUILD_ROWS = 512
BAND_MERGE_ROWS = 256
GDN_ROWS = 4 * GDN_CHUNK
MEMATTN_ROWS = 512
OUT_ROWS = 512
TRI_BASE = 16
BOUND_REL, BOUND_ABS = 1.01, 1.0
FAR = 1e9
TAKEN = -3e38


def _cparams(sem):
    return pltpu.CompilerParams(dimension_semantics=sem, vmem_limit_bytes=VMEM_LIMIT)


def _dot(a, b):
    return jnp.dot(a, b, preferred_element_type=F32)


def _dot_nt(a, b):
    return lax.dot_general(a, b, (((1,), (1,)), ((), ())), preferred_element_type=F32)


def _dot_hi(a, b):
    return jnp.dot(a, b, preferred_element_type=F32, precision=HI)


def _dot_nt_hi(a, b):
    return lax.dot_general(a, b, (((1,), (1,)), ((), ())), preferred_element_type=F32, precision=HI)


def _iota(shape, dim):
    return lax.broadcasted_iota(jnp.int32, shape, dim)


def _proj_kernel(x_ref, w_ref, b_ref, s_ref, o_ref, wb_ref):
    @pl.when(pl.program_id(1) == 0)
    def _():
        wb_ref[...] = w_ref[...].astype(BF16)

    acc = _dot(x_ref[...], wb_ref[...])
    o_ref[...] = ((acc + b_ref[...]) * s_ref[...]).astype(o_ref.dtype)


def _proj(xb, w, b, scale, layer, out_dtype, tm, tn):
    m, k = xb.shape
    n = b.shape[-1]
    assert m % tm == 0 and n % tn == 0
    return pl.pallas_call(
        _proj_kernel,
        grid=(n // tn, m // tm),
        in_specs=[
            pl.BlockSpec((tm, k), lambda j, i: (i, 0)),
            pl.BlockSpec((None, k, tn), lambda j, i: (layer, 0, j)),
            pl.BlockSpec((None, 1, tn), lambda j, i: (layer, 0, j)),
            pl.BlockSpec((1, tn), lambda j, i: (0, j)),
        ],
        out_specs=pl.BlockSpec((tm, tn), lambda j, i: (i, j)),
        out_shape=jax.ShapeDtypeStruct((m, n), out_dtype),
        scratch_shapes=[pltpu.VMEM((k, tn), BF16)],
        compiler_params=_cparams(("parallel", "arbitrary")),
        name="proj",
    )(xb, w, b, scale)


def _memkv_kernel(mem_ref, g_ref, b_ref, w_ref, o_ref):
    x = mem_ref[0]
    mu = jnp.mean(x, axis=-1, keepdims=True)
    xc = x - mu
    var = jnp.mean(xc * xc, axis=-1, keepdims=True)
    y = xc * lax.rsqrt(var + LN_EPS) * g_ref[...] + b_ref[...]
    o_ref[0] = _dot(y.astype(BF16), w_ref[...]).astype(o_ref.dtype)


def _memkv(mem, g, b, w):
    bsz, mlen, d = mem.shape
    n = w.shape[1]
    return pl.pallas_call(
        _memkv_kernel,
        grid=(bsz,),
        in_specs=[
            pl.BlockSpec((1, mlen, d), lambda i: (i, 0, 0)),
            pl.BlockSpec((1, d), lambda i: (0, 0)),
            pl.BlockSpec((1, d), lambda i: (0, 0)),
            pl.BlockSpec((d, n), lambda i: (0, 0)),
        ],
        out_specs=pl.BlockSpec((1, mlen, n), lambda i: (i, 0, 0)),
        out_shape=jax.ShapeDtypeStruct((bsz, mlen, n), BF16),
        compiler_params=_cparams(("parallel",)),
        name="memkv",
    )(mem, g, b, w)


def _fcum_kernel(h_ref, o_ref, *, blk):
    seq = h_ref.shape[1]
    tri = (_iota((blk, blk), 0) >= _iota((blk, blk), 1)).astype(F32)

    def body(i, carry):
        rows = pl.ds(pl.multiple_of(i * blk, blk), blk)
        logf = jax.nn.log_sigmoid(h_ref[0, rows, :])
        c = _dot_hi(tri, logf) + carry
        o_ref[0, rows, :] = c * LOG2E
        return c[blk - 1:blk, :]

    lax.fori_loop(0, seq // blk, body, jnp.zeros((1, LANES), F32))


def _fcum(hf):
    bsz, seq, _ = hf.shape
    return pl.pallas_call(
        functools.partial(_fcum_kernel, blk=LANES),
        grid=(bsz,),
        in_specs=[pl.BlockSpec((1, seq, LANES), lambda b: (b, 0, 0))],
        out_specs=pl.BlockSpec((1, seq, LANES), lambda b: (b, 0, 0)),
        out_shape=jax.ShapeDtypeStruct((bsz, seq, LANES), F32),
        compiler_params=_cparams(("parallel",)),
        name="fcum",
    )(hf)


def _head_masks(rows):
    lane = _iota((rows, LANES), 1)
    return lane < HEAD_DIM


def _split3(x):
    hi = x.astype(BF16).astype(F32)
    r = x - hi
    mid = r.astype(BF16).astype(F32)
    return hi, mid, r - mid


def _flash_scratch(tq):
    return [pltpu.VMEM((2 * tq, LANES), F32), pltpu.VMEM((2 * tq, tq), F32), pltpu.VMEM((2 * tq, tq), F32),
            pltpu.VMEM((2 * tq, LANES), F32), pltpu.VMEM((2 * tq, LANES), F32)]


def _augment_values(vx_ref, rows, v):
    vx_ref[rows, 0:LANES] = v
    vx_ref[rows, LANES:] = jnp.ones((v.shape[0], LANES), BF16)


def _flash_causal(qx, kx_ref, vx_ref, scratch, i, tq, past_tiles=None):
    acc_ref, sa_ref, sb_ref, m_ref, l_ref = scratch

    def logits(kt, dst):
        rows = pl.ds(pl.multiple_of(kt * tq, tq), tq)
        dst[...] = _dot_nt(qx, kx_ref[rows, :])

    r = _iota((2 * tq, tq), 0)
    causal = jnp.where(r >= tq, r - tq, r) >= _iota((2 * tq, tq), 1)

    def finish(src, kt, masked=False):
        rows = pl.ds(pl.multiple_of(kt * tq, tq), tq)
        s = src[...]
        if masked:
            s = jnp.where(causal, s, NEG)
        m = m_ref[...]
        m_new = jnp.maximum(m, jnp.max(s, axis=-1, keepdims=True))
        alpha = jnp.exp2(m - m_new)
        p = jnp.exp2((s - jnp.tile(m_new, (1, tq // LANES))).astype(BF16))
        pv = _dot(p, vx_ref[rows, :])
        l_ref[...] = alpha * l_ref[...] + pv[:, LANES:]
        m_ref[...] = m_new
        acc_ref[...] = alpha * acc_ref[...] + pv[:, 0:LANES]

    acc_ref[...] = jnp.zeros_like(acc_ref)
    m_ref[...] = jnp.full((2 * tq, LANES), NEG, F32)
    l_ref[...] = jnp.zeros((2 * tq, LANES), F32)
    tile = lambda u: jnp.maximum(i - 1 - u, 0)
    logits(i, sa_ref)
    logits(tile(0), sb_ref)
    finish(sa_ref, i, masked=True)
    n = i if past_tiles is None else past_tiles(m_ref)

    def body(v, carry):
        logits(tile(2 * v + 1), sa_ref)
        finish(sb_ref, tile(2 * v))
        logits(tile(2 * v + 2), sb_ref)
        finish(sa_ref, tile(2 * v + 1))
        return carry

    lax.fori_loop(0, n // 2, body, 0)

    @pl.when(n % 2 == 1)
    def _():
        finish(sb_ref, tile(n - 1))

    o = acc_ref[...] / l_ref[...]
    return jnp.where(_head_masks(tq), o[0:tq], o[tq:])


def _head_sumsq(x):
    hm = _head_masks(x.shape[0])
    sq = x * x
    return (jnp.sum(jnp.where(hm, sq, 0.0), axis=1, keepdims=True), jnp.sum(jnp.where(hm, 0.0, sq), axis=1, keepdims=True))


def _fox_kernel(q_ref, k_ref, v_ref, f_ref, o_ref, kx_ref, vx_ref, kn_ref, *flash, tq):
    p = pl.program_id(1)
    i = pl.program_id(2)
    seq = k_ref.shape[1]
    bt = FLASH_BUILD_ROWS

    @pl.when(i == 0)
    def _():
        src = _iota((LANES, LANES), 0)
        dst = _iota((LANES, LANES), 1)
        pm = jnp.logical_or(jnp.logical_and(src == 2 * p, dst < 3),
                            jnp.logical_and(src == 2 * p + 1, jnp.logical_and(dst >= 3, dst < 6))).astype(F32)
        sub = _iota((bt, LANES), 1) % 3

        def build(c, carry):
            rows = pl.ds(pl.multiple_of(c * bt, bt), bt)
            hi, mid, lo = _split3(_dot_hi(f_ref[0, rows, :], pm))
            k = k_ref[0, rows, :]
            kx_ref[rows, 0:LANES] = k
            kx_ref[rows, LANES:] = jnp.where(sub == 0, hi, jnp.where(sub == 1, mid, lo)).astype(BF16)
            _augment_values(vx_ref, rows, v_ref[0, rows, :])
            return tuple(jnp.maximum(c0, jnp.max(s, axis=0, keepdims=True))
                         for c0, s in zip(carry, _head_sumsq(k.astype(F32))))

        zero = jnp.zeros((1, 1), F32)
        kn = lax.fori_loop(0, seq // bt, build, (zero, zero))
        kn_ref[0:1, :] = jnp.broadcast_to(kn[0], (1, LANES))
        kn_ref[1:2, :] = jnp.broadcast_to(kn[1], (1, LANES))

    q = q_ref[0].astype(F32)
    lane = _iota((tq, LANES), 1)
    hm = lane < HEAD_DIM
    top = jnp.concatenate([jnp.where(hm, q, 0.0), jnp.where(lane < 3, -1.0, 0.0)], axis=1)
    bot = jnp.concatenate([jnp.where(hm, 0.0, q), jnp.where(jnp.logical_and(lane >= 3, lane < 6), -1.0, 0.0)], axis=1)
    qx = jnp.concatenate([top, bot], axis=0).astype(BF16)

    def past_tiles(m_ref):
        nt = seq // tq
        fend = f_ref[0, pl.ds(tq - 1, nt, stride=tq), :]
        lane_t = _iota((nt, LANES), 1)
        tpos = _iota((nt, 1), 0)
        m = m_ref[...]
        need = tpos < 0
        for h, qn2 in enumerate(_head_sumsq(q)):
            m_min = jnp.min(m[h * tq:(h + 1) * tq, 0:1], axis=0, keepdims=True)
            qk = jnp.sqrt(jnp.max(qn2, axis=0, keepdims=True) * kn_ref[h:h + 1, 0:1]) * BOUND_REL + BOUND_ABS
            f_h = jnp.sum(jnp.where(lane_t == 2 * p + h, fend, 0.0), axis=1, keepdims=True)
            need = jnp.logical_or(need, qk - f_h > m_min - FLASH_UNDERFLOW)
        first = jnp.min(jnp.where(jnp.logical_and(need, tpos < i), tpos, i).astype(F32))
        return i - first.astype(jnp.int32)

    o_ref[0] = _flash_causal(qx, kx_ref, vx_ref, flash, i, tq, past_tiles).astype(o_ref.dtype)


def _fox(hb, fcol, tq=FOX_ROWS):
    bsz, seq, _ = hb.shape
    assert seq % tq == 0
    return pl.pallas_call(
        functools.partial(_fox_kernel, tq=tq),
        grid=(bsz, PAIRS, seq // tq),
        in_specs=[
            pl.BlockSpec((1, tq, LANES), lambda b, p, i: (b, i, CB_CQ + p)),
            pl.BlockSpec((1, seq, LANES), lambda b, p, i: (b, 0, CB_CK + p)),
            pl.BlockSpec((1, seq, LANES), lambda b, p, i: (b, 0, CB_CV + p)),
            pl.BlockSpec((1, seq, LANES), lambda b, p, i: (b, 0, 0)),
        ],
        out_specs=pl.BlockSpec((1, tq, LANES), lambda b, p, i: (b, i, p)),
        out_shape=jax.ShapeDtypeStruct((bsz, seq, WIDTH), BF16),
        scratch_shapes=[pltpu.VMEM((seq, 2 * LANES), BF16), pltpu.VMEM((seq, 2 * LANES), BF16),
                        pltpu.VMEM((8, LANES), F32)] + _flash_scratch(tq),
        compiler_params=_cparams(("parallel", "parallel", "arbitrary")),
        name="fox",
    )(hb, hb, hb, fcol)


def _moba_kernel(q_ref, k_ref, v_ref, o_ref, kx_ref, vx_ref, kmean_ref, *flash, tq):
    i = pl.program_id(2)
    seq = k_ref.shape[1]
    blk = MOBA_BLOCK

    @pl.when(i == 0)
    def _():
        kmean_ref[...] = jnp.zeros_like(kmean_ref)
        lane = _iota((blk, LANES), 1)

        def build(n, carry):
            rows = pl.ds(pl.multiple_of(n * blk, blk), blk)
            k = k_ref[0, rows, :]
            kx_ref[rows, 0:LANES] = k
            kx_ref[rows, LANES:] = jnp.where(lane == n, 1.0, 0.0).astype(BF16)
            _augment_values(vx_ref, rows, v_ref[0, rows, :])
            kmean_ref[pl.ds(n, 1), :] = jnp.sum(k.astype(F32), axis=0, keepdims=True) * (1.0 / blk)
            return carry

        lax.fori_loop(0, seq // blk, build, 0)

    q = q_ref[0].astype(F32)
    hm = _head_masks(tq)
    nbp = -(-(seq // blk) // 8) * 8
    blk_id = _iota((nbp, tq), 0)
    blk_f = blk_id.astype(F32)
    own = i * (tq // blk) + _iota((nbp, tq), 1) // blk
    valid = blk_id < own
    kmean = kmean_ref[0:nbp, :]
    halves = []
    for h in range(2):
        qh = jnp.where(hm, q, 0.0) if h == 0 else jnp.where(hm, 0.0, q)
        g = jnp.where(valid, _dot_nt_hi(kmean, qh), NEG)
        bias = jnp.where(blk_id == own, 0.0, NEG)
        for _ in range(MOBA_TOPK):
            mx = jnp.max(g, axis=0, keepdims=True)
            first = jnp.min(jnp.where(g == mx, blk_f, FAR), axis=0, keepdims=True)
            pick = blk_f == first
            bias = jnp.where(jnp.logical_and(pick, valid), 0.0, bias)
            g = jnp.where(pick, TAKEN, g)
        bias = jnp.concatenate([bias, jnp.zeros((LANES - nbp, tq), F32)], axis=0).T
        halves.append(jnp.concatenate([qh, bias], axis=1))
    qx = jnp.concatenate(halves, axis=0).astype(BF16)
    o_ref[0] = _flash_causal(qx, kx_ref, vx_ref, flash, i, tq).astype(o_ref.dtype)


def _moba(hb, tq=MOBA_ROWS):
    bsz, seq, _ = hb.shape
    assert seq % tq == 0 and tq % MOBA_BLOCK == 0 and seq // MOBA_BLOCK <= LANES
    return pl.pallas_call(
        functools.partial(_moba_kernel, tq=tq),
        grid=(bsz, PAIRS, seq // tq),
        in_specs=[
            pl.BlockSpec((1, tq, LANES), lambda b, p, i: (b, i, CB_AQ + p)),
            pl.BlockSpec((1, seq, LANES), lambda b, p, i: (b, 0, CB_AK + p)),
            pl.BlockSpec((1, seq, LANES), lambda b, p, i: (b, 0, CB_AV + p)),
        ],
        out_specs=pl.BlockSpec((1, tq, LANES), lambda b, p, i: (b, i, p)),
        out_shape=jax.ShapeDtypeStruct((bsz, seq, WIDTH), BF16),
        scratch_shapes=[pltpu.VMEM((seq, 2 * LANES), BF16), pltpu.VMEM((seq, 2 * LANES), BF16),
                        pltpu.VMEM((LANES, LANES), F32)] + _flash_scratch(tq),
        compiler_params=_cparams(("parallel", "parallel", "arbitrary")),
        name="moba",
    )(hb, hb, hb)


def _band_kernel(q0, q1, q2, k0c, k0p, k1c, k1p, k2c, k2p, vc, vp, o_ref, qb, kb, vb, nb, mb, sb, *, tile):
    jt = pl.program_id(2)
    for g, (qr, kc, kp) in enumerate(((q0, k0c, k0p), (q1, k1c, k1p), (q2, k2c, k2p))):
        qb[g] = qr[0].astype(F32)
        kb[g, 0:tile, :] = kp[0].astype(F32)
        kb[g, tile:, :] = kc[0].astype(F32)
    vb[0:tile, :] = vp[0].astype(F32)
    vb[tile:, :] = vc[0].astype(F32)

    hm = _head_masks(BAND)
    qi = _iota((BAND, 2 * BAND), 0)
    kj = _iota((BAND, 2 * BAND), 1)
    dist = BAND + qi - kj
    band = jnp.logical_and(dist >= 0, dist <= BAND)
    cur_half = kj >= BAND
    nblocks = tile // BAND

    for g, dil in enumerate(DILATIONS):
        per_stream = nblocks // dil

        def step(it, carry, g=g, dil=dil, per_stream=per_stream):
            starts, kstarts, valids = [], [], []
            for j in range(BAND_GROUP):
                idx = it * BAND_GROUP + j
                n = idx % per_stream
                start = idx // per_stream + n * (BAND * dil)
                starts.append(start)
                kstarts.append(tile + start - BAND * dil)
                valids.append(jnp.logical_and(band, jnp.logical_or(cur_half, jnp.logical_or(jt > 0, n > 0))))
            qs = [qb[g, pl.ds(s, BAND, stride=dil), :] for s in starts]
            kks = [kb[g, pl.ds(s, 2 * BAND, stride=dil), :].astype(BF16) for s in kstarts]
            ones = jnp.ones((2 * BAND, LANES), BF16)
            vvs = [jnp.concatenate([vb[pl.ds(s, 2 * BAND, stride=dil), :].astype(BF16), ones], axis=1) for s in kstarts]
            logits = []
            for q, kk, valid in zip(qs, kks, valids):
                for h in range(2):
                    qh = (jnp.where(hm, q, 0.0) if h == 0 else jnp.where(hm, 0.0, q)).astype(BF16)
                    logits.append(jnp.where(valid, _dot_nt(qh, kk), NEG))
            ms = [jnp.max(s, axis=-1, keepdims=True) for s in logits]
            ps = [jnp.exp((s - m).astype(BF16)) for s, m in zip(logits, ms)]
            pvs = [_dot(p, vvs[i // 2]) for i, p in enumerate(ps)]
            nums = [pv[:, 0:LANES] for pv in pvs]
            ss = [pv[:, LANES:] for pv in pvs]
            for j, s in enumerate(starts):
                rows = pl.ds(s, BAND, stride=dil)
                nb[g, rows, :] = jnp.where(hm, nums[2 * j], nums[2 * j + 1])
                mb[g, rows, :] = jnp.where(hm, ms[2 * j], ms[2 * j + 1])
                sb[g, rows, :] = jnp.where(hm, ss[2 * j], ss[2 * j + 1])
            return carry

        lax.fori_loop(0, nblocks // BAND_GROUP, step, 0)

    ch = BAND_MERGE_ROWS

    def merge(i, carry):
        rows = pl.ds(pl.multiple_of(i * ch, ch), ch)
        m_all = jnp.maximum(jnp.maximum(mb[0, rows, :], mb[1, rows, :]), mb[2, rows, :])
        num = jnp.zeros((ch, LANES), F32)
        den = jnp.zeros((ch, LANES), F32)
        for g in range(len(DILATIONS)):
            w = jnp.exp(mb[g, rows, :] - m_all)
            num = num + nb[g, rows, :] * w
            den = den + sb[g, rows, :] * w
        o_ref[0, rows, :] = (num / den).astype(o_ref.dtype)
        return carry

    lax.fori_loop(0, tile // ch, merge, 0)


def _dilated(hb, tile=BAND_TILE):
    bsz, seq, _ = hb.shape
    ng = len(DILATIONS)
    nblocks = tile // BAND
    assert seq % tile == 0 and nblocks % BAND_GROUP == 0 and all(nblocks % d == 0 for d in DILATIONS)
    cur = lambda cb: pl.BlockSpec((1, tile, LANES), lambda b, p, j: (b, j, cb + p))
    prv = lambda cb: pl.BlockSpec((1, tile, LANES), lambda b, p, j: (b, jnp.maximum(j - 1, 0), cb + p))
    in_specs = [cur(CB_BQ + PAIRS * g) for g in range(ng)]
    for g in range(ng):
        in_specs += [cur(CB_BK + PAIRS * g), prv(CB_BK + PAIRS * g)]
    in_specs += [cur(CB_BV), prv(CB_BV)]
    acc = pltpu.VMEM((ng, tile, LANES), F32)
    return pl.pallas_call(
        functools.partial(_band_kernel, tile=tile),
        grid=(bsz, PAIRS, seq // tile),
        in_specs=in_specs,
        out_specs=pl.BlockSpec((1, tile, LANES), lambda b, p, j: (b, j, p)),
        out_shape=jax.ShapeDtypeStruct((bsz, seq, WIDTH), BF16),
        scratch_shapes=[acc, pltpu.VMEM((ng, 2 * tile, LANES), F32), pltpu.VMEM((2 * tile, LANES), F32), acc, acc, acc],
        compiler_params=_cparams(("parallel", "parallel", "arbitrary")),
        name="band",
    )(*([hb] * len(in_specs)))


def _dot01(a, b, nt=False, pieces=3):
    f = _dot_nt if nt else _dot
    if a.dtype == BF16:
        return sum(f(a, piece.astype(BF16)) for piece in _split3(b)[:pieces])
    return sum(f(piece.astype(BF16), b) for piece in _split3(a)[:pieces])


def _tri_inverse_all(ms):
    c = ms[0].shape[0]
    ri = _iota((c, c), 0)
    ci = _iota((c, c), 1)
    base = TRI_BASE
    inblk = ri // base == ci // base
    eye = jnp.where(ri == ci, 1.0, 0.0)
    ps = [jnp.where(inblk, -m, 0.0) for m in ms]
    ts = [eye + p for p in ps]
    for _ in range(3):
        pbs = [p.astype(BF16) for p in ps]
        ps = [_dot(pb, pb) for pb in pbs]
        ts = [t + _dot(t.astype(BF16), p.astype(BF16)) for t, p in zip(ts, ps)]
    size = base
    while size < c:
        lower = jnp.logical_and(ri // (2 * size) == ci // (2 * size), ri // size != ci // size)
        tbs = [t.astype(BF16) for t in ts]
        xs = [_dot(tb, jnp.where(lower, m, 0.0).astype(BF16)) for tb, m in zip(tbs, ms)]
        ts = [t - _dot(x.astype(BF16), tb) for t, x, tb in zip(ts, xs, tbs)]
        size *= 2
    return ts


def _gdn_kernel(h_ref, cw_ref, alog_ref, dtb_ref, nw_ref, o_ref,
                xe_ref, u_s, w_s, qd_s, kd_s, qk_s, egl_s, o_s, st_ref, *, tile):
    cc = GDN_CHUNK
    nc = tile // cc
    t = pl.program_id(1)

    @pl.when(t == 0)
    def _():
        xe_ref[0:8, :] = jnp.zeros((8, 3 * WIDTH), F32)
        st_ref[...] = jnp.zeros_like(st_ref)

    x = h_ref[0, :, LANES:]
    xe_ref[8:8 + tile, :] = x
    y = jnp.zeros((tile, 3 * WIDTH), F32)
    for j in range(CONV_K):
        y = y + cw_ref[j:j + 1, :] * xe_ref[8 - (CONV_K - 1) + j:8 - (CONV_K - 1) + j + tile, :]
    xe_ref[0:8, :] = x[tile - 8:tile, :]
    y = y * _sigmoid(y)

    bd = ((_iota((WIDTH, WIDTH), 0) // HEAD_DIM) == (_iota((WIDTH, WIDTH), 1) // HEAD_DIM)).astype(BF16)
    q = y[:, 0:WIDTH]
    k = y[:, WIDTH:2 * WIDTH]
    v = y[:, 2 * WIDTH:]
    q = q * lax.rsqrt(_dot01(q * q, bd, pieces=2) + RMS_EPS) * (HEAD_DIM ** -0.5)
    k = k * lax.rsqrt(_dot01(k * k, bd, pieces=2) + RMS_EPS)

    hs = h_ref[0, :, 0:LANES]
    er = _iota((LANES, WIDTH), 0)
    ec = _iota((LANES, WIDTH), 1) // HEAD_DIM
    beta = _sigmoid(_dot01(hs, (er == ec + HEADS).astype(BF16), pieces=2))
    g = -jnp.exp(alog_ref[...]) * jax.nn.softplus(_dot01(hs, (er == ec + 2 * HEADS).astype(BF16), pieces=2) + dtb_ref[...])

    tri = (_iota((cc, cc), 0) >= _iota((cc, cc), 1)).astype(BF16)
    gc = jnp.concatenate([_dot01(tri, g[c * cc:(c + 1) * cc]) for c in range(nc)], axis=0)
    glast = jnp.broadcast_to(gc.reshape(nc, cc, WIDTH)[:, cc - 1:cc, :], (nc, cc, WIDTH)).reshape(tile, WIDTH)
    eg = jnp.exp(gc)
    kb = k * beta
    vb = v * beta
    wb = kb * eg
    qd_s[...] = q * eg
    kd_s[...] = k * jnp.exp(glast - gc)
    egl_s[...] = jnp.exp(glast)

    ri = _iota((cc, cc), 0)
    ci = _iota((cc, cc), 1)
    incl = ri >= ci
    strict = ri > ci
    lane = _iota((cc, LANES), 1)
    hm = lane < HEAD_DIM
    sls = [slice(p * LANES, (p + 1) * LANES) for p in range(PAIRS)]
    keeps = [hm, jnp.logical_not(hm)]
    heads = [(p, h) for p in range(PAIRS) for h in range(2)]
    chains = [(p, h, c) for p, h in heads for c in range(nc)]
    rws = [slice(c * cc, (c + 1) * cc) for c in range(nc)]
    kps = [k[:, sl].astype(BF16) for sl in sls]
    rhss = [jnp.concatenate([vb[:, sl], wb[:, sl]], axis=1).astype(BF16) for sl in sls]
    gct = gc.T
    rowf = {(p, h): gct[p * LANES + h * HEAD_DIM:p * LANES + h * HEAD_DIM + 1, :] for p, h in heads}
    kbm = {(p, h): jnp.where(jnp.tile(keeps[h], (nc, 1)), kb[:, sls[p]], 0.0).astype(BF16) for p, h in heads}
    qm = {(p, h): jnp.where(jnp.tile(keeps[h], (nc, 1)), q[:, sls[p]], 0.0).astype(BF16) for p, h in heads}
    kks = [_dot_nt(kbm[p, h][rws[c]], kps[p][rws[c]]) for p, h, c in chains]
    qks = [_dot_nt(qm[p, h][rws[c]], kps[p][rws[c]]) for p, h, c in chains]
    mms = []
    for (p, h, c), kk, qk in zip(chains, kks, qks):
        col = p * LANES + h * HEAD_DIM
        diff = gc[rws[c], col:col + 1] - rowf[p, h][:, rws[c]]
        decay = jnp.where(incl, jnp.exp(jnp.where(incl, diff, 0.0)), 0.0)
        mms.append(jnp.where(strict, kk * decay, 0.0))
        qk_s[2 * p + h, rws[c], :] = qk * decay
    tinvs = _tri_inverse_all(mms)
    rs = {ch: _dot(tinv.astype(BF16), rhss[ch[0]][rws[ch[2]]]) for ch, tinv in zip(chains, tinvs)}
    for p in range(PAIRS):
        for c in range(nc):
            r0, r1 = rs[p, 0, c], rs[p, 1, c]
            u_s[rws[c], sls[p]] = jnp.where(hm, r0[:, :LANES], r1[:, :LANES])
            w_s[rws[c], sls[p]] = jnp.where(hm, r0[:, LANES:], r1[:, LANES:])

    bdiag = (_iota((LANES, LANES), 0) // HEAD_DIM) == (_iota((LANES, LANES), 1) // HEAD_DIM)
    for c in range(nc):
        rows = rws[c]
        sts = [st_ref[p] for p in range(PAIRS)]
        stbs = [st.astype(BF16) for st in sts]
        wss = [_dot(w_s[rows, sl].astype(BF16), stb) for sl, stb in zip(sls, stbs)]
        qss = [_dot(qd_s[rows, sl].astype(BF16), stb) for sl, stb in zip(sls, stbs)]
        vnbs = [(u_s[rows, sl] - ws).astype(BF16) for sl, ws in zip(sls, wss)]
        upds = [lax.dot_general(kd_s[rows, sl].astype(BF16), vnb, (((0,), (0,)), ((), ())), preferred_element_type=F32)
                for sl, vnb in zip(sls, vnbs)]
        intra = [(_dot(qk_s[2 * p, rows, :].astype(BF16), vnbs[p]), _dot(qk_s[2 * p + 1, rows, :].astype(BF16), vnbs[p]))
                 for p in range(PAIRS)]
        for p in range(PAIRS):
            st_ref[p] = sts[p] * egl_s[c * cc:c * cc + 1, sls[p]] + jnp.where(bdiag, upds[p], 0.0)
            o_s[rows, sls[p]] = qss[p] + jnp.where(hm, intra[p][0], intra[p][1])

    o = o_s[...]
    ms = _dot01(o * o, bd, pieces=2) * (1.0 / HEAD_DIM)
    o_ref[0] = (o * lax.rsqrt(ms + RMS_EPS) * nw_ref[...]).astype(o_ref.dtype)


def _gdn(hf, conv_w, a_log, dt_bias, norm_w, tile=GDN_ROWS):
    bsz, seq, _ = hf.shape
    rep = lambda a: jnp.repeat(a.astype(F32), HEAD_DIM)[None, :]
    wide = pltpu.VMEM((tile, WIDTH), F32)
    return pl.pallas_call(
        functools.partial(_gdn_kernel, tile=tile),
        grid=(bsz, seq // tile),
        in_specs=[
            pl.BlockSpec((1, tile, HF_COLS), lambda b, t: (b, t, 0)),
            pl.BlockSpec((CONV_K, 3 * WIDTH), lambda b, t: (0, 0)),
            pl.BlockSpec((1, WIDTH), lambda b, t: (0, 0)),
            pl.BlockSpec((1, WIDTH), lambda b, t: (0, 0)),
            pl.BlockSpec((1, WIDTH), lambda b, t: (0, 0)),
        ],
        out_specs=pl.BlockSpec((1, tile, WIDTH), lambda b, t: (b, t, 0)),
        out_shape=jax.ShapeDtypeStruct((bsz, seq, WIDTH), BF16),
        scratch_shapes=[pltpu.VMEM((tile + 8, 3 * WIDTH), F32), wide, wide, wide, wide,
                        pltpu.VMEM((HEADS, tile, LANES), F32), wide, wide,
                        pltpu.VMEM((PAIRS, LANES, LANES), F32)],
        compiler_params=_cparams(("parallel", "arbitrary")),
        name="gdn",
    )(hf, conv_w.astype(F32), rep(a_log), rep(dt_bias), jnp.tile(norm_w.astype(F32), HEADS)[None, :])


def _memattn_kernel(q_ref, kv_ref, o_ref):
    tq = q_ref.shape[1]
    q = q_ref[0]
    k = kv_ref[0, :, 0:WIDTH]
    v = kv_ref[0, :, WIDTH:2 * WIDTH]
    head = _iota((tq, WIDTH), 1) // MEM_HEAD_DIM
    zero = jnp.zeros_like(q)
    out = jnp.zeros((tq, WIDTH), F32)
    for h in range(MEM_HEADS):
        s = _dot_nt(jnp.where(head == h, q, zero), k)
        m = jnp.max(s, axis=-1, keepdims=True)
        p = jnp.exp(s - m)
        o = _dot(p.astype(BF16), v) / jnp.sum(p, axis=-1, keepdims=True)
        out = jnp.where(head == h, o, out)
    o_ref[0] = out.astype(o_ref.dtype)


def _memattn(he, kv, layer, tq=MEMATTN_ROWS):
    bsz, seq, _ = he.shape
    mlen = kv.shape[1]
    return pl.pallas_call(
        _memattn_kernel,
        grid=(bsz, seq // tq),
        in_specs=[
            pl.BlockSpec((1, tq, WIDTH), lambda b, i: (b, i, 0)),
            pl.BlockSpec((1, mlen, 2 * WIDTH), lambda b, i: (b, 0, layer)),
        ],
        out_specs=pl.BlockSpec((1, tq, WIDTH), lambda b, i: (b, i, 0)),
        out_shape=jax.ShapeDtypeStruct((bsz, seq, WIDTH), BF16),
        compiler_params=_cparams(("parallel", "parallel")),
        name="memattn",
    )(he, kv)


def _sigmoid(x):
    return 0.5 * jnp.tanh(0.5 * x) + 0.5


def _out_kernel(oa, ob, oc, od, oe, z_ref, ml_ref, x_ref, wb_ref, wo_ref, g_ref, b_ref, y_ref, yb_ref, *, alpha):
    tm = x_ref.shape[0]
    acc = jnp.zeros((tm, D_MODEL), F32)
    for n, o_ref in enumerate((oa, ob, oc, od, oe)):
        z = z_ref[:, n * WIDTH:(n + 1) * WIDTH]
        gated = o_ref[...] * (z * _sigmoid(z))
        yn = _dot(gated, wb_ref[n])
        acc = acc + _sigmoid(ml_ref[:, n * D_MODEL:(n + 1) * D_MODEL]).astype(F32) * yn
    r = alpha * x_ref[...] + _dot(acc.astype(BF16), wo_ref[...])
    mu = jnp.mean(r, axis=-1, keepdims=True)
    rc = r - mu
    var = jnp.mean(rc * rc, axis=-1, keepdims=True)
    y = rc * lax.rsqrt(var + LN_EPS) * g_ref[...] + b_ref[...]
    y_ref[...] = y
    yb_ref[...] = y.astype(BF16)


def _out(branches, z, ml, x, w_branch, w_out, ln_g, ln_b, alpha, tm=OUT_ROWS):
    m = x.shape[0]
    row = lambda n: pl.BlockSpec((tm, n), lambda i: (i, 0))
    return pl.pallas_call(
        functools.partial(_out_kernel, alpha=alpha),
        grid=(m // tm,),
        in_specs=[row(WIDTH)] * N_BRANCH + [row(N_BRANCH * WIDTH), row(N_BRANCH * D_MODEL), row(D_MODEL),
                  pl.BlockSpec((N_BRANCH, WIDTH, D_MODEL), lambda i: (0, 0, 0)),
                  pl.BlockSpec((D_MODEL, D_MODEL), lambda i: (0, 0)),
                  pl.BlockSpec((1, D_MODEL), lambda i: (0, 0)),
                  pl.BlockSpec((1, D_MODEL), lambda i: (0, 0))],
        out_specs=[row(D_MODEL), row(D_MODEL)],
        out_shape=[jax.ShapeDtypeStruct((m, D_MODEL), F32), jax.ShapeDtypeStruct((m, D_MODEL), BF16)],
        compiler_params=_cparams(("parallel",)),
        name="out",
    )(*branches, z, ml, x, w_branch, w_out, ln_g, ln_b)


def _split_weights(w_in, b_in):
    scale = np.ones((HB_COLS,), np.float32)
    for cb in (CB_AQ, CB_CQ):
        scale[cb * LANES:cb * LANES + WIDTH] = HEAD_DIM ** -0.5 * LOG2E
    scale[CB_BQ * LANES:CB_BQ * LANES + 3 * WIDTH] = HEAD_DIM ** -0.5
    sc = np.concatenate([np.arange(o, o + HEADS) for o in (O_CF, O_DBETA, O_DDECAY)])
    pad = jnp.zeros(w_in.shape[:2] + (LANES - 3 * HEADS,), w_in.dtype)
    w_hf = jnp.concatenate([w_in[..., sc], pad, w_in[..., O_D:O_D + 3 * WIDTH]], axis=-1)
    b_hf = jnp.concatenate([b_in[..., sc], pad[:, 0], b_in[..., O_D:O_D + 3 * WIDTH]], axis=-1)
    span = lambda a, o, n: a[..., o:o + n]
    groups = {
        "hb": (span(w_in, 0, HB_COLS), span(b_in, 0, HB_COLS), scale),
        "he": (span(w_in, O_E, WIDTH), span(b_in, O_E, WIDTH), np.full((WIDTH,), MEM_HEAD_DIM ** -0.5, np.float32)),
        "hf": (w_hf, b_hf, np.ones((HF_COLS,), np.float32)),
        "z": (span(w_in, O_Z, N_BRANCH * WIDTH), span(b_in, O_Z, N_BRANCH * WIDTH), np.ones((N_BRANCH * WIDTH,), np.float32)),
        "ml": (span(w_in, O_MERGE, N_BRANCH * D_MODEL), span(b_in, O_MERGE, N_BRANCH * D_MODEL),
               np.ones((N_BRANCH * D_MODEL,), np.float32)),
    }
    return {k: (w.astype(F32), b.astype(F32)[:, None, :], jnp.asarray(s)[None, :]) for k, (w, b, s) in groups.items()}


def _layer(x, xb, kv, layer, gw, conv_w, a_log, dt_bias, gdn_norm_w, w_branch, w_out, ln_g, ln_b, alpha):
    bsz, seq, d = x.shape
    m = bsz * seq
    xf = x.reshape(m, d)
    proj = lambda name, dtype: _proj(xb, *gw[name], layer, dtype, PROJ_ROWS, PROJ_COLS[name])
    hb = proj("hb", BF16).reshape(bsz, seq, HB_COLS)
    he = proj("he", BF16).reshape(bsz, seq, WIDTH)
    hf = proj("hf", F32).reshape(bsz, seq, HF_COLS)
    z = proj("z", BF16)
    ml = proj("ml", BF16)

    o_a = _moba(hb)
    o_b = _dilated(hb)
    o_c = _fox(hb, _fcum(hf))
    o_d = _gdn(hf, conv_w, a_log, dt_bias, gdn_norm_w)
    o_e = _memattn(he, kv, layer)
    branches = [o.reshape(m, WIDTH) for o in (o_a, o_b, o_c, o_d, o_e)]
    y, yb = _out(branches, z, ml, xf, w_branch.astype(BF16), w_out.astype(BF16),
                 ln_g.astype(F32)[None, :], ln_b.astype(F32)[None, :], alpha)
    return y.reshape(bsz, seq, d), yb


def kernel(x, mem, mem_ln_g, mem_ln_b, w_in, b_in, conv_w, a_log, dt_bias, gdn_norm_w, w_mem_kv, w_branch, w_out, ln_g, ln_b):
    depth = w_in.shape[0]
    alpha = float((2 * depth) ** 0.25)
    w_kv = jnp.concatenate([w_mem_kv[l] for l in range(depth)], axis=1).astype(BF16)
    kv = _memkv(mem.astype(F32), mem_ln_g.astype(F32)[None, :], mem_ln_b.astype(F32)[None, :], w_kv)
    x = x.astype(F32)
    xb = x.reshape(-1, x.shape[-1]).astype(BF16)
    gw = _split_weights(w_in, b_in)
    for l in range(depth):
        x, xb = _layer(x, xb, kv, l, gw, conv_w[l], a_log[l], dt_bias[l], gdn_norm_w[l],
                       w_branch[l], w_out[l], ln_g[l], ln_b[l], alpha)
    return x
```

```python
import functools

import jax
import jax.numpy as jnp
import numpy as np
from jax import lax
from jax.experimental import pallas as pl
from jax.experimental.pallas import tpu as pltpu

F32 = jnp.float32
BF16 = jnp.bfloat16
HI = lax.Precision.HIGHEST

D_MODEL = 1024
HEAD_DIM = 64
HEADS = 6
WIDTH = HEADS * HEAD_DIM
N_BRANCH = 5
LANES = 128
PAIRS = WIDTH // LANES
MOBA_BLOCK = 256
MOBA_TOPK = 3
DILATIONS = (1, 4, 16)
BAND = 128
BAND_TILE = 2048
BAND_GROUP = 8
GDN_CHUNK = 128
CONV_K = 4
MEM_HEADS = 4
MEM_HEAD_DIM = WIDTH // MEM_HEADS
NEG = -1e30
LOG2E = float(np.log2(np.e))
FLASH_UNDERFLOW = 160.0
LN_EPS = 1e-5
RMS_EPS = 1e-6

_SPLIT = (3 * WIDTH, 6 * WIDTH, WIDTH, 3 * WIDTH, HEADS, 3 * WIDTH, HEADS, HEADS, WIDTH, N_BRANCH * WIDTH, N_BRANCH * D_MODEL)
_OFF = tuple(int(v) for v in np.concatenate([[0], np.cumsum(_SPLIT)]))
(O_A, O_BQK, O_BV, O_C, O_CF, O_D, O_DBETA, O_DDECAY, O_E, O_Z, O_MERGE, _) = _OFF

CB_AQ, CB_AK, CB_AV = 0, 3, 6
CB_BQ, CB_BK, CB_BV = 9, 18, 27
CB_CQ, CB_CK, CB_CV = 30, 33, 36
HB_COLS = 39 * LANES
assert (O_A, O_BQK, O_BV, O_C, O_CF) == tuple(LANES * c for c in (CB_AQ, CB_BQ, CB_BV, CB_CQ, 39))
HF_COLS = LANES + 3 * WIDTH

VMEM_LIMIT = 56 * 1024 * 1024

PROJ_ROWS = 2048
PROJ_COLS = {"hb": HB_COLS // 3, "he": WIDTH, "hf": HF_COLS, "z": N_BRANCH * WIDTH, "ml": 1024}
FOX_ROWS = 256
MOBA_ROWS = 2 * MOBA_BLOCK
FLASH_BUILD_ROWS = 512
BAND_MERGE_ROWS = 256
GDN_ROWS = 4 * GDN_CHUNK
MEMATTN_ROWS = 512
OUT_ROWS = 512
TRI_BASE = 16
BOUND_REL, BOUND_ABS = 1.01, 1.0
FAR = 1e9
TAKEN = -3e38


def _cparams(sem):
    return pltpu.CompilerParams(dimension_semantics=sem, vmem_limit_bytes=VMEM_LIMIT)


def _dot(a, b):
    return jnp.dot(a, b, preferred_element_type=F32)


def _dot_nt(a, b):
    return lax.dot_general(a, b, (((1,), (1,)), ((), ())), preferred_element_type=F32)


def _dot_nt_hi(a, b):
    return lax.dot_general(a, b, (((1,), (1,)), ((), ())), preferred_element_type=F32, precision=HI)


def _iota(shape, dim):
    return lax.broadcasted_iota(jnp.int32, shape, dim)


def _proj_kernel(x_ref, w_ref, b_ref, s_ref, o_ref, wb_ref):
    @pl.when(pl.program_id(1) == 0)
    def _():
        wb_ref[...] = w_ref[...].astype(BF16)

    acc = _dot(x_ref[...], wb_ref[...])
    o_ref[...] = ((acc + b_ref[...]) * s_ref[...]).astype(o_ref.dtype)


def _proj(xb, w, b, scale, layer, out_dtype, tm, tn):
    m, k = xb.shape
    n = b.shape[-1]
    assert m % tm == 0 and n % tn == 0
    return pl.pallas_call(
        _proj_kernel,
        grid=(n // tn, m // tm),
        in_specs=[
            pl.BlockSpec((tm, k), lambda j, i: (i, 0)),
            pl.BlockSpec((None, k, tn), lambda j, i: (layer, 0, j)),
            pl.BlockSpec((None, 1, tn), lambda j, i: (layer, 0, j)),
            pl.BlockSpec((1, tn), lambda j, i: (0, j)),
        ],
        out_specs=pl.BlockSpec((tm, tn), lambda j, i: (i, j)),
        out_shape=jax.ShapeDtypeStruct((m, n), out_dtype),
        scratch_shapes=[pltpu.VMEM((k, tn), BF16)],
        compiler_params=_cparams(("parallel", "arbitrary")),
        name="proj",
    )(xb, w, b, scale)


def _memkv_kernel(mem_ref, g_ref, b_ref, w_ref, o_ref):
    x = mem_ref[0]
    mu = jnp.mean(x, axis=-1, keepdims=True)
    xc = x - mu
    var = jnp.mean(xc * xc, axis=-1, keepdims=True)
    y = xc * lax.rsqrt(var + LN_EPS) * g_ref[...] + b_ref[...]
    o_ref[0] = _dot(y.astype(BF16), w_ref[...]).astype(o_ref.dtype)


def _memkv(mem, g, b, w):
    bsz, mlen, d = mem.shape
    n = w.shape[1]
    return pl.pallas_call(
        _memkv_kernel,
        grid=(bsz,),
        in_specs=[
            pl.BlockSpec((1, mlen, d), lambda i: (i, 0, 0)),
            pl.BlockSpec((1, d), lambda i: (0, 0)),
            pl.BlockSpec((1, d), lambda i: (0, 0)),
            pl.BlockSpec((d, n), lambda i: (0, 0)),
        ],
        out_specs=pl.BlockSpec((1, mlen, n), lambda i: (i, 0, 0)),
        out_shape=jax.ShapeDtypeStruct((bsz, mlen, n), BF16),
        compiler_params=_cparams(("parallel",)),
        name="memkv",
    )(mem, g, b, w)


def _fcum_kernel(h_ref, o_ref, *, blk):
    seq = h_ref.shape[1]
    tri = (_iota((blk, blk), 0) >= _iota((blk, blk), 1)).astype(BF16)

    def body(i, carry):
        rows = pl.ds(pl.multiple_of(i * blk, blk), blk)
        logf = jax.nn.log_sigmoid(h_ref[0, rows, :])
        c = _dot01(tri, logf) + carry
        o_ref[0, rows, :] = c * LOG2E
        return c[blk - 1:blk, :]

    lax.fori_loop(0, seq // blk, body, jnp.zeros((1, LANES), F32))


def _fcum(hf):
    bsz, seq, _ = hf.shape
    return pl.pallas_call(
        functools.partial(_fcum_kernel, blk=FLASH_BUILD_ROWS),
        grid=(bsz,),
        in_specs=[pl.BlockSpec((1, seq, LANES), lambda b: (b, 0, 0))],
        out_specs=pl.BlockSpec((1, seq, LANES), lambda b: (b, 0, 0)),
        out_shape=jax.ShapeDtypeStruct((bsz, seq, LANES), F32),
        compiler_params=_cparams(("parallel",)),
        name="fcum",
    )(hf)


def _head_masks(rows):
    lane = _iota((rows, LANES), 1)
    return lane < HEAD_DIM


def _split3(x):
    hi = x.astype(BF16).astype(F32)
    r = x - hi
    mid = r.astype(BF16).astype(F32)
    return hi, mid, r - mid


def _flash_scratch(tq):
    return [pltpu.VMEM((2 * tq, LANES), F32), pltpu.VMEM((2 * tq, tq), F32), pltpu.VMEM((2 * tq, tq), F32),
            pltpu.VMEM((2 * tq, LANES), F32), pltpu.VMEM((2 * tq, LANES), F32)]


def _augment_values(vx_ref, rows, v):
    vx_ref[rows, 0:LANES] = v
    vx_ref[rows, LANES:] = jnp.ones((v.shape[0], LANES), BF16)


def _flash_causal(qx, kx_ref, vx_ref, scratch, i, tq, past_tiles=None):
    acc_ref, sa_ref, sb_ref, m_ref, l_ref = scratch

    def logits(kt, dst):
        rows = pl.ds(pl.multiple_of(kt * tq, tq), tq)
        dst[...] = _dot_nt(qx, kx_ref[rows, :])

    r = _iota((2 * tq, tq), 0)
    causal = jnp.where(r >= tq, r - tq, r) >= _iota((2 * tq, tq), 1)

    def finish(src, kt, masked=False):
        rows = pl.ds(pl.multiple_of(kt * tq, tq), tq)
        s = src[...]
        if masked:
            s = jnp.where(causal, s, NEG)
        m = m_ref[...]
        m_new = jnp.maximum(m, jnp.max(s, axis=-1, keepdims=True))
        alpha = jnp.exp2(m - m_new)
        p = jnp.exp2((s - jnp.tile(m_new, (1, tq // LANES))).astype(BF16))
        pv = _dot(p, vx_ref[rows, :])
        l_ref[...] = alpha * l_ref[...] + pv[:, LANES:]
        m_ref[...] = m_new
        acc_ref[...] = alpha * acc_ref[...] + pv[:, 0:LANES]

    acc_ref[...] = jnp.zeros_like(acc_ref)
    m_ref[...] = jnp.full((2 * tq, LANES), NEG, F32)
    l_ref[...] = jnp.zeros((2 * tq, LANES), F32)
    tile = lambda u: jnp.maximum(i - 1 - u, 0)
    logits(i, sa_ref)
    logits(tile(0), sb_ref)
    finish(sa_ref, i, masked=True)
    n = i if past_tiles is None else past_tiles(m_ref)

    def body(v, carry):
        logits(tile(2 * v + 1), sa_ref)
        finish(sb_ref, tile(2 * v))
        logits(tile(2 * v + 2), sb_ref)
        finish(sa_ref, tile(2 * v + 1))
        return carry

    lax.fori_loop(0, n // 2, body, 0)

    @pl.when(n % 2 == 1)
    def _():
        finish(sb_ref, tile(n - 1))

    o = acc_ref[...] / l_ref[...]
    return jnp.where(_head_masks(tq), o[0:tq], o[tq:])


def _head_sumsq(x):
    hm = _head_masks(x.shape[0])
    sq = x * x
    return (jnp.sum(jnp.where(hm, sq, 0.0), axis=1, keepdims=True), jnp.sum(jnp.where(hm, 0.0, sq), axis=1, keepdims=True))


def _fox_kernel(q_ref, k_ref, v_ref, f_ref, o_ref, kx_ref, vx_ref, kn_ref, *flash, tq):
    p = pl.program_id(1)
    i = pl.program_id(2)
    seq = k_ref.shape[1]
    bt = FLASH_BUILD_ROWS

    @pl.when(i == 0)
    def _():
        src = _iota((LANES, LANES), 0)
        dst = _iota((LANES, LANES), 1)
        pm = jnp.logical_or(jnp.logical_and(src == 2 * p, dst < 3),
                            jnp.logical_and(src == 2 * p + 1, jnp.logical_and(dst >= 3, dst < 6))).astype(BF16)
        sub = _iota((bt, LANES), 1) % 3

        def build(c, carry):
            rows = pl.ds(pl.multiple_of(c * bt, bt), bt)
            hi, mid, lo = _split3(_dot01(f_ref[0, rows, :], pm))
            k = k_ref[0, rows, :]
            kx_ref[rows, 0:LANES] = k
            kx_ref[rows, LANES:] = jnp.where(sub == 0, hi, jnp.where(sub == 1, mid, lo)).astype(BF16)
            _augment_values(vx_ref, rows, v_ref[0, rows, :])
            return tuple(jnp.maximum(c0, jnp.max(s, axis=0, keepdims=True))
                         for c0, s in zip(carry, _head_sumsq(k.astype(F32))))

        zero = jnp.zeros((1, 1), F32)
        kn = lax.fori_loop(0, seq // bt, build, (zero, zero))
        kn_ref[0:1, :] = jnp.broadcast_to(kn[0], (1, LANES))
        kn_ref[1:2, :] = jnp.broadcast_to(kn[1], (1, LANES))

    q = q_ref[0].astype(F32)
    lane = _iota((tq, LANES), 1)
    hm = lane < HEAD_DIM
    top = jnp.concatenate([jnp.where(hm, q, 0.0), jnp.where(lane < 3, -1.0, 0.0)], axis=1)
    bot = jnp.concatenate([jnp.where(hm, 0.0, q), jnp.where(jnp.logical_and(lane >= 3, lane < 6), -1.0, 0.0)], axis=1)
    qx = jnp.concatenate([top, bot], axis=0).astype(BF16)

    def past_tiles(m_ref):
        nt = seq // tq
        fend = f_ref[0, pl.ds(tq - 1, nt, stride=tq), :]
        lane_t = _iota((nt, LANES), 1)
        tpos = _iota((nt, 1), 0)
        m = m_ref[...]
        need = tpos < 0
        for h, qn2 in enumerate(_head_sumsq(q)):
            m_min = jnp.min(m[h * tq:(h + 1) * tq, 0:1], axis=0, keepdims=True)
            qk = jnp.sqrt(jnp.max(qn2, axis=0, keepdims=True) * kn_ref[h:h + 1, 0:1]) * BOUND_REL + BOUND_ABS
            f_h = jnp.sum(jnp.where(lane_t == 2 * p + h, fend, 0.0), axis=1, keepdims=True)
            need = jnp.logical_or(need, qk - f_h > m_min - FLASH_UNDERFLOW)
        first = jnp.min(jnp.where(jnp.logical_and(need, tpos < i), tpos, i).astype(F32))
        return i - first.astype(jnp.int32)

    o_ref[0] = _flash_causal(qx, kx_ref, vx_ref, flash, i, tq, past_tiles).astype(o_ref.dtype)


def _fox(hb, fcol, tq=FOX_ROWS):
    bsz, seq, _ = hb.shape
    assert seq % tq == 0
    return pl.pallas_call(
        functools.partial(_fox_kernel, tq=tq),
        grid=(bsz, PAIRS, seq // tq),
        in_specs=[
            pl.BlockSpec((1, tq, LANES), lambda b, p, i: (b, i, CB_CQ + p)),
            pl.BlockSpec((1, seq, LANES), lambda b, p, i: (b, 0, CB_CK + p)),
            pl.BlockSpec((1, seq, LANES), lambda b, p, i: (b, 0, CB_CV + p)),
            pl.BlockSpec((1, seq, LANES), lambda b, p, i: (b, 0, 0)),
        ],
        out_specs=pl.BlockSpec((1, tq, LANES), lambda b, p, i: (b, i, p)),
        out_shape=jax.ShapeDtypeStruct((bsz, seq, WIDTH), BF16),
        scratch_shapes=[pltpu.VMEM((seq, 2 * LANES), BF16), pltpu.VMEM((seq, 2 * LANES), BF16),
                        pltpu.VMEM((8, LANES), F32)] + _flash_scratch(tq),
        compiler_params=_cparams(("parallel", "parallel", "arbitrary")),
        name="fox",
    )(hb, hb, hb, fcol)


def _moba_kernel(q_ref, k_ref, v_ref, o_ref, kx_ref, vx_ref, kmean_ref, *flash, tq):
    i = pl.program_id(2)
    seq = k_ref.shape[1]
    blk = MOBA_BLOCK

    @pl.when(i == 0)
    def _():
        kmean_ref[...] = jnp.zeros_like(kmean_ref)
        lane = _iota((blk, LANES), 1)

        def build(n, carry):
            rows = pl.ds(pl.multiple_of(n * blk, blk), blk)
            k = k_ref[0, rows, :]
            kx_ref[rows, 0:LANES] = k
            kx_ref[rows, LANES:] = jnp.where(lane == n, 1.0, 0.0).astype(BF16)
            _augment_values(vx_ref, rows, v_ref[0, rows, :])
            kmean_ref[pl.ds(n, 1), :] = jnp.sum(k.astype(F32), axis=0, keepdims=True) * (1.0 / blk)
            return carry

        lax.fori_loop(0, seq // blk, build, 0)

    q = q_ref[0].astype(F32)
    hm = _head_masks(tq)
    nbp = -(-(seq // blk) // 8) * 8
    blk_id = _iota((nbp, tq), 0)
    blk_f = blk_id.astype(F32)
    own = i * (tq // blk) + _iota((nbp, tq), 1) // blk
    valid = blk_id < own
    kmean = kmean_ref[0:nbp, :]
    halves = []
    for h in range(2):
        qh = jnp.where(hm, q, 0.0) if h == 0 else jnp.where(hm, 0.0, q)
        g = jnp.where(valid, _dot_nt_hi(kmean, qh), NEG)
        bias = jnp.where(blk_id == own, 0.0, NEG)
        for _ in range(MOBA_TOPK):
            mx = jnp.max(g, axis=0, keepdims=True)
            first = jnp.min(jnp.where(g == mx, blk_f, FAR), axis=0, keepdims=True)
            pick = blk_f == first
            bias = jnp.where(jnp.logical_and(pick, valid), 0.0, bias)
            g = jnp.where(pick, TAKEN, g)
        bias = jnp.concatenate([bias, jnp.zeros((LANES - nbp, tq), F32)], axis=0).T
        halves.append(jnp.concatenate([qh, bias], axis=1))
    qx = jnp.concatenate(halves, axis=0).astype(BF16)
    o_ref[0] = _flash_causal(qx, kx_ref, vx_ref, flash, i, tq).astype(o_ref.dtype)


def _moba(hb, tq=MOBA_ROWS):
    bsz, seq, _ = hb.shape
    assert seq % tq == 0 and tq % MOBA_BLOCK == 0 and seq // MOBA_BLOCK <= LANES
    return pl.pallas_call(
        functools.partial(_moba_kernel, tq=tq),
        grid=(bsz, PAIRS, seq // tq),
        in_specs=[
            pl.BlockSpec((1, tq, LANES), lambda b, p, i: (b, i, CB_AQ + p)),
            pl.BlockSpec((1, seq, LANES), lambda b, p, i: (b, 0, CB_AK + p)),
            pl.BlockSpec((1, seq, LANES), lambda b, p, i: (b, 0, CB_AV + p)),
        ],
        out_specs=pl.BlockSpec((1, tq, LANES), lambda b, p, i: (b, i, p)),
        out_shape=jax.ShapeDtypeStruct((bsz, seq, WIDTH), BF16),
        scratch_shapes=[pltpu.VMEM((seq, 2 * LANES), BF16), pltpu.VMEM((seq, 2 * LANES), BF16),
                        pltpu.VMEM((LANES, LANES), F32)] + _flash_scratch(tq),
        compiler_params=_cparams(("parallel", "parallel", "arbitrary")),
        name="moba",
    )(hb, hb, hb)


def _band_kernel(q0, q1, q2, k0c, k0p, k1c, k1p, k2c, k2p, vc, vp, o_ref, qb, kb, vb, nb, mb, sb, *, tile):
    jt = pl.program_id(2)
    for g, (qr, kc, kp) in enumerate(((q0, k0c, k0p), (q1, k1c, k1p), (q2, k2c, k2p))):
        qb[g] = qr[0].astype(F32)
        kb[g, 0:tile, :] = kp[0].astype(F32)
        kb[g, tile:, :] = kc[0].astype(F32)
    vb[0:tile, :] = vp[0].astype(F32)
    vb[tile:, :] = vc[0].astype(F32)

    hm = _head_masks(BAND)
    qi = _iota((BAND, 2 * BAND), 0)
    kj = _iota((BAND, 2 * BAND), 1)
    dist = BAND + qi - kj
    band = jnp.logical_and(dist >= 0, dist <= BAND)
    cur_half = kj >= BAND
    nblocks = tile // BAND

    for g, dil in enumerate(DILATIONS):
        per_stream = nblocks // dil

        def step(it, carry, g=g, dil=dil, per_stream=per_stream):
            starts, kstarts, valids = [], [], []
            for j in range(BAND_GROUP):
                idx = it * BAND_GROUP + j
                n = idx % per_stream
                start = idx // per_stream + n * (BAND * dil)
                starts.append(start)
                kstarts.append(tile + start - BAND * dil)
                valids.append(jnp.logical_and(band, jnp.logical_or(cur_half, jnp.logical_or(jt > 0, n > 0))))
            qs = [qb[g, pl.ds(s, BAND, stride=dil), :] for s in starts]
            kks = [kb[g, pl.ds(s, 2 * BAND, stride=dil), :].astype(BF16) for s in kstarts]
            ones = jnp.ones((2 * BAND, LANES), BF16)
            vvs = [jnp.concatenate([vb[pl.ds(s, 2 * BAND, stride=dil), :].astype(BF16), ones], axis=1) for s in kstarts]
            logits = []
            for q, kk, valid in zip(qs, kks, valids):
                for h in range(2):
                    qh = (jnp.where(hm, q, 0.0) if h == 0 else jnp.where(hm, 0.0, q)).astype(BF16)
                    logits.append(jnp.where(valid, _dot_nt(qh, kk), NEG))
            ms = [jnp.max(s, axis=-1, keepdims=True) for s in logits]
            ps = [jnp.exp((s - m).astype(BF16)) for s, m in zip(logits, ms)]
            pvs = [_dot(p, vvs[i // 2]) for i, p in enumerate(ps)]
            nums = [pv[:, 0:LANES] for pv in pvs]
            ss = [pv[:, LANES:] for pv in pvs]
            for j, s in enumerate(starts):
                rows = pl.ds(s, BAND, stride=dil)
                nb[g, rows, :] = jnp.where(hm, nums[2 * j], nums[2 * j + 1])
                mb[g, rows, :] = jnp.where(hm, ms[2 * j], ms[2 * j + 1])
                sb[g, rows, :] = jnp.where(hm, ss[2 * j], ss[2 * j + 1])
            return carry

        lax.fori_loop(0, nblocks // BAND_GROUP, step, 0)

    ch = BAND_MERGE_ROWS

    def merge(i, carry):
        rows = pl.ds(pl.multiple_of(i * ch, ch), ch)
        m_all = jnp.maximum(jnp.maximum(mb[0, rows, :], mb[1, rows, :]), mb[2, rows, :])
        num = jnp.zeros((ch, LANES), F32)
        den = jnp.zeros((ch, LANES), F32)
        for g in range(len(DILATIONS)):
            w = jnp.exp(mb[g, rows, :] - m_all)
            num = num + nb[g, rows, :] * w
            den = den + sb[g, rows, :] * w
        o_ref[0, rows, :] = (num / den).astype(o_ref.dtype)
        return carry

    lax.fori_loop(0, tile // ch, merge, 0)


def _dilated(hb, tile=BAND_TILE):
    bsz, seq, _ = hb.shape
    ng = len(DILATIONS)
    nblocks = tile // BAND
    assert seq % tile == 0 and nblocks % BAND_GROUP == 0 and all(nblocks % d == 0 for d in DILATIONS)
    cur = lambda cb: pl.BlockSpec((1, tile, LANES), lambda b, p, j: (b, j, cb + p))
    prv = lambda cb: pl.BlockSpec((1, tile, LANES), lambda b, p, j: (b, jnp.maximum(j - 1, 0), cb + p))
    in_specs = [cur(CB_BQ + PAIRS * g) for g in range(ng)]
    for g in range(ng):
        in_specs += [cur(CB_BK + PAIRS * g), prv(CB_BK + PAIRS * g)]
    in_specs += [cur(CB_BV), prv(CB_BV)]
    acc = pltpu.VMEM((ng, tile, LANES), F32)
    return pl.pallas_call(
        functools.partial(_band_kernel, tile=tile),
        grid=(bsz, PAIRS, seq // tile),
        in_specs=in_specs,
        out_specs=pl.BlockSpec((1, tile, LANES), lambda b, p, j: (b, j, p)),
        out_shape=jax.ShapeDtypeStruct((bsz, seq, WIDTH), BF16),
        scratch_shapes=[acc, pltpu.VMEM((ng, 2 * tile, LANES), F32), pltpu.VMEM((2 * tile, LANES), F32), acc, acc, acc],
        compiler_params=_cparams(("parallel", "parallel", "arbitrary")),
        name="band",
    )(*([hb] * len(in_specs)))


def _dot01(a, b, nt=False, pieces=3):
    f = _dot_nt if nt else _dot
    if a.dtype == BF16:
        return sum(f(a, piece.astype(BF16)) for piece in _split3(b)[:pieces])
    return sum(f(piece.astype(BF16), b) for piece in _split3(a)[:pieces])


def _tri_inverse_all(ms):
    c = ms[0].shape[0]
    ri = _iota((c, c), 0)
    ci = _iota((c, c), 1)
    base = TRI_BASE
    inblk = ri // base == ci // base
    eye = jnp.where(ri == ci, 1.0, 0.0)
    ps = [jnp.where(inblk, -m, 0.0) for m in ms]
    ts = [eye + p for p in ps]
    for _ in range(3):
        pbs = [p.astype(BF16) for p in ps]
        ps = [_dot(pb, pb) for pb in pbs]
        ts = [t + _dot(t.astype(BF16), p.astype(BF16)) for t, p in zip(ts, ps)]
    size = base
    while size < c:
        lower = jnp.logical_and(ri // (2 * size) == ci // (2 * size), ri // size != ci // size)
        tbs = [t.astype(BF16) for t in ts]
        xs = [_dot(tb, jnp.where(lower, m, 0.0).astype(BF16)) for tb, m in zip(tbs, ms)]
        ts = [t - _dot(x.astype(BF16), tb) for t, x, tb in zip(ts, xs, tbs)]
        size *= 2
    return ts


def _gdn_kernel(h_ref, cw_ref, alog_ref, dtb_ref, nw_ref, o_ref,
                xe_ref, u_s, w_s, qd_s, kd_s, qk_s, egl_s, o_s, st_ref, *, tile):
    cc = GDN_CHUNK
    nc = tile // cc
    t = pl.program_id(1)

    @pl.when(t == 0)
    def _():
        xe_ref[0:8, :] = jnp.zeros((8, 3 * WIDTH), F32)
        st_ref[...] = jnp.zeros_like(st_ref)

    x = h_ref[0, :, LANES:]
    xe_ref[8:8 + tile, :] = x
    y = jnp.zeros((tile, 3 * WIDTH), F32)
    for j in range(CONV_K):
        y = y + cw_ref[j:j + 1, :] * xe_ref[8 - (CONV_K - 1) + j:8 - (CONV_K - 1) + j + tile, :]
    xe_ref[0:8, :] = x[tile - 8:tile, :]
    y = y * _sigmoid(y)

    bd = ((_iota((WIDTH, WIDTH), 0) // HEAD_DIM) == (_iota((WIDTH, WIDTH), 1) // HEAD_DIM)).astype(BF16)
    q = y[:, 0:WIDTH]
    k = y[:, WIDTH:2 * WIDTH]
    v = y[:, 2 * WIDTH:]
    q = q * lax.rsqrt(_dot01(q * q, bd, pieces=2) + RMS_EPS) * (HEAD_DIM ** -0.5)
    k = k * lax.rsqrt(_dot01(k * k, bd, pieces=2) + RMS_EPS)

    hs = h_ref[0, :, 0:LANES]
    er = _iota((LANES, WIDTH), 0)
    ec = _iota((LANES, WIDTH), 1) // HEAD_DIM
    beta = _sigmoid(_dot01(hs, (er == ec + HEADS).astype(BF16), pieces=2))
    g = -jnp.exp(alog_ref[...]) * jax.nn.softplus(_dot01(hs, (er == ec + 2 * HEADS).astype(BF16), pieces=2) + dtb_ref[...])

    tri = (_iota((cc, cc), 0) >= _iota((cc, cc), 1)).astype(BF16)
    gc = jnp.concatenate([_dot01(tri, g[c * cc:(c + 1) * cc]) for c in range(nc)], axis=0)
    glast = jnp.broadcast_to(gc.reshape(nc, cc, WIDTH)[:, cc - 1:cc, :], (nc, cc, WIDTH)).reshape(tile, WIDTH)
    eg = jnp.exp(gc)
    kb = k * beta
    vb = v * beta
    wb = kb * eg
    qd_s[...] = q * eg
    kd_s[...] = k * jnp.exp(glast - gc)
    egl_s[...] = jnp.exp(glast)

    ri = _iota((cc, cc), 0)
    ci = _iota((cc, cc), 1)
    incl = ri >= ci
    strict = ri > ci
    lane = _iota((cc, LANES), 1)
    hm = lane < HEAD_DIM
    sls = [slice(p * LANES, (p + 1) * LANES) for p in range(PAIRS)]
    keeps = [hm, jnp.logical_not(hm)]
    heads = [(p, h) for p in range(PAIRS) for h in range(2)]
    chains = [(p, h, c) for p, h in heads for c in range(nc)]
    rws = [slice(c * cc, (c + 1) * cc) for c in range(nc)]
    kps = [k[:, sl].astype(BF16) for sl in sls]
    rhss = [jnp.concatenate([vb[:, sl], wb[:, sl]], axis=1).astype(BF16) for sl in sls]
    gct = gc.T
    rowf = {(p, h): gct[p * LANES + h * HEAD_DIM:p * LANES + h * HEAD_DIM + 1, :] for p, h in heads}
    kbm = {(p, h): jnp.where(jnp.tile(keeps[h], (nc, 1)), kb[:, sls[p]], 0.0).astype(BF16) for p, h in heads}
    qm = {(p, h): jnp.where(jnp.tile(keeps[h], (nc, 1)), q[:, sls[p]], 0.0).astype(BF16) for p, h in heads}
    kks = [_dot_nt(kbm[p, h][rws[c]], kps[p][rws[c]]) for p, h, c in chains]
    qks = [_dot_nt(qm[p, h][rws[c]], kps[p][rws[c]]) for p, h, c in chains]
    mms = []
    for (p, h, c), kk, qk in zip(chains, kks, qks):
        col = p * LANES + h * HEAD_DIM
        diff = gc[rws[c], col:col + 1] - rowf[p, h][:, rws[c]]
        decay = jnp.where(incl, jnp.exp(jnp.where(incl, diff, 0.0)), 0.0)
        mms.append(jnp.where(strict, kk * decay, 0.0))
        qk_s[2 * p + h, rws[c], :] = qk * decay
    tinvs = _tri_inverse_all(mms)
    rs = {ch: _dot(tinv.astype(BF16), rhss[ch[0]][rws[ch[2]]]) for ch, tinv in zip(chains, tinvs)}
    for p in range(PAIRS):
        for c in range(nc):
            r0, r1 = rs[p, 0, c], rs[p, 1, c]
            u_s[rws[c], sls[p]] = jnp.where(hm, r0[:, :LANES], r1[:, :LANES])
            w_s[rws[c], sls[p]] = jnp.where(hm, r0[:, LANES:], r1[:, LANES:])

    bdiag = (_iota((LANES, LANES), 0) // HEAD_DIM) == (_iota((LANES, LANES), 1) // HEAD_DIM)
    for c in range(nc):
        rows = rws[c]
        sts = [st_ref[p] for p in range(PAIRS)]
        stbs = [st.astype(BF16) for st in sts]
        wss = [_dot(w_s[rows, sl].astype(BF16), stb) for sl, stb in zip(sls, stbs)]
        qss = [_dot(qd_s[rows, sl].astype(BF16), stb) for sl, stb in zip(sls, stbs)]
        vnbs = [(u_s[rows, sl] - ws).astype(BF16) for sl, ws in zip(sls, wss)]
        upds = [lax.dot_general(kd_s[rows, sl].astype(BF16), vnb, (((0,), (0,)), ((), ())), preferred_element_type=F32)
                for sl, vnb in zip(sls, vnbs)]
        intra = [(_dot(qk_s[2 * p, rows, :].astype(BF16), vnbs[p]), _dot(qk_s[2 * p + 1, rows, :].astype(BF16), vnbs[p]))
                 for p in range(PAIRS)]
        for p in range(PAIRS):
            st_ref[p] = sts[p] * egl_s[c * cc:c * cc + 1, sls[p]] + jnp.where(bdiag, upds[p], 0.0)
            o_s[rows, sls[p]] = qss[p] + jnp.where(hm, intra[p][0], intra[p][1])

    o = o_s[...]
    ms = _dot01(o * o, bd, pieces=2) * (1.0 / HEAD_DIM)
    o_ref[0] = (o * lax.rsqrt(ms + RMS_EPS) * nw_ref[...]).astype(o_ref.dtype)


def _gdn(hf, conv_w, a_log, dt_bias, norm_w, tile=GDN_ROWS):
    bsz, seq, _ = hf.shape
    rep = lambda a: jnp.repeat(a.astype(F32), HEAD_DIM)[None, :]
    wide = pltpu.VMEM((tile, WIDTH), F32)
    return pl.pallas_call(
        functools.partial(_gdn_kernel, tile=tile),
        grid=(bsz, seq // tile),
        in_specs=[
            pl.BlockSpec((1, tile, HF_COLS), lambda b, t: (b, t, 0)),
            pl.BlockSpec((CONV_K, 3 * WIDTH), lambda b, t: (0, 0)),
            pl.BlockSpec((1, WIDTH), lambda b, t: (0, 0)),
            pl.BlockSpec((1, WIDTH), lambda b, t: (0, 0)),
            pl.BlockSpec((1, WIDTH), lambda b, t: (0, 0)),
        ],
        out_specs=pl.BlockSpec((1, tile, WIDTH), lambda b, t: (b, t, 0)),
        out_shape=jax.ShapeDtypeStruct((bsz, seq, WIDTH), BF16),
        scratch_shapes=[pltpu.VMEM((tile + 8, 3 * WIDTH), F32), wide, wide, wide, wide,
                        pltpu.VMEM((HEADS, tile, LANES), F32), wide, wide,
                        pltpu.VMEM((PAIRS, LANES, LANES), F32)],
        compiler_params=_cparams(("parallel", "arbitrary")),
        name="gdn",
    )(hf, conv_w.astype(F32), rep(a_log), rep(dt_bias), jnp.tile(norm_w.astype(F32), HEADS)[None, :])


def _memattn_kernel(q_ref, kv_ref, o_ref):
    tq = q_ref.shape[1]
    q = q_ref[0]
    k = kv_ref[0, :, 0:WIDTH]
    v = kv_ref[0, :, WIDTH:2 * WIDTH]
    head = _iota((tq, WIDTH), 1) // MEM_HEAD_DIM
    zero = jnp.zeros_like(q)
    out = jnp.zeros((tq, WIDTH), F32)
    for h in range(MEM_HEADS):
        s = _dot_nt(jnp.where(head == h, q, zero), k)
        m = jnp.max(s, axis=-1, keepdims=True)
        p = jnp.exp(s - m)
        o = _dot(p.astype(BF16), v) / jnp.sum(p, axis=-1, keepdims=True)
        out = jnp.where(head == h, o, out)
    o_ref[0] = out.astype(o_ref.dtype)


def _memattn(he, kv, layer, tq=MEMATTN_ROWS):
    bsz, seq, _ = he.shape
    mlen = kv.shape[1]
    return pl.pallas_call(
        _memattn_kernel,
        grid=(bsz, seq // tq),
        in_specs=[
            pl.BlockSpec((1, tq, WIDTH), lambda b, i: (b, i, 0)),
            pl.BlockSpec((1, mlen, 2 * WIDTH), lambda b, i: (b, 0, layer)),
        ],
        out_specs=pl.BlockSpec((1, tq, WIDTH), lambda b, i: (b, i, 0)),
        out_shape=jax.ShapeDtypeStruct((bsz, seq, WIDTH), BF16),
        compiler_params=_cparams(("parallel", "parallel")),
        name="memattn",
    )(he, kv)


def _sigmoid(x):
    return 0.5 * jnp.tanh(0.5 * x) + 0.5


def _out_kernel(oa, ob, oc, od, oe, z_ref, ml_ref, x_ref, wb_ref, wo_ref, g_ref, b_ref, y_ref, yb_ref, *, alpha):
    tm = x_ref.shape[0]
    acc = jnp.zeros((tm, D_MODEL), F32)
    for n, o_ref in enumerate((oa, ob, oc, od, oe)):
        z = z_ref[:, n * WIDTH:(n + 1) * WIDTH]
        gated = o_ref[...] * (z * _sigmoid(z))
        yn = _dot(gated, wb_ref[n])
        acc = acc + _sigmoid(ml_ref[:, n * D_MODEL:(n + 1) * D_MODEL]).astype(F32) * yn
    r = alpha * x_ref[...] + _dot(acc.astype(BF16), wo_ref[...])
    mu = jnp.mean(r, axis=-1, keepdims=True)
    rc = r - mu
    var = jnp.mean(rc * rc, axis=-1, keepdims=True)
    y = rc * lax.rsqrt(var + LN_EPS) * g_ref[...] + b_ref[...]
    y_ref[...] = y
    yb_ref[...] = y.astype(BF16)


def _out(branches, z, ml, x, w_branch, w_out, ln_g, ln_b, alpha, tm=OUT_ROWS):
    m = x.shape[0]
    row = lambda n: pl.BlockSpec((tm, n), lambda i: (i, 0))
    return pl.pallas_call(
        functools.partial(_out_kernel, alpha=alpha),
        grid=(m // tm,),
        in_specs=[row(WIDTH)] * N_BRANCH + [row(N_BRANCH * WIDTH), row(N_BRANCH * D_MODEL), row(D_MODEL),
                  pl.BlockSpec((N_BRANCH, WIDTH, D_MODEL), lambda i: (0, 0, 0)),
                  pl.BlockSpec((D_MODEL, D_MODEL), lambda i: (0, 0)),
                  pl.BlockSpec((1, D_MODEL), lambda i: (0, 0)),
                  pl.BlockSpec((1, D_MODEL), lambda i: (0, 0))],
        out_specs=[row(D_MODEL), row(D_MODEL)],
        out_shape=[jax.ShapeDtypeStruct((m, D_MODEL), F32), jax.ShapeDtypeStruct((m, D_MODEL), BF16)],
        compiler_params=_cparams(("parallel",)),
        name="out",
    )(*branches, z, ml, x, w_branch, w_out, ln_g, ln_b)


def _split_weights(w_in, b_in):
    scale = np.ones((HB_COLS,), np.float32)
    for cb in (CB_AQ, CB_CQ):
        scale[cb * LANES:cb * LANES + WIDTH] = HEAD_DIM ** -0.5 * LOG2E
    scale[CB_BQ * LANES:CB_BQ * LANES + 3 * WIDTH] = HEAD_DIM ** -0.5
    sc = np.concatenate([np.arange(o, o + HEADS) for o in (O_CF, O_DBETA, O_DDECAY)])
    pad = jnp.zeros(w_in.shape[:2] + (LANES - 3 * HEADS,), w_in.dtype)
    w_hf = jnp.concatenate([w_in[..., sc], pad, w_in[..., O_D:O_D + 3 * WIDTH]], axis=-1)
    b_hf = jnp.concatenate([b_in[..., sc], pad[:, 0], b_in[..., O_D:O_D + 3 * WIDTH]], axis=-1)
    span = lambda a, o, n: a[..., o:o + n]
    groups = {
        "hb": (span(w_in, 0, HB_COLS), span(b_in, 0, HB_COLS), scale),
        "he": (span(w_in, O_E, WIDTH), span(b_in, O_E, WIDTH), np.full((WIDTH,), MEM_HEAD_DIM ** -0.5, np.float32)),
        "hf": (w_hf, b_hf, np.ones((HF_COLS,), np.float32)),
        "z": (span(w_in, O_Z, N_BRANCH * WIDTH), span(b_in, O_Z, N_BRANCH * WIDTH), np.ones((N_BRANCH * WIDTH,), np.float32)),
        "ml": (span(w_in, O_MERGE, N_BRANCH * D_MODEL), span(b_in, O_MERGE, N_BRANCH * D_MODEL),
               np.ones((N_BRANCH * D_MODEL,), np.float32)),
    }
    return {k: (w.astype(F32), b.astype(F32)[:, None, :], jnp.asarray(s)[None, :]) for k, (w, b, s) in groups.items()}


def _layer(x, xb, kv, layer, gw, conv_w, a_log, dt_bias, gdn_norm_w, w_branch, w_out, ln_g, ln_b, alpha):
    bsz, seq, d = x.shape
    m = bsz * seq
    xf = x.reshape(m, d)
    proj = lambda name, dtype: _proj(xb, *gw[name], layer, dtype, PROJ_ROWS, PROJ_COLS[name])
    hb = proj("hb", BF16).reshape(bsz, seq, HB_COLS)
    he = proj("he", BF16).reshape(bsz, seq, WIDTH)
    hf = proj("hf", F32).reshape(bsz, seq, HF_COLS)
    z = proj("z", BF16)
    ml = proj("ml", BF16)

    o_a = _moba(hb)
    o_b = _dilated(hb)
    o_c = _fox(hb, _fcum(hf))
    o_d = _gdn(hf, conv_w, a_log, dt_bias, gdn_norm_w)
    o_e = _memattn(he, kv, layer)
    branches = [o.reshape(m, WIDTH) for o in (o_a, o_b, o_c, o_d, o_e)]
    y, yb = _out(branches, z, ml, xf, w_branch.astype(BF16), w_out.astype(BF16),
                 ln_g.astype(F32)[None, :], ln_b.astype(F32)[None, :], alpha)
    return y.reshape(bsz, seq, d), yb


def kernel(x, mem, mem_ln_g, mem_ln_b, w_in, b_in, conv_w, a_log, dt_bias, gdn_norm_w, w_mem_kv, w_branch, w_out, ln_g, ln_b):
    depth = w_in.shape[0]
    alpha = float((2 * depth) ** 0.25)
    w_kv = jnp.concatenate([w_mem_kv[l] for l in range(depth)], axis=1).astype(BF16)
    kv = _memkv(mem.astype(F32), mem_ln_g.astype(F32)[None, :], mem_ln_b.astype(F32)[None, :], w_kv)
    x = x.astype(F32)
    xb = x.reshape(-1, x.shape[-1]).astype(BF16)
    gw = _split_weights(w_in, b_in)
    for l in range(depth):
        x, xb = _layer(x, xb, kv, l, gw, conv_w[l], a_log[l], dt_bias[l], gdn_norm_w[l],
                       w_branch[l], w_out[l], ln_g[l], ln_b[l], alpha)
    return x
```

```python
import functools

import jax
import jax.numpy as jnp
import numpy as np
from jax import lax
from jax.experimental import pallas as pl
from jax.experimental.pallas import tpu as pltpu

F32 = jnp.float32
BF16 = jnp.bfloat16
HI = lax.Precision.HIGHEST

D_MODEL = 1024
HEAD_DIM = 64
HEADS = 6
WIDTH = HEADS * HEAD_DIM
N_BRANCH = 5
LANES = 128
PAIRS = WIDTH // LANES
MOBA_BLOCK = 256
MOBA_TOPK = 3
DILATIONS = (1, 4, 16)
BAND = 128
BAND_TILE = 2048
BAND_GROUP = 16
GDN_CHUNK = 128
CONV_K = 4
MEM_HEADS = 4
MEM_HEAD_DIM = WIDTH // MEM_HEADS
NEG = -1e30
LOG2E = float(np.log2(np.e))
FLASH_UNDERFLOW = 160.0
LN_EPS = 1e-5
RMS_EPS = 1e-6

_SPLIT = (3 * WIDTH, 6 * WIDTH, WIDTH, 3 * WIDTH, HEADS, 3 * WIDTH, HEADS, HEADS, WIDTH, N_BRANCH * WIDTH, N_BRANCH * D_MODEL)
_OFF = tuple(int(v) for v in np.concatenate([[0], np.cumsum(_SPLIT)]))
(O_A, O_BQK, O_BV, O_C, O_CF, O_D, O_DBETA, O_DDECAY, O_E, O_Z, O_MERGE, _) = _OFF

CB_AQ, CB_AK, CB_AV = 0, 3, 6
CB_BQ, CB_BK, CB_BV = 9, 18, 27
CB_CQ, CB_CK, CB_CV = 30, 33, 36
HB_COLS = 39 * LANES
assert (O_A, O_BQK, O_BV, O_C, O_CF) == tuple(LANES * c for c in (CB_AQ, CB_BQ, CB_BV, CB_CQ, 39))
HF_COLS = LANES + 3 * WIDTH

VMEM_LIMIT = 56 * 1024 * 1024

PROJ_ROWS = 2048
PROJ_COLS = {"hb": HB_COLS // 3, "he": WIDTH, "hf": HF_COLS, "z": N_BRANCH * WIDTH, "ml": 1024}
FOX_ROWS = 256
MOBA_ROWS = 2 * MOBA_BLOCK
FLASH_BUILD_ROWS = 512
BAND_MERGE_ROWS = 256
GDN_ROWS = 4 * GDN_CHUNK
MEMATTN_ROWS = 512
OUT_ROWS = 512
TRI_BASE = 16
BOUND_REL, BOUND_ABS = 1.01, 1.0
FAR = 1e9
TAKEN = -3e38


def _cparams(sem):
    return pltpu.CompilerParams(dimension_semantics=sem, vmem_limit_bytes=VMEM_LIMIT)


def _dot(a, b):
    return jnp.dot(a, b, preferred_element_type=F32)


def _dot_nt(a, b):
    return lax.dot_general(a, b, (((1,), (1,)), ((), ())), preferred_element_type=F32)


def _dot_nt_hi(a, b):
    return lax.dot_general(a, b, (((1,), (1,)), ((), ())), preferred_element_type=F32, precision=HI)


def _iota(shape, dim):
    return lax.broadcasted_iota(jnp.int32, shape, dim)


def _proj_kernel(x_ref, w_ref, b_ref, s_ref, o_ref, wb_ref):
    @pl.when(pl.program_id(1) == 0)
    def _():
        wb_ref[...] = w_ref[...].astype(BF16)

    acc = _dot(x_ref[...], wb_ref[...])
    o_ref[...] = ((acc + b_ref[...]) * s_ref[...]).astype(o_ref.dtype)


def _proj(xb, w, b, scale, layer, out_dtype, tm, tn):
    m, k = xb.shape
    n = b.shape[-1]
    assert m % tm == 0 and n % tn == 0
    return pl.pallas_call(
        _proj_kernel,
        grid=(n // tn, m // tm),
        in_specs=[
            pl.BlockSpec((tm, k), lambda j, i: (i, 0)),
            pl.BlockSpec((None, k, tn), lambda j, i: (layer, 0, j)),
            pl.BlockSpec((None, 1, tn), lambda j, i: (layer, 0, j)),
            pl.BlockSpec((1, tn), lambda j, i: (0, j)),
        ],
        out_specs=pl.BlockSpec((tm, tn), lambda j, i: (i, j)),
        out_shape=jax.ShapeDtypeStruct((m, n), out_dtype),
        scratch_shapes=[pltpu.VMEM((k, tn), BF16)],
        compiler_params=_cparams(("parallel", "arbitrary")),
        name="proj",
    )(xb, w, b, scale)


def _memkv_kernel(mem_ref, g_ref, b_ref, w_ref, o_ref):
    x = mem_ref[0]
    mu = jnp.mean(x, axis=-1, keepdims=True)
    xc = x - mu
    var = jnp.mean(xc * xc, axis=-1, keepdims=True)
    y = xc * lax.rsqrt(var + LN_EPS) * g_ref[...] + b_ref[...]
    o_ref[0] = _dot(y.astype(BF16), w_ref[...]).astype(o_ref.dtype)


def _memkv(mem, g, b, w):
    bsz, mlen, d = mem.shape
    n = w.shape[1]
    return pl.pallas_call(
        _memkv_kernel,
        grid=(bsz,),
        in_specs=[
            pl.BlockSpec((1, mlen, d), lambda i: (i, 0, 0)),
            pl.BlockSpec((1, d), lambda i: (0, 0)),
            pl.BlockSpec((1, d), lambda i: (0, 0)),
            pl.BlockSpec((d, n), lambda i: (0, 0)),
        ],
        out_specs=pl.BlockSpec((1, mlen, n), lambda i: (i, 0, 0)),
        out_shape=jax.ShapeDtypeStruct((bsz, mlen, n), BF16),
        compiler_params=_cparams(("parallel",)),
        name="memkv",
    )(mem, g, b, w)


def _fcum_kernel(h_ref, o_ref, *, blk):
    seq = h_ref.shape[1]
    tri = (_iota((blk, blk), 0) >= _iota((blk, blk), 1)).astype(BF16)

    def body(i, carry):
        rows = pl.ds(pl.multiple_of(i * blk, blk), blk)
        logf = jax.nn.log_sigmoid(h_ref[0, rows, :])
        c = _dot01(tri, logf) + carry
        o_ref[0, rows, :] = c * LOG2E
        return c[blk - 1:blk, :]

    lax.fori_loop(0, seq // blk, body, jnp.zeros((1, LANES), F32))


def _fcum(hf):
    bsz, seq, _ = hf.shape
    return pl.pallas_call(
        functools.partial(_fcum_kernel, blk=FLASH_BUILD_ROWS),
        grid=(bsz,),
        in_specs=[pl.BlockSpec((1, seq, LANES), lambda b: (b, 0, 0))],
        out_specs=pl.BlockSpec((1, seq, LANES), lambda b: (b, 0, 0)),
        out_shape=jax.ShapeDtypeStruct((bsz, seq, LANES), F32),
        compiler_params=_cparams(("parallel",)),
        name="fcum",
    )(hf)


def _head_masks(rows):
    lane = _iota((rows, LANES), 1)
    return lane < HEAD_DIM


def _split3(x):
    hi = x.astype(BF16).astype(F32)
    r = x - hi
    mid = r.astype(BF16).astype(F32)
    return hi, mid, r - mid


def _flash_scratch(tq):
    return [pltpu.VMEM((2 * tq, LANES), F32), pltpu.VMEM((2 * tq, tq), F32), pltpu.VMEM((2 * tq, tq), F32),
            pltpu.VMEM((2 * tq, LANES), F32), pltpu.VMEM((2 * tq, LANES), F32)]


def _augment_values(vx_ref, rows, v):
    vx_ref[rows, 0:LANES] = v
    vx_ref[rows, LANES:] = jnp.ones((v.shape[0], LANES), BF16)


def _flash_causal(qx, kx_ref, vx_ref, scratch, i, tq, past_tiles=None):
    acc_ref, sa_ref, sb_ref, m_ref, l_ref = scratch

    def logits(kt, dst):
        rows = pl.ds(pl.multiple_of(kt * tq, tq), tq)
        dst[...] = _dot_nt(qx, kx_ref[rows, :])

    r = _iota((2 * tq, tq), 0)
    causal = jnp.where(r >= tq, r - tq, r) >= _iota((2 * tq, tq), 1)

    def finish(src, kt, masked=False):
        rows = pl.ds(pl.multiple_of(kt * tq, tq), tq)
        s = src[...]
        if masked:
            s = jnp.where(causal, s, NEG)
        m = m_ref[...]
        m_new = jnp.maximum(m, jnp.max(s, axis=-1, keepdims=True))
        alpha = jnp.exp2(m - m_new)
        p = jnp.exp2((s - jnp.tile(m_new, (1, tq // LANES))).astype(BF16))
        pv = _dot(p, vx_ref[rows, :])
        l_ref[...] = alpha * l_ref[...] + pv[:, LANES:]
        m_ref[...] = m_new
        acc_ref[...] = alpha * acc_ref[...] + pv[:, 0:LANES]

    acc_ref[...] = jnp.zeros_like(acc_ref)
    m_ref[...] = jnp.full((2 * tq, LANES), NEG, F32)
    l_ref[...] = jnp.zeros((2 * tq, LANES), F32)
    tile = lambda u: jnp.maximum(i - 1 - u, 0)
    logits(i, sa_ref)
    logits(tile(0), sb_ref)
    finish(sa_ref, i, masked=True)
    n = i if past_tiles is None else past_tiles(m_ref)

    def body(v, carry):
        logits(tile(2 * v + 1), sa_ref)
        finish(sb_ref, tile(2 * v))
        logits(tile(2 * v + 2), sb_ref)
        finish(sa_ref, tile(2 * v + 1))
        return carry

    lax.fori_loop(0, n // 2, body, 0)

    @pl.when(n % 2 == 1)
    def _():
        finish(sb_ref, tile(n - 1))

    o = acc_ref[...] / l_ref[...]
    return jnp.where(_head_masks(tq), o[0:tq], o[tq:])


def _head_sumsq(x):
    hm = _head_masks(x.shape[0])
    sq = x * x
    return (jnp.sum(jnp.where(hm, sq, 0.0), axis=1, keepdims=True), jnp.sum(jnp.where(hm, 0.0, sq), axis=1, keepdims=True))


def _fox_kernel(q_ref, k_ref, v_ref, f_ref, o_ref, kx_ref, vx_ref, kn_ref, *flash, tq):
    p = pl.program_id(1)
    i = pl.program_id(2)
    seq = k_ref.shape[1]
    bt = FLASH_BUILD_ROWS

    @pl.when(i == 0)
    def _():
        src = _iota((LANES, LANES), 0)
        dst = _iota((LANES, LANES), 1)
        pm = jnp.logical_or(jnp.logical_and(src == 2 * p, dst < 3),
                            jnp.logical_and(src == 2 * p + 1, jnp.logical_and(dst >= 3, dst < 6))).astype(BF16)
        sub = _iota((bt, LANES), 1) % 3

        def build(c, carry):
            rows = pl.ds(pl.multiple_of(c * bt, bt), bt)
            hi, mid, lo = _split3(_dot01(f_ref[0, rows, :], pm))
            k = k_ref[0, rows, :]
            kx_ref[rows, 0:LANES] = k
            kx_ref[rows, LANES:] = jnp.where(sub == 0, hi, jnp.where(sub == 1, mid, lo)).astype(BF16)
            _augment_values(vx_ref, rows, v_ref[0, rows, :])
            return tuple(jnp.maximum(c0, jnp.max(s, axis=0, keepdims=True))
                         for c0, s in zip(carry, _head_sumsq(k.astype(F32))))

        zero = jnp.zeros((1, 1), F32)
        kn = lax.fori_loop(0, seq // bt, build, (zero, zero))
        kn_ref[0:1, :] = jnp.broadcast_to(kn[0], (1, LANES))
        kn_ref[1:2, :] = jnp.broadcast_to(kn[1], (1, LANES))

    q = q_ref[0].astype(F32)
    lane = _iota((tq, LANES), 1)
    hm = lane < HEAD_DIM
    top = jnp.concatenate([jnp.where(hm, q, 0.0), jnp.where(lane < 3, -1.0, 0.0)], axis=1)
    bot = jnp.concatenate([jnp.where(hm, 0.0, q), jnp.where(jnp.logical_and(lane >= 3, lane < 6), -1.0, 0.0)], axis=1)
    qx = jnp.concatenate([top, bot], axis=0).astype(BF16)

    def past_tiles(m_ref):
        nt = seq // tq
        fend = f_ref[0, pl.ds(tq - 1, nt, stride=tq), :]
        lane_t = _iota((nt, LANES), 1)
        tpos = _iota((nt, 1), 0)
        m = m_ref[...]
        need = tpos < 0
        for h, qn2 in enumerate(_head_sumsq(q)):
            m_min = jnp.min(m[h * tq:(h + 1) * tq, 0:1], axis=0, keepdims=True)
            qk = jnp.sqrt(jnp.max(qn2, axis=0, keepdims=True) * kn_ref[h:h + 1, 0:1]) * BOUND_REL + BOUND_ABS
            f_h = jnp.sum(jnp.where(lane_t == 2 * p + h, fend, 0.0), axis=1, keepdims=True)
            need = jnp.logical_or(need, qk - f_h > m_min - FLASH_UNDERFLOW)
        first = jnp.min(jnp.where(jnp.logical_and(need, tpos < i), tpos, i).astype(F32))
        return i - first.astype(jnp.int32)

    o_ref[0] = _flash_causal(qx, kx_ref, vx_ref, flash, i, tq, past_tiles).astype(o_ref.dtype)


def _fox(hb, fcol, tq=FOX_ROWS):
    bsz, seq, _ = hb.shape
    assert seq % tq == 0
    return pl.pallas_call(
        functools.partial(_fox_kernel, tq=tq),
        grid=(bsz, PAIRS, seq // tq),
        in_specs=[
            pl.BlockSpec((1, tq, LANES), lambda b, p, i: (b, i, CB_CQ + p)),
            pl.BlockSpec((1, seq, LANES), lambda b, p, i: (b, 0, CB_CK + p)),
            pl.BlockSpec((1, seq, LANES), lambda b, p, i: (b, 0, CB_CV + p)),
            pl.BlockSpec((1, seq, LANES), lambda b, p, i: (b, 0, 0)),
        ],
        out_specs=pl.BlockSpec((1, tq, LANES), lambda b, p, i: (b, i, p)),
        out_shape=jax.ShapeDtypeStruct((bsz, seq, WIDTH), BF16),
        scratch_shapes=[pltpu.VMEM((seq, 2 * LANES), BF16), pltpu.VMEM((seq, 2 * LANES), BF16),
                        pltpu.VMEM((8, LANES), F32)] + _flash_scratch(tq),
        compiler_params=_cparams(("parallel", "parallel", "arbitrary")),
        name="fox",
    )(hb, hb, hb, fcol)


def _moba_kernel(q_ref, k_ref, v_ref, o_ref, kx_ref, vx_ref, kmean_ref, *flash, tq):
    i = pl.program_id(2)
    seq = k_ref.shape[1]
    blk = MOBA_BLOCK

    @pl.when(i == 0)
    def _():
        kmean_ref[...] = jnp.zeros_like(kmean_ref)
        lane = _iota((blk, LANES), 1)

        def build(n, carry):
            rows = pl.ds(pl.multiple_of(n * blk, blk), blk)
            k = k_ref[0, rows, :]
            kx_ref[rows, 0:LANES] = k
            kx_ref[rows, LANES:] = jnp.where(lane == n, 1.0, 0.0).astype(BF16)
            _augment_values(vx_ref, rows, v_ref[0, rows, :])
            kmean_ref[pl.ds(n, 1), :] = jnp.sum(k.astype(F32), axis=0, keepdims=True) * (1.0 / blk)
            return carry

        lax.fori_loop(0, seq // blk, build, 0)

    q = q_ref[0].astype(F32)
    hm = _head_masks(tq)
    nbp = -(-(seq // blk) // 8) * 8
    blk_id = _iota((nbp, tq), 0)
    blk_f = blk_id.astype(F32)
    own = i * (tq // blk) + _iota((nbp, tq), 1) // blk
    valid = blk_id < own
    kmean = kmean_ref[0:nbp, :]
    halves = []
    for h in range(2):
        qh = jnp.where(hm, q, 0.0) if h == 0 else jnp.where(hm, 0.0, q)
        g = jnp.where(valid, _dot_nt_hi(kmean, qh), NEG)
        bias = jnp.where(blk_id == own, 0.0, NEG)
        for _ in range(MOBA_TOPK):
            mx = jnp.max(g, axis=0, keepdims=True)
            first = jnp.min(jnp.where(g == mx, blk_f, FAR), axis=0, keepdims=True)
            pick = blk_f == first
            bias = jnp.where(jnp.logical_and(pick, valid), 0.0, bias)
            g = jnp.where(pick, TAKEN, g)
        bias = jnp.concatenate([bias, jnp.zeros((LANES - nbp, tq), F32)], axis=0).T
        halves.append(jnp.concatenate([qh, bias], axis=1))
    qx = jnp.concatenate(halves, axis=0).astype(BF16)
    o_ref[0] = _flash_causal(qx, kx_ref, vx_ref, flash, i, tq).astype(o_ref.dtype)


def _moba(hb, tq=MOBA_ROWS):
    bsz, seq, _ = hb.shape
    assert seq % tq == 0 and tq % MOBA_BLOCK == 0 and seq // MOBA_BLOCK <= LANES
    return pl.pallas_call(
        functools.partial(_moba_kernel, tq=tq),
        grid=(bsz, PAIRS, seq // tq),
        in_specs=[
            pl.BlockSpec((1, tq, LANES), lambda b, p, i: (b, i, CB_AQ + p)),
            pl.BlockSpec((1, seq, LANES), lambda b, p, i: (b, 0, CB_AK + p)),
            pl.BlockSpec((1, seq, LANES), lambda b, p, i: (b, 0, CB_AV + p)),
        ],
        out_specs=pl.BlockSpec((1, tq, LANES), lambda b, p, i: (b, i, p)),
        out_shape=jax.ShapeDtypeStruct((bsz, seq, WIDTH), BF16),
        scratch_shapes=[pltpu.VMEM((seq, 2 * LANES), BF16), pltpu.VMEM((seq, 2 * LANES), BF16),
                        pltpu.VMEM((LANES, LANES), F32)] + _flash_scratch(tq),
        compiler_params=_cparams(("parallel", "parallel", "arbitrary")),
        name="moba",
    )(hb, hb, hb)


def _band_kernel(q0, q1, q2, k0c, k0p, k1c, k1p, k2c, k2p, vc, vp, o_ref, qb, kb, vb, nb, mb, sb, *, tile):
    jt = pl.program_id(2)
    for g, (qr, kc, kp) in enumerate(((q0, k0c, k0p), (q1, k1c, k1p), (q2, k2c, k2p))):
        qb[g] = qr[0].astype(F32)
        kb[g, 0:tile, :] = kp[0].astype(F32)
        kb[g, tile:, :] = kc[0].astype(F32)
    vb[0:tile, :] = vp[0].astype(F32)
    vb[tile:, :] = vc[0].astype(F32)

    hm = _head_masks(BAND)
    qi = _iota((BAND, 2 * BAND), 0)
    kj = _iota((BAND, 2 * BAND), 1)
    dist = BAND + qi - kj
    band = jnp.logical_and(dist >= 0, dist <= BAND)
    cur_half = kj >= BAND
    nblocks = tile // BAND

    for g, dil in enumerate(DILATIONS):
        per_stream = nblocks // dil

        def step(it, carry, g=g, dil=dil, per_stream=per_stream):
            starts, kstarts, valids = [], [], []
            for j in range(BAND_GROUP):
                idx = it * BAND_GROUP + j
                n = idx % per_stream
                start = idx // per_stream + n * (BAND * dil)
                starts.append(start)
                kstarts.append(tile + start - BAND * dil)
                valids.append(jnp.logical_and(band, jnp.logical_or(cur_half, jnp.logical_or(jt > 0, n > 0))))
            qs = [qb[g, pl.ds(s, BAND, stride=dil), :] for s in starts]
            kks = [kb[g, pl.ds(s, 2 * BAND, stride=dil), :].astype(BF16) for s in kstarts]
            ones = jnp.ones((2 * BAND, LANES), BF16)
            vvs = [jnp.concatenate([vb[pl.ds(s, 2 * BAND, stride=dil), :].astype(BF16), ones], axis=1) for s in kstarts]
            logits = []
            for q, kk, valid in zip(qs, kks, valids):
                for h in range(2):
                    qh = (jnp.where(hm, q, 0.0) if h == 0 else jnp.where(hm, 0.0, q)).astype(BF16)
                    logits.append(jnp.where(valid, _dot_nt(qh, kk), NEG))
            ms = [jnp.max(s, axis=-1, keepdims=True) for s in logits]
            ps = [jnp.exp((s - m).astype(BF16)) for s, m in zip(logits, ms)]
            pvs = [_dot(p, vvs[i // 2]) for i, p in enumerate(ps)]
            nums = [pv[:, 0:LANES] for pv in pvs]
            ss = [pv[:, LANES:] for pv in pvs]
            for j, s in enumerate(starts):
                rows = pl.ds(s, BAND, stride=dil)
                nb[g, rows, :] = jnp.where(hm, nums[2 * j], nums[2 * j + 1])
                mb[g, rows, :] = jnp.where(hm, ms[2 * j], ms[2 * j + 1])
                sb[g, rows, :] = jnp.where(hm, ss[2 * j], ss[2 * j + 1])
            return carry

        lax.fori_loop(0, nblocks // BAND_GROUP, step, 0)

    ch = BAND_MERGE_ROWS

    def merge(i, carry):
        rows = pl.ds(pl.multiple_of(i * ch, ch), ch)
        m_all = jnp.maximum(jnp.maximum(mb[0, rows, :], mb[1, rows, :]), mb[2, rows, :])
        num = jnp.zeros((ch, LANES), F32)
        den = jnp.zeros((ch, LANES), F32)
        for g in range(len(DILATIONS)):
            w = jnp.exp(mb[g, rows, :] - m_all)
            num = num + nb[g, rows, :] * w
            den = den + sb[g, rows, :] * w
        o_ref[0, rows, :] = (num / den).astype(o_ref.dtype)
        return carry

    lax.fori_loop(0, tile // ch, merge, 0)


def _dilated(hb, tile=BAND_TILE):
    bsz, seq, _ = hb.shape
    ng = len(DILATIONS)
    nblocks = tile // BAND
    assert seq % tile == 0 and nblocks % BAND_GROUP == 0 and all(nblocks % d == 0 for d in DILATIONS)
    cur = lambda cb: pl.BlockSpec((1, tile, LANES), lambda b, p, j: (b, j, cb + p))
    prv = lambda cb: pl.BlockSpec((1, tile, LANES), lambda b, p, j: (b, jnp.maximum(j - 1, 0), cb + p))
    in_specs = [cur(CB_BQ + PAIRS * g) for g in range(ng)]
    for g in range(ng):
        in_specs += [cur(CB_BK + PAIRS * g), prv(CB_BK + PAIRS * g)]
    in_specs += [cur(CB_BV), prv(CB_BV)]
    acc = pltpu.VMEM((ng, tile, LANES), F32)
    return pl.pallas_call(
        functools.partial(_band_kernel, tile=tile),
        grid=(bsz, PAIRS, seq // tile),
        in_specs=in_specs,
        out_specs=pl.BlockSpec((1, tile, LANES), lambda b, p, j: (b, j, p)),
        out_shape=jax.ShapeDtypeStruct((bsz, seq, WIDTH), BF16),
        scratch_shapes=[acc, pltpu.VMEM((ng, 2 * tile, LANES), F32), pltpu.VMEM((2 * tile, LANES), F32), acc, acc, acc],
        compiler_params=_cparams(("parallel", "parallel", "arbitrary")),
        name="band",
    )(*([hb] * len(in_specs)))


def _dot01(a, b, nt=False, pieces=3):
    f = _dot_nt if nt else _dot
    if a.dtype == BF16:
        return sum(f(a, piece.astype(BF16)) for piece in _split3(b)[:pieces])
    return sum(f(piece.astype(BF16), b) for piece in _split3(a)[:pieces])


def _tri_inverse_all(ms):
    c = ms[0].shape[0]
    ri = _iota((c, c), 0)
    ci = _iota((c, c), 1)
    base = TRI_BASE
    inblk = ri // base == ci // base
    eye = jnp.where(ri == ci, 1.0, 0.0)
    ps = [jnp.where(inblk, -m, 0.0) for m in ms]
    ts = [eye + p for p in ps]
    for _ in range(3):
        pbs = [p.astype(BF16) for p in ps]
        ps = [_dot(pb, pb) for pb in pbs]
        ts = [t + _dot(t.astype(BF16), p.astype(BF16)) for t, p in zip(ts, ps)]
    size = base
    while size < c:
        lower = jnp.logical_and(ri // (2 * size) == ci // (2 * size), ri // size != ci // size)
        tbs = [t.astype(BF16) for t in ts]
        xs = [_dot(tb, jnp.where(lower, m, 0.0).astype(BF16)) for tb, m in zip(tbs, ms)]
        ts = [t - _dot(x.astype(BF16), tb) for t, x, tb in zip(ts, xs, tbs)]
        size *= 2
    return ts


def _gdn_kernel(h_ref, cw_ref, alog_ref, dtb_ref, nw_ref, o_ref,
                xe_ref, u_s, w_s, qd_s, kd_s, qk_s, egl_s, o_s, st_ref, *, tile):
    cc = GDN_CHUNK
    nc = tile // cc
    t = pl.program_id(1)

    @pl.when(t == 0)
    def _():
        xe_ref[0:8, :] = jnp.zeros((8, 3 * WIDTH), F32)
        st_ref[...] = jnp.zeros_like(st_ref)

    x = h_ref[0, :, LANES:]
    xe_ref[8:8 + tile, :] = x
    y = jnp.zeros((tile, 3 * WIDTH), F32)
    for j in range(CONV_K):
        y = y + cw_ref[j:j + 1, :] * xe_ref[8 - (CONV_K - 1) + j:8 - (CONV_K - 1) + j + tile, :]
    xe_ref[0:8, :] = x[tile - 8:tile, :]
    y = y * _sigmoid(y)

    bd = ((_iota((WIDTH, WIDTH), 0) // HEAD_DIM) == (_iota((WIDTH, WIDTH), 1) // HEAD_DIM)).astype(BF16)
    q = y[:, 0:WIDTH]
    k = y[:, WIDTH:2 * WIDTH]
    v = y[:, 2 * WIDTH:]
    q = q * lax.rsqrt(_dot01(q * q, bd, pieces=2) + RMS_EPS) * (HEAD_DIM ** -0.5)
    k = k * lax.rsqrt(_dot01(k * k, bd, pieces=2) + RMS_EPS)

    hs = h_ref[0, :, 0:LANES]
    er = _iota((LANES, WIDTH), 0)
    ec = _iota((LANES, WIDTH), 1) // HEAD_DIM
    beta = _sigmoid(_dot01(hs, (er == ec + HEADS).astype(BF16), pieces=2))
    g = -jnp.exp(alog_ref[...]) * jax.nn.softplus(_dot01(hs, (er == ec + 2 * HEADS).astype(BF16), pieces=2) + dtb_ref[...])

    tri = (_iota((cc, cc), 0) >= _iota((cc, cc), 1)).astype(BF16)
    gc = jnp.concatenate([_dot01(tri, g[c * cc:(c + 1) * cc]) for c in range(nc)], axis=0)
    glast = jnp.broadcast_to(gc.reshape(nc, cc, WIDTH)[:, cc - 1:cc, :], (nc, cc, WIDTH)).reshape(tile, WIDTH)
    eg = jnp.exp(gc)
    kb = k * beta
    vb = v * beta
    wb = kb * eg
    qd_s[...] = q * eg
    kd_s[...] = k * jnp.exp(glast - gc)
    egl_s[...] = jnp.exp(glast)

    ri = _iota((cc, cc), 0)
    ci = _iota((cc, cc), 1)
    incl = ri >= ci
    strict = ri > ci
    lane = _iota((cc, LANES), 1)
    hm = lane < HEAD_DIM
    sls = [slice(p * LANES, (p + 1) * LANES) for p in range(PAIRS)]
    keeps = [hm, jnp.logical_not(hm)]
    heads = [(p, h) for p in range(PAIRS) for h in range(2)]
    chains = [(p, h, c) for p, h in heads for c in range(nc)]
    rws = [slice(c * cc, (c + 1) * cc) for c in range(nc)]
    kps = [k[:, sl].astype(BF16) for sl in sls]
    rhss = [jnp.concatenate([vb[:, sl], wb[:, sl]], axis=1).astype(BF16) for sl in sls]
    gct = gc.T
    rowf = {(p, h): gct[p * LANES + h * HEAD_DIM:p * LANES + h * HEAD_DIM + 1, :] for p, h in heads}
    kbm = {(p, h): jnp.where(jnp.tile(keeps[h], (nc, 1)), kb[:, sls[p]], 0.0).astype(BF16) for p, h in heads}
    qm = {(p, h): jnp.where(jnp.tile(keeps[h], (nc, 1)), q[:, sls[p]], 0.0).astype(BF16) for p, h in heads}
    kks = [_dot_nt(kbm[p, h][rws[c]], kps[p][rws[c]]) for p, h, c in chains]
    qks = [_dot_nt(qm[p, h][rws[c]], kps[p][rws[c]]) for p, h, c in chains]
    mms = []
    for (p, h, c), kk, qk in zip(chains, kks, qks):
        col = p * LANES + h * HEAD_DIM
        diff = gc[rws[c], col:col + 1] - rowf[p, h][:, rws[c]]
        decay = jnp.where(incl, jnp.exp(jnp.where(incl, diff, 0.0)), 0.0)
        mms.append(jnp.where(strict, kk * decay, 0.0))
        qk_s[2 * p + h, rws[c], :] = qk * decay
    tinvs = _tri_inverse_all(mms)
    rs = {ch: _dot(tinv.astype(BF16), rhss[ch[0]][rws[ch[2]]]) for ch, tinv in zip(chains, tinvs)}
    for p in range(PAIRS):
        for c in range(nc):
            r0, r1 = rs[p, 0, c], rs[p, 1, c]
            u_s[rws[c], sls[p]] = jnp.where(hm, r0[:, :LANES], r1[:, :LANES])
            w_s[rws[c], sls[p]] = jnp.where(hm, r0[:, LANES:], r1[:, LANES:])

    bdiag = (_iota((LANES, LANES), 0) // HEAD_DIM) == (_iota((LANES, LANES), 1) // HEAD_DIM)
    for c in range(nc):
        rows = rws[c]
        sts = [st_ref[p] for p in range(PAIRS)]
        stbs = [st.astype(BF16) for st in sts]
        wss = [_dot(w_s[rows, sl].astype(BF16), stb) for sl, stb in zip(sls, stbs)]
        qss = [_dot(qd_s[rows, sl].astype(BF16), stb) for sl, stb in zip(sls, stbs)]
        vnbs = [(u_s[rows, sl] - ws).astype(BF16) for sl, ws in zip(sls, wss)]
        upds = [lax.dot_general(kd_s[rows, sl].astype(BF16), vnb, (((0,), (0,)), ((), ())), preferred_element_type=F32)
                for sl, vnb in zip(sls, vnbs)]
        intra = [(_dot(qk_s[2 * p, rows, :].astype(BF16), vnbs[p]), _dot(qk_s[2 * p + 1, rows, :].astype(BF16), vnbs[p]))
                 for p in range(PAIRS)]
        for p in range(PAIRS):
            st_ref[p] = sts[p] * egl_s[c * cc:c * cc + 1, sls[p]] + jnp.where(bdiag, upds[p], 0.0)
            o_s[rows, sls[p]] = qss[p] + jnp.where(hm, intra[p][0], intra[p][1])

    o = o_s[...]
    ms = _dot01(o * o, bd, pieces=2) * (1.0 / HEAD_DIM)
    o_ref[0] = (o * lax.rsqrt(ms + RMS_EPS) * nw_ref[...]).astype(o_ref.dtype)


def _gdn(hf, conv_w, a_log, dt_bias, norm_w, tile=GDN_ROWS):
    bsz, seq, _ = hf.shape
    rep = lambda a: jnp.repeat(a.astype(F32), HEAD_DIM)[None, :]
    wide = pltpu.VMEM((tile, WIDTH), F32)
    return pl.pallas_call(
        functools.partial(_gdn_kernel, tile=tile),
        grid=(bsz, seq // tile),
        in_specs=[
            pl.BlockSpec((1, tile, HF_COLS), lambda b, t: (b, t, 0)),
            pl.BlockSpec((CONV_K, 3 * WIDTH), lambda b, t: (0, 0)),
            pl.BlockSpec((1, WIDTH), lambda b, t: (0, 0)),
            pl.BlockSpec((1, WIDTH), lambda b, t: (0, 0)),
            pl.BlockSpec((1, WIDTH), lambda b, t: (0, 0)),
        ],
        out_specs=pl.BlockSpec((1, tile, WIDTH), lambda b, t: (b, t, 0)),
        out_shape=jax.ShapeDtypeStruct((bsz, seq, WIDTH), BF16),
        scratch_shapes=[pltpu.VMEM((tile + 8, 3 * WIDTH), F32), wide, wide, wide, wide,
                        pltpu.VMEM((HEADS, tile, LANES), F32), wide, wide,
                        pltpu.VMEM((PAIRS, LANES, LANES), F32)],
        compiler_params=_cparams(("parallel", "arbitrary")),
        name="gdn",
    )(hf, conv_w.astype(F32), rep(a_log), rep(dt_bias), jnp.tile(norm_w.astype(F32), HEADS)[None, :])


def _memattn_kernel(q_ref, kv_ref, o_ref):
    tq = q_ref.shape[1]
    q = q_ref[0]
    k = kv_ref[0, :, 0:WIDTH]
    v = kv_ref[0, :, WIDTH:2 * WIDTH]
    head = _iota((tq, WIDTH), 1) // MEM_HEAD_DIM
    zero = jnp.zeros_like(q)
    out = jnp.zeros((tq, WIDTH), F32)
    for h in range(MEM_HEADS):
        s = _dot_nt(jnp.where(head == h, q, zero), k)
        m = jnp.max(s, axis=-1, keepdims=True)
        p = jnp.exp(s - m)
        o = _dot(p.astype(BF16), v) / jnp.sum(p, axis=-1, keepdims=True)
        out = jnp.where(head == h, o, out)
    o_ref[0] = out.astype(o_ref.dtype)


def _memattn(he, kv, layer, tq=MEMATTN_ROWS):
    bsz, seq, _ = he.shape
    mlen = kv.shape[1]
    return pl.pallas_call(
        _memattn_kernel,
        grid=(bsz, seq // tq),
        in_specs=[
            pl.BlockSpec((1, tq, WIDTH), lambda b, i: (b, i, 0)),
            pl.BlockSpec((1, mlen, 2 * WIDTH), lambda b, i: (b, 0, layer)),
        ],
        out_specs=pl.BlockSpec((1, tq, WIDTH), lambda b, i: (b, i, 0)),
        out_shape=jax.ShapeDtypeStruct((bsz, seq, WIDTH), BF16),
        compiler_params=_cparams(("parallel", "parallel")),
        name="memattn",
    )(he, kv)


def _sigmoid(x):
    return 0.5 * jnp.tanh(0.5 * x) + 0.5


def _out_kernel(oa, ob, oc, od, oe, z_ref, ml_ref, x_ref, wb_ref, wo_ref, g_ref, b_ref, y_ref, yb_ref, *, alpha):
    tm = x_ref.shape[0]
    acc = jnp.zeros((tm, D_MODEL), F32)
    for n, o_ref in enumerate((oa, ob, oc, od, oe)):
        z = z_ref[:, n * WIDTH:(n + 1) * WIDTH]
        gated = o_ref[...] * (z * _sigmoid(z))
        yn = _dot(gated, wb_ref[n])
        acc = acc + _sigmoid(ml_ref[:, n * D_MODEL:(n + 1) * D_MODEL]).astype(F32) * yn
    r = alpha * x_ref[...] + _dot(acc.astype(BF16), wo_ref[...])
    mu = jnp.mean(r, axis=-1, keepdims=True)
    rc = r - mu
    var = jnp.mean(rc * rc, axis=-1, keepdims=True)
    y = rc * lax.rsqrt(var + LN_EPS) * g_ref[...] + b_ref[...]
    y_ref[...] = y
    yb_ref[...] = y.astype(BF16)


def _out(branches, z, ml, x, w_branch, w_out, ln_g, ln_b, alpha, tm=OUT_ROWS):
    m = x.shape[0]
    row = lambda n: pl.BlockSpec((tm, n), lambda i: (i, 0))
    return pl.pallas_call(
        functools.partial(_out_kernel, alpha=alpha),
        grid=(m // tm,),
        in_specs=[row(WIDTH)] * N_BRANCH + [row(N_BRANCH * WIDTH), row(N_BRANCH * D_MODEL), row(D_MODEL),
                  pl.BlockSpec((N_BRANCH, WIDTH, D_MODEL), lambda i: (0, 0, 0)),
                  pl.BlockSpec((D_MODEL, D_MODEL), lambda i: (0, 0)),
                  pl.BlockSpec((1, D_MODEL), lambda i: (0, 0)),
                  pl.BlockSpec((1, D_MODEL), lambda i: (0, 0))],
        out_specs=[row(D_MODEL), row(D_MODEL)],
        out_shape=[jax.ShapeDtypeStruct((m, D_MODEL), F32), jax.ShapeDtypeStruct((m, D_MODEL), BF16)],
        compiler_params=_cparams(("parallel",)),
        name="out",
    )(*branches, z, ml, x, w_branch, w_out, ln_g, ln_b)


def _split_weights(w_in, b_in):
    scale = np.ones((HB_COLS,), np.float32)
    for cb in (CB_AQ, CB_CQ):
        scale[cb * LANES:cb * LANES + WIDTH] = HEAD_DIM ** -0.5 * LOG2E
    scale[CB_BQ * LANES:CB_BQ * LANES + 3 * WIDTH] = HEAD_DIM ** -0.5
    sc = np.concatenate([np.arange(o, o + HEADS) for o in (O_CF, O_DBETA, O_DDECAY)])
    pad = jnp.zeros(w_in.shape[:2] + (LANES - 3 * HEADS,), w_in.dtype)
    w_hf = jnp.concatenate([w_in[..., sc], pad, w_in[..., O_D:O_D + 3 * WIDTH]], axis=-1)
    b_hf = jnp.concatenate([b_in[..., sc], pad[:, 0], b_in[..., O_D:O_D + 3 * WIDTH]], axis=-1)
    span = lambda a, o, n: a[..., o:o + n]
    groups = {
        "hb": (span(w_in, 0, HB_COLS), span(b_in, 0, HB_COLS), scale),
        "he": (span(w_in, O_E, WIDTH), span(b_in, O_E, WIDTH), np.full((WIDTH,), MEM_HEAD_DIM ** -0.5, np.float32)),
        "hf": (w_hf, b_hf, np.ones((HF_COLS,), np.float32)),
        "z": (span(w_in, O_Z, N_BRANCH * WIDTH), span(b_in, O_Z, N_BRANCH * WIDTH), np.ones((N_BRANCH * WIDTH,), np.float32)),
        "ml": (span(w_in, O_MERGE, N_BRANCH * D_MODEL), span(b_in, O_MERGE, N_BRANCH * D_MODEL),
               np.ones((N_BRANCH * D_MODEL,), np.float32)),
    }
    return {k: (w.astype(F32), b.astype(F32)[:, None, :], jnp.asarray(s)[None, :]) for k, (w, b, s) in groups.items()}


def _layer(x, xb, kv, layer, gw, conv_w, a_log, dt_bias, gdn_norm_w, w_branch, w_out, ln_g, ln_b, alpha):
    bsz, seq, d = x.shape
    m = bsz * seq
    xf = x.reshape(m, d)
    proj = lambda name, dtype: _proj(xb, *gw[name], layer, dtype, PROJ_ROWS, PROJ_COLS[name])
    hb = proj("hb", BF16).reshape(bsz, seq, HB_COLS)
    he = proj("he", BF16).reshape(bsz, seq, WIDTH)
    hf = proj("hf", F32).reshape(bsz, seq, HF_COLS)
    z = proj("z", BF16)
    ml = proj("ml", BF16)

    o_a = _moba(hb)
    o_b = _dilated(hb)
    o_c = _fox(hb, _fcum(hf))
    o_d = _gdn(hf, conv_w, a_log, dt_bias, gdn_norm_w)
    o_e = _memattn(he, kv, layer)
    branches = [o.reshape(m, WIDTH) for o in (o_a, o_b, o_c, o_d, o_e)]
    y, yb = _out(branches, z, ml, xf, w_branch.astype(BF16), w_out.astype(BF16),
                 ln_g.astype(F32)[None, :], ln_b.astype(F32)[None, :], alpha)
    return y.reshape(bsz, seq, d), yb


def kernel(x, mem, mem_ln_g, mem_ln_b, w_in, b_in, conv_w, a_log, dt_bias, gdn_norm_w, w_mem_kv, w_branch, w_out, ln_g, ln_b):
    depth = w_in.shape[0]
    alpha = float((2 * depth) ** 0.25)
    w_kv = jnp.concatenate([w_mem_kv[l] for l in range(depth)], axis=1).astype(BF16)
    kv = _memkv(mem.astype(F32), mem_ln_g.astype(F32)[None, :], mem_ln_b.astype(F32)[None, :], w_kv)
    x = x.astype(F32)
    xb = x.reshape(-1, x.shape[-1]).astype(BF16)
    gw = _split_weights(w_in, b_in)
    for l in range(depth):
        x, xb = _layer(x, xb, kv, l, gw, conv_w[l], a_log[l], dt_bias[l], gdn_norm_w[l],
                       w_branch[l], w_out[l], ln_g[l], ln_b[l], alpha)
    return x
```

```python
import functools

import jax
import jax.numpy as jnp
import numpy as np
from jax import lax
from jax.experimental import pallas as pl
from jax.experimental.pallas import tpu as pltpu

F32 = jnp.float32
BF16 = jnp.bfloat16
HI = lax.Precision.HIGHEST

D_MODEL = 1024
HEAD_DIM = 64
HEADS = 6
WIDTH = HEADS * HEAD_DIM
N_BRANCH = 5
LANES = 128
PAIRS = WIDTH // LANES
MOBA_BLOCK = 256
MOBA_TOPK = 3
DILATIONS = (1, 4, 16)
BAND = 128
BAND_TILE = 2048
BAND_GROUP = 16
GDN_CHUNK = 128
CONV_K = 4
MEM_HEADS = 4
MEM_HEAD_DIM = WIDTH // MEM_HEADS
NEG = -1e30
LOG2E = float(np.log2(np.e))
FLASH_UNDERFLOW = 160.0
LN_EPS = 1e-5
RMS_EPS = 1e-6

_SPLIT = (3 * WIDTH, 6 * WIDTH, WIDTH, 3 * WIDTH, HEADS, 3 * WIDTH, HEADS, HEADS, WIDTH, N_BRANCH * WIDTH, N_BRANCH * D_MODEL)
_OFF = tuple(int(v) for v in np.concatenate([[0], np.cumsum(_SPLIT)]))
(O_A, O_BQK, O_BV, O_C, O_CF, O_D, O_DBETA, O_DDECAY, O_E, O_Z, O_MERGE, _) = _OFF

CB_AQ, CB_AK, CB_AV = 0, 3, 6
CB_BQ, CB_BK, CB_BV = 9, 18, 27
CB_CQ, CB_CK, CB_CV = 30, 33, 36
HB_COLS = 39 * LANES
assert (O_A, O_BQK, O_BV, O_C, O_CF) == tuple(LANES * c for c in (CB_AQ, CB_BQ, CB_BV, CB_CQ, 39))
HF_COLS = LANES + 3 * WIDTH

VMEM_LIMIT = 56 * 1024 * 1024

PROJ_ROWS = 2048
PROJ_COLS = {"hb": HB_COLS // 3, "he": WIDTH, "hf": HF_COLS, "z": N_BRANCH * WIDTH, "ml": 1024}
FOX_ROWS = 256
MOBA_ROWS = 2 * MOBA_BLOCK
FLASH_BUILD_ROWS = 512
BAND_MERGE_ROWS = 256
GDN_ROWS = 4 * GDN_CHUNK
MEMATTN_ROWS = 512
OUT_ROWS = 512
TRI_BASE = 16
BOUND_REL, BOUND_ABS = 1.01, 1.0
FAR = 1e9
TAKEN = -3e38


def _cparams(sem):
    return pltpu.CompilerParams(dimension_semantics=sem, vmem_limit_bytes=VMEM_LIMIT)


def _dot(a, b):
    return jnp.dot(a, b, preferred_element_type=F32)


def _dot_nt(a, b):
    return lax.dot_general(a, b, (((1,), (1,)), ((), ())), preferred_element_type=F32)


def _dot_nt_hi(a, b):
    return lax.dot_general(a, b, (((1,), (1,)), ((), ())), preferred_element_type=F32, precision=HI)


def _iota(shape, dim):
    return lax.broadcasted_iota(jnp.int32, shape, dim)


def _proj_kernel(x_ref, w_ref, b_ref, s_ref, o_ref, wb_ref):
    @pl.when(pl.program_id(1) == 0)
    def _():
        wb_ref[...] = w_ref[...].astype(BF16)

    acc = _dot(x_ref[...], wb_ref[...])
    o_ref[...] = ((acc + b_ref[...]) * s_ref[...]).astype(o_ref.dtype)


def _proj(xb, w, b, scale, layer, out_dtype, tm, tn):
    m, k = xb.shape
    n = b.shape[-1]
    assert m % tm == 0 and n % tn == 0
    return pl.pallas_call(
        _proj_kernel,
        grid=(n // tn, m // tm),
        in_specs=[
            pl.BlockSpec((tm, k), lambda j, i: (i, 0)),
            pl.BlockSpec((None, k, tn), lambda j, i: (layer, 0, j)),
            pl.BlockSpec((None, 1, tn), lambda j, i: (layer, 0, j)),
            pl.BlockSpec((1, tn), lambda j, i: (0, j)),
        ],
        out_specs=pl.BlockSpec((tm, tn), lambda j, i: (i, j)),
        out_shape=jax.ShapeDtypeStruct((m, n), out_dtype),
        scratch_shapes=[pltpu.VMEM((k, tn), BF16)],
        compiler_params=_cparams(("parallel", "arbitrary")),
        name="proj",
    )(xb, w, b, scale)


def _memkv_kernel(mem_ref, g_ref, b_ref, w_ref, o_ref):
    x = mem_ref[0]
    mu = jnp.mean(x, axis=-1, keepdims=True)
    xc = x - mu
    var = jnp.mean(xc * xc, axis=-1, keepdims=True)
    y = xc * lax.rsqrt(var + LN_EPS) * g_ref[...] + b_ref[...]
    o_ref[0] = _dot(y.astype(BF16), w_ref[...]).astype(o_ref.dtype)


def _memkv(mem, g, b, w):
    bsz, mlen, d = mem.shape
    n = w.shape[1]
    return pl.pallas_call(
        _memkv_kernel,
        grid=(bsz,),
        in_specs=[
            pl.BlockSpec((1, mlen, d), lambda i: (i, 0, 0)),
            pl.BlockSpec((1, d), lambda i: (0, 0)),
            pl.BlockSpec((1, d), lambda i: (0, 0)),
            pl.BlockSpec((d, n), lambda i: (0, 0)),
        ],
        out_specs=pl.BlockSpec((1, mlen, n), lambda i: (i, 0, 0)),
        out_shape=jax.ShapeDtypeStruct((bsz, mlen, n), BF16),
        compiler_params=_cparams(("parallel",)),
        name="memkv",
    )(mem, g, b, w)


def _fcum_kernel(h_ref, o_ref, *, blk):
    seq = h_ref.shape[1]
    tri = (_iota((blk, blk), 0) >= _iota((blk, blk), 1)).astype(BF16)

    def body(i, carry):
        rows = pl.ds(pl.multiple_of(i * blk, blk), blk)
        logf = jax.nn.log_sigmoid(h_ref[0, rows, :])
        c = _dot01(tri, logf) + carry
        o_ref[0, rows, :] = c * LOG2E
        return c[blk - 1:blk, :]

    lax.fori_loop(0, seq // blk, body, jnp.zeros((1, LANES), F32))


def _fcum(hf):
    bsz, seq, _ = hf.shape
    return pl.pallas_call(
        functools.partial(_fcum_kernel, blk=FLASH_BUILD_ROWS),
        grid=(bsz,),
        in_specs=[pl.BlockSpec((1, seq, LANES), lambda b: (b, 0, 0))],
        out_specs=pl.BlockSpec((1, seq, LANES), lambda b: (b, 0, 0)),
        out_shape=jax.ShapeDtypeStruct((bsz, seq, LANES), F32),
        compiler_params=_cparams(("parallel",)),
        name="fcum",
    )(hf)


def _head_masks(rows):
    lane = _iota((rows, LANES), 1)
    return lane < HEAD_DIM


def _split3(x):
    hi = x.astype(BF16).astype(F32)
    r = x - hi
    mid = r.astype(BF16).astype(F32)
    return hi, mid, r - mid


def _flash_scratch(tq):
    return [pltpu.VMEM((2 * tq, LANES), F32), pltpu.VMEM((2 * tq, tq), F32), pltpu.VMEM((2 * tq, tq), F32),
            pltpu.VMEM((2 * tq, LANES), F32), pltpu.VMEM((2 * tq, LANES), F32)]


def _augment_values(vx_ref, rows, v):
    vx_ref[rows, 0:LANES] = v
    vx_ref[rows, LANES:] = jnp.ones((v.shape[0], LANES), BF16)


def _flash_causal(qx, kx_ref, vx_ref, scratch, i, tq, past_tiles=None):
    acc_ref, sa_ref, sb_ref, m_ref, l_ref = scratch

    def logits(kt, dst):
        rows = pl.ds(pl.multiple_of(kt * tq, tq), tq)
        dst[...] = _dot_nt(qx, kx_ref[rows, :])

    r = _iota((2 * tq, tq), 0)
    causal = jnp.where(r >= tq, r - tq, r) >= _iota((2 * tq, tq), 1)

    def finish(src, kt, masked=False):
        rows = pl.ds(pl.multiple_of(kt * tq, tq), tq)
        s = src[...]
        if masked:
            s = jnp.where(causal, s, NEG)
        m = m_ref[...]
        m_new = jnp.maximum(m, jnp.max(s, axis=-1, keepdims=True))
        alpha = jnp.exp2(m - m_new)
        p = jnp.exp2((s - jnp.tile(m_new, (1, tq // LANES))).astype(BF16))
        pv = _dot(p, vx_ref[rows, :])
        l_ref[...] = alpha * l_ref[...] + pv[:, LANES:]
        m_ref[...] = m_new
        acc_ref[...] = alpha * acc_ref[...] + pv[:, 0:LANES]

    acc_ref[...] = jnp.zeros_like(acc_ref)
    m_ref[...] = jnp.full((2 * tq, LANES), NEG, F32)
    l_ref[...] = jnp.zeros((2 * tq, LANES), F32)
    tile = lambda u: jnp.maximum(i - 1 - u, 0)
    logits(i, sa_ref)
    logits(tile(0), sb_ref)
    finish(sa_ref, i, masked=True)
    n = i if past_tiles is None else past_tiles(m_ref)

    def body(v, carry):
        logits(tile(2 * v + 1), sa_ref)
        finish(sb_ref, tile(2 * v))
        logits(tile(2 * v + 2), sb_ref)
        finish(sa_ref, tile(2 * v + 1))
        return carry

    lax.fori_loop(0, n // 2, body, 0)

    @pl.when(n % 2 == 1)
    def _():
        finish(sb_ref, tile(n - 1))

    o = acc_ref[...] / l_ref[...]
    return jnp.where(_head_masks(tq), o[0:tq], o[tq:])


def _head_sumsq(x):
    hm = _head_masks(x.shape[0])
    sq = x * x
    return (jnp.sum(jnp.where(hm, sq, 0.0), axis=1, keepdims=True), jnp.sum(jnp.where(hm, 0.0, sq), axis=1, keepdims=True))


def _fox_kernel(q_ref, k_ref, v_ref, f_ref, o_ref, kx_ref, vx_ref, kn_ref, *flash, tq):
    p = pl.program_id(1)
    i = pl.program_id(2)
    seq = k_ref.shape[1]
    bt = FLASH_BUILD_ROWS

    @pl.when(i == 0)
    def _():
        src = _iota((LANES, LANES), 0)
        dst = _iota((LANES, LANES), 1)
        pm = jnp.logical_or(jnp.logical_and(src == 2 * p, dst < 3),
                            jnp.logical_and(src == 2 * p + 1, jnp.logical_and(dst >= 3, dst < 6))).astype(BF16)
        sub = _iota((bt, LANES), 1) % 3

        def build(c, carry):
            rows = pl.ds(pl.multiple_of(c * bt, bt), bt)
            hi, mid, lo = _split3(_dot01(f_ref[0, rows, :], pm))
            k = k_ref[0, rows, :]
            kx_ref[rows, 0:LANES] = k
            kx_ref[rows, LANES:] = jnp.where(sub == 0, hi, jnp.where(sub == 1, mid, lo)).astype(BF16)
            _augment_values(vx_ref, rows, v_ref[0, rows, :])
            return tuple(jnp.maximum(c0, jnp.max(s, axis=0, keepdims=True))
                         for c0, s in zip(carry, _head_sumsq(k.astype(F32))))

        zero = jnp.zeros((1, 1), F32)
        kn = lax.fori_loop(0, seq // bt, build, (zero, zero))
        kn_ref[0:1, :] = jnp.broadcast_to(kn[0], (1, LANES))
        kn_ref[1:2, :] = jnp.broadcast_to(kn[1], (1, LANES))

    q = q_ref[0].astype(F32)
    lane = _iota((tq, LANES), 1)
    hm = lane < HEAD_DIM
    top = jnp.concatenate([jnp.where(hm, q, 0.0), jnp.where(lane < 3, -1.0, 0.0)], axis=1)
    bot = jnp.concatenate([jnp.where(hm, 0.0, q), jnp.where(jnp.logical_and(lane >= 3, lane < 6), -1.0, 0.0)], axis=1)
    qx = jnp.concatenate([top, bot], axis=0).astype(BF16)

    def past_tiles(m_ref):
        nt = seq // tq
        fend = f_ref[0, pl.ds(tq - 1, nt, stride=tq), :]
        lane_t = _iota((nt, LANES), 1)
        tpos = _iota((nt, 1), 0)
        m = m_ref[...]
        need = tpos < 0
        for h, qn2 in enumerate(_head_sumsq(q)):
            m_min = jnp.min(m[h * tq:(h + 1) * tq, 0:1], axis=0, keepdims=True)
            qk = jnp.sqrt(jnp.max(qn2, axis=0, keepdims=True) * kn_ref[h:h + 1, 0:1]) * BOUND_REL + BOUND_ABS
            f_h = jnp.sum(jnp.where(lane_t == 2 * p + h, fend, 0.0), axis=1, keepdims=True)
            need = jnp.logical_or(need, qk - f_h > m_min - FLASH_UNDERFLOW)
        first = jnp.min(jnp.where(jnp.logical_and(need, tpos < i), tpos, i).astype(F32))
        return i - first.astype(jnp.int32)

    o_ref[0] = _flash_causal(qx, kx_ref, vx_ref, flash, i, tq, past_tiles).astype(o_ref.dtype)


def _fox(hb, fcol, tq=FOX_ROWS):
    bsz, seq, _ = hb.shape
    assert seq % tq == 0
    return pl.pallas_call(
        functools.partial(_fox_kernel, tq=tq),
        grid=(bsz, PAIRS, seq // tq),
        in_specs=[
            pl.BlockSpec((1, tq, LANES), lambda b, p, i: (b, i, CB_CQ + p)),
            pl.BlockSpec((1, seq, LANES), lambda b, p, i: (b, 0, CB_CK + p)),
            pl.BlockSpec((1, seq, LANES), lambda b, p, i: (b, 0, CB_CV + p)),
            pl.BlockSpec((1, seq, LANES), lambda b, p, i: (b, 0, 0)),
        ],
        out_specs=pl.BlockSpec((1, tq, LANES), lambda b, p, i: (b, i, p)),
        out_shape=jax.ShapeDtypeStruct((bsz, seq, WIDTH), BF16),
        scratch_shapes=[pltpu.VMEM((seq, 2 * LANES), BF16), pltpu.VMEM((seq, 2 * LANES), BF16),
                        pltpu.VMEM((8, LANES), F32)] + _flash_scratch(tq),
        compiler_params=_cparams(("parallel", "parallel", "arbitrary")),
        name="fox",
    )(hb, hb, hb, fcol)


def _moba_kernel(q_ref, k_ref, v_ref, o_ref, kx_ref, vx_ref, kmean_ref, *flash, tq):
    i = pl.program_id(2)
    seq = k_ref.shape[1]
    blk = MOBA_BLOCK

    @pl.when(i == 0)
    def _():
        kmean_ref[...] = jnp.zeros_like(kmean_ref)
        lane = _iota((blk, LANES), 1)

        def build(n, carry):
            rows = pl.ds(pl.multiple_of(n * blk, blk), blk)
            k = k_ref[0, rows, :]
            kx_ref[rows, 0:LANES] = k
            kx_ref[rows, LANES:] = jnp.where(lane == n, 1.0, 0.0).astype(BF16)
            _augment_values(vx_ref, rows, v_ref[0, rows, :])
            kmean_ref[pl.ds(n, 1), :] = jnp.sum(k.astype(F32), axis=0, keepdims=True) * (1.0 / blk)
            return carry

        lax.fori_loop(0, seq // blk, build, 0)

    q = q_ref[0].astype(F32)
    hm = _head_masks(tq)
    nbp = -(-(seq // blk) // 8) * 8
    blk_id = _iota((nbp, tq), 0)
    blk_f = blk_id.astype(F32)
    own = i * (tq // blk) + _iota((nbp, tq), 1) // blk
    valid = blk_id < own
    kmean = kmean_ref[0:nbp, :]
    halves = []
    qhs = (jnp.where(hm, q, 0.0), jnp.where(hm, 0.0, q))
    gates = [_dot_nt_hi(kmean, qh) for qh in qhs]
    for qh, gate in zip(qhs, gates):
        g = jnp.where(valid, gate, NEG)
        bias = jnp.where(blk_id == own, 0.0, NEG)
        for _ in range(MOBA_TOPK):
            mx = jnp.max(g, axis=0, keepdims=True)
            first = jnp.min(jnp.where(g == mx, blk_f, FAR), axis=0, keepdims=True)
            pick = blk_f == first
            bias = jnp.where(jnp.logical_and(pick, valid), 0.0, bias)
            g = jnp.where(pick, TAKEN, g)
        bias = jnp.concatenate([bias, jnp.zeros((LANES - nbp, tq), F32)], axis=0).T
        halves.append(jnp.concatenate([qh, bias], axis=1))
    qx = jnp.concatenate(halves, axis=0).astype(BF16)
    o_ref[0] = _flash_causal(qx, kx_ref, vx_ref, flash, i, tq).astype(o_ref.dtype)


def _moba(hb, tq=MOBA_ROWS):
    bsz, seq, _ = hb.shape
    assert seq % tq == 0 and tq % MOBA_BLOCK == 0 and seq // MOBA_BLOCK <= LANES
    return pl.pallas_call(
        functools.partial(_moba_kernel, tq=tq),
        grid=(bsz, PAIRS, seq // tq),
        in_specs=[
            pl.BlockSpec((1, tq, LANES), lambda b, p, i: (b, i, CB_AQ + p)),
            pl.BlockSpec((1, seq, LANES), lambda b, p, i: (b, 0, CB_AK + p)),
            pl.BlockSpec((1, seq, LANES), lambda b, p, i: (b, 0, CB_AV + p)),
        ],
        out_specs=pl.BlockSpec((1, tq, LANES), lambda b, p, i: (b, i, p)),
        out_shape=jax.ShapeDtypeStruct((bsz, seq, WIDTH), BF16),
        scratch_shapes=[pltpu.VMEM((seq, 2 * LANES), BF16), pltpu.VMEM((seq, 2 * LANES), BF16),
                        pltpu.VMEM((LANES, LANES), F32)] + _flash_scratch(tq),
        compiler_params=_cparams(("parallel", "parallel", "arbitrary")),
        name="moba",
    )(hb, hb, hb)


def _band_kernel(q0, q1, q2, k0c, k0p, k1c, k1p, k2c, k2p, vc, vp, o_ref, qb, kb, vb, nb, mb, sb, *, tile):
    jt = pl.program_id(2)
    for g, (qr, kc, kp) in enumerate(((q0, k0c, k0p), (q1, k1c, k1p), (q2, k2c, k2p))):
        qb[g] = qr[0].astype(F32)
        kb[g, 0:tile, :] = kp[0].astype(F32)
        kb[g, tile:, :] = kc[0].astype(F32)
    vb[0:tile, :] = vp[0].astype(F32)
    vb[tile:, :] = vc[0].astype(F32)

    hm = _head_masks(BAND)
    qi = _iota((BAND, 2 * BAND), 0)
    kj = _iota((BAND, 2 * BAND), 1)
    dist = BAND + qi - kj
    band = jnp.logical_and(dist >= 0, dist <= BAND)
    cur_half = kj >= BAND
    nblocks = tile // BAND

    for g, dil in enumerate(DILATIONS):
        per_stream = nblocks // dil

        def step(it, carry, g=g, dil=dil, per_stream=per_stream):
            starts, kstarts, valids = [], [], []
            for j in range(BAND_GROUP):
                idx = it * BAND_GROUP + j
                n = idx % per_stream
                start = idx // per_stream + n * (BAND * dil)
                starts.append(start)
                kstarts.append(tile + start - BAND * dil)
                valids.append(jnp.logical_and(band, jnp.logical_or(cur_half, jnp.logical_or(jt > 0, n > 0))))
            qs = [qb[g, pl.ds(s, BAND, stride=dil), :] for s in starts]
            kks = [kb[g, pl.ds(s, 2 * BAND, stride=dil), :].astype(BF16) for s in kstarts]
            ones = jnp.ones((2 * BAND, LANES), BF16)
            vvs = [jnp.concatenate([vb[pl.ds(s, 2 * BAND, stride=dil), :].astype(BF16), ones], axis=1) for s in kstarts]
            logits = []
            for q, kk, valid in zip(qs, kks, valids):
                for h in range(2):
                    qh = (jnp.where(hm, q, 0.0) if h == 0 else jnp.where(hm, 0.0, q)).astype(BF16)
                    logits.append(jnp.where(valid, _dot_nt(qh, kk), NEG))
            ms = [jnp.max(s, axis=-1, keepdims=True) for s in logits]
            ps = [jnp.exp((s - m).astype(BF16)) for s, m in zip(logits, ms)]
            pvs = [_dot(p, vvs[i // 2]) for i, p in enumerate(ps)]
            nums = [pv[:, 0:LANES] for pv in pvs]
            ss = [pv[:, LANES:] for pv in pvs]
            for j, s in enumerate(starts):
                rows = pl.ds(s, BAND, stride=dil)
                nb[g, rows, :] = jnp.where(hm, nums[2 * j], nums[2 * j + 1])
                mb[g, rows, :] = jnp.where(hm, ms[2 * j], ms[2 * j + 1])
                sb[g, rows, :] = jnp.where(hm, ss[2 * j], ss[2 * j + 1])
            return carry

        lax.fori_loop(0, nblocks // BAND_GROUP, step, 0)

    ch = BAND_MERGE_ROWS

    def merge(i, carry):
        rows = pl.ds(pl.multiple_of(i * ch, ch), ch)
        m_all = jnp.maximum(jnp.maximum(mb[0, rows, :], mb[1, rows, :]), mb[2, rows, :])
        num = jnp.zeros((ch, LANES), F32)
        den = jnp.zeros((ch, LANES), F32)
        for g in range(len(DILATIONS)):
            w = jnp.exp(mb[g, rows, :] - m_all)
            num = num + nb[g, rows, :] * w
            den = den + sb[g, rows, :] * w
        o_ref[0, rows, :] = (num / den).astype(o_ref.dtype)
        return carry

    lax.fori_loop(0, tile // ch, merge, 0)


def _dilated(hb, tile=BAND_TILE):
    bsz, seq, _ = hb.shape
    ng = len(DILATIONS)
    nblocks = tile // BAND
    assert seq % tile == 0 and nblocks % BAND_GROUP == 0 and all(nblocks % d == 0 for d in DILATIONS)
    cur = lambda cb: pl.BlockSpec((1, tile, LANES), lambda b, p, j: (b, j, cb + p))
    prv = lambda cb: pl.BlockSpec((1, tile, LANES), lambda b, p, j: (b, jnp.maximum(j - 1, 0), cb + p))
    in_specs = [cur(CB_BQ + PAIRS * g) for g in range(ng)]
    for g in range(ng):
        in_specs += [cur(CB_BK + PAIRS * g), prv(CB_BK + PAIRS * g)]
    in_specs += [cur(CB_BV), prv(CB_BV)]
    acc = pltpu.VMEM((ng, tile, LANES), F32)
    return pl.pallas_call(
        functools.partial(_band_kernel, tile=tile),
        grid=(bsz, PAIRS, seq // tile),
        in_specs=in_specs,
        out_specs=pl.BlockSpec((1, tile, LANES), lambda b, p, j: (b, j, p)),
        out_shape=jax.ShapeDtypeStruct((bsz, seq, WIDTH), BF16),
        scratch_shapes=[acc, pltpu.VMEM((ng, 2 * tile, LANES), F32), pltpu.VMEM((2 * tile, LANES), F32), acc, acc, acc],
        compiler_params=_cparams(("parallel", "parallel", "arbitrary")),
        name="band",
    )(*([hb] * len(in_specs)))


def _dot01(a, b, nt=False, pieces=3):
    f = _dot_nt if nt else _dot
    if a.dtype == BF16:
        return sum(f(a, piece.astype(BF16)) for piece in _split3(b)[:pieces])
    return sum(f(piece.astype(BF16), b) for piece in _split3(a)[:pieces])


def _tri_inverse_all(ms):
    c = ms[0].shape[0]
    ri = _iota((c, c), 0)
    ci = _iota((c, c), 1)
    base = TRI_BASE
    inblk = ri // base == ci // base
    eye = jnp.where(ri == ci, 1.0, 0.0)
    ps = [jnp.where(inblk, -m, 0.0) for m in ms]
    ts = [eye + p for p in ps]
    for _ in range(3):
        pbs = [p.astype(BF16) for p in ps]
        ps = [_dot(pb, pb) for pb in pbs]
        ts = [t + _dot(t.astype(BF16), p.astype(BF16)) for t, p in zip(ts, ps)]
    size = base
    while size < c:
        lower = jnp.logical_and(ri // (2 * size) == ci // (2 * size), ri // size != ci // size)
        tbs = [t.astype(BF16) for t in ts]
        xs = [_dot(tb, jnp.where(lower, m, 0.0).astype(BF16)) for tb, m in zip(tbs, ms)]
        ts = [t - _dot(x.astype(BF16), tb) for t, x, tb in zip(ts, xs, tbs)]
        size *= 2
    return ts


def _gdn_kernel(h_ref, cw_ref, alog_ref, dtb_ref, nw_ref, o_ref,
                xe_ref, u_s, w_s, qd_s, kd_s, qk_s, egl_s, o_s, st_ref, *, tile):
    cc = GDN_CHUNK
    nc = tile // cc
    t = pl.program_id(1)

    @pl.when(t == 0)
    def _():
        xe_ref[0:8, :] = jnp.zeros((8, 3 * WIDTH), F32)
        st_ref[...] = jnp.zeros_like(st_ref)

    x = h_ref[0, :, LANES:]
    xe_ref[8:8 + tile, :] = x
    y = jnp.zeros((tile, 3 * WIDTH), F32)
    for j in range(CONV_K):
        y = y + cw_ref[j:j + 1, :] * xe_ref[8 - (CONV_K - 1) + j:8 - (CONV_K - 1) + j + tile, :]
    xe_ref[0:8, :] = x[tile - 8:tile, :]
    y = y * _sigmoid(y)

    bd = ((_iota((WIDTH, WIDTH), 0) // HEAD_DIM) == (_iota((WIDTH, WIDTH), 1) // HEAD_DIM)).astype(BF16)
    q = y[:, 0:WIDTH]
    k = y[:, WIDTH:2 * WIDTH]
    v = y[:, 2 * WIDTH:]
    q = q * lax.rsqrt(_dot01(q * q, bd, pieces=2) + RMS_EPS) * (HEAD_DIM ** -0.5)
    k = k * lax.rsqrt(_dot01(k * k, bd, pieces=2) + RMS_EPS)

    hs = h_ref[0, :, 0:LANES]
    er = _iota((LANES, WIDTH), 0)
    ec = _iota((LANES, WIDTH), 1) // HEAD_DIM
    beta = _sigmoid(_dot01(hs, (er == ec + HEADS).astype(BF16), pieces=2))
    g = -jnp.exp(alog_ref[...]) * jax.nn.softplus(_dot01(hs, (er == ec + 2 * HEADS).astype(BF16), pieces=2) + dtb_ref[...])

    tri = (_iota((cc, cc), 0) >= _iota((cc, cc), 1)).astype(BF16)
    gc = jnp.concatenate([_dot01(tri, g[c * cc:(c + 1) * cc]) for c in range(nc)], axis=0)
    glast = jnp.broadcast_to(gc.reshape(nc, cc, WIDTH)[:, cc - 1:cc, :], (nc, cc, WIDTH)).reshape(tile, WIDTH)
    eg = jnp.exp(gc)
    kb = k * beta
    vb = v * beta
    wb = kb * eg
    qd_s[...] = q * eg
    kd_s[...] = k * jnp.exp(glast - gc)
    egl_s[...] = jnp.exp(glast)

    ri = _iota((cc, cc), 0)
    ci = _iota((cc, cc), 1)
    incl = ri >= ci
    strict = ri > ci
    lane = _iota((cc, LANES), 1)
    hm = lane < HEAD_DIM
    sls = [slice(p * LANES, (p + 1) * LANES) for p in range(PAIRS)]
    keeps = [hm, jnp.logical_not(hm)]
    heads = [(p, h) for p in range(PAIRS) for h in range(2)]
    chains = [(p, h, c) for p, h in heads for c in range(nc)]
    rws = [slice(c * cc, (c + 1) * cc) for c in range(nc)]
    kps = [k[:, sl].astype(BF16) for sl in sls]
    rhss = [jnp.concatenate([vb[:, sl], wb[:, sl]], axis=1).astype(BF16) for sl in sls]
    gct = gc.T
    rowf = {(p, h): gct[p * LANES + h * HEAD_DIM:p * LANES + h * HEAD_DIM + 1, :] for p, h in heads}
    kbm = {(p, h): jnp.where(jnp.tile(keeps[h], (nc, 1)), kb[:, sls[p]], 0.0).astype(BF16) for p, h in heads}
    qm = {(p, h): jnp.where(jnp.tile(keeps[h], (nc, 1)), q[:, sls[p]], 0.0).astype(BF16) for p, h in heads}
    kks = [_dot_nt(kbm[p, h][rws[c]], kps[p][rws[c]]) for p, h, c in chains]
    qks = [_dot_nt(qm[p, h][rws[c]], kps[p][rws[c]]) for p, h, c in chains]
    mms = []
    for (p, h, c), kk, qk in zip(chains, kks, qks):
        col = p * LANES + h * HEAD_DIM
        diff = gc[rws[c], col:col + 1] - rowf[p, h][:, rws[c]]
        decay = jnp.where(incl, jnp.exp(jnp.where(incl, diff, 0.0)), 0.0)
        mms.append(jnp.where(strict, kk * decay, 0.0))
        qk_s[2 * p + h, rws[c], :] = qk * decay
    tinvs = _tri_inverse_all(mms)
    rs = {ch: _dot(tinv.astype(BF16), rhss[ch[0]][rws[ch[2]]]) for ch, tinv in zip(chains, tinvs)}
    for p in range(PAIRS):
        for c in range(nc):
            r0, r1 = rs[p, 0, c], rs[p, 1, c]
            u_s[rws[c], sls[p]] = jnp.where(hm, r0[:, :LANES], r1[:, :LANES])
            w_s[rws[c], sls[p]] = jnp.where(hm, r0[:, LANES:], r1[:, LANES:])

    bdiag = (_iota((LANES, LANES), 0) // HEAD_DIM) == (_iota((LANES, LANES), 1) // HEAD_DIM)
    for c in range(nc):
        rows = rws[c]
        sts = [st_ref[p] for p in range(PAIRS)]
        stbs = [st.astype(BF16) for st in sts]
        wss = [_dot(w_s[rows, sl].astype(BF16), stb) for sl, stb in zip(sls, stbs)]
        qss = [_dot(qd_s[rows, sl].astype(BF16), stb) for sl, stb in zip(sls, stbs)]
        vnbs = [(u_s[rows, sl] - ws).astype(BF16) for sl, ws in zip(sls, wss)]
        upds = [lax.dot_general(kd_s[rows, sl].astype(BF16), vnb, (((0,), (0,)), ((), ())), preferred_element_type=F32)
                for sl, vnb in zip(sls, vnbs)]
        intra = [(_dot(qk_s[2 * p, rows, :].astype(BF16), vnbs[p]), _dot(qk_s[2 * p + 1, rows, :].astype(BF16), vnbs[p]))
                 for p in range(PAIRS)]
        for p in range(PAIRS):
            st_ref[p] = sts[p] * egl_s[c * cc:c * cc + 1, sls[p]] + jnp.where(bdiag, upds[p], 0.0)
            o_s[rows, sls[p]] = qss[p] + jnp.where(hm, intra[p][0], intra[p][1])

    o = o_s[...]
    ms = _dot01(o * o, bd, pieces=2) * (1.0 / HEAD_DIM)
    o_ref[0] = (o * lax.rsqrt(ms + RMS_EPS) * nw_ref[...]).astype(o_ref.dtype)


def _gdn(hf, conv_w, a_log, dt_bias, norm_w, tile=GDN_ROWS):
    bsz, seq, _ = hf.shape
    rep = lambda a: jnp.repeat(a.astype(F32), HEAD_DIM)[None, :]
    wide = pltpu.VMEM((tile, WIDTH), F32)
    return pl.pallas_call(
        functools.partial(_gdn_kernel, tile=tile),
        grid=(bsz, seq // tile),
        in_specs=[
            pl.BlockSpec((1, tile, HF_COLS), lambda b, t: (b, t, 0)),
            pl.BlockSpec((CONV_K, 3 * WIDTH), lambda b, t: (0, 0)),
            pl.BlockSpec((1, WIDTH), lambda b, t: (0, 0)),
            pl.BlockSpec((1, WIDTH), lambda b, t: (0, 0)),
            pl.BlockSpec((1, WIDTH), lambda b, t: (0, 0)),
        ],
        out_specs=pl.BlockSpec((1, tile, WIDTH), lambda b, t: (b, t, 0)),
        out_shape=jax.ShapeDtypeStruct((bsz, seq, WIDTH), BF16),
        scratch_shapes=[pltpu.VMEM((tile + 8, 3 * WIDTH), F32), wide, wide, wide, wide,
                        pltpu.VMEM((HEADS, tile, LANES), F32), wide, wide,
                        pltpu.VMEM((PAIRS, LANES, LANES), F32)],
        compiler_params=_cparams(("parallel", "arbitrary")),
        name="gdn",
    )(hf, conv_w.astype(F32), rep(a_log), rep(dt_bias), jnp.tile(norm_w.astype(F32), HEADS)[None, :])


def _memattn_kernel(q_ref, kv_ref, o_ref):
    tq = q_ref.shape[1]
    q = q_ref[0]
    k = kv_ref[0, :, 0:WIDTH]
    v = kv_ref[0, :, WIDTH:2 * WIDTH]
    head = _iota((tq, WIDTH), 1) // MEM_HEAD_DIM
    zero = jnp.zeros_like(q)
    logits = [_dot_nt(jnp.where(head == h, q, zero), k) for h in range(MEM_HEADS)]
    ps = [jnp.exp(s - jnp.max(s, axis=-1, keepdims=True)) for s in logits]
    outs = [_dot(p.astype(BF16), v) / jnp.sum(p, axis=-1, keepdims=True) for p in ps]
    out = outs[0]
    for h in range(1, MEM_HEADS):
        out = jnp.where(head == h, outs[h], out)
    o_ref[0] = out.astype(o_ref.dtype)


def _memattn(he, kv, layer, tq=MEMATTN_ROWS):
    bsz, seq, _ = he.shape
    mlen = kv.shape[1]
    return pl.pallas_call(
        _memattn_kernel,
        grid=(bsz, seq // tq),
        in_specs=[
            pl.BlockSpec((1, tq, WIDTH), lambda b, i: (b, i, 0)),
            pl.BlockSpec((1, mlen, 2 * WIDTH), lambda b, i: (b, 0, layer)),
        ],
        out_specs=pl.BlockSpec((1, tq, WIDTH), lambda b, i: (b, i, 0)),
        out_shape=jax.ShapeDtypeStruct((bsz, seq, WIDTH), BF16),
        compiler_params=_cparams(("parallel", "parallel")),
        name="memattn",
    )(he, kv)


def _sigmoid(x):
    return 0.5 * jnp.tanh(0.5 * x) + 0.5


def _out_kernel(oa, ob, oc, od, oe, z_ref, ml_ref, x_ref, wb_ref, wo_ref, g_ref, b_ref, y_ref, yb_ref, *, alpha):
    tm = x_ref.shape[0]
    acc = jnp.zeros((tm, D_MODEL), F32)
    for n, o_ref in enumerate((oa, ob, oc, od, oe)):
        z = z_ref[:, n * WIDTH:(n + 1) * WIDTH]
        gated = o_ref[...] * (z * _sigmoid(z))
        yn = _dot(gated, wb_ref[n])
        acc = acc + _sigmoid(ml_ref[:, n * D_MODEL:(n + 1) * D_MODEL]).astype(F32) * yn
    r = alpha * x_ref[...] + _dot(acc.astype(BF16), wo_ref[...])
    mu = jnp.mean(r, axis=-1, keepdims=True)
    rc = r - mu
    var = jnp.mean(rc * rc, axis=-1, keepdims=True)
    y = rc * lax.rsqrt(var + LN_EPS) * g_ref[...] + b_ref[...]
    y_ref[...] = y
    yb_ref[...] = y.astype(BF16)


def _out(branches, z, ml, x, w_branch, w_out, ln_g, ln_b, alpha, tm=OUT_ROWS):
    m = x.shape[0]
    row = lambda n: pl.BlockSpec((tm, n), lambda i: (i, 0))
    return pl.pallas_call(
        functools.partial(_out_kernel, alpha=alpha),
        grid=(m // tm,),
        in_specs=[row(WIDTH)] * N_BRANCH + [row(N_BRANCH * WIDTH), row(N_BRANCH * D_MODEL), row(D_MODEL),
                  pl.BlockSpec((N_BRANCH, WIDTH, D_MODEL), lambda i: (0, 0, 0)),
                  pl.BlockSpec((D_MODEL, D_MODEL), lambda i: (0, 0)),
                  pl.BlockSpec((1, D_MODEL), lambda i: (0, 0)),
                  pl.BlockSpec((1, D_MODEL), lambda i: (0, 0))],
        out_specs=[row(D_MODEL), row(D_MODEL)],
        out_shape=[jax.ShapeDtypeStruct((m, D_MODEL), F32), jax.ShapeDtypeStruct((m, D_MODEL), BF16)],
        compiler_params=_cparams(("parallel",)),
        name="out",
    )(*branches, z, ml, x, w_branch, w_out, ln_g, ln_b)


def _split_weights(w_in, b_in):
    scale = np.ones((HB_COLS,), np.float32)
    for cb in (CB_AQ, CB_CQ):
        scale[cb * LANES:cb * LANES + WIDTH] = HEAD_DIM ** -0.5 * LOG2E
    scale[CB_BQ * LANES:CB_BQ * LANES + 3 * WIDTH] = HEAD_DIM ** -0.5
    sc = np.concatenate([np.arange(o, o + HEADS) for o in (O_CF, O_DBETA, O_DDECAY)])
    pad = jnp.zeros(w_in.shape[:2] + (LANES - 3 * HEADS,), w_in.dtype)
    w_hf = jnp.concatenate([w_in[..., sc], pad, w_in[..., O_D:O_D + 3 * WIDTH]], axis=-1)
    b_hf = jnp.concatenate([b_in[..., sc], pad[:, 0], b_in[..., O_D:O_D + 3 * WIDTH]], axis=-1)
    span = lambda a, o, n: a[..., o:o + n]
    groups = {
        "hb": (span(w_in, 0, HB_COLS), span(b_in, 0, HB_COLS), scale),
        "he": (span(w_in, O_E, WIDTH), span(b_in, O_E, WIDTH), np.full((WIDTH,), MEM_HEAD_DIM ** -0.5, np.float32)),
        "hf": (w_hf, b_hf, np.ones((HF_COLS,), np.float32)),
        "z": (span(w_in, O_Z, N_BRANCH * WIDTH), span(b_in, O_Z, N_BRANCH * WIDTH), np.ones((N_BRANCH * WIDTH,), np.float32)),
        "ml": (span(w_in, O_MERGE, N_BRANCH * D_MODEL), span(b_in, O_MERGE, N_BRANCH * D_MODEL),
               np.ones((N_BRANCH * D_MODEL,), np.float32)),
    }
    return {k: (w.astype(F32), b.astype(F32)[:, None, :], jnp.asarray(s)[None, :]) for k, (w, b, s) in groups.items()}


def _layer(x, xb, kv, layer, gw, conv_w, a_log, dt_bias, gdn_norm_w, w_branch, w_out, ln_g, ln_b, alpha):
    bsz, seq, d = x.shape
    m = bsz * seq
    xf = x.reshape(m, d)
    proj = lambda name, dtype: _proj(xb, *gw[name], layer, dtype, PROJ_ROWS, PROJ_COLS[name])
    hb = proj("hb", BF16).reshape(bsz, seq, HB_COLS)
    he = proj("he", BF16).reshape(bsz, seq, WIDTH)
    hf = proj("hf", F32).reshape(bsz, seq, HF_COLS)
    z = proj("z", BF16)
    ml = proj("ml", BF16)

    o_a = _moba(hb)
    o_b = _dilated(hb)
    o_c = _fox(hb, _fcum(hf))
    o_d = _gdn(hf, conv_w, a_log, dt_bias, gdn_norm_w)
    o_e = _memattn(he, kv, layer)
    branches = [o.reshape(m, WIDTH) for o in (o_a, o_b, o_c, o_d, o_e)]
    y, yb = _out(branches, z, ml, xf, w_branch.astype(BF16), w_out.astype(BF16),
                 ln_g.astype(F32)[None, :], ln_b.astype(F32)[None, :], alpha)
    return y.reshape(bsz, seq, d), yb


def kernel(x, mem, mem_ln_g, mem_ln_b, w_in, b_in, conv_w, a_log, dt_bias, gdn_norm_w, w_mem_kv, w_branch, w_out, ln_g, ln_b):
    depth = w_in.shape[0]
    alpha = float((2 * depth) ** 0.25)
    w_kv = jnp.concatenate([w_mem_kv[l] for l in range(depth)], axis=1).astype(BF16)
    kv = _memkv(mem.astype(F32), mem_ln_g.astype(F32)[None, :], mem_ln_b.astype(F32)[None, :], w_kv)
    x = x.astype(F32)
    xb = x.reshape(-1, x.shape[-1]).astype(BF16)
    gw = _split_weights(w_in, b_in)
    for l in range(depth):
        x, xb = _layer(x, xb, kv, l, gw, conv_w[l], a_log[l], dt_bias[l], gdn_norm_w[l],
                       w_branch[l], w_out[l], ln_g[l], ln_b[l], alpha)
    return x
```

```python
import functools

import jax
import jax.numpy as jnp
import numpy as np
from jax import lax
from jax.experimental import pallas as pl
from jax.experimental.pallas import tpu as pltpu

F32 = jnp.float32
BF16 = jnp.bfloat16
HI = lax.Precision.HIGHEST

D_MODEL = 1024
HEAD_DIM = 64
HEADS = 6
WIDTH = HEADS * HEAD_DIM
N_BRANCH = 5
LANES = 128
PAIRS = WIDTH // LANES
MOBA_BLOCK = 256
MOBA_TOPK = 3
DILATIONS = (1, 4, 16)
BAND = 128
BAND_TILE = 2048
BAND_GROUP = 16
GDN_CHUNK = 128
CONV_K = 4
MEM_HEADS = 4
MEM_HEAD_DIM = WIDTH // MEM_HEADS
NEG = -1e30
LOG2E = float(np.log2(np.e))
FLASH_UNDERFLOW = 160.0
LN_EPS = 1e-5
RMS_EPS = 1e-6

_SPLIT = (3 * WIDTH, 6 * WIDTH, WIDTH, 3 * WIDTH, HEADS, 3 * WIDTH, HEADS, HEADS, WIDTH, N_BRANCH * WIDTH, N_BRANCH * D_MODEL)
_OFF = tuple(int(v) for v in np.concatenate([[0], np.cumsum(_SPLIT)]))
(O_A, O_BQK, O_BV, O_C, O_CF, O_D, O_DBETA, O_DDECAY, O_E, O_Z, O_MERGE, _) = _OFF

CB_AQ, CB_AK, CB_AV = 0, 3, 6
CB_BQ, CB_BK, CB_BV = 9, 18, 27
CB_CQ, CB_CK, CB_CV = 30, 33, 36
HB_COLS = 39 * LANES
assert (O_A, O_BQK, O_BV, O_C, O_CF) == tuple(LANES * c for c in (CB_AQ, CB_BQ, CB_BV, CB_CQ, 39))
HF_COLS = LANES + 3 * WIDTH

VMEM_LIMIT = 56 * 1024 * 1024

PROJ_ROWS = 2048
PROJ_COLS = {"hb": HB_COLS // 3, "he": WIDTH, "hf": HF_COLS, "z": N_BRANCH * WIDTH, "ml": 1024}
FOX_ROWS = 512
MOBA_ROWS = 2 * MOBA_BLOCK
FLASH_BUILD_ROWS = 512
BAND_MERGE_ROWS = 256
GDN_ROWS = 4 * GDN_CHUNK
MEMATTN_ROWS = 512
OUT_ROWS = 512
TRI_BASE = 16
BOUND_REL, BOUND_ABS = 1.01, 1.0
FAR = 1e9
TAKEN = -3e38


def _cparams(sem):
    return pltpu.CompilerParams(dimension_semantics=sem, vmem_limit_bytes=VMEM_LIMIT)


def _dot(a, b):
    return jnp.dot(a, b, preferred_element_type=F32)


def _dot_nt(a, b):
    return lax.dot_general(a, b, (((1,), (1,)), ((), ())), preferred_element_type=F32)


def _dot_nt_hi(a, b):
    return lax.dot_general(a, b, (((1,), (1,)), ((), ())), preferred_element_type=F32, precision=HI)


def _iota(shape, dim):
    return lax.broadcasted_iota(jnp.int32, shape, dim)


def _proj_kernel(x_ref, w_ref, b_ref, s_ref, o_ref, wb_ref):
    @pl.when(pl.program_id(1) == 0)
    def _():
        wb_ref[...] = w_ref[...].astype(BF16)

    acc = _dot(x_ref[...], wb_ref[...])
    o_ref[...] = ((acc + b_ref[...]) * s_ref[...]).astype(o_ref.dtype)


def _proj(xb, w, b, scale, layer, out_dtype, tm, tn):
    m, k = xb.shape
    n = b.shape[-1]
    assert m % tm == 0 and n % tn == 0
    return pl.pallas_call(
        _proj_kernel,
        grid=(n // tn, m // tm),
        in_specs=[
            pl.BlockSpec((tm, k), lambda j, i: (i, 0)),
            pl.BlockSpec((None, k, tn), lambda j, i: (layer, 0, j)),
            pl.BlockSpec((None, 1, tn), lambda j, i: (layer, 0, j)),
            pl.BlockSpec((1, tn), lambda j, i: (0, j)),
        ],
        out_specs=pl.BlockSpec((tm, tn), lambda j, i: (i, j)),
        out_shape=jax.ShapeDtypeStruct((m, n), out_dtype),
        scratch_shapes=[pltpu.VMEM((k, tn), BF16)],
        compiler_params=_cparams(("parallel", "arbitrary")),
        name="proj",
    )(xb, w, b, scale)


def _memkv_kernel(mem_ref, g_ref, b_ref, w_ref, o_ref):
    x = mem_ref[0]
    mu = jnp.mean(x, axis=-1, keepdims=True)
    xc = x - mu
    var = jnp.mean(xc * xc, axis=-1, keepdims=True)
    y = xc * lax.rsqrt(var + LN_EPS) * g_ref[...] + b_ref[...]
    o_ref[0] = _dot(y.astype(BF16), w_ref[...]).astype(o_ref.dtype)


def _memkv(mem, g, b, w):
    bsz, mlen, d = mem.shape
    n = w.shape[1]
    return pl.pallas_call(
        _memkv_kernel,
        grid=(bsz,),
        in_specs=[
            pl.BlockSpec((1, mlen, d), lambda i: (i, 0, 0)),
            pl.BlockSpec((1, d), lambda i: (0, 0)),
            pl.BlockSpec((1, d), lambda i: (0, 0)),
            pl.BlockSpec((d, n), lambda i: (0, 0)),
        ],
        out_specs=pl.BlockSpec((1, mlen, n), lambda i: (i, 0, 0)),
        out_shape=jax.ShapeDtypeStruct((bsz, mlen, n), BF16),
        compiler_params=_cparams(("parallel",)),
        name="memkv",
    )(mem, g, b, w)


def _fcum_kernel(h_ref, o_ref, *, blk):
    seq = h_ref.shape[1]
    tri = (_iota((blk, blk), 0) >= _iota((blk, blk), 1)).astype(BF16)

    def body(i, carry):
        rows = pl.ds(pl.multiple_of(i * blk, blk), blk)
        logf = jax.nn.log_sigmoid(h_ref[0, rows, :])
        c = _dot01(tri, logf) + carry
        o_ref[0, rows, :] = c * LOG2E
        return c[blk - 1:blk, :]

    lax.fori_loop(0, seq // blk, body, jnp.zeros((1, LANES), F32))


def _fcum(hf):
    bsz, seq, _ = hf.shape
    return pl.pallas_call(
        functools.partial(_fcum_kernel, blk=FLASH_BUILD_ROWS),
        grid=(bsz,),
        in_specs=[pl.BlockSpec((1, seq, LANES), lambda b: (b, 0, 0))],
        out_specs=pl.BlockSpec((1, seq, LANES), lambda b: (b, 0, 0)),
        out_shape=jax.ShapeDtypeStruct((bsz, seq, LANES), F32),
        compiler_params=_cparams(("parallel",)),
        name="fcum",
    )(hf)


def _head_masks(rows):
    lane = _iota((rows, LANES), 1)
    return lane < HEAD_DIM


def _split3(x):
    hi = x.astype(BF16).astype(F32)
    r = x - hi
    mid = r.astype(BF16).astype(F32)
    return hi, mid, r - mid


def _flash_scratch(tq):
    return [pltpu.VMEM((2 * tq, LANES), F32), pltpu.VMEM((2 * tq, tq), F32), pltpu.VMEM((2 * tq, tq), F32),
            pltpu.VMEM((2 * tq, LANES), F32), pltpu.VMEM((2 * tq, LANES), F32)]


def _augment_values(vx_ref, rows, v):
    vx_ref[rows, 0:LANES] = v
    vx_ref[rows, LANES:] = jnp.ones((v.shape[0], LANES), BF16)


def _flash_causal(qx, kx_ref, vx_ref, scratch, i, tq, past_tiles=None):
    acc_ref, sa_ref, sb_ref, m_ref, l_ref = scratch

    def logits(kt, dst):
        rows = pl.ds(pl.multiple_of(kt * tq, tq), tq)
        dst[...] = _dot_nt(qx, kx_ref[rows, :])

    r = _iota((2 * tq, tq), 0)
    causal = jnp.where(r >= tq, r - tq, r) >= _iota((2 * tq, tq), 1)

    def finish(src, kt, masked=False):
        rows = pl.ds(pl.multiple_of(kt * tq, tq), tq)
        s = src[...]
        if masked:
            s = jnp.where(causal, s, NEG)
        m = m_ref[...]
        m_new = jnp.maximum(m, jnp.max(s, axis=-1, keepdims=True))
        alpha = jnp.exp2(m - m_new)
        p = jnp.exp2((s - jnp.tile(m_new, (1, tq // LANES))).astype(BF16))
        pv = _dot(p, vx_ref[rows, :])
        l_ref[...] = alpha * l_ref[...] + pv[:, LANES:]
        m_ref[...] = m_new
        acc_ref[...] = alpha * acc_ref[...] + pv[:, 0:LANES]

    acc_ref[...] = jnp.zeros_like(acc_ref)
    m_ref[...] = jnp.full((2 * tq, LANES), NEG, F32)
    l_ref[...] = jnp.zeros((2 * tq, LANES), F32)
    tile = lambda u: jnp.maximum(i - 1 - u, 0)
    logits(i, sa_ref)
    logits(tile(0), sb_ref)
    finish(sa_ref, i, masked=True)
    n = i if past_tiles is None else past_tiles(m_ref)

    def body(v, carry):
        logits(tile(2 * v + 1), sa_ref)
        finish(sb_ref, tile(2 * v))
        logits(tile(2 * v + 2), sb_ref)
        finish(sa_ref, tile(2 * v + 1))
        return carry

    lax.fori_loop(0, n // 2, body, 0)

    @pl.when(n % 2 == 1)
    def _():
        finish(sb_ref, tile(n - 1))

    o = acc_ref[...] / l_ref[...]
    return jnp.where(_head_masks(tq), o[0:tq], o[tq:])


def _head_sumsq(x):
    hm = _head_masks(x.shape[0])
    sq = x * x
    return (jnp.sum(jnp.where(hm, sq, 0.0), axis=1, keepdims=True), jnp.sum(jnp.where(hm, 0.0, sq), axis=1, keepdims=True))


def _fox_kernel(q_ref, k_ref, v_ref, f_ref, o_ref, kx_ref, vx_ref, kn_ref, *flash, tq):
    p = pl.program_id(1)
    i = pl.program_id(2)
    seq = k_ref.shape[1]
    bt = FLASH_BUILD_ROWS

    @pl.when(i == 0)
    def _():
        src = _iota((LANES, LANES), 0)
        dst = _iota((LANES, LANES), 1)
        pm = jnp.logical_or(jnp.logical_and(src == 2 * p, dst < 3),
                            jnp.logical_and(src == 2 * p + 1, jnp.logical_and(dst >= 3, dst < 6))).astype(BF16)
        sub = _iota((bt, LANES), 1) % 3

        def build(c, carry):
            rows = pl.ds(pl.multiple_of(c * bt, bt), bt)
            hi, mid, lo = _split3(_dot01(f_ref[0, rows, :], pm))
            k = k_ref[0, rows, :]
            kx_ref[rows, 0:LANES] = k
            kx_ref[rows, LANES:] = jnp.where(sub == 0, hi, jnp.where(sub == 1, mid, lo)).astype(BF16)
            _augment_values(vx_ref, rows, v_ref[0, rows, :])
            return tuple(jnp.maximum(c0, jnp.max(s, axis=0, keepdims=True))
                         for c0, s in zip(carry, _head_sumsq(k.astype(F32))))

        zero = jnp.zeros((1, 1), F32)
        kn = lax.fori_loop(0, seq // bt, build, (zero, zero))
        kn_ref[0:1, :] = jnp.broadcast_to(kn[0], (1, LANES))
        kn_ref[1:2, :] = jnp.broadcast_to(kn[1], (1, LANES))

    q = q_ref[0].astype(F32)
    lane = _iota((tq, LANES), 1)
    hm = lane < HEAD_DIM
    top = jnp.concatenate([jnp.where(hm, q, 0.0), jnp.where(lane < 3, -1.0, 0.0)], axis=1)
    bot = jnp.concatenate([jnp.where(hm, 0.0, q), jnp.where(jnp.logical_and(lane >= 3, lane < 6), -1.0, 0.0)], axis=1)
    qx = jnp.concatenate([top, bot], axis=0).astype(BF16)

    def past_tiles(m_ref):
        nt = seq // tq
        fend = f_ref[0, pl.ds(tq - 1, nt, stride=tq), :]
        lane_t = _iota((nt, LANES), 1)
        tpos = _iota((nt, 1), 0)
        m = m_ref[...]
        need = tpos < 0
        for h, qn2 in enumerate(_head_sumsq(q)):
            m_min = jnp.min(m[h * tq:(h + 1) * tq, 0:1], axis=0, keepdims=True)
            qk = jnp.sqrt(jnp.max(qn2, axis=0, keepdims=True) * kn_ref[h:h + 1, 0:1]) * BOUND_REL + BOUND_ABS
            f_h = jnp.sum(jnp.where(lane_t == 2 * p + h, fend, 0.0), axis=1, keepdims=True)
            need = jnp.logical_or(need, qk - f_h > m_min - FLASH_UNDERFLOW)
        first = jnp.min(jnp.where(jnp.logical_and(need, tpos < i), tpos, i).astype(F32))
        return i - first.astype(jnp.int32)

    o_ref[0] = _flash_causal(qx, kx_ref, vx_ref, flash, i, tq, past_tiles).astype(o_ref.dtype)


def _fox(hb, fcol, tq=FOX_ROWS):
    bsz, seq, _ = hb.shape
    assert seq % tq == 0
    return pl.pallas_call(
        functools.partial(_fox_kernel, tq=tq),
        grid=(bsz, PAIRS, seq // tq),
        in_specs=[
            pl.BlockSpec((1, tq, LANES), lambda b, p, i: (b, i, CB_CQ + p)),
            pl.BlockSpec((1, seq, LANES), lambda b, p, i: (b, 0, CB_CK + p)),
            pl.BlockSpec((1, seq, LANES), lambda b, p, i: (b, 0, CB_CV + p)),
            pl.BlockSpec((1, seq, LANES), lambda b, p, i: (b, 0, 0)),
        ],
        out_specs=pl.BlockSpec((1, tq, LANES), lambda b, p, i: (b, i, p)),
        out_shape=jax.ShapeDtypeStruct((bsz, seq, WIDTH), BF16),
        scratch_shapes=[pltpu.VMEM((seq, 2 * LANES), BF16), pltpu.VMEM((seq, 2 * LANES), BF16),
                        pltpu.VMEM((8, LANES), F32)] + _flash_scratch(tq),
        compiler_params=_cparams(("parallel", "parallel", "arbitrary")),
        name="fox",
    )(hb, hb, hb, fcol)


def _moba_kernel(q_ref, k_ref, v_ref, o_ref, kx_ref, vx_ref, kmean_ref, *flash, tq):
    i = pl.program_id(2)
    seq = k_ref.shape[1]
    blk = MOBA_BLOCK

    @pl.when(i == 0)
    def _():
        kmean_ref[...] = jnp.zeros_like(kmean_ref)
        lane = _iota((blk, LANES), 1)

        def build(n, carry):
            rows = pl.ds(pl.multiple_of(n * blk, blk), blk)
            k = k_ref[0, rows, :]
            kx_ref[rows, 0:LANES] = k
            kx_ref[rows, LANES:] = jnp.where(lane == n, 1.0, 0.0).astype(BF16)
            _augment_values(vx_ref, rows, v_ref[0, rows, :])
            kmean_ref[pl.ds(n, 1), :] = jnp.sum(k.astype(F32), axis=0, keepdims=True) * (1.0 / blk)
            return carry

        lax.fori_loop(0, seq // blk, build, 0)

    q = q_ref[0].astype(F32)
    hm = _head_masks(tq)
    nbp = -(-(seq // blk) // 8) * 8
    blk_id = _iota((nbp, tq), 0)
    blk_f = blk_id.astype(F32)
    own = i * (tq // blk) + _iota((nbp, tq), 1) // blk
    valid = blk_id < own
    kmean = kmean_ref[0:nbp, :]
    halves = []
    qhs = (jnp.where(hm, q, 0.0), jnp.where(hm, 0.0, q))
    gates = [_dot_nt_hi(kmean, qh) for qh in qhs]
    for qh, gate in zip(qhs, gates):
        g = jnp.where(valid, gate, NEG)
        bias = jnp.where(blk_id == own, 0.0, NEG)
        for _ in range(MOBA_TOPK):
            mx = jnp.max(g, axis=0, keepdims=True)
            first = jnp.min(jnp.where(g == mx, blk_f, FAR), axis=0, keepdims=True)
            pick = blk_f == first
            bias = jnp.where(jnp.logical_and(pick, valid), 0.0, bias)
            g = jnp.where(pick, TAKEN, g)
        bias = jnp.concatenate([bias, jnp.zeros((LANES - nbp, tq), F32)], axis=0).T
        halves.append(jnp.concatenate([qh, bias], axis=1))
    qx = jnp.concatenate(halves, axis=0).astype(BF16)
    o_ref[0] = _flash_causal(qx, kx_ref, vx_ref, flash, i, tq).astype(o_ref.dtype)


def _moba(hb, tq=MOBA_ROWS):
    bsz, seq, _ = hb.shape
    assert seq % tq == 0 and tq % MOBA_BLOCK == 0 and seq // MOBA_BLOCK <= LANES
    return pl.pallas_call(
        functools.partial(_moba_kernel, tq=tq),
        grid=(bsz, PAIRS, seq // tq),
        in_specs=[
            pl.BlockSpec((1, tq, LANES), lambda b, p, i: (b, i, CB_AQ + p)),
            pl.BlockSpec((1, seq, LANES), lambda b, p, i: (b, 0, CB_AK + p)),
            pl.BlockSpec((1, seq, LANES), lambda b, p, i: (b, 0, CB_AV + p)),
        ],
        out_specs=pl.BlockSpec((1, tq, LANES), lambda b, p, i: (b, i, p)),
        out_shape=jax.ShapeDtypeStruct((bsz, seq, WIDTH), BF16),
        scratch_shapes=[pltpu.VMEM((seq, 2 * LANES), BF16), pltpu.VMEM((seq, 2 * LANES), BF16),
                        pltpu.VMEM((LANES, LANES), F32)] + _flash_scratch(tq),
        compiler_params=_cparams(("parallel", "parallel", "arbitrary")),
        name="moba",
    )(hb, hb, hb)


def _band_kernel(q0, q1, q2, k0c, k0p, k1c, k1p, k2c, k2p, vc, vp, o_ref, qb, kb, vb, nb, mb, sb, *, tile):
    jt = pl.program_id(2)
    for g, (qr, kc, kp) in enumerate(((q0, k0c, k0p), (q1, k1c, k1p), (q2, k2c, k2p))):
        qb[g] = qr[0].astype(F32)
        kb[g, 0:tile, :] = kp[0].astype(F32)
        kb[g, tile:, :] = kc[0].astype(F32)
    vb[0:tile, :] = vp[0].astype(F32)
    vb[tile:, :] = vc[0].astype(F32)

    hm = _head_masks(BAND)
    qi = _iota((BAND, 2 * BAND), 0)
    kj = _iota((BAND, 2 * BAND), 1)
    dist = BAND + qi - kj
    band = jnp.logical_and(dist >= 0, dist <= BAND)
    cur_half = kj >= BAND
    nblocks = tile // BAND

    for g, dil in enumerate(DILATIONS):
        per_stream = nblocks // dil

        def step(it, carry, g=g, dil=dil, per_stream=per_stream):
            starts, kstarts, valids = [], [], []
            for j in range(BAND_GROUP):
                idx = it * BAND_GROUP + j
                n = idx % per_stream
                start = idx // per_stream + n * (BAND * dil)
                starts.append(start)
                kstarts.append(tile + start - BAND * dil)
                valids.append(jnp.logical_and(band, jnp.logical_or(cur_half, jnp.logical_or(jt > 0, n > 0))))
            qs = [qb[g, pl.ds(s, BAND, stride=dil), :] for s in starts]
            kks = [kb[g, pl.ds(s, 2 * BAND, stride=dil), :].astype(BF16) for s in kstarts]
            ones = jnp.ones((2 * BAND, LANES), BF16)
            vvs = [jnp.concatenate([vb[pl.ds(s, 2 * BAND, stride=dil), :].astype(BF16), ones], axis=1) for s in kstarts]
            logits = []
            for q, kk, valid in zip(qs, kks, valids):
                for h in range(2):
                    qh = (jnp.where(hm, q, 0.0) if h == 0 else jnp.where(hm, 0.0, q)).astype(BF16)
                    logits.append(jnp.where(valid, _dot_nt(qh, kk), NEG))
            ms = [jnp.max(s, axis=-1, keepdims=True) for s in logits]
            ps = [jnp.exp((s - m).astype(BF16)) for s, m in zip(logits, ms)]
            pvs = [_dot(p, vvs[i // 2]) for i, p in enumerate(ps)]
            nums = [pv[:, 0:LANES] for pv in pvs]
            ss = [pv[:, LANES:] for pv in pvs]
            for j, s in enumerate(starts):
                rows = pl.ds(s, BAND, stride=dil)
                nb[g, rows, :] = jnp.where(hm, nums[2 * j], nums[2 * j + 1])
                mb[g, rows, :] = jnp.where(hm, ms[2 * j], ms[2 * j + 1])
                sb[g, rows, :] = jnp.where(hm, ss[2 * j], ss[2 * j + 1])
            return carry

        lax.fori_loop(0, nblocks // BAND_GROUP, step, 0)

    ch = BAND_MERGE_ROWS

    def merge(i, carry):
        rows = pl.ds(pl.multiple_of(i * ch, ch), ch)
        m_all = jnp.maximum(jnp.maximum(mb[0, rows, :], mb[1, rows, :]), mb[2, rows, :])
        num = jnp.zeros((ch, LANES), F32)
        den = jnp.zeros((ch, LANES), F32)
        for g in range(len(DILATIONS)):
            w = jnp.exp(mb[g, rows, :] - m_all)
            num = num + nb[g, rows, :] * w
            den = den + sb[g, rows, :] * w
        o_ref[0, rows, :] = (num / den).astype(o_ref.dtype)
        return carry

    lax.fori_loop(0, tile // ch, merge, 0)


def _dilated(hb, tile=BAND_TILE):
    bsz, seq, _ = hb.shape
    ng = len(DILATIONS)
    nblocks = tile // BAND
    assert seq % tile == 0 and nblocks % BAND_GROUP == 0 and all(nblocks % d == 0 for d in DILATIONS)
    cur = lambda cb: pl.BlockSpec((1, tile, LANES), lambda b, p, j: (b, j, cb + p))
    prv = lambda cb: pl.BlockSpec((1, tile, LANES), lambda b, p, j: (b, jnp.maximum(j - 1, 0), cb + p))
    in_specs = [cur(CB_BQ + PAIRS * g) for g in range(ng)]
    for g in range(ng):
        in_specs += [cur(CB_BK + PAIRS * g), prv(CB_BK + PAIRS * g)]
    in_specs += [cur(CB_BV), prv(CB_BV)]
    acc = pltpu.VMEM((ng, tile, LANES), F32)
    return pl.pallas_call(
        functools.partial(_band_kernel, tile=tile),
        grid=(bsz, PAIRS, seq // tile),
        in_specs=in_specs,
        out_specs=pl.BlockSpec((1, tile, LANES), lambda b, p, j: (b, j, p)),
        out_shape=jax.ShapeDtypeStruct((bsz, seq, WIDTH), BF16),
        scratch_shapes=[acc, pltpu.VMEM((ng, 2 * tile, LANES), F32), pltpu.VMEM((2 * tile, LANES), F32), acc, acc, acc],
        compiler_params=_cparams(("parallel", "parallel", "arbitrary")),
        name="band",
    )(*([hb] * len(in_specs)))


def _dot01(a, b, nt=False, pieces=3):
    f = _dot_nt if nt else _dot
    if a.dtype == BF16:
        return sum(f(a, piece.astype(BF16)) for piece in _split3(b)[:pieces])
    return sum(f(piece.astype(BF16), b) for piece in _split3(a)[:pieces])


def _tri_inverse_all(ms):
    c = ms[0].shape[0]
    ri = _iota((c, c), 0)
    ci = _iota((c, c), 1)
    base = TRI_BASE
    inblk = ri // base == ci // base
    eye = jnp.where(ri == ci, 1.0, 0.0)
    ps = [jnp.where(inblk, -m, 0.0) for m in ms]
    ts = [eye + p for p in ps]
    for _ in range(3):
        pbs = [p.astype(BF16) for p in ps]
        ps = [_dot(pb, pb) for pb in pbs]
        ts = [t + _dot(t.astype(BF16), p.astype(BF16)) for t, p in zip(ts, ps)]
    size = base
    while size < c:
        lower = jnp.logical_and(ri // (2 * size) == ci // (2 * size), ri // size != ci // size)
        tbs = [t.astype(BF16) for t in ts]
        xs = [_dot(tb, jnp.where(lower, m, 0.0).astype(BF16)) for tb, m in zip(tbs, ms)]
        ts = [t - _dot(x.astype(BF16), tb) for t, x, tb in zip(ts, xs, tbs)]
        size *= 2
    return ts


def _gdn_kernel(h_ref, cw_ref, alog_ref, dtb_ref, nw_ref, o_ref,
                xe_ref, u_s, w_s, qd_s, kd_s, qk_s, egl_s, o_s, st_ref, *, tile):
    cc = GDN_CHUNK
    nc = tile // cc
    t = pl.program_id(1)

    @pl.when(t == 0)
    def _():
        xe_ref[0:8, :] = jnp.zeros((8, 3 * WIDTH), F32)
        st_ref[...] = jnp.zeros_like(st_ref)

    x = h_ref[0, :, LANES:]
    xe_ref[8:8 + tile, :] = x
    y = jnp.zeros((tile, 3 * WIDTH), F32)
    for j in range(CONV_K):
        y = y + cw_ref[j:j + 1, :] * xe_ref[8 - (CONV_K - 1) + j:8 - (CONV_K - 1) + j + tile, :]
    xe_ref[0:8, :] = x[tile - 8:tile, :]
    y = y * _sigmoid(y)

    bd = ((_iota((WIDTH, WIDTH), 0) // HEAD_DIM) == (_iota((WIDTH, WIDTH), 1) // HEAD_DIM)).astype(BF16)
    q = y[:, 0:WIDTH]
    k = y[:, WIDTH:2 * WIDTH]
    v = y[:, 2 * WIDTH:]
    q = q * lax.rsqrt(_dot01(q * q, bd, pieces=2) + RMS_EPS) * (HEAD_DIM ** -0.5)
    k = k * lax.rsqrt(_dot01(k * k, bd, pieces=2) + RMS_EPS)

    hs = h_ref[0, :, 0:LANES]
    er = _iota((LANES, WIDTH), 0)
    ec = _iota((LANES, WIDTH), 1) // HEAD_DIM
    beta = _sigmoid(_dot01(hs, (er == ec + HEADS).astype(BF16), pieces=2))
    g = -jnp.exp(alog_ref[...]) * jax.nn.softplus(_dot01(hs, (er == ec + 2 * HEADS).astype(BF16), pieces=2) + dtb_ref[...])

    tri = (_iota((cc, cc), 0) >= _iota((cc, cc), 1)).astype(BF16)
    gc = jnp.concatenate([_dot01(tri, g[c * cc:(c + 1) * cc]) for c in range(nc)], axis=0)
    glast = jnp.broadcast_to(gc.reshape(nc, cc, WIDTH)[:, cc - 1:cc, :], (nc, cc, WIDTH)).reshape(tile, WIDTH)
    eg = jnp.exp(gc)
    kb = k * beta
    vb = v * beta
    wb = kb * eg
    qd_s[...] = q * eg
    kd_s[...] = k * jnp.exp(glast - gc)
    egl_s[...] = jnp.exp(glast)

    ri = _iota((cc, cc), 0)
    ci = _iota((cc, cc), 1)
    incl = ri >= ci
    strict = ri > ci
    lane = _iota((cc, LANES), 1)
    hm = lane < HEAD_DIM
    sls = [slice(p * LANES, (p + 1) * LANES) for p in range(PAIRS)]
    keeps = [hm, jnp.logical_not(hm)]
    heads = [(p, h) for p in range(PAIRS) for h in range(2)]
    chains = [(p, h, c) for p, h in heads for c in range(nc)]
    rws = [slice(c * cc, (c + 1) * cc) for c in range(nc)]
    kps = [k[:, sl].astype(BF16) for sl in sls]
    rhss = [jnp.concatenate([vb[:, sl], wb[:, sl]], axis=1).astype(BF16) for sl in sls]
    gct = gc.T
    rowf = {(p, h): gct[p * LANES + h * HEAD_DIM:p * LANES + h * HEAD_DIM + 1, :] for p, h in heads}
    kbm = {(p, h): jnp.where(jnp.tile(keeps[h], (nc, 1)), kb[:, sls[p]], 0.0).astype(BF16) for p, h in heads}
    qm = {(p, h): jnp.where(jnp.tile(keeps[h], (nc, 1)), q[:, sls[p]], 0.0).astype(BF16) for p, h in heads}
    kks = [_dot_nt(kbm[p, h][rws[c]], kps[p][rws[c]]) for p, h, c in chains]
    qks = [_dot_nt(qm[p, h][rws[c]], kps[p][rws[c]]) for p, h, c in chains]
    mms = []
    for (p, h, c), kk, qk in zip(chains, kks, qks):
        col = p * LANES + h * HEAD_DIM
        diff = gc[rws[c], col:col + 1] - rowf[p, h][:, rws[c]]
        decay = jnp.where(incl, jnp.exp(jnp.where(incl, diff, 0.0)), 0.0)
        mms.append(jnp.where(strict, kk * decay, 0.0))
        qk_s[2 * p + h, rws[c], :] = qk * decay
    tinvs = _tri_inverse_all(mms)
    rs = {ch: _dot(tinv.astype(BF16), rhss[ch[0]][rws[ch[2]]]) for ch, tinv in zip(chains, tinvs)}
    for p in range(PAIRS):
        for c in range(nc):
            r0, r1 = rs[p, 0, c], rs[p, 1, c]
            u_s[rws[c], sls[p]] = jnp.where(hm, r0[:, :LANES], r1[:, :LANES])
            w_s[rws[c], sls[p]] = jnp.where(hm, r0[:, LANES:], r1[:, LANES:])

    bdiag = (_iota((LANES, LANES), 0) // HEAD_DIM) == (_iota((LANES, LANES), 1) // HEAD_DIM)
    for c in range(nc):
        rows = rws[c]
        sts = [st_ref[p] for p in range(PAIRS)]
        stbs = [st.astype(BF16) for st in sts]
        wss = [_dot(w_s[rows, sl].astype(BF16), stb) for sl, stb in zip(sls, stbs)]
        qss = [_dot(qd_s[rows, sl].astype(BF16), stb) for sl, stb in zip(sls, stbs)]
        vnbs = [(u_s[rows, sl] - ws).astype(BF16) for sl, ws in zip(sls, wss)]
        upds = [lax.dot_general(kd_s[rows, sl].astype(BF16), vnb, (((0,), (0,)), ((), ())), preferred_element_type=F32)
                for sl, vnb in zip(sls, vnbs)]
        intra = [(_dot(qk_s[2 * p, rows, :].astype(BF16), vnbs[p]), _dot(qk_s[2 * p + 1, rows, :].astype(BF16), vnbs[p]))
                 for p in range(PAIRS)]
        for p in range(PAIRS):
            st_ref[p] = sts[p] * egl_s[c * cc:c * cc + 1, sls[p]] + jnp.where(bdiag, upds[p], 0.0)
            o_s[rows, sls[p]] = qss[p] + jnp.where(hm, intra[p][0], intra[p][1])

    o = o_s[...]
    ms = _dot01(o * o, bd, pieces=2) * (1.0 / HEAD_DIM)
    o_ref[0] = (o * lax.rsqrt(ms + RMS_EPS) * nw_ref[...]).astype(o_ref.dtype)


def _gdn(hf, conv_w, a_log, dt_bias, norm_w, tile=GDN_ROWS):
    bsz, seq, _ = hf.shape
    rep = lambda a: jnp.repeat(a.astype(F32), HEAD_DIM)[None, :]
    wide = pltpu.VMEM((tile, WIDTH), F32)
    return pl.pallas_call(
        functools.partial(_gdn_kernel, tile=tile),
        grid=(bsz, seq // tile),
        in_specs=[
            pl.BlockSpec((1, tile, HF_COLS), lambda b, t: (b, t, 0)),
            pl.BlockSpec((CONV_K, 3 * WIDTH), lambda b, t: (0, 0)),
            pl.BlockSpec((1, WIDTH), lambda b, t: (0, 0)),
            pl.BlockSpec((1, WIDTH), lambda b, t: (0, 0)),
            pl.BlockSpec((1, WIDTH), lambda b, t: (0, 0)),
        ],
        out_specs=pl.BlockSpec((1, tile, WIDTH), lambda b, t: (b, t, 0)),
        out_shape=jax.ShapeDtypeStruct((bsz, seq, WIDTH), BF16),
        scratch_shapes=[pltpu.VMEM((tile + 8, 3 * WIDTH), F32), wide, wide, wide, wide,
                        pltpu.VMEM((HEADS, tile, LANES), F32), wide, wide,
                        pltpu.VMEM((PAIRS, LANES, LANES), F32)],
        compiler_params=_cparams(("parallel", "arbitrary")),
        name="gdn",
    )(hf, conv_w.astype(F32), rep(a_log), rep(dt_bias), jnp.tile(norm_w.astype(F32), HEADS)[None, :])


def _memattn_kernel(q_ref, kv_ref, o_ref):
    tq = q_ref.shape[1]
    q = q_ref[0]
    k = kv_ref[0, :, 0:WIDTH]
    v = kv_ref[0, :, WIDTH:2 * WIDTH]
    head = _iota((tq, WIDTH), 1) // MEM_HEAD_DIM
    zero = jnp.zeros_like(q)
    logits = [_dot_nt(jnp.where(head == h, q, zero), k) for h in range(MEM_HEADS)]
    ps = [jnp.exp(s - jnp.max(s, axis=-1, keepdims=True)) for s in logits]
    outs = [_dot(p.astype(BF16), v) / jnp.sum(p, axis=-1, keepdims=True) for p in ps]
    out = outs[0]
    for h in range(1, MEM_HEADS):
        out = jnp.where(head == h, outs[h], out)
    o_ref[0] = out.astype(o_ref.dtype)


def _memattn(he, kv, layer, tq=MEMATTN_ROWS):
    bsz, seq, _ = he.shape
    mlen = kv.shape[1]
    return pl.pallas_call(
        _memattn_kernel,
        grid=(bsz, seq // tq),
        in_specs=[
            pl.BlockSpec((1, tq, WIDTH), lambda b, i: (b, i, 0)),
            pl.BlockSpec((1, mlen, 2 * WIDTH), lambda b, i: (b, 0, layer)),
        ],
        out_specs=pl.BlockSpec((1, tq, WIDTH), lambda b, i: (b, i, 0)),
        out_shape=jax.ShapeDtypeStruct((bsz, seq, WIDTH), BF16),
        compiler_params=_cparams(("parallel", "parallel")),
        name="memattn",
    )(he, kv)


def _sigmoid(x):
    return 0.5 * jnp.tanh(0.5 * x) + 0.5


def _out_kernel(oa, ob, oc, od, oe, z_ref, ml_ref, x_ref, wb_ref, wo_ref, g_ref, b_ref, y_ref, yb_ref, *, alpha):
    tm = x_ref.shape[0]
    acc = jnp.zeros((tm, D_MODEL), F32)
    for n, o_ref in enumerate((oa, ob, oc, od, oe)):
        z = z_ref[:, n * WIDTH:(n + 1) * WIDTH]
        gated = o_ref[...] * (z * _sigmoid(z))
        yn = _dot(gated, wb_ref[n])
        acc = acc + _sigmoid(ml_ref[:, n * D_MODEL:(n + 1) * D_MODEL]).astype(F32) * yn
    r = alpha * x_ref[...] + _dot(acc.astype(BF16), wo_ref[...])
    mu = jnp.mean(r, axis=-1, keepdims=True)
    rc = r - mu
    var = jnp.mean(rc * rc, axis=-1, keepdims=True)
    y = rc * lax.rsqrt(var + LN_EPS) * g_ref[...] + b_ref[...]
    y_ref[...] = y
    yb_ref[...] = y.astype(BF16)


def _out(branches, z, ml, x, w_branch, w_out, ln_g, ln_b, alpha, tm=OUT_ROWS):
    m = x.shape[0]
    row = lambda n: pl.BlockSpec((tm, n), lambda i: (i, 0))
    return pl.pallas_call(
        functools.partial(_out_kernel, alpha=alpha),
        grid=(m // tm,),
        in_specs=[row(WIDTH)] * N_BRANCH + [row(N_BRANCH * WIDTH), row(N_BRANCH * D_MODEL), row(D_MODEL),
                  pl.BlockSpec((N_BRANCH, WIDTH, D_MODEL), lambda i: (0, 0, 0)),
                  pl.BlockSpec((D_MODEL, D_MODEL), lambda i: (0, 0)),
                  pl.BlockSpec((1, D_MODEL), lambda i: (0, 0)),
                  pl.BlockSpec((1, D_MODEL), lambda i: (0, 0))],
        out_specs=[row(D_MODEL), row(D_MODEL)],
        out_shape=[jax.ShapeDtypeStruct((m, D_MODEL), F32), jax.ShapeDtypeStruct((m, D_MODEL), BF16)],
        compiler_params=_cparams(("parallel",)),
        name="out",
    )(*branches, z, ml, x, w_branch, w_out, ln_g, ln_b)


def _split_weights(w_in, b_in):
    scale = np.ones((HB_COLS,), np.float32)
    for cb in (CB_AQ, CB_CQ):
        scale[cb * LANES:cb * LANES + WIDTH] = HEAD_DIM ** -0.5 * LOG2E
    scale[CB_BQ * LANES:CB_BQ * LANES + 3 * WIDTH] = HEAD_DIM ** -0.5
    sc = np.concatenate([np.arange(o, o + HEADS) for o in (O_CF, O_DBETA, O_DDECAY)])
    pad = jnp.zeros(w_in.shape[:2] + (LANES - 3 * HEADS,), w_in.dtype)
    w_hf = jnp.concatenate([w_in[..., sc], pad, w_in[..., O_D:O_D + 3 * WIDTH]], axis=-1)
    b_hf = jnp.concatenate([b_in[..., sc], pad[:, 0], b_in[..., O_D:O_D + 3 * WIDTH]], axis=-1)
    span = lambda a, o, n: a[..., o:o + n]
    groups = {
        "hb": (span(w_in, 0, HB_COLS), span(b_in, 0, HB_COLS), scale),
        "he": (span(w_in, O_E, WIDTH), span(b_in, O_E, WIDTH), np.full((WIDTH,), MEM_HEAD_DIM ** -0.5, np.float32)),
        "hf": (w_hf, b_hf, np.ones((HF_COLS,), np.float32)),
        "z": (span(w_in, O_Z, N_BRANCH * WIDTH), span(b_in, O_Z, N_BRANCH * WIDTH), np.ones((N_BRANCH * WIDTH,), np.float32)),
        "ml": (span(w_in, O_MERGE, N_BRANCH * D_MODEL), span(b_in, O_MERGE, N_BRANCH * D_MODEL),
               np.ones((N_BRANCH * D_MODEL,), np.float32)),
    }
    return {k: (w.astype(F32), b.astype(F32)[:, None, :], jnp.asarray(s)[None, :]) for k, (w, b, s) in groups.items()}


def _layer(x, xb, kv, layer, gw, conv_w, a_log, dt_bias, gdn_norm_w, w_branch, w_out, ln_g, ln_b, alpha):
    bsz, seq, d = x.shape
    m = bsz * seq
    xf = x.reshape(m, d)
    proj = lambda name, dtype: _proj(xb, *gw[name], layer, dtype, PROJ_ROWS, PROJ_COLS[name])
    hb = proj("hb", BF16).reshape(bsz, seq, HB_COLS)
    he = proj("he", BF16).reshape(bsz, seq, WIDTH)
    hf = proj("hf", F32).reshape(bsz, seq, HF_COLS)
    z = proj("z", BF16)
    ml = proj("ml", BF16)

    o_a = _moba(hb)
    o_b = _dilated(hb)
    o_c = _fox(hb, _fcum(hf))
    o_d = _gdn(hf, conv_w, a_log, dt_bias, gdn_norm_w)
    o_e = _memattn(he, kv, layer)
    branches = [o.reshape(m, WIDTH) for o in (o_a, o_b, o_c, o_d, o_e)]
    y, yb = _out(branches, z, ml, xf, w_branch.astype(BF16), w_out.astype(BF16),
                 ln_g.astype(F32)[None, :], ln_b.astype(F32)[None, :], alpha)
    return y.reshape(bsz, seq, d), yb


def kernel(x, mem, mem_ln_g, mem_ln_b, w_in, b_in, conv_w, a_log, dt_bias, gdn_norm_w, w_mem_kv, w_branch, w_out, ln_g, ln_b):
    depth = w_in.shape[0]
    alpha = float((2 * depth) ** 0.25)
    w_kv = jnp.concatenate([w_mem_kv[l] for l in range(depth)], axis=1).astype(BF16)
    kv = _memkv(mem.astype(F32), mem_ln_g.astype(F32)[None, :], mem_ln_b.astype(F32)[None, :], w_kv)
    x = x.astype(F32)
    xb = x.reshape(-1, x.shape[-1]).astype(BF16)
    gw = _split_weights(w_in, b_in)
    for l in range(depth):
        x, xb = _layer(x, xb, kv, l, gw, conv_w[l], a_log[l], dt_bias[l], gdn_norm_w[l],
                       w_branch[l], w_out[l], ln_g[l], ln_b[l], alpha)
    return x
```
